```python
import jax, jax.numpy as jnp
from jax import lax
import numpy as np

D_MODEL = 1024
BATCH = 2
SEQ = 8192
DEPTH = 1

CHUNK = 64
N_ATTN_HEADS = 8
HEAD_DIM = 64
D_ATTN = N_ATTN_HEADS * HEAD_DIM
LEFT_CHUNKS = 8
BAND = (LEFT_CHUNKS + 1) * CHUNK
REL_MAX = 128
REL_MIN = -(CHUNK - 1)
N_REL = REL_MAX - REL_MIN + 1
D_CONV = D_MODEL - D_ATTN
CONV_WIDTH = 31
D_MIX = D_ATTN + D_CONV
D_IN_COLS = 3 * D_ATTN + 2 * D_CONV
N_EXPERTS = 32
TOP_K = 4
D_EXPERT = D_MODEL
SWIGLU_ALPHA = 1.702
SWIGLU_LIMIT = 7.0
RMS_EPS = 1e-5
LN_EPS = 1e-5

kernel_name = "hymba_conformer_chunkattn_moe_block"


def rms_norm(x, g):
    xf = x.astype(jnp.float32)
    y = xf * lax.rsqrt(jnp.mean(xf * xf, axis=-1, keepdims=True) + RMS_EPS)
    return (y * g.astype(jnp.float32)).astype(x.dtype)


def layer_norm(x, g, b):
    xf = x.astype(jnp.float32)
    mu = jnp.mean(xf, axis=-1, keepdims=True)
    var = jnp.mean(jnp.square(xf - mu), axis=-1, keepdims=True)
    y = (xf - mu) * lax.rsqrt(var + LN_EPS)
    return (y * g.astype(jnp.float32) + b.astype(jnp.float32)).astype(x.dtype)


def chunked_rel_attention(q, k, v, rel_bias):
    b, s, h, dh = q.shape
    nc = s // CHUNK
    qc = q.reshape(b, nc, CHUNK, h, dh)
    pad = ((0, 0), (LEFT_CHUNKS * CHUNK, 0), (0, 0), (0, 0))
    kp = jnp.pad(k, pad).reshape(b, nc + LEFT_CHUNKS, CHUNK, h, dh)
    vp = jnp.pad(v, pad).reshape(b, nc + LEFT_CHUNKS, CHUNK, h, dh)
    k_band = jnp.concatenate([kp[:, j:j + nc] for j in range(LEFT_CHUNKS + 1)], axis=2)
    v_band = jnp.concatenate([vp[:, j:j + nc] for j in range(LEFT_CHUNKS + 1)], axis=2)

    iq = jnp.arange(CHUNK)[:, None]
    kk = jnp.arange(BAND)[None, :]
    dist = iq + LEFT_CHUNKS * CHUNK - kk
    idx = jnp.clip(dist, REL_MIN, REL_MAX) - REL_MIN
    bias = rel_bias[:, idx].astype(jnp.float32)

    chunk_of_key = jnp.arange(nc)[:, None] - LEFT_CHUNKS + jnp.arange(BAND)[None, :] // CHUNK
    valid = chunk_of_key >= 0

    scale = HEAD_DIM ** -0.5
    scores = jnp.einsum('bcqhd,bckhd->bchqk', qc, k_band).astype(jnp.float32) * scale
    scores = scores + bias[None, None]
    scores = jnp.where(valid[None, :, None, None, :], scores, -jnp.inf)
    probs = jax.nn.softmax(scores, axis=-1).astype(v.dtype)
    out = jnp.einsum('bchqk,bckhd->bcqhd', probs, v_band)
    return out.reshape(b, s, h * dh)


def conformer_conv(u, dw_w, dw_b, ln_g, ln_b):
    a, gate = jnp.split(u, 2, axis=-1)
    hcv = a * jax.nn.sigmoid(gate)
    hcv = lax.conv_general_dilated(
        hcv, dw_w[:, None, :].astype(hcv.dtype),
        window_strides=(1,),
        padding=((CONV_WIDTH - 1, 0),),
        dimension_numbers=('NWC', 'WIO', 'NWC'),
        feature_group_count=D_CONV) + dw_b
    hcv = layer_norm(hcv, ln_g, ln_b)
    return jax.nn.silu(hcv)


def clamped_swiglu_expert(t, wg, bg, wu, bu, wd, bd):
    gate = jnp.minimum(t @ wg + bg, SWIGLU_LIMIT)
    up = jnp.clip(t @ wu + bu, -SWIGLU_LIMIT, SWIGLU_LIMIT)
    hid = (up + 1.0) * (gate * jax.nn.sigmoid(SWIGLU_ALPHA * gate))
    return hid @ wd + bd


def moe_ffn(x, router_w, router_b, w_gate, b_gate, w_up, b_up, w_down, b_down):
    b, s, d = x.shape
    t = x.reshape(b * s, d)
    logits = (t @ router_w + router_b).astype(jnp.float32)
    top_val, top_idx = lax.top_k(logits, TOP_K)
    top_w = jax.nn.softmax(top_val, axis=-1)
    combine = jnp.sum(jax.nn.one_hot(top_idx, N_EXPERTS, dtype=jnp.float32) * top_w[..., None], axis=1)
    combine = combine.astype(x.dtype)
    y = jnp.zeros_like(t)
    for e in range(N_EXPERTS):
        y = y + combine[:, e:e + 1] * clamped_swiglu_expert(
            t, w_gate[e], b_gate[e], w_up[e], b_up[e], w_down[e], b_down[e])
    return y.reshape(b, s, d)


def setup_inputs(seed: int = 0) -> dict:
    key = jax.random.key(seed)
    ks = jax.random.split(key, 20)
    f32 = jnp.float32
    L = DEPTH

    def nrm(k, shape, scale):
        return scale * jax.random.normal(k, shape, f32)

    return {
        "x": nrm(ks[0], (BATCH, SEQ, D_MODEL), 1.0),
        "norm_mix_g": 1.0 + nrm(ks[1], (L, D_MODEL), 0.02),
        "w_in": nrm(ks[2], (L, D_MODEL, D_IN_COLS), D_MODEL ** -0.5),
        "conv_dw_w": nrm(ks[3], (L, CONV_WIDTH, D_CONV), CONV_WIDTH ** -0.5),
        "conv_dw_b": nrm(ks[4], (L, D_CONV), 0.02),
        "conv_ln_g": 1.0 + nrm(ks[5], (L, D_CONV), 0.02),
        "conv_ln_b": nrm(ks[6], (L, D_CONV), 0.02),
        "rel_bias": nrm(ks[7], (L, N_ATTN_HEADS, N_REL), 0.1),
        "w_out": nrm(ks[8], (L, D_MIX, D_MODEL), D_MIX ** -0.5),
        "norm_ffn_g": 1.0 + nrm(ks[9], (L, D_MODEL), 0.02),
        "router_w": nrm(ks[10], (L, D_MODEL, N_EXPERTS), D_MODEL ** -0.5),
        "router_b": nrm(ks[11], (L, N_EXPERTS), 0.01),
        "exp_w_gate": nrm(ks[12], (L, N_EXPERTS, D_MODEL, D_EXPERT), D_MODEL ** -0.5),
        "exp_b_gate": nrm(ks[13], (L, N_EXPERTS, D_EXPERT), 0.02),
        "exp_w_up": nrm(ks[14], (L, N_EXPERTS, D_MODEL, D_EXPERT), D_MODEL ** -0.5),
        "exp_b_up": nrm(ks[15], (L, N_EXPERTS, D_EXPERT), 0.02),
        "exp_w_down": nrm(ks[16], (L, N_EXPERTS, D_EXPERT, D_MODEL), D_EXPERT ** -0.5),
        "exp_b_down": nrm(ks[17], (L, N_EXPERTS, D_MODEL), 0.02),
        "norm_final_g": 1.0 + nrm(ks[18], (D_MODEL,), 0.02),
    }


def reference(x, norm_mix_g, w_in, conv_dw_w, conv_dw_b, conv_ln_g, conv_ln_b, rel_bias,
              w_out, norm_ffn_g, router_w, router_b, exp_w_gate, exp_b_gate, exp_w_up,
              exp_b_up, exp_w_down, exp_b_down, norm_final_g):
    b, s, _ = x.shape
    for l in range(DEPTH):
        h = rms_norm(x, norm_mix_g[l])
        proj = h @ w_in[l]
        q, k, v, conv_in = jnp.split(proj, [D_ATTN, 2 * D_ATTN, 3 * D_ATTN], axis=-1)
        q = q.reshape(b, s, N_ATTN_HEADS, HEAD_DIM)
        k = k.reshape(b, s, N_ATTN_HEADS, HEAD_DIM)
        v = v.reshape(b, s, N_ATTN_HEADS, HEAD_DIM)
        attn_out = chunked_rel_attention(q, k, v, rel_bias[l])
        conv_out = conformer_conv(conv_in, conv_dw_w[l], conv_dw_b[l],
                                  conv_ln_g[l], conv_ln_b[l])
        mixed = jnp.concatenate([attn_out, conv_out], axis=-1) @ w_out[l]
        x = x + mixed
        x = x + moe_ffn(rms_norm(x, norm_ffn_g[l]), router_w[l], router_b[l],
                        exp_w_gate[l], exp_b_gate[l], exp_w_up[l], exp_b_up[l],
                        exp_w_down[l], exp_b_down[l])
    return rms_norm(x, norm_final_g)
```

```python
import functools

import jax
import jax.numpy as jnp
from jax import lax
from jax.experimental import pallas as pl
from jax.experimental.pallas import tpu as pltpu

F32 = jnp.float32
BF16 = jnp.bfloat16
I32 = jnp.int32

D_MODEL = 1024
CHUNK = 64
N_HEADS = 8
HEAD_DIM = 64
D_ATTN = N_HEADS * HEAD_DIM
LEFT_CHUNKS = 8
REL_MAX = 128
REL_MIN = -(CHUNK - 1)
D_CONV = D_MODEL - D_ATTN
CONV_WIDTH = 31
N_EXPERTS = 32
TOP_K = 4
SWIGLU_ALPHA = 1.702
SWIGLU_LIMIT = 7.0
RMS_EPS = 1e-5
LN_EPS = 1e-5

LEFT_PAD = LEFT_CHUNKS * CHUNK
IN_ROWS = 512
Q_ROWS = 2 * CHUNK
BAND_ROWS = Q_ROWS + LEFT_PAD
HEADS_PER_STEP = 4
GROUP_LANES = HEADS_PER_STEP * HEAD_DIM
MIX_ROWS = 256
HALO_ROWS = 32
EXP_ROWS = 256
NEG_BIG = -1e30
VMEM_LIMIT = 56 * 1024 * 1024


def _cast_rows(src_ref, dst_ref, rows, step=128):
    def body(c, carry):
        r = pl.multiple_of(c * step, step)
        dst_ref[pl.ds(r, step), :] = src_ref[pl.ds(r, step), :].astype(dst_ref.dtype)
        return carry
    lax.fori_loop(0, rows // step, body, 0)


def _inproj_kernel(x_ref, g_ref, w_ref, q_ref, k_ref, v_ref, h_ref, wbf_ref):
    b = pl.program_id(0)
    j = pl.program_id(1)

    @pl.when((b == 0) & (j == 0))
    def _():
        _cast_rows(w_ref, wbf_ref, D_MODEL)

    @pl.when(j == 0)
    def _():
        k_ref[...] = jnp.zeros_like(k_ref)
        v_ref[...] = jnp.zeros_like(v_ref)

    @pl.when(j > 0)
    def _():
        x = x_ref[...]
        ms = jnp.mean(x * x, axis=-1, keepdims=True)
        hb = ((x * lax.rsqrt(ms + RMS_EPS)) * g_ref[...]).astype(BF16)

        def proj(c0, width):
            return jnp.dot(hb, wbf_ref[:, c0:c0 + width], preferred_element_type=F32)

        q_ref[...] = (proj(0, D_ATTN) * (HEAD_DIM ** -0.5)).astype(BF16)
        k_ref[...] = proj(D_ATTN, D_ATTN).astype(BF16)
        v_ref[...] = proj(2 * D_ATTN, D_ATTN).astype(BF16)
        a = proj(3 * D_ATTN, D_CONV)
        gate = proj(3 * D_ATTN + D_CONV, D_CONV)
        h_ref[...] = a * jax.nn.sigmoid(gate)


def _inproj(x, norm_g, w_in):
    bsz, seq, _ = x.shape
    nblk = seq // IN_ROWS
    d_cols = w_in.shape[1]
    row_blk = lambda b, j: (b, jnp.maximum(j - 1, 0), 0)
    return pl.pallas_call(
        _inproj_kernel,
        grid=(bsz, nblk + 1),
        in_specs=[
            pl.BlockSpec((None, IN_ROWS, D_MODEL), row_blk),
            pl.BlockSpec((1, D_MODEL), lambda b, j: (0, 0)),
            pl.BlockSpec((D_MODEL, d_cols), lambda b, j: (0, 0)),
        ],
        out_specs=[
            pl.BlockSpec((None, IN_ROWS, D_ATTN), row_blk),
            pl.BlockSpec((None, IN_ROWS, D_ATTN), lambda b, j: (b, j, 0)),
            pl.BlockSpec((None, IN_ROWS, D_ATTN), lambda b, j: (b, j, 0)),
            pl.BlockSpec((None, IN_ROWS, D_CONV), row_blk),
        ],
        out_shape=[
            jax.ShapeDtypeStruct((bsz, seq, D_ATTN), BF16),
            jax.ShapeDtypeStruct((bsz, seq + LEFT_PAD, D_ATTN), BF16),
            jax.ShapeDtypeStruct((bsz, seq + LEFT_PAD, D_ATTN), BF16),
            jax.ShapeDtypeStruct((bsz, seq, D_CONV), F32),
        ],
        scratch_shapes=[pltpu.VMEM((D_MODEL, d_cols), BF16)],
        compiler_params=pltpu.CompilerParams(
            dimension_semantics=("arbitrary", "arbitrary"), vmem_limit_bytes=VMEM_LIMIT),
        name="inproj",
    )(x, norm_g.reshape(1, D_MODEL), w_in)


def _attn_kernel(q_ref, k_ref, v_ref, bias_ref, o_ref):
    i = pl.program_id(2)
    start = pl.multiple_of(i * Q_ROWS, Q_ROWS)
    q = q_ref[...]
    lane = lax.broadcasted_iota(I32, q.shape, 1) // HEAD_DIM
    qs = jnp.concatenate(
        [jnp.where(lane == h, q, jnp.zeros_like(q)) for h in range(HEADS_PER_STEP)], axis=0)
    kb = k_ref[pl.ds(start, BAND_ROWS), :]
    vb = v_ref[pl.ds(start, BAND_ROWS), :]
    s = lax.dot_general(qs, kb, (((1,), (1,)), ((), ())), preferred_element_type=F32)
    s = s + bias_ref[...]
    col = lax.broadcasted_iota(I32, s.shape, 1)
    s = jnp.where(col >= LEFT_PAD - i * Q_ROWS, s, NEG_BIG)
    m = jnp.max(s, axis=-1, keepdims=True)
    p = jnp.exp(s - m)
    l = jnp.sum(p, axis=-1, keepdims=True)
    o = jnp.dot(p.astype(BF16), vb, preferred_element_type=F32) / l
    out = o[0:Q_ROWS]
    for h in range(1, HEADS_PER_STEP):
        out = jnp.where(lane == h, o[h * Q_ROWS:(h + 1) * Q_ROWS], out)
    o_ref[...] = out.astype(o_ref.dtype)


def _attention(q, kpad, vpad, bias):
    bsz, seq, _ = q.shape
    nq = seq // Q_ROWS
    ngroups = N_HEADS // HEADS_PER_STEP
    return pl.pallas_call(
        _attn_kernel,
        grid=(bsz, ngroups, nq),
        in_specs=[
            pl.BlockSpec((None, Q_ROWS, GROUP_LANES), lambda b, g, i: (b, i, g)),
            pl.BlockSpec((None, seq + LEFT_PAD, GROUP_LANES), lambda b, g, i: (b, 0, g)),
            pl.BlockSpec((None, seq + LEFT_PAD, GROUP_LANES), lambda b, g, i: (b, 0, g)),
            pl.BlockSpec((HEADS_PER_STEP * Q_ROWS, BAND_ROWS), lambda b, g, i: (g, 0)),
        ],
        out_specs=pl.BlockSpec((None, Q_ROWS, GROUP_LANES), lambda b, g, i: (b, i, g)),
        out_shape=jax.ShapeDtypeStruct((bsz, seq, D_ATTN), BF16),
        compiler_params=pltpu.CompilerParams(
            dimension_semantics=("arbitrary", "arbitrary", "arbitrary"),
            vmem_limit_bytes=VMEM_LIMIT),
        name="chunk_attn",
    )(q, kpad, vpad, bias)


def _band_bias(rel_bias):
    r = jnp.arange(Q_ROWS)[:, None]
    m = jnp.arange(BAND_ROWS)[None, :]
    dist = r + LEFT_PAD - m
    cq = r // CHUNK
    ck = m // CHUNK
    in_band = (ck >= cq) & (ck <= cq + LEFT_CHUNKS)
    idx = jnp.clip(dist, REL_MIN, REL_MAX) - REL_MIN
    bias = jnp.where(in_band[None], rel_bias[:, idx].astype(F32), NEG_BIG)
    return bias.reshape(N_HEADS * Q_ROWS, BAND_ROWS)


def _mix_route_kernel(tiles_per_seq,
                      x_ref, a_ref, hc_ref, hh_ref, cw_ref, cb_ref, lg_ref, lb_ref, wo_ref,
                      ng_ref, rwt_ref, rb_ref,
                      x1_ref, t_ref, idx_ref, wgt_ref, rank_ref, cnt_ref,
                      wobf_ref, hw_ref, cntacc_ref):
    i = pl.program_id(0)

    @pl.when(i == 0)
    def _():
        _cast_rows(wo_ref, wobf_ref, D_MODEL)
        cntacc_ref[...] = jnp.zeros_like(cntacc_ref)

    seq_start = (i % tiles_per_seq) == 0
    hw_ref[0:HALO_ROWS, :] = jnp.where(seq_start, 0.0, hh_ref[...])
    hw_ref[HALO_ROWS:HALO_ROWS + MIX_ROWS, :] = hc_ref[...]
    off = HALO_ROWS - (CONV_WIDTH - 1)
    acc = hw_ref[pl.ds(off, MIX_ROWS), :] * cw_ref[0:1, :]
    for j in range(1, CONV_WIDTH):
        acc = acc + hw_ref[pl.ds(off + j, MIX_ROWS), :] * cw_ref[j:j + 1, :]
    acc = acc + cb_ref[...]
    mu = jnp.mean(acc, axis=-1, keepdims=True)
    d = acc - mu
    var = jnp.mean(d * d, axis=-1, keepdims=True)
    y = d * lax.rsqrt(var + LN_EPS) * lg_ref[...] + lb_ref[...]
    conv_out = (y * jax.nn.sigmoid(y)).astype(BF16)

    mixed = jnp.dot(a_ref[...], wobf_ref[0:D_ATTN, :], preferred_element_type=F32)
    mixed = mixed + jnp.dot(conv_out, wobf_ref[D_ATTN:D_MODEL, :], preferred_element_type=F32)
    x1 = x_ref[...] + mixed
    x1_ref[...] = x1
    ms = jnp.mean(x1 * x1, axis=-1, keepdims=True)
    t = (x1 * lax.rsqrt(ms + RMS_EPS)) * ng_ref[...]
    t_ref[...] = t

    logits = lax.dot_general(rwt_ref[...], t, (((1,), (1,)), ((), ())),
                             precision=lax.Precision.HIGHEST,
                             preferred_element_type=F32) + rb_ref[...]
    e_iota = lax.broadcasted_iota(I32, logits.shape, 0)
    vals, idxs, hots = [], [], []
    for _ in range(TOP_K):
        m = jnp.max(logits, axis=0, keepdims=True)
        am = jnp.min(jnp.where(logits == m, e_iota, N_EXPERTS), axis=0, keepdims=True)
        hot = e_iota == am
        vals.append(m)
        idxs.append(am)
        hots.append(hot)
        logits = jnp.where(hot, -jnp.inf, logits)
    exps = [jnp.exp(v - vals[0]) for v in vals]
    den = exps[0] + exps[1] + exps[2] + exps[3]
    wts = [e / den for e in exps]

    hot_sum = (hots[0] | hots[1] | hots[2] | hots[3])
    hot_f = hot_sum.astype(F32)
    ra = lax.broadcasted_iota(I32, (MIX_ROWS, MIX_ROWS), 0)
    rc = lax.broadcasted_iota(I32, (MIX_ROWS, MIX_ROWS), 1)
    upper = (ra < rc).astype(BF16)
    prefix = jnp.dot(hot_f.astype(BF16), upper, preferred_element_type=F32)
    base = prefix + cntacc_ref[...]
    ranks = [jnp.sum(jnp.where(h, base, 0.0), axis=0, keepdims=True) for h in hots]
    cntacc_ref[...] = cntacc_ref[...] + jnp.sum(hot_f, axis=1, keepdims=True)

    idx_ref[...] = jnp.concatenate(idxs, axis=0)
    rank_ref[...] = jnp.concatenate(ranks, axis=0).astype(I32)
    cnt_ref[...] = jnp.broadcast_to(cntacc_ref[...], cnt_ref.shape)
    w8 = jnp.concatenate(wts + [jnp.zeros((8 - TOP_K, MIX_ROWS), F32)], axis=0)
    wgt_ref[...] = w8.T


def _mix_route(x2, attn2, hcv2, seq, conv_w, conv_b, ln_g, ln_b, w_out, norm_g, router_w, router_b):
    n_tok = x2.shape[0]
    ntiles = n_tok // MIX_ROWS
    halo_per_tile = MIX_ROWS // HALO_ROWS
    row = lambda i: (i, 0)
    const = lambda i: (0, 0)
    vec = lambda v: v.reshape(1, -1)
    return pl.pallas_call(
        functools.partial(_mix_route_kernel, seq // MIX_ROWS),
        grid=(ntiles,),
        in_specs=[
            pl.BlockSpec((MIX_ROWS, D_MODEL), row),
            pl.BlockSpec((MIX_ROWS, D_ATTN), row),
            pl.BlockSpec((MIX_ROWS, D_CONV), row),
            pl.BlockSpec((HALO_ROWS, D_CONV), lambda i: (jnp.maximum(i * halo_per_tile - 1, 0), 0)),
            pl.BlockSpec((CONV_WIDTH, D_CONV), const),
            pl.BlockSpec((1, D_CONV), const),
            pl.BlockSpec((1, D_CONV), const),
            pl.BlockSpec((1, D_CONV), const),
            pl.BlockSpec((D_MODEL, D_MODEL), const),
            pl.BlockSpec((1, D_MODEL), const),
            pl.BlockSpec((N_EXPERTS, D_MODEL), const),
            pl.BlockSpec((N_EXPERTS, 1), const),
        ],
        out_specs=[
            pl.BlockSpec((MIX_ROWS, D_MODEL), row),
            pl.BlockSpec((MIX_ROWS, D_MODEL), row),
            pl.BlockSpec((TOP_K, MIX_ROWS), lambda i: (0, i)),
            pl.BlockSpec((MIX_ROWS, 8), row),
            pl.BlockSpec((TOP_K, MIX_ROWS), lambda i: (0, i)),
            pl.BlockSpec((N_EXPERTS, 128), const),
        ],
        out_shape=[
            jax.ShapeDtypeStruct((n_tok, D_MODEL), F32),
            jax.ShapeDtypeStruct((n_tok, D_MODEL), F32),
            jax.ShapeDtypeStruct((TOP_K, n_tok), I32),
            jax.ShapeDtypeStruct((n_tok, 8), F32),
            jax.ShapeDtypeStruct((TOP_K, n_tok), I32),
            jax.ShapeDtypeStruct((N_EXPERTS, 128), F32),
        ],
        scratch_shapes=[
            pltpu.VMEM((D_MODEL, D_MODEL), BF16),
            pltpu.VMEM((HALO_ROWS + MIX_ROWS, D_CONV), F32),
            pltpu.VMEM((N_EXPERTS, 1), F32),
        ],
        compiler_params=pltpu.CompilerParams(
            dimension_semantics=("arbitrary",), vmem_limit_bytes=VMEM_LIMIT),
        name="mix_route",
    )(x2, attn2, hcv2, hcv2, conv_w, vec(conv_b), vec(ln_g), vec(ln_b), w_out, vec(norm_g),
      router_w.T, router_b.reshape(N_EXPERTS, 1))


def _gather_copy(t_hbm, xbuf, sem, slot, tok, r):
    return pltpu.make_async_copy(t_hbm.at[pl.ds(tok, 1)], xbuf.at[slot, pl.ds(r, 1)], sem.at[slot])


def _scatter_copy(obuf, slab_hbm, sem, slot, dst, r):
    return pltpu.make_async_copy(obuf.at[slot, pl.ds(r, 1)], slab_hbm.at[pl.ds(dst, 1)], sem.at[slot])


def _expert_kernel(texp_ref, tfirst_ref, tvalid_ref,
                   src_ref, src_next_ref, dst_ref, t_hbm,
                   wg_ref, bg_ref, wu_ref, bu_ref, wd_ref, bd_ref,
                   slab_hbm,
                   xbuf, obuf, wbf, gsem, ssem):
    i = pl.program_id(0)
    n = pl.num_programs(0)
    slot = i % 2
    valid = tvalid_ref[i] == 1

    def start_gather(tok_ref, dst_slot):
        def body(r, carry):
            _gather_copy(t_hbm, xbuf, gsem, dst_slot, tok_ref[0, r], r).start()
            return carry
        lax.fori_loop(0, EXP_ROWS, body, 0, unroll=8)

    @pl.when(i == 0)
    def _():
        start_gather(src_ref, 0)
        obuf[1] = jnp.zeros((EXP_ROWS, D_MODEL), F32)
        spare = pltpu.make_async_copy(
            obuf.at[1], slab_hbm.at[pl.ds(slab_hbm.shape[0] - EXP_ROWS, EXP_ROWS)], ssem.at[1])
        spare.start()
        spare.wait()

    @pl.when(valid)
    def _():
        pltpu.make_async_copy(t_hbm.at[pl.ds(0, EXP_ROWS)], xbuf.at[slot], gsem.at[slot]).wait()

    @pl.when((i + 1 < n) & (tvalid_ref[jnp.minimum(i + 1, n - 1)] == 1))
    def _():
        start_gather(src_next_ref, 1 - slot)

    @pl.when((i >= 2) & (tvalid_ref[jnp.maximum(i - 2, 0)] == 1))
    def _():
        pltpu.make_async_copy(obuf.at[slot], slab_hbm.at[pl.ds(0, EXP_ROWS)], ssem.at[slot]).wait()

    @pl.when(valid)
    def _():
        @pl.when(tfirst_ref[i] == 1)
        def _():
            for mtx, w_ref in enumerate((wg_ref, wu_ref, wd_ref)):
                _cast_rows(w_ref, wbf.at[mtx], D_MODEL)

        xs = xbuf[slot].astype(BF16)
        acc = None
        cn = 256
        for c in range(D_MODEL // cn):
            sl = slice(c * cn, (c + 1) * cn)
            g = jnp.dot(xs, wbf[0, :, sl], preferred_element_type=F32) + bg_ref[:, sl]
            u = jnp.dot(xs, wbf[1, :, sl], preferred_element_type=F32) + bu_ref[:, sl]
            g = jnp.minimum(g, SWIGLU_LIMIT)
            u = jnp.clip(u, -SWIGLU_LIMIT, SWIGLU_LIMIT)
            hid = ((u + 1.0) * (g * jax.nn.sigmoid(SWIGLU_ALPHA * g))).astype(BF16)
            part = jnp.dot(hid, wbf[2, sl, :], preferred_element_type=F32)
            acc = part if acc is None else acc + part
        obuf[slot] = acc + bd_ref[...]

        def body(r, carry):
            _scatter_copy(obuf, slab_hbm, ssem, slot, dst_ref[0, r], r).start()
            return carry
        lax.fori_loop(0, EXP_ROWS, body, 0, unroll=8)


def _experts(t2, plan, w_gate, b_gate, w_up, b_up, w_down, b_down, n_tiles):
    n_tok = t2.shape[0]
    tile_expert, tile_first, tile_valid, src_tok, dst_row = plan
    idx_blk = pl.BlockSpec((None, 1, EXP_ROWS), lambda i, te, tf, tv: (i, 0, 0),
                           memory_space=pltpu.SMEM)
    idx_next_blk = pl.BlockSpec((None, 1, EXP_ROWS),
                                lambda i, te, tf, tv: (jnp.minimum(i + 1, n_tiles - 1), 0, 0),
                                memory_space=pltpu.SMEM)
    w_blk = pl.BlockSpec((None, D_MODEL, D_MODEL), lambda i, te, tf, tv: (te[i], 0, 0))
    b_blk = pl.BlockSpec((None, 1, D_MODEL), lambda i, te, tf, tv: (te[i], 0, 0))
    grid_spec = pltpu.PrefetchScalarGridSpec(
        num_scalar_prefetch=3,
        grid=(n_tiles,),
        in_specs=[idx_blk, idx_next_blk, idx_blk,
                  pl.BlockSpec(memory_space=pl.ANY),
                  w_blk, b_blk, w_blk, b_blk, w_blk, b_blk],
        out_specs=pl.BlockSpec(memory_space=pl.ANY),
        scratch_shapes=[
            pltpu.VMEM((2, EXP_ROWS, D_MODEL), F32),
            pltpu.VMEM((2, EXP_ROWS, D_MODEL), F32),
            pltpu.VMEM((3, D_MODEL, D_MODEL), BF16),
            pltpu.SemaphoreType.DMA((2,)),
            pltpu.SemaphoreType.DMA((2,)),
        ],
    )
    b3 = lambda b: b.reshape(N_EXPERTS, 1, D_MODEL)
    return pl.pallas_call(
        _expert_kernel,
        grid_spec=grid_spec,
        out_shape=jax.ShapeDtypeStruct((TOP_K * n_tok + EXP_ROWS, D_MODEL), F32),
        compiler_params=pltpu.CompilerParams(
            dimension_semantics=("arbitrary",), vmem_limit_bytes=VMEM_LIMIT),
        name="experts",
    )(tile_expert, tile_first, tile_valid,
      src_tok, src_tok, dst_row, t2,
      w_gate, b3(b_gate), w_up, b3(b_up), w_down, b3(b_down))


def _routing_plan(idx, rank, counts, n_tok, n_tiles):
    tiles_per = (counts + EXP_ROWS - 1) // EXP_ROWS
    tile_end = jnp.cumsum(tiles_per)
    tile_begin = tile_end - tiles_per
    n_valid = tile_end[-1]
    pos = (tile_begin * EXP_ROWS)[idx] + rank
    pair = jnp.arange(TOP_K * n_tok, dtype=I32)
    slot_pair = jnp.full((n_tiles * EXP_ROWS,), -1, I32).at[pos.reshape(-1)].set(
        pair, unique_indices=True)
    slots = jnp.arange(n_tiles * EXP_ROWS, dtype=I32)
    filled = slot_pair >= 0
    src_tok = jnp.where(filled, slot_pair % n_tok, 0)
    dst_row = jnp.where(filled, slot_pair, TOP_K * n_tok + slots % EXP_ROWS)
    tiles = jnp.arange(n_tiles, dtype=I32)
    tile_valid = tiles < n_valid
    tile_expert = jnp.searchsorted(tile_end, jnp.minimum(tiles, n_valid - 1), side="right").astype(I32)
    tile_expert = jnp.minimum(tile_expert, N_EXPERTS - 1)
    tile_first = tile_valid & (tiles == tile_begin[tile_expert])
    shp = (n_tiles, 1, EXP_ROWS)
    return (tile_expert, tile_first.astype(I32), tile_valid.astype(I32),
            src_tok.reshape(shp), dst_row.reshape(shp))


def _combine_kernel(x1_ref, s0_ref, s1_ref, s2_ref, s3_ref, w_ref, g_ref, o_ref):
    w = w_ref[...]
    y = x1_ref[...]
    for k, s_ref in enumerate((s0_ref, s1_ref, s2_ref, s3_ref)):
        y = y + w[:, k:k + 1] * s_ref[...]
    ms = jnp.mean(y * y, axis=-1, keepdims=True)
    o_ref[...] = (y * lax.rsqrt(ms + RMS_EPS)) * g_ref[...]


def _combine(x1, slabs, wgt, norm_g):
    n_tok = x1.shape[0]
    rows = MIX_ROWS
    ntiles = n_tok // rows
    slab_spec = lambda k: pl.BlockSpec((rows, D_MODEL), lambda i, k=k: (k * ntiles + i, 0))
    return pl.pallas_call(
        _combine_kernel,
        grid=(ntiles,),
        in_specs=[pl.BlockSpec((rows, D_MODEL), lambda i: (i, 0))]
        + [slab_spec(k) for k in range(TOP_K)]
        + [pl.BlockSpec((rows, 8), lambda i: (i, 0)),
           pl.BlockSpec((1, D_MODEL), lambda i: (0, 0))],
        out_specs=pl.BlockSpec((rows, D_MODEL), lambda i: (i, 0)),
        out_shape=jax.ShapeDtypeStruct((n_tok, D_MODEL), F32),
        compiler_params=pltpu.CompilerParams(
            dimension_semantics=("arbitrary",), vmem_limit_bytes=VMEM_LIMIT),
        name="combine",
    )(x1, slabs, slabs, slabs, slabs, wgt, norm_g.reshape(1, D_MODEL))


def kernel(x, norm_mix_g, w_in, conv_dw_w, conv_dw_b, conv_ln_g, conv_ln_b, rel_bias, w_out,
           norm_ffn_g, router_w, router_b, exp_w_gate, exp_b_gate, exp_w_up, exp_b_up,
           exp_w_down, exp_b_down, norm_final_g):
    bsz, seq, _ = x.shape
    n_tok = bsz * seq
    assert norm_mix_g.shape[0] == 1, "single-layer block"
    assert seq % IN_ROWS == 0 and seq % MIX_ROWS == 0 and IN_ROWS == LEFT_PAD
    n_tiles = (TOP_K * n_tok) // EXP_ROWS + N_EXPERTS - 1 + 2

    q, kpad, vpad, hcv = _inproj(x, norm_mix_g[0], w_in[0])
    attn = _attention(q, kpad, vpad, _band_bias(rel_bias[0]))
    x1, t, idx, wgt, rank, cnt = _mix_route(
        x.reshape(n_tok, D_MODEL), attn.reshape(n_tok, D_ATTN), hcv.reshape(n_tok, D_CONV), seq,
        conv_dw_w[0], conv_dw_b[0], conv_ln_g[0], conv_ln_b[0], w_out[0], norm_ffn_g[0],
        router_w[0], router_b[0])
    plan = _routing_plan(idx, rank, cnt[:, 0].astype(I32), n_tok, n_tiles)
    slabs = _experts(t, plan, exp_w_gate[0], exp_b_gate[0], exp_w_up[0], exp_b_up[0],
                     exp_w_down[0], exp_b_down[0], n_tiles)
    out = _combine(x1, slabs, wgt, norm_final_g)
    return out.reshape(bsz, seq, D_MODEL)
```

```python
import functools

import jax
import jax.numpy as jnp
from jax import lax
from jax.experimental import pallas as pl
from jax.experimental.pallas import tpu as pltpu
from jax.experimental.pallas import tpu_sc as plsc

F32 = jnp.float32
BF16 = jnp.bfloat16
I32 = jnp.int32

D_MODEL = 1024
CHUNK = 64
N_HEADS = 8
HEAD_DIM = 64
D_ATTN = N_HEADS * HEAD_DIM
LEFT_CHUNKS = 8
REL_MAX = 128
REL_MIN = -(CHUNK - 1)
D_CONV = D_MODEL - D_ATTN
CONV_WIDTH = 31
N_EXPERTS = 32
TOP_K = 4
SWIGLU_ALPHA = 1.702
SWIGLU_LIMIT = 7.0
RMS_EPS = 1e-5
LN_EPS = 1e-5

LEFT_PAD = LEFT_CHUNKS * CHUNK
IN_ROWS = 512
Q_ROWS = 2 * CHUNK
BAND_ROWS = Q_ROWS + LEFT_PAD
HEADS_PER_STEP = 4
GROUP_LANES = HEADS_PER_STEP * HEAD_DIM
MIX_ROWS = 256
HALO_ROWS = 32
EXP_ROWS = 256
SC_ROWS = 64
NEG_BIG = -1e30
VMEM_LIMIT = 56 * 1024 * 1024


def _cast_rows(src_ref, dst_ref, rows, step=128):
    def body(c, carry):
        r = pl.multiple_of(c * step, step)
        dst_ref[pl.ds(r, step), :] = src_ref[pl.ds(r, step), :].astype(dst_ref.dtype)
        return carry
    lax.fori_loop(0, rows // step, body, 0)


def _inproj_kernel(x_ref, g_ref, w_ref, q_ref, k_ref, v_ref, h_ref, wbf_ref):
    b = pl.program_id(0)
    j = pl.program_id(1)

    @pl.when((b == 0) & (j == 0))
    def _():
        _cast_rows(w_ref, wbf_ref, D_MODEL)

    @pl.when(j == 0)
    def _():
        k_ref[...] = jnp.zeros_like(k_ref)
        v_ref[...] = jnp.zeros_like(v_ref)

    @pl.when(j > 0)
    def _():
        x = x_ref[...]
        ms = jnp.mean(x * x, axis=-1, keepdims=True)
        hb = ((x * lax.rsqrt(ms + RMS_EPS)) * g_ref[...]).astype(BF16)

        def proj(c0, width):
            return jnp.dot(hb, wbf_ref[:, c0:c0 + width], preferred_element_type=F32)

        q_ref[...] = (proj(0, D_ATTN) * (HEAD_DIM ** -0.5)).astype(BF16)
        k_ref[...] = proj(D_ATTN, D_ATTN).astype(BF16)
        v_ref[...] = proj(2 * D_ATTN, D_ATTN).astype(BF16)
        a = proj(3 * D_ATTN, D_CONV)
        gate = proj(3 * D_ATTN + D_CONV, D_CONV)
        h_ref[...] = a * jax.nn.sigmoid(gate)


def _inproj(x, norm_g, w_in):
    bsz, seq, _ = x.shape
    nblk = seq // IN_ROWS
    d_cols = w_in.shape[1]
    row_blk = lambda b, j: (b, jnp.maximum(j - 1, 0), 0)
    return pl.pallas_call(
        _inproj_kernel,
        grid=(bsz, nblk + 1),
        in_specs=[
            pl.BlockSpec((None, IN_ROWS, D_MODEL), row_blk),
            pl.BlockSpec((1, D_MODEL), lambda b, j: (0, 0)),
            pl.BlockSpec((D_MODEL, d_cols), lambda b, j: (0, 0)),
        ],
        out_specs=[
            pl.BlockSpec((None, IN_ROWS, D_ATTN), row_blk),
            pl.BlockSpec((None, IN_ROWS, D_ATTN), lambda b, j: (b, j, 0)),
            pl.BlockSpec((None, IN_ROWS, D_ATTN), lambda b, j: (b, j, 0)),
            pl.BlockSpec((None, IN_ROWS, D_CONV), row_blk),
        ],
        out_shape=[
            jax.ShapeDtypeStruct((bsz, seq, D_ATTN), BF16),
            jax.ShapeDtypeStruct((bsz, seq + LEFT_PAD, D_ATTN), BF16),
            jax.ShapeDtypeStruct((bsz, seq + LEFT_PAD, D_ATTN), BF16),
            jax.ShapeDtypeStruct((bsz, seq, D_CONV), F32),
        ],
        scratch_shapes=[pltpu.VMEM((D_MODEL, d_cols), BF16)],
        compiler_params=pltpu.CompilerParams(
            dimension_semantics=("arbitrary", "arbitrary"), vmem_limit_bytes=VMEM_LIMIT),
        name="inproj",
    )(x, norm_g.reshape(1, D_MODEL), w_in)


def _attn_kernel(q_ref, k_ref, v_ref, bias_ref, o_ref):
    i = pl.program_id(2)
    start = pl.multiple_of(i * Q_ROWS, Q_ROWS)
    q = q_ref[...]
    lane = lax.broadcasted_iota(I32, q.shape, 1) // HEAD_DIM
    qs = jnp.concatenate(
        [jnp.where(lane == h, q, jnp.zeros_like(q)) for h in range(HEADS_PER_STEP)], axis=0)
    kb = k_ref[pl.ds(start, BAND_ROWS), :]
    vb = v_ref[pl.ds(start, BAND_ROWS), :]
    s = lax.dot_general(qs, kb, (((1,), (1,)), ((), ())), preferred_element_type=F32)
    s = s + bias_ref[...]
    col = lax.broadcasted_iota(I32, s.shape, 1)
    s = jnp.where(col >= LEFT_PAD - i * Q_ROWS, s, NEG_BIG)
    m = jnp.max(s, axis=-1, keepdims=True)
    p = jnp.exp(s - m)
    l = jnp.sum(p, axis=-1, keepdims=True)
    o = jnp.dot(p.astype(BF16), vb, preferred_element_type=F32) / l
    out = o[0:Q_ROWS]
    for h in range(1, HEADS_PER_STEP):
        out = jnp.where(lane == h, o[h * Q_ROWS:(h + 1) * Q_ROWS], out)
    o_ref[...] = out.astype(o_ref.dtype)


def _attention(q, kpad, vpad, bias):
    bsz, seq, _ = q.shape
    nq = seq // Q_ROWS
    ngroups = N_HEADS // HEADS_PER_STEP
    return pl.pallas_call(
        _attn_kernel,
        grid=(bsz, ngroups, nq),
        in_specs=[
            pl.BlockSpec((None, Q_ROWS, GROUP_LANES), lambda b, g, i: (b, i, g)),
            pl.BlockSpec((None, seq + LEFT_PAD, GROUP_LANES), lambda b, g, i: (b, 0, g)),
            pl.BlockSpec((None, seq + LEFT_PAD, GROUP_LANES), lambda b, g, i: (b, 0, g)),
            pl.BlockSpec((HEADS_PER_STEP * Q_ROWS, BAND_ROWS), lambda b, g, i: (g, 0)),
        ],
        out_specs=pl.BlockSpec((None, Q_ROWS, GROUP_LANES), lambda b, g, i: (b, i, g)),
        out_shape=jax.ShapeDtypeStruct((bsz, seq, D_ATTN), BF16),
        compiler_params=pltpu.CompilerParams(
            dimension_semantics=("arbitrary", "arbitrary", "arbitrary"),
            vmem_limit_bytes=VMEM_LIMIT),
        name="chunk_attn",
    )(q, kpad, vpad, bias)


def _band_bias(rel_bias):
    r = jnp.arange(Q_ROWS)[:, None]
    m = jnp.arange(BAND_ROWS)[None, :]
    dist = r + LEFT_PAD - m
    cq = r // CHUNK
    ck = m // CHUNK
    in_band = (ck >= cq) & (ck <= cq + LEFT_CHUNKS)
    idx = jnp.clip(dist, REL_MIN, REL_MAX) - REL_MIN
    bias = jnp.where(in_band[None], rel_bias[:, idx].astype(F32), NEG_BIG)
    return bias.reshape(N_HEADS * Q_ROWS, BAND_ROWS)


def _mix_route_kernel(tiles_per_seq,
                      x_ref, a_ref, hc_ref, hh_ref, cw_ref, cb_ref, lg_ref, lb_ref, wo_ref,
                      ng_ref, rwt_ref, rb_ref,
                      x1_ref, t_ref, idx_ref, wgt_ref, rank_ref, cnt_ref,
                      wobf_ref, hw_ref, cntacc_ref):
    i = pl.program_id(0)

    @pl.when(i == 0)
    def _():
        _cast_rows(wo_ref, wobf_ref, D_MODEL)
        cntacc_ref[...] = jnp.zeros_like(cntacc_ref)

    seq_start = (i % tiles_per_seq) == 0
    hw_ref[0:HALO_ROWS, :] = jnp.where(seq_start, 0.0, hh_ref[...])
    hw_ref[HALO_ROWS:HALO_ROWS + MIX_ROWS, :] = hc_ref[...]
    off = HALO_ROWS - (CONV_WIDTH - 1)
    acc = hw_ref[pl.ds(off, MIX_ROWS), :] * cw_ref[0:1, :]
    for j in range(1, CONV_WIDTH):
        acc = acc + hw_ref[pl.ds(off + j, MIX_ROWS), :] * cw_ref[j:j + 1, :]
    acc = acc + cb_ref[...]
    mu = jnp.mean(acc, axis=-1, keepdims=True)
    d = acc - mu
    var = jnp.mean(d * d, axis=-1, keepdims=True)
    y = d * lax.rsqrt(var + LN_EPS) * lg_ref[...] + lb_ref[...]
    conv_out = (y * jax.nn.sigmoid(y)).astype(BF16)

    mixed = jnp.dot(a_ref[...], wobf_ref[0:D_ATTN, :], preferred_element_type=F32)
    mixed = mixed + jnp.dot(conv_out, wobf_ref[D_ATTN:D_MODEL, :], preferred_element_type=F32)
    x1 = x_ref[...] + mixed
    x1_ref[...] = x1
    ms = jnp.mean(x1 * x1, axis=-1, keepdims=True)
    t = (x1 * lax.rsqrt(ms + RMS_EPS)) * ng_ref[...]
    t_ref[...] = t

    logits = lax.dot_general(rwt_ref[...], t, (((1,), (1,)), ((), ())),
                             precision=lax.Precision.HIGHEST,
                             preferred_element_type=F32) + rb_ref[...]
    e_iota = lax.broadcasted_iota(I32, logits.shape, 0)
    vals, idxs, hots = [], [], []
    for _ in range(TOP_K):
        m = jnp.max(logits, axis=0, keepdims=True)
        am = jnp.min(jnp.where(logits == m, e_iota, N_EXPERTS), axis=0, keepdims=True)
        hot = e_iota == am
        vals.append(m)
        idxs.append(am)
        hots.append(hot)
        logits = jnp.where(hot, -jnp.inf, logits)
    exps = [jnp.exp(v - vals[0]) for v in vals]
    den = exps[0] + exps[1] + exps[2] + exps[3]
    wts = [e / den for e in exps]

    hot_sum = (hots[0] | hots[1] | hots[2] | hots[3])
    hot_f = hot_sum.astype(F32)
    ra = lax.broadcasted_iota(I32, (MIX_ROWS, MIX_ROWS), 0)
    rc = lax.broadcasted_iota(I32, (MIX_ROWS, MIX_ROWS), 1)
    upper = (ra < rc).astype(BF16)
    prefix = jnp.dot(hot_f.astype(BF16), upper, preferred_element_type=F32)
    base = prefix + cntacc_ref[...]
    ranks = [jnp.sum(jnp.where(h, base, 0.0), axis=0, keepdims=True) for h in hots]
    cntacc_ref[...] = cntacc_ref[...] + jnp.sum(hot_f, axis=1, keepdims=True)

    idx_ref[...] = jnp.concatenate(idxs, axis=0)
    rank_ref[...] = jnp.concatenate(ranks, axis=0).astype(I32)
    cnt_ref[...] = jnp.broadcast_to(cntacc_ref[...], cnt_ref.shape)
    w8 = jnp.concatenate(wts + [jnp.zeros((8 - TOP_K, MIX_ROWS), F32)], axis=0)
    wgt_ref[...] = w8.T


def _mix_route(x2, attn2, hcv2, seq, conv_w, conv_b, ln_g, ln_b, w_out, norm_g, router_w, router_b):
    n_tok = x2.shape[0]
    ntiles = n_tok // MIX_ROWS
    halo_per_tile = MIX_ROWS // HALO_ROWS
    row = lambda i: (i, 0)
    const = lambda i: (0, 0)
    vec = lambda v: v.reshape(1, -1)
    return pl.pallas_call(
        functools.partial(_mix_route_kernel, seq // MIX_ROWS),
        grid=(ntiles,),
        in_specs=[
            pl.BlockSpec((MIX_ROWS, D_MODEL), row),
            pl.BlockSpec((MIX_ROWS, D_ATTN), row),
            pl.BlockSpec((MIX_ROWS, D_CONV), row),
            pl.BlockSpec((HALO_ROWS, D_CONV), lambda i: (jnp.maximum(i * halo_per_tile - 1, 0), 0)),
            pl.BlockSpec((CONV_WIDTH, D_CONV), const),
            pl.BlockSpec((1, D_CONV), const),
            pl.BlockSpec((1, D_CONV), const),
            pl.BlockSpec((1, D_CONV), const),
            pl.BlockSpec((D_MODEL, D_MODEL), const),
            pl.BlockSpec((1, D_MODEL), const),
            pl.BlockSpec((N_EXPERTS, D_MODEL), const),
            pl.BlockSpec((N_EXPERTS, 1), const),
        ],
        out_specs=[
            pl.BlockSpec((MIX_ROWS, D_MODEL), row),
            pl.BlockSpec((MIX_ROWS, D_MODEL), row),
            pl.BlockSpec((TOP_K, MIX_ROWS), lambda i: (0, i)),
            pl.BlockSpec((MIX_ROWS, 8), row),
            pl.BlockSpec((TOP_K, MIX_ROWS), lambda i: (0, i)),
            pl.BlockSpec((N_EXPERTS, 128), const),
        ],
        out_shape=[
            jax.ShapeDtypeStruct((n_tok, D_MODEL), F32),
            jax.ShapeDtypeStruct((n_tok, D_MODEL), F32),
            jax.ShapeDtypeStruct((TOP_K, n_tok), I32),
            jax.ShapeDtypeStruct((n_tok, 8), F32),
            jax.ShapeDtypeStruct((TOP_K, n_tok), I32),
            jax.ShapeDtypeStruct((N_EXPERTS, 128), F32),
        ],
        scratch_shapes=[
            pltpu.VMEM((D_MODEL, D_MODEL), BF16),
            pltpu.VMEM((HALO_ROWS + MIX_ROWS, D_CONV), F32),
            pltpu.VMEM((N_EXPERTS, 1), F32),
        ],
        compiler_params=pltpu.CompilerParams(
            dimension_semantics=("arbitrary",), vmem_limit_bytes=VMEM_LIMIT),
        name="mix_route",
    )(x2, attn2, hcv2, hcv2, conv_w, vec(conv_b), vec(ln_g), vec(ln_b), w_out, vec(norm_g),
      router_w.T, router_b.reshape(N_EXPERTS, 1))


def _tile_plan(counts, n_tiles):
    tiles_per = (counts + EXP_ROWS - 1) // EXP_ROWS
    tile_end = jnp.cumsum(tiles_per)
    tile_begin = tile_end - tiles_per
    n_valid = tile_end[-1]
    tiles = jnp.arange(n_tiles, dtype=I32)
    tile_valid = tiles < n_valid
    capped = jnp.minimum(tiles, n_valid - 1)
    tile_expert = jnp.sum((capped[:, None] >= tile_end[None, :]).astype(I32), axis=1)
    tile_expert = jnp.minimum(tile_expert, N_EXPERTS - 1)
    is_first = (tiles[:, None] == tile_begin[None, :]) & (tiles_per[None, :] > 0)
    tile_first = tile_valid & jnp.any(is_first, axis=1)
    group_start = tile_begin * EXP_ROWS
    return tile_expert, tile_first.astype(I32), tile_valid.astype(I32), group_start.astype(I32)


def _slot_kernel(gstart_ref, idx_ref, rank_ref, pos_ref):
    idx = idx_ref[...]
    pos = rank_ref[...]
    for e in range(N_EXPERTS):
        pos = pos + jnp.where(idx == e, gstart_ref[e], 0)
    pos_ref[...] = pos


def _slots(group_start, idx, rank):
    full = pl.BlockSpec(idx.shape, lambda i, gs: (0, 0))
    return pl.pallas_call(
        _slot_kernel,
        grid_spec=pltpu.PrefetchScalarGridSpec(
            num_scalar_prefetch=1, grid=(1,), in_specs=[full, full], out_specs=full),
        out_shape=jax.ShapeDtypeStruct(idx.shape, I32),
        name="slots",
    )(group_start, idx, rank)


def _sc_mesh():
    return plsc.VectorSubcoreMesh(core_axis_name="core", subcore_axis_name="subcore")


def _sc_worker():
    info = plsc.get_sparse_core_info()
    wid = lax.axis_index("subcore") * info.num_cores + lax.axis_index("core")
    return wid, info.num_cores * info.num_subcores


def _dispatch(t2, pos_flat, n_slots):
    n_tok, d = t2.shape
    n_workers = 32
    per_w = n_tok // n_workers
    assert per_w % SC_ROWS == 0

    @functools.partial(
        pl.kernel, mesh=_sc_mesh(),
        out_type=jax.ShapeDtypeStruct((n_slots, d), t2.dtype),
        scratch_types=[pltpu.VMEM((SC_ROWS,), I32), pltpu.VMEM((SC_ROWS, d), t2.dtype)],
        name="dispatch",
    )
    def k(t_hbm, pos_hbm, o_hbm, idx_v, rows_v):
        wid, nw = _sc_worker()
        assert nw == n_workers

        @pl.loop(0, per_w // SC_ROWS)
        def _(c):
            base = pl.multiple_of(wid * per_w + c * SC_ROWS, SC_ROWS)
            pltpu.sync_copy(t_hbm.at[pl.ds(base, SC_ROWS)], rows_v)
            for kk in range(TOP_K):
                pltpu.sync_copy(pos_hbm.at[pl.ds(kk * n_tok + base, SC_ROWS)], idx_v)
                pltpu.sync_copy(rows_v, o_hbm.at[idx_v])

    return k(t2, pos_flat)


def _gather_back(sorted_out, pos_flat):
    n_rows = pos_flat.shape[0]
    d = sorted_out.shape[1]
    n_workers = 32
    per_w = n_rows // n_workers
    assert per_w % SC_ROWS == 0

    @functools.partial(
        pl.kernel, mesh=_sc_mesh(),
        out_type=jax.ShapeDtypeStruct((n_rows, d), sorted_out.dtype),
        scratch_types=[pltpu.VMEM((SC_ROWS,), I32), pltpu.VMEM((SC_ROWS, d), sorted_out.dtype)],
        name="gather_back",
    )
    def k(s_hbm, pos_hbm, o_hbm, idx_v, rows_v):
        wid, nw = _sc_worker()
        assert nw == n_workers

        @pl.loop(0, per_w // SC_ROWS)
        def _(c):
            base = pl.multiple_of(wid * per_w + c * SC_ROWS, SC_ROWS)
            pltpu.sync_copy(pos_hbm.at[pl.ds(base, SC_ROWS)], idx_v)
            pltpu.sync_copy(s_hbm.at[idx_v], rows_v)
            pltpu.sync_copy(rows_v, o_hbm.at[pl.ds(base, SC_ROWS)])

    return k(sorted_out, pos_flat)


def _expert_kernel(texp_ref, tfirst_ref, tvalid_ref,
                   x_ref, wg_ref, bg_ref, wu_ref, bu_ref, wd_ref, bd_ref,
                   o_ref, wbf):
    i = pl.program_id(0)

    @pl.when(tvalid_ref[i] == 1)
    def _():
        @pl.when(tfirst_ref[i] == 1)
        def _():
            for mtx, w_ref in enumerate((wg_ref, wu_ref, wd_ref)):
                _cast_rows(w_ref, wbf.at[mtx], D_MODEL)

        xs = x_ref[...].astype(BF16)
        acc = None
        cn = 256
        for c in range(D_MODEL // cn):
            sl = slice(c * cn, (c + 1) * cn)
            g = jnp.dot(xs, wbf[0, :, sl], preferred_element_type=F32) + bg_ref[:, sl]
            u = jnp.dot(xs, wbf[1, :, sl], preferred_element_type=F32) + bu_ref[:, sl]
            g = jnp.minimum(g, SWIGLU_LIMIT)
            u = jnp.clip(u, -SWIGLU_LIMIT, SWIGLU_LIMIT)
            hid = ((u + 1.0) * (g * jax.nn.sigmoid(SWIGLU_ALPHA * g))).astype(BF16)
            part = jnp.dot(hid, wbf[2, sl, :], preferred_element_type=F32)
            acc = part if acc is None else acc + part
        o_ref[...] = acc + bd_ref[...]


def _experts(sorted_t, tile_expert, tile_first, tile_valid,
             w_gate, b_gate, w_up, b_up, w_down, b_down, n_tiles):
    row_blk = pl.BlockSpec((EXP_ROWS, D_MODEL), lambda i, te, tf, tv: (i, 0))
    w_blk = pl.BlockSpec((None, D_MODEL, D_MODEL), lambda i, te, tf, tv: (te[i], 0, 0))
    b_blk = pl.BlockSpec((None, 1, D_MODEL), lambda i, te, tf, tv: (te[i], 0, 0))
    grid_spec = pltpu.PrefetchScalarGridSpec(
        num_scalar_prefetch=3,
        grid=(n_tiles,),
        in_specs=[row_blk, w_blk, b_blk, w_blk, b_blk, w_blk, b_blk],
        out_specs=row_blk,
        scratch_shapes=[pltpu.VMEM((3, D_MODEL, D_MODEL), BF16)],
    )
    b3 = lambda b: b.reshape(N_EXPERTS, 1, D_MODEL)
    return pl.pallas_call(
        _expert_kernel,
        grid_spec=grid_spec,
        out_shape=jax.ShapeDtypeStruct((n_tiles * EXP_ROWS, D_MODEL), F32),
        compiler_params=pltpu.CompilerParams(
            dimension_semantics=("arbitrary",), vmem_limit_bytes=VMEM_LIMIT),
        name="experts",
    )(tile_expert, tile_first, tile_valid, sorted_t,
      w_gate, b3(b_gate), w_up, b3(b_up), w_down, b3(b_down))


def _combine_kernel(x1_ref, s0_ref, s1_ref, s2_ref, s3_ref, w_ref, g_ref, o_ref):
    w = w_ref[...]
    y = x1_ref[...]
    for k, s_ref in enumerate((s0_ref, s1_ref, s2_ref, s3_ref)):
        y = y + w[:, k:k + 1] * s_ref[...]
    ms = jnp.mean(y * y, axis=-1, keepdims=True)
    o_ref[...] = (y * lax.rsqrt(ms + RMS_EPS)) * g_ref[...]


def _combine(x1, slabs, wgt, norm_g):
    n_tok = x1.shape[0]
    rows = MIX_ROWS
    ntiles = n_tok // rows
    slab_spec = lambda k: pl.BlockSpec((rows, D_MODEL), lambda i, k=k: (k * ntiles + i, 0))
    return pl.pallas_call(
        _combine_kernel,
        grid=(ntiles,),
        in_specs=[pl.BlockSpec((rows, D_MODEL), lambda i: (i, 0))]
        + [slab_spec(k) for k in range(TOP_K)]
        + [pl.BlockSpec((rows, 8), lambda i: (i, 0)),
           pl.BlockSpec((1, D_MODEL), lambda i: (0, 0))],
        out_specs=pl.BlockSpec((rows, D_MODEL), lambda i: (i, 0)),
        out_shape=jax.ShapeDtypeStruct((n_tok, D_MODEL), F32),
        compiler_params=pltpu.CompilerParams(
            dimension_semantics=("arbitrary",), vmem_limit_bytes=VMEM_LIMIT),
        name="combine",
    )(x1, slabs, slabs, slabs, slabs, wgt, norm_g.reshape(1, D_MODEL))


def kernel(x, norm_mix_g, w_in, conv_dw_w, conv_dw_b, conv_ln_g, conv_ln_b, rel_bias, w_out,
           norm_ffn_g, router_w, router_b, exp_w_gate, exp_b_gate, exp_w_up, exp_b_up,
           exp_w_down, exp_b_down, norm_final_g):
    bsz, seq, _ = x.shape
    n_tok = bsz * seq
    assert norm_mix_g.shape[0] == 1, "single-layer block"
    assert seq % IN_ROWS == 0 and seq % MIX_ROWS == 0 and IN_ROWS == LEFT_PAD
    n_tiles = (TOP_K * n_tok) // EXP_ROWS + N_EXPERTS - 1

    q, kpad, vpad, hcv = _inproj(x, norm_mix_g[0], w_in[0])
    attn = _attention(q, kpad, vpad, _band_bias(rel_bias[0]))
    x1, t, idx, wgt, rank, cnt = _mix_route(
        x.reshape(n_tok, D_MODEL), attn.reshape(n_tok, D_ATTN), hcv.reshape(n_tok, D_CONV), seq,
        conv_dw_w[0], conv_dw_b[0], conv_ln_g[0], conv_ln_b[0], w_out[0], norm_ffn_g[0],
        router_w[0], router_b[0])
    tile_expert, tile_first, tile_valid, group_start = _tile_plan(cnt[:, 0].astype(I32), n_tiles)
    pos_flat = _slots(group_start, idx, rank).reshape(TOP_K * n_tok)
    sorted_t = _dispatch(t, pos_flat, n_tiles * EXP_ROWS)
    sorted_out = _experts(sorted_t, tile_expert, tile_first, tile_valid,
                          exp_w_gate[0], exp_b_gate[0], exp_w_up[0], exp_b_up[0],
                          exp_w_down[0], exp_b_down[0], n_tiles)
    slabs = _gather_back(sorted_out, pos_flat)
    out = _combine(x1, slabs, wgt, norm_final_g)
    return out.reshape(bsz, seq, D_MODEL)
```

```python
import functools

import jax
import jax.numpy as jnp
from jax import lax
from jax.experimental import pallas as pl
from jax.experimental.pallas import tpu as pltpu
from jax.experimental.pallas import tpu_sc as plsc

F32 = jnp.float32
BF16 = jnp.bfloat16
I32 = jnp.int32

D_MODEL = 1024
CHUNK = 64
N_HEADS = 8
HEAD_DIM = 64
D_ATTN = N_HEADS * HEAD_DIM
LEFT_CHUNKS = 8
REL_MAX = 128
REL_MIN = -(CHUNK - 1)
D_CONV = D_MODEL - D_ATTN
CONV_WIDTH = 31
N_EXPERTS = 32
TOP_K = 4
SWIGLU_ALPHA = 1.702
SWIGLU_LIMIT = 7.0
RMS_EPS = 1e-5
LN_EPS = 1e-5

LEFT_PAD = LEFT_CHUNKS * CHUNK
IN_ROWS = 512
Q_ROWS = 2 * CHUNK
BAND_ROWS = Q_ROWS + LEFT_PAD
HEADS_PER_STEP = 4
GROUP_LANES = HEADS_PER_STEP * HEAD_DIM
MIX_ROWS = 256
HALO_ROWS = 32
EXP_ROWS = 256
SC_ROWS = 64
NEG_BIG = -1e30
VMEM_LIMIT = 56 * 1024 * 1024


def _cast_rows(src_ref, dst_ref, rows, step=128):
    def body(c, carry):
        r = pl.multiple_of(c * step, step)
        dst_ref[pl.ds(r, step), :] = src_ref[pl.ds(r, step), :].astype(dst_ref.dtype)
        return carry
    lax.fori_loop(0, rows // step, body, 0)


def _inproj_kernel(x_ref, g_ref, w_ref, q_ref, k_ref, v_ref, h_ref, wbf_ref):
    b = pl.program_id(0)
    j = pl.program_id(1)

    @pl.when((b == 0) & (j == 0))
    def _():
        _cast_rows(w_ref, wbf_ref, D_MODEL)

    @pl.when(j == 0)
    def _():
        k_ref[...] = jnp.zeros_like(k_ref)
        v_ref[...] = jnp.zeros_like(v_ref)

    @pl.when(j > 0)
    def _():
        x = x_ref[...]
        ms = jnp.mean(x * x, axis=-1, keepdims=True)
        hb = ((x * lax.rsqrt(ms + RMS_EPS)) * g_ref[...]).astype(BF16)

        def proj(c0, width):
            return jnp.dot(hb, wbf_ref[:, c0:c0 + width], preferred_element_type=F32)

        q_ref[...] = (proj(0, D_ATTN) * (HEAD_DIM ** -0.5)).astype(BF16)
        k_ref[...] = proj(D_ATTN, D_ATTN).astype(BF16)
        v_ref[...] = proj(2 * D_ATTN, D_ATTN).astype(BF16)
        a = proj(3 * D_ATTN, D_CONV)
        gate = proj(3 * D_ATTN + D_CONV, D_CONV)
        h_ref[...] = a * jax.nn.sigmoid(gate)


def _inproj(x, norm_g, w_in):
    bsz, seq, _ = x.shape
    nblk = seq // IN_ROWS
    d_cols = w_in.shape[1]
    row_blk = lambda b, j: (b, jnp.maximum(j - 1, 0), 0)
    return pl.pallas_call(
        _inproj_kernel,
        grid=(bsz, nblk + 1),
        in_specs=[
            pl.BlockSpec((None, IN_ROWS, D_MODEL), row_blk),
            pl.BlockSpec((1, D_MODEL), lambda b, j: (0, 0)),
            pl.BlockSpec((D_MODEL, d_cols), lambda b, j: (0, 0)),
        ],
        out_specs=[
            pl.BlockSpec((None, IN_ROWS, D_ATTN), row_blk),
            pl.BlockSpec((None, IN_ROWS, D_ATTN), lambda b, j: (b, j, 0)),
            pl.BlockSpec((None, IN_ROWS, D_ATTN), lambda b, j: (b, j, 0)),
            pl.BlockSpec((None, IN_ROWS, D_CONV), row_blk),
        ],
        out_shape=[
            jax.ShapeDtypeStruct((bsz, seq, D_ATTN), BF16),
            jax.ShapeDtypeStruct((bsz, seq + LEFT_PAD, D_ATTN), BF16),
            jax.ShapeDtypeStruct((bsz, seq + LEFT_PAD, D_ATTN), BF16),
            jax.ShapeDtypeStruct((bsz, seq, D_CONV), F32),
        ],
        scratch_shapes=[pltpu.VMEM((D_MODEL, d_cols), BF16)],
        compiler_params=pltpu.CompilerParams(
            dimension_semantics=("arbitrary", "arbitrary"), vmem_limit_bytes=VMEM_LIMIT),
        name="inproj",
    )(x, norm_g.reshape(1, D_MODEL), w_in)


def _attn_kernel(q_ref, k_ref, v_ref, bias_ref, o_ref):
    i = pl.program_id(1)
    start = pl.multiple_of(i * Q_ROWS, Q_ROWS)
    lane = lax.broadcasted_iota(I32, (Q_ROWS, GROUP_LANES), 1) // HEAD_DIM
    col = lax.broadcasted_iota(I32, (HEADS_PER_STEP * Q_ROWS, BAND_ROWS), 1)
    key_ok = col >= LEFT_PAD - i * Q_ROWS
    for g in range(N_HEADS // HEADS_PER_STEP):
        lanes = slice(g * GROUP_LANES, (g + 1) * GROUP_LANES)
        q = q_ref[:, lanes]
        qs = jnp.concatenate(
            [jnp.where(lane == h, q, jnp.zeros_like(q)) for h in range(HEADS_PER_STEP)], axis=0)
        kb = k_ref[pl.ds(start, BAND_ROWS), lanes]
        vb = v_ref[pl.ds(start, BAND_ROWS), lanes]
        s = lax.dot_general(qs, kb, (((1,), (1,)), ((), ())), preferred_element_type=F32)
        rows = slice(g * HEADS_PER_STEP * Q_ROWS, (g + 1) * HEADS_PER_STEP * Q_ROWS)
        s = jnp.where(key_ok, s + bias_ref[rows, :], NEG_BIG)
        m = jnp.max(s, axis=-1, keepdims=True)
        p = jnp.exp(s - m)
        l = jnp.sum(p, axis=-1, keepdims=True)
        o = jnp.dot(p.astype(BF16), vb, preferred_element_type=F32) / l
        out = o[0:Q_ROWS]
        for h in range(1, HEADS_PER_STEP):
            out = jnp.where(lane == h, o[h * Q_ROWS:(h + 1) * Q_ROWS], out)
        o_ref[:, lanes] = out.astype(o_ref.dtype)


def _attention(q, kpad, vpad, bias):
    bsz, seq, _ = q.shape
    nq = seq // Q_ROWS
    return pl.pallas_call(
        _attn_kernel,
        grid=(bsz, nq),
        in_specs=[
            pl.BlockSpec((None, Q_ROWS, D_ATTN), lambda b, i: (b, i, 0)),
            pl.BlockSpec((None, seq + LEFT_PAD, D_ATTN), lambda b, i: (b, 0, 0)),
            pl.BlockSpec((None, seq + LEFT_PAD, D_ATTN), lambda b, i: (b, 0, 0)),
            pl.BlockSpec((N_HEADS * Q_ROWS, BAND_ROWS), lambda b, i: (0, 0)),
        ],
        out_specs=pl.BlockSpec((None, Q_ROWS, D_ATTN), lambda b, i: (b, i, 0)),
        out_shape=jax.ShapeDtypeStruct((bsz, seq, D_ATTN), BF16),
        compiler_params=pltpu.CompilerParams(
            dimension_semantics=("arbitrary", "arbitrary"), vmem_limit_bytes=VMEM_LIMIT),
        name="chunk_attn",
    )(q, kpad, vpad, bias)


def _band_bias(rel_bias):
    n_rel = REL_MAX - REL_MIN + 1
    far = jnp.broadcast_to(rel_bias[:, n_rel - 1:n_rel], (N_HEADS, BAND_ROWS - 1 - REL_MAX))
    near = rel_bias[:, ::-1]
    ahead = jnp.broadcast_to(rel_bias[:, 0:1], (N_HEADS, Q_ROWS - 1 + REL_MIN))
    diag = jnp.concatenate([far, near, ahead], axis=1).astype(F32)
    bias = jnp.stack(
        [diag[:, Q_ROWS - 1 - r:Q_ROWS - 1 - r + BAND_ROWS] for r in range(Q_ROWS)], axis=1)
    r = jnp.arange(Q_ROWS)[:, None]
    m = jnp.arange(BAND_ROWS)[None, :]
    cq = r // CHUNK
    ck = m // CHUNK
    in_band = (ck >= cq) & (ck <= cq + LEFT_CHUNKS)
    bias = jnp.where(in_band[None], bias, NEG_BIG)
    return bias.reshape(N_HEADS * Q_ROWS, BAND_ROWS)


def _mix_route_kernel(tiles_per_seq,
                      x_ref, a_ref, hc_ref, hh_ref, cw_ref, cb_ref, lg_ref, lb_ref, wo_ref,
                      ng_ref, rwt_ref, rb_ref,
                      x1_ref, t_ref, idx_ref, wgt_ref, rank_ref, cnt_ref,
                      wobf_ref, hw_ref, cntacc_ref):
    i = pl.program_id(0)

    @pl.when(i == 0)
    def _():
        _cast_rows(wo_ref, wobf_ref, D_MODEL)
        cntacc_ref[...] = jnp.zeros_like(cntacc_ref)

    seq_start = (i % tiles_per_seq) == 0
    hw_ref[0:HALO_ROWS, :] = jnp.where(seq_start, 0.0, hh_ref[...])
    hw_ref[HALO_ROWS:HALO_ROWS + MIX_ROWS, :] = hc_ref[...]
    off = HALO_ROWS - (CONV_WIDTH - 1)
    acc = hw_ref[pl.ds(off, MIX_ROWS), :] * cw_ref[0:1, :]
    for j in range(1, CONV_WIDTH):
        acc = acc + hw_ref[pl.ds(off + j, MIX_ROWS), :] * cw_ref[j:j + 1, :]
    acc = acc + cb_ref[...]
    mu = jnp.mean(acc, axis=-1, keepdims=True)
    d = acc - mu
    var = jnp.mean(d * d, axis=-1, keepdims=True)
    y = d * lax.rsqrt(var + LN_EPS) * lg_ref[...] + lb_ref[...]
    conv_out = (y * jax.nn.sigmoid(y)).astype(BF16)

    mixed = jnp.dot(a_ref[...], wobf_ref[0:D_ATTN, :], preferred_element_type=F32)
    mixed = mixed + jnp.dot(conv_out, wobf_ref[D_ATTN:D_MODEL, :], preferred_element_type=F32)
    x1 = x_ref[...] + mixed
    x1_ref[...] = x1
    ms = jnp.mean(x1 * x1, axis=-1, keepdims=True)
    t = (x1 * lax.rsqrt(ms + RMS_EPS)) * ng_ref[...]
    t_ref[...] = t

    logits = lax.dot_general(rwt_ref[...], t, (((1,), (1,)), ((), ())),
                             precision=lax.Precision.HIGHEST,
                             preferred_element_type=F32) + rb_ref[...]
    e_iota = lax.broadcasted_iota(I32, logits.shape, 0)
    vals, idxs, hots = [], [], []
    for _ in range(TOP_K):
        m = jnp.max(logits, axis=0, keepdims=True)
        am = jnp.min(jnp.where(logits == m, e_iota, N_EXPERTS), axis=0, keepdims=True)
        hot = e_iota == am
        vals.append(m)
        idxs.append(am)
        hots.append(hot)
        logits = jnp.where(hot, -jnp.inf, logits)
    exps = [jnp.exp(v - vals[0]) for v in vals]
    den = exps[0] + exps[1] + exps[2] + exps[3]
    wts = [e / den for e in exps]

    hot_sum = (hots[0] | hots[1] | hots[2] | hots[3])
    hot_f = hot_sum.astype(F32)
    ra = lax.broadcasted_iota(I32, (MIX_ROWS, MIX_ROWS), 0)
    rc = lax.broadcasted_iota(I32, (MIX_ROWS, MIX_ROWS), 1)
    upper = (ra < rc).astype(BF16)
    prefix = jnp.dot(hot_f.astype(BF16), upper, preferred_element_type=F32)
    base = prefix + cntacc_ref[...]
    ranks = [jnp.sum(jnp.where(h, base, 0.0), axis=0, keepdims=True) for h in hots]
    cntacc_ref[...] = cntacc_ref[...] + jnp.sum(hot_f, axis=1, keepdims=True)

    idx_ref[...] = jnp.concatenate(idxs, axis=0)
    rank_ref[...] = jnp.concatenate(ranks, axis=0).astype(I32)
    cnt_ref[...] = jnp.broadcast_to(cntacc_ref[...], cnt_ref.shape)
    w8 = jnp.concatenate(wts + [jnp.zeros((8 - TOP_K, MIX_ROWS), F32)], axis=0)
    wgt_ref[...] = w8.T


def _mix_route(x2, attn2, hcv2, seq, conv_w, conv_b, ln_g, ln_b, w_out, norm_g, router_w, router_b):
    n_tok = x2.shape[0]
    ntiles = n_tok // MIX_ROWS
    halo_per_tile = MIX_ROWS // HALO_ROWS
    row = lambda i: (i, 0)
    const = lambda i: (0, 0)
    vec = lambda v: v.reshape(1, -1)
    return pl.pallas_call(
        functools.partial(_mix_route_kernel, seq // MIX_ROWS),
        grid=(ntiles,),
        in_specs=[
            pl.BlockSpec((MIX_ROWS, D_MODEL), row),
            pl.BlockSpec((MIX_ROWS, D_ATTN), row),
            pl.BlockSpec((MIX_ROWS, D_CONV), row),
            pl.BlockSpec((HALO_ROWS, D_CONV), lambda i: (jnp.maximum(i * halo_per_tile - 1, 0), 0)),
            pl.BlockSpec((CONV_WIDTH, D_CONV), const),
            pl.BlockSpec((1, D_CONV), const),
            pl.BlockSpec((1, D_CONV), const),
            pl.BlockSpec((1, D_CONV), const),
            pl.BlockSpec((D_MODEL, D_MODEL), const),
            pl.BlockSpec((1, D_MODEL), const),
            pl.BlockSpec((N_EXPERTS, D_MODEL), const),
            pl.BlockSpec((N_EXPERTS, 1), const),
        ],
        out_specs=[
            pl.BlockSpec((MIX_ROWS, D_MODEL), row),
            pl.BlockSpec((MIX_ROWS, D_MODEL), row),
            pl.BlockSpec((TOP_K, MIX_ROWS), lambda i: (0, i)),
            pl.BlockSpec((MIX_ROWS, 8), row),
            pl.BlockSpec((TOP_K, MIX_ROWS), lambda i: (0, i)),
            pl.BlockSpec((N_EXPERTS, 128), const),
        ],
        out_shape=[
            jax.ShapeDtypeStruct((n_tok, D_MODEL), F32),
            jax.ShapeDtypeStruct((n_tok, D_MODEL), F32),
            jax.ShapeDtypeStruct((TOP_K, n_tok), I32),
            jax.ShapeDtypeStruct((n_tok, 8), F32),
            jax.ShapeDtypeStruct((TOP_K, n_tok), I32),
            jax.ShapeDtypeStruct((N_EXPERTS, 128), F32),
        ],
        scratch_shapes=[
            pltpu.VMEM((D_MODEL, D_MODEL), BF16),
            pltpu.VMEM((HALO_ROWS + MIX_ROWS, D_CONV), F32),
            pltpu.VMEM((N_EXPERTS, 1), F32),
        ],
        compiler_params=pltpu.CompilerParams(
            dimension_semantics=("arbitrary",), vmem_limit_bytes=VMEM_LIMIT),
        name="mix_route",
    )(x2, attn2, hcv2, hcv2, conv_w, vec(conv_b), vec(ln_g), vec(ln_b), w_out, vec(norm_g),
      router_w.T, router_b.reshape(N_EXPERTS, 1))


def _tile_plan(counts, n_tiles):
    tiles_per = (counts + EXP_ROWS - 1) // EXP_ROWS
    tile_end = jnp.cumsum(tiles_per)
    tile_begin = tile_end - tiles_per
    n_valid = tile_end[-1]
    tiles = jnp.arange(n_tiles, dtype=I32)
    tile_valid = tiles < n_valid
    capped = jnp.minimum(tiles, n_valid - 1)
    tile_expert = jnp.sum((capped[:, None] >= tile_end[None, :]).astype(I32), axis=1)
    tile_expert = jnp.minimum(tile_expert, N_EXPERTS - 1)
    is_first = (tiles[:, None] == tile_begin[None, :]) & (tiles_per[None, :] > 0)
    tile_first = tile_valid & jnp.any(is_first, axis=1)
    group_start = tile_begin * EXP_ROWS
    return tile_expert, tile_first.astype(I32), tile_valid.astype(I32), group_start.astype(I32)


def _slot_kernel(gstart_ref, idx_ref, rank_ref, pos_ref):
    idx = idx_ref[...]
    pos = rank_ref[...]
    for e in range(N_EXPERTS):
        pos = pos + jnp.where(idx == e, gstart_ref[e], 0)
    pos_ref[...] = pos


def _slots(group_start, idx, rank):
    full = pl.BlockSpec(idx.shape, lambda i, gs: (0, 0))
    return pl.pallas_call(
        _slot_kernel,
        grid_spec=pltpu.PrefetchScalarGridSpec(
            num_scalar_prefetch=1, grid=(1,), in_specs=[full, full], out_specs=full),
        out_shape=jax.ShapeDtypeStruct(idx.shape, I32),
        name="slots",
    )(group_start, idx, rank)


def _sc_mesh():
    return plsc.VectorSubcoreMesh(core_axis_name="core", subcore_axis_name="subcore")


def _sc_worker():
    info = plsc.get_sparse_core_info()
    wid = lax.axis_index("subcore") * info.num_cores + lax.axis_index("core")
    return wid, info.num_cores * info.num_subcores


def _dispatch(t2, pos_flat, n_slots):
    n_tok, d = t2.shape
    n_workers = 32
    per_w = n_tok // n_workers
    assert per_w % SC_ROWS == 0

    @functools.partial(
        pl.kernel, mesh=_sc_mesh(),
        out_type=jax.ShapeDtypeStruct((n_slots, d), t2.dtype),
        scratch_types=[pltpu.VMEM((SC_ROWS,), I32), pltpu.VMEM((SC_ROWS, d), t2.dtype)],
        name="dispatch",
    )
    def k(t_hbm, pos_hbm, o_hbm, idx_v, rows_v):
        wid, nw = _sc_worker()
        assert nw == n_workers

        @pl.loop(0, per_w // SC_ROWS)
        def _(c):
            base = pl.multiple_of(wid * per_w + c * SC_ROWS, SC_ROWS)
            pltpu.sync_copy(t_hbm.at[pl.ds(base, SC_ROWS)], rows_v)
            for kk in range(TOP_K):
                pltpu.sync_copy(pos_hbm.at[pl.ds(kk * n_tok + base, SC_ROWS)], idx_v)
                pltpu.sync_copy(rows_v, o_hbm.at[idx_v])

    return k(t2, pos_flat)


def _gather_back(sorted_out, pos_flat):
    n_rows = pos_flat.shape[0]
    d = sorted_out.shape[1]
    n_workers = 32
    per_w = n_rows // n_workers
    assert per_w % SC_ROWS == 0

    @functools.partial(
        pl.kernel, mesh=_sc_mesh(),
        out_type=jax.ShapeDtypeStruct((n_rows, d), sorted_out.dtype),
        scratch_types=[pltpu.VMEM((SC_ROWS,), I32), pltpu.VMEM((SC_ROWS, d), sorted_out.dtype)],
        name="gather_back",
    )
    def k(s_hbm, pos_hbm, o_hbm, idx_v, rows_v):
        wid, nw = _sc_worker()
        assert nw == n_workers

        @pl.loop(0, per_w // SC_ROWS)
        def _(c):
            base = pl.multiple_of(wid * per_w + c * SC_ROWS, SC_ROWS)
            pltpu.sync_copy(pos_hbm.at[pl.ds(base, SC_ROWS)], idx_v)
            pltpu.sync_copy(s_hbm.at[idx_v], rows_v)
            pltpu.sync_copy(rows_v, o_hbm.at[pl.ds(base, SC_ROWS)])

    return k(sorted_out, pos_flat)


def _expert_kernel(texp_ref, tfirst_ref, tvalid_ref,
                   x_ref, wg_ref, bg_ref, wu_ref, bu_ref, wd_ref, bd_ref,
                   o_ref, wbf):
    i = pl.program_id(0)

    @pl.when(tvalid_ref[i] == 1)
    def _():
        @pl.when(tfirst_ref[i] == 1)
        def _():
            for mtx, w_ref in enumerate((wg_ref, wu_ref, wd_ref)):
                _cast_rows(w_ref, wbf.at[mtx], D_MODEL)

        xs = x_ref[...].astype(BF16)
        acc = None
        cn = 256
        for c in range(D_MODEL // cn):
            sl = slice(c * cn, (c + 1) * cn)
            g = jnp.dot(xs, wbf[0, :, sl], preferred_element_type=F32) + bg_ref[:, sl]
            u = jnp.dot(xs, wbf[1, :, sl], preferred_element_type=F32) + bu_ref[:, sl]
            g = jnp.minimum(g, SWIGLU_LIMIT)
            u = jnp.clip(u, -SWIGLU_LIMIT, SWIGLU_LIMIT)
            hid = ((u + 1.0) * (g * jax.nn.sigmoid(SWIGLU_ALPHA * g))).astype(BF16)
            part = jnp.dot(hid, wbf[2, sl, :], preferred_element_type=F32)
            acc = part if acc is None else acc + part
        o_ref[...] = acc + bd_ref[...]


def _experts(sorted_t, tile_expert, tile_first, tile_valid,
             w_gate, b_gate, w_up, b_up, w_down, b_down, n_tiles):
    row_blk = pl.BlockSpec((EXP_ROWS, D_MODEL), lambda i, te, tf, tv: (i, 0))
    w_blk = pl.BlockSpec((None, D_MODEL, D_MODEL), lambda i, te, tf, tv: (te[i], 0, 0))
    b_blk = pl.BlockSpec((None, 1, D_MODEL), lambda i, te, tf, tv: (te[i], 0, 0))
    grid_spec = pltpu.PrefetchScalarGridSpec(
        num_scalar_prefetch=3,
        grid=(n_tiles,),
        in_specs=[row_blk, w_blk, b_blk, w_blk, b_blk, w_blk, b_blk],
        out_specs=row_blk,
        scratch_shapes=[pltpu.VMEM((3, D_MODEL, D_MODEL), BF16)],
    )
    b3 = lambda b: b.reshape(N_EXPERTS, 1, D_MODEL)
    return pl.pallas_call(
        _expert_kernel,
        grid_spec=grid_spec,
        out_shape=jax.ShapeDtypeStruct((n_tiles * EXP_ROWS, D_MODEL), F32),
        compiler_params=pltpu.CompilerParams(
            dimension_semantics=("arbitrary",), vmem_limit_bytes=VMEM_LIMIT),
        name="experts",
    )(tile_expert, tile_first, tile_valid, sorted_t,
      w_gate, b3(b_gate), w_up, b3(b_up), w_down, b3(b_down))


def _combine_kernel(x1_ref, s0_ref, s1_ref, s2_ref, s3_ref, w_ref, g_ref, o_ref):
    w = w_ref[...]
    y = x1_ref[...]
    for k, s_ref in enumerate((s0_ref, s1_ref, s2_ref, s3_ref)):
        y = y + w[:, k:k + 1] * s_ref[...]
    ms = jnp.mean(y * y, axis=-1, keepdims=True)
    o_ref[...] = (y * lax.rsqrt(ms + RMS_EPS)) * g_ref[...]


def _combine(x1, slabs, wgt, norm_g):
    n_tok = x1.shape[0]
    rows = MIX_ROWS
    ntiles = n_tok // rows
    slab_spec = lambda k: pl.BlockSpec((rows, D_MODEL), lambda i, k=k: (k * ntiles + i, 0))
    return pl.pallas_call(
        _combine_kernel,
        grid=(ntiles,),
        in_specs=[pl.BlockSpec((rows, D_MODEL), lambda i: (i, 0))]
        + [slab_spec(k) for k in range(TOP_K)]
        + [pl.BlockSpec((rows, 8), lambda i: (i, 0)),
           pl.BlockSpec((1, D_MODEL), lambda i: (0, 0))],
        out_specs=pl.BlockSpec((rows, D_MODEL), lambda i: (i, 0)),
        out_shape=jax.ShapeDtypeStruct((n_tok, D_MODEL), F32),
        compiler_params=pltpu.CompilerParams(
            dimension_semantics=("arbitrary",), vmem_limit_bytes=VMEM_LIMIT),
        name="combine",
    )(x1, slabs, slabs, slabs, slabs, wgt, norm_g.reshape(1, D_MODEL))


def kernel(x, norm_mix_g, w_in, conv_dw_w, conv_dw_b, conv_ln_g, conv_ln_b, rel_bias, w_out,
           norm_ffn_g, router_w, router_b, exp_w_gate, exp_b_gate, exp_w_up, exp_b_up,
           exp_w_down, exp_b_down, norm_final_g):
    bsz, seq, _ = x.shape
    n_tok = bsz * seq
    assert norm_mix_g.shape[0] == 1, "single-layer block"
    assert seq % IN_ROWS == 0 and seq % MIX_ROWS == 0 and IN_ROWS == LEFT_PAD
    n_tiles = (TOP_K * n_tok) // EXP_ROWS + N_EXPERTS - 1

    q, kpad, vpad, hcv = _inproj(x, norm_mix_g[0], w_in[0])
    attn = _attention(q, kpad, vpad, _band_bias(rel_bias[0]))
    x1, t, idx, wgt, rank, cnt = _mix_route(
        x.reshape(n_tok, D_MODEL), attn.reshape(n_tok, D_ATTN), hcv.reshape(n_tok, D_CONV), seq,
        conv_dw_w[0], conv_dw_b[0], conv_ln_g[0], conv_ln_b[0], w_out[0], norm_ffn_g[0],
        router_w[0], router_b[0])
    tile_expert, tile_first, tile_valid, group_start = _tile_plan(cnt[:, 0].astype(I32), n_tiles)
    pos_flat = _slots(group_start, idx, rank).reshape(TOP_K * n_tok)
    sorted_t = _dispatch(t, pos_flat, n_tiles * EXP_ROWS)
    sorted_out = _experts(sorted_t, tile_expert, tile_first, tile_valid,
                          exp_w_gate[0], exp_b_gate[0], exp_w_up[0], exp_b_up[0],
                          exp_w_down[0], exp_b_down[0], n_tiles)
    slabs = _gather_back(sorted_out, pos_flat)
    out = _combine(x1, slabs, wgt, norm_final_g)
    return out.reshape(bsz, seq, D_MODEL)
```

```python
import functools

import jax
import jax.numpy as jnp
from jax import lax
from jax.experimental import pallas as pl
from jax.experimental.pallas import tpu as pltpu
from jax.experimental.pallas import tpu_sc as plsc

F32 = jnp.float32
BF16 = jnp.bfloat16
I32 = jnp.int32

D_MODEL = 1024
CHUNK = 64
N_HEADS = 8
HEAD_DIM = 64
D_ATTN = N_HEADS * HEAD_DIM
LEFT_CHUNKS = 8
REL_MAX = 128
REL_MIN = -(CHUNK - 1)
D_CONV = D_MODEL - D_ATTN
CONV_WIDTH = 31
N_EXPERTS = 32
TOP_K = 4
SWIGLU_ALPHA = 1.702
SWIGLU_LIMIT = 7.0
RMS_EPS = 1e-5
LN_EPS = 1e-5

LEFT_PAD = LEFT_CHUNKS * CHUNK
IN_ROWS = 512
Q_ROWS = 2 * CHUNK
BAND_ROWS = Q_ROWS + LEFT_PAD
HEADS_PER_STEP = 4
GROUP_LANES = HEADS_PER_STEP * HEAD_DIM
MIX_ROWS = 256
HALO_ROWS = 32
SHIFT_ROWS = HALO_ROWS + MIX_ROWS - 8
EXP_ROWS = 256
SC_ROWS = 64
NEG_BIG = -1e30
VMEM_LIMIT = 56 * 1024 * 1024


def _cast_rows(src_ref, dst_ref, rows, step=128):
    def body(c, carry):
        r = pl.multiple_of(c * step, step)
        dst_ref[pl.ds(r, step), :] = src_ref[pl.ds(r, step), :].astype(dst_ref.dtype)
        return carry
    lax.fori_loop(0, rows // step, body, 0)


def _inproj_kernel(x_ref, g_ref, w_ref, q_ref, k_ref, v_ref, h_ref, wbf_ref):
    b = pl.program_id(0)
    j = pl.program_id(1)

    @pl.when((b == 0) & (j == 0))
    def _():
        _cast_rows(w_ref, wbf_ref, D_MODEL)

    @pl.when(j == 0)
    def _():
        k_ref[...] = jnp.zeros_like(k_ref)
        v_ref[...] = jnp.zeros_like(v_ref)

    @pl.when(j > 0)
    def _():
        x = x_ref[...]
        ms = jnp.mean(x * x, axis=-1, keepdims=True)
        hb = ((x * lax.rsqrt(ms + RMS_EPS)) * g_ref[...]).astype(BF16)

        def proj(c0, width):
            return jnp.dot(hb, wbf_ref[:, c0:c0 + width], preferred_element_type=F32)

        q_ref[...] = (proj(0, D_ATTN) * (HEAD_DIM ** -0.5)).astype(BF16)
        k_ref[...] = proj(D_ATTN, D_ATTN).astype(BF16)
        v_ref[...] = proj(2 * D_ATTN, D_ATTN).astype(BF16)
        a = proj(3 * D_ATTN, D_CONV)
        gate = proj(3 * D_ATTN + D_CONV, D_CONV)
        h_ref[...] = a * jax.nn.sigmoid(gate)


def _inproj(x, norm_g, w_in):
    bsz, seq, _ = x.shape
    nblk = seq // IN_ROWS
    d_cols = w_in.shape[1]
    row_blk = lambda b, j: (b, jnp.maximum(j - 1, 0), 0)
    return pl.pallas_call(
        _inproj_kernel,
        grid=(bsz, nblk + 1),
        in_specs=[
            pl.BlockSpec((None, IN_ROWS, D_MODEL), row_blk),
            pl.BlockSpec((1, D_MODEL), lambda b, j: (0, 0)),
            pl.BlockSpec((D_MODEL, d_cols), lambda b, j: (0, 0)),
        ],
        out_specs=[
            pl.BlockSpec((None, IN_ROWS, D_ATTN), row_blk),
            pl.BlockSpec((None, IN_ROWS, D_ATTN), lambda b, j: (b, j, 0)),
            pl.BlockSpec((None, IN_ROWS, D_ATTN), lambda b, j: (b, j, 0)),
            pl.BlockSpec((None, IN_ROWS, D_CONV), row_blk),
        ],
        out_shape=[
            jax.ShapeDtypeStruct((bsz, seq, D_ATTN), BF16),
            jax.ShapeDtypeStruct((bsz, seq + LEFT_PAD, D_ATTN), BF16),
            jax.ShapeDtypeStruct((bsz, seq + LEFT_PAD, D_ATTN), BF16),
            jax.ShapeDtypeStruct((bsz, seq, D_CONV), F32),
        ],
        scratch_shapes=[pltpu.VMEM((D_MODEL, d_cols), BF16)],
        compiler_params=pltpu.CompilerParams(
            dimension_semantics=("arbitrary", "arbitrary"), vmem_limit_bytes=VMEM_LIMIT),
        name="inproj",
    )(x, norm_g.reshape(1, D_MODEL), w_in)


def _attn_kernel(q_ref, k_ref, v_ref, bias_ref, o_ref):
    i = pl.program_id(1)
    start = pl.multiple_of(i * Q_ROWS, Q_ROWS)
    lane = lax.broadcasted_iota(I32, (Q_ROWS, GROUP_LANES), 1) // HEAD_DIM
    col = lax.broadcasted_iota(I32, (HEADS_PER_STEP * Q_ROWS, BAND_ROWS), 1)
    key_ok = col >= LEFT_PAD - i * Q_ROWS
    for g in range(N_HEADS // HEADS_PER_STEP):
        lanes = slice(g * GROUP_LANES, (g + 1) * GROUP_LANES)
        q = q_ref[:, lanes]
        qs = jnp.concatenate(
            [jnp.where(lane == h, q, jnp.zeros_like(q)) for h in range(HEADS_PER_STEP)], axis=0)
        kb = k_ref[pl.ds(start, BAND_ROWS), lanes]
        vb = v_ref[pl.ds(start, BAND_ROWS), lanes]
        s = lax.dot_general(qs, kb, (((1,), (1,)), ((), ())), preferred_element_type=F32)
        rows = slice(g * HEADS_PER_STEP * Q_ROWS, (g + 1) * HEADS_PER_STEP * Q_ROWS)
        s = jnp.where(key_ok, s + bias_ref[rows, :], NEG_BIG)
        m = jnp.max(s, axis=-1, keepdims=True)
        p = jnp.exp(s - m)
        l = jnp.sum(p, axis=-1, keepdims=True)
        o = jnp.dot(p.astype(BF16), vb, preferred_element_type=F32) / l
        out = o[0:Q_ROWS]
        for h in range(1, HEADS_PER_STEP):
            out = jnp.where(lane == h, o[h * Q_ROWS:(h + 1) * Q_ROWS], out)
        o_ref[:, lanes] = out.astype(o_ref.dtype)


def _attention(q, kpad, vpad, bias):
    bsz, seq, _ = q.shape
    nq = seq // Q_ROWS
    return pl.pallas_call(
        _attn_kernel,
        grid=(bsz, nq),
        in_specs=[
            pl.BlockSpec((None, Q_ROWS, D_ATTN), lambda b, i: (b, i, 0)),
            pl.BlockSpec((None, seq + LEFT_PAD, D_ATTN), lambda b, i: (b, 0, 0)),
            pl.BlockSpec((None, seq + LEFT_PAD, D_ATTN), lambda b, i: (b, 0, 0)),
            pl.BlockSpec((N_HEADS * Q_ROWS, BAND_ROWS), lambda b, i: (0, 0)),
        ],
        out_specs=pl.BlockSpec((None, Q_ROWS, D_ATTN), lambda b, i: (b, i, 0)),
        out_shape=jax.ShapeDtypeStruct((bsz, seq, D_ATTN), BF16),
        compiler_params=pltpu.CompilerParams(
            dimension_semantics=("arbitrary", "arbitrary"), vmem_limit_bytes=VMEM_LIMIT),
        name="chunk_attn",
    )(q, kpad, vpad, bias)


def _band_bias(rel_bias):
    n_rel = REL_MAX - REL_MIN + 1
    far = jnp.broadcast_to(rel_bias[:, n_rel - 1:n_rel], (N_HEADS, BAND_ROWS - 1 - REL_MAX))
    near = rel_bias[:, ::-1]
    ahead = jnp.broadcast_to(rel_bias[:, 0:1], (N_HEADS, Q_ROWS - 1 + REL_MIN))
    diag = jnp.concatenate([far, near, ahead], axis=1).astype(F32)
    bias = jnp.stack(
        [diag[:, Q_ROWS - 1 - r:Q_ROWS - 1 - r + BAND_ROWS] for r in range(Q_ROWS)], axis=1)
    r = jnp.arange(Q_ROWS)[:, None]
    m = jnp.arange(BAND_ROWS)[None, :]
    cq = r // CHUNK
    ck = m // CHUNK
    in_band = (ck >= cq) & (ck <= cq + LEFT_CHUNKS)
    bias = jnp.where(in_band[None], bias, NEG_BIG)
    return bias.reshape(N_HEADS * Q_ROWS, BAND_ROWS)


def _mix_route_kernel(tiles_per_seq,
                      x_ref, a_ref, hc_ref, hh_ref, cw_ref, cb_ref, lg_ref, lb_ref, wo_ref,
                      ng_ref, rwt_ref, rb_ref,
                      x1_ref, t_ref, idx_ref, wgt_ref, rank_ref, cnt_ref,
                      wobf_ref, hw_ref, sh_ref, cntacc_ref):
    i = pl.program_id(0)

    @pl.when(i == 0)
    def _():
        _cast_rows(wo_ref, wobf_ref, D_MODEL)
        cntacc_ref[...] = jnp.zeros_like(cntacc_ref)

    seq_start = (i % tiles_per_seq) == 0
    hw_ref[0:HALO_ROWS, :] = jnp.where(seq_start, 0.0, hh_ref[...])
    hw_ref[HALO_ROWS:HALO_ROWS + MIX_ROWS, :] = hc_ref[...]
    off = HALO_ROWS - (CONV_WIDTH - 1)
    for b in range(1, 8):
        sh_ref[b - 1] = hw_ref[pl.ds(b, SHIFT_ROWS), :]
    acc = None
    for j in range(CONV_WIDTH):
        a, b = divmod(off + j, 8)
        src = hw_ref if b == 0 else sh_ref.at[b - 1]
        term = src[pl.ds(8 * a, MIX_ROWS), :] * cw_ref[j:j + 1, :]
        acc = term if acc is None else acc + term
    acc = acc + cb_ref[...]
    mu = jnp.mean(acc, axis=-1, keepdims=True)
    d = acc - mu
    var = jnp.mean(d * d, axis=-1, keepdims=True)
    y = d * lax.rsqrt(var + LN_EPS) * lg_ref[...] + lb_ref[...]
    conv_out = (y * jax.nn.sigmoid(y)).astype(BF16)

    mixed = jnp.dot(a_ref[...], wobf_ref[0:D_ATTN, :], preferred_element_type=F32)
    mixed = mixed + jnp.dot(conv_out, wobf_ref[D_ATTN:D_MODEL, :], preferred_element_type=F32)
    x1 = x_ref[...] + mixed
    x1_ref[...] = x1
    ms = jnp.mean(x1 * x1, axis=-1, keepdims=True)
    t = (x1 * lax.rsqrt(ms + RMS_EPS)) * ng_ref[...]
    t_ref[...] = t

    logits = lax.dot_general(rwt_ref[...], t, (((1,), (1,)), ((), ())),
                             precision=lax.Precision.HIGHEST,
                             preferred_element_type=F32) + rb_ref[...]
    e_iota = lax.broadcasted_iota(I32, logits.shape, 0)
    vals, idxs, hots = [], [], []
    for _ in range(TOP_K):
        m = jnp.max(logits, axis=0, keepdims=True)
        am = jnp.min(jnp.where(logits == m, e_iota, N_EXPERTS), axis=0, keepdims=True)
        hot = e_iota == am
        vals.append(m)
        idxs.append(am)
        hots.append(hot)
        logits = jnp.where(hot, -jnp.inf, logits)
    exps = [jnp.exp(v - vals[0]) for v in vals]
    den = exps[0] + exps[1] + exps[2] + exps[3]
    wts = [e / den for e in exps]

    hot_sum = (hots[0] | hots[1] | hots[2] | hots[3])
    hot_f = hot_sum.astype(F32)
    ra = lax.broadcasted_iota(I32, (MIX_ROWS, MIX_ROWS), 0)
    rc = lax.broadcasted_iota(I32, (MIX_ROWS, MIX_ROWS), 1)
    upper = (ra < rc).astype(BF16)
    prefix = jnp.dot(hot_f.astype(BF16), upper, preferred_element_type=F32)
    base = prefix + cntacc_ref[...]
    ranks = [jnp.sum(jnp.where(h, base, 0.0), axis=0, keepdims=True) for h in hots]
    cntacc_ref[...] = cntacc_ref[...] + jnp.sum(hot_f, axis=1, keepdims=True)

    idx_ref[...] = jnp.concatenate(idxs, axis=0)
    rank_ref[...] = jnp.concatenate(ranks, axis=0).astype(I32)
    cnt_ref[...] = jnp.broadcast_to(cntacc_ref[...], cnt_ref.shape)
    w8 = jnp.concatenate(wts + [jnp.zeros((8 - TOP_K, MIX_ROWS), F32)], axis=0)
    wgt_ref[...] = w8.T


def _mix_route(x2, attn2, hcv2, seq, conv_w, conv_b, ln_g, ln_b, w_out, norm_g, router_w, router_b):
    n_tok = x2.shape[0]
    ntiles = n_tok // MIX_ROWS
    halo_per_tile = MIX_ROWS // HALO_ROWS
    row = lambda i: (i, 0)
    const = lambda i: (0, 0)
    vec = lambda v: v.reshape(1, -1)
    return pl.pallas_call(
        functools.partial(_mix_route_kernel, seq // MIX_ROWS),
        grid=(ntiles,),
        in_specs=[
            pl.BlockSpec((MIX_ROWS, D_MODEL), row),
            pl.BlockSpec((MIX_ROWS, D_ATTN), row),
            pl.BlockSpec((MIX_ROWS, D_CONV), row),
            pl.BlockSpec((HALO_ROWS, D_CONV), lambda i: (jnp.maximum(i * halo_per_tile - 1, 0), 0)),
            pl.BlockSpec((CONV_WIDTH, D_CONV), const),
            pl.BlockSpec((1, D_CONV), const),
            pl.BlockSpec((1, D_CONV), const),
            pl.BlockSpec((1, D_CONV), const),
            pl.BlockSpec((D_MODEL, D_MODEL), const),
            pl.BlockSpec((1, D_MODEL), const),
            pl.BlockSpec((N_EXPERTS, D_MODEL), const),
            pl.BlockSpec((N_EXPERTS, 1), const),
        ],
        out_specs=[
            pl.BlockSpec((MIX_ROWS, D_MODEL), row),
            pl.BlockSpec((MIX_ROWS, D_MODEL), row),
            pl.BlockSpec((TOP_K, MIX_ROWS), lambda i: (0, i)),
            pl.BlockSpec((MIX_ROWS, 8), row),
            pl.BlockSpec((TOP_K, MIX_ROWS), lambda i: (0, i)),
            pl.BlockSpec((N_EXPERTS, 128), const),
        ],
        out_shape=[
            jax.ShapeDtypeStruct((n_tok, D_MODEL), F32),
            jax.ShapeDtypeStruct((n_tok, D_MODEL), F32),
            jax.ShapeDtypeStruct((TOP_K, n_tok), I32),
            jax.ShapeDtypeStruct((n_tok, 8), F32),
            jax.ShapeDtypeStruct((TOP_K, n_tok), I32),
            jax.ShapeDtypeStruct((N_EXPERTS, 128), F32),
        ],
        scratch_shapes=[
            pltpu.VMEM((D_MODEL, D_MODEL), BF16),
            pltpu.VMEM((HALO_ROWS + MIX_ROWS, D_CONV), F32),
            pltpu.VMEM((7, SHIFT_ROWS, D_CONV), F32),
            pltpu.VMEM((N_EXPERTS, 1), F32),
        ],
        compiler_params=pltpu.CompilerParams(
            dimension_semantics=("arbitrary",), vmem_limit_bytes=VMEM_LIMIT),
        name="mix_route",
    )(x2, attn2, hcv2, hcv2, conv_w, vec(conv_b), vec(ln_g), vec(ln_b), w_out, vec(norm_g),
      router_w.T, router_b.reshape(N_EXPERTS, 1))


def _tile_plan(counts, n_tiles):
    tiles_per = (counts + EXP_ROWS - 1) // EXP_ROWS
    tile_end = jnp.cumsum(tiles_per)
    tile_begin = tile_end - tiles_per
    n_valid = tile_end[-1]
    tiles = jnp.arange(n_tiles, dtype=I32)
    tile_valid = tiles < n_valid
    capped = jnp.minimum(tiles, n_valid - 1)
    tile_expert = jnp.sum((capped[:, None] >= tile_end[None, :]).astype(I32), axis=1)
    tile_expert = jnp.minimum(tile_expert, N_EXPERTS - 1)
    is_first = (tiles[:, None] == tile_begin[None, :]) & (tiles_per[None, :] > 0)
    tile_first = tile_valid & jnp.any(is_first, axis=1)
    group_start = tile_begin * EXP_ROWS
    experts = jnp.arange(N_EXPERTS, dtype=I32)
    nonempty = tiles_per > 0
    parity = (jnp.cumsum(nonempty.astype(I32)) - 1) % 2
    later = nonempty[None, :] & (experts[None, :] > experts[:, None])
    nxt = jnp.min(jnp.where(later, experts[None, :], N_EXPERTS), axis=1)
    nxt = jnp.where(nxt == N_EXPERTS, -1, nxt)
    hot = (tile_expert[:, None] == experts[None, :]).astype(I32)
    tile_slot = jnp.sum(hot * parity[None, :], axis=1)
    tile_next = jnp.sum(hot * nxt[None, :], axis=1)
    flags = (tile_expert, tile_first.astype(I32), tile_valid.astype(I32),
             tile_slot.astype(I32), tile_next.astype(I32))
    return flags, group_start.astype(I32)


def _slot_kernel(gstart_ref, idx_ref, rank_ref, pos_ref):
    idx = idx_ref[...]
    pos = rank_ref[...]
    for e in range(N_EXPERTS):
        pos = pos + jnp.where(idx == e, gstart_ref[e], 0)
    pos_ref[...] = pos


def _slots(group_start, idx, rank):
    full = pl.BlockSpec(idx.shape, lambda i, gs: (0, 0))
    return pl.pallas_call(
        _slot_kernel,
        grid_spec=pltpu.PrefetchScalarGridSpec(
            num_scalar_prefetch=1, grid=(1,), in_specs=[full, full], out_specs=full),
        out_shape=jax.ShapeDtypeStruct(idx.shape, I32),
        name="slots",
    )(group_start, idx, rank)


def _sc_mesh():
    return plsc.VectorSubcoreMesh(core_axis_name="core", subcore_axis_name="subcore")


def _sc_worker():
    info = plsc.get_sparse_core_info()
    wid = lax.axis_index("subcore") * info.num_cores + lax.axis_index("core")
    return wid, info.num_cores * info.num_subcores


def _dispatch(t2, pos_flat, n_slots):
    n_tok, d = t2.shape
    n_workers = 32
    per_w = n_tok // n_workers
    assert per_w % SC_ROWS == 0

    @functools.partial(
        pl.kernel, mesh=_sc_mesh(),
        out_type=jax.ShapeDtypeStruct((n_slots, d), t2.dtype),
        scratch_types=[pltpu.VMEM((SC_ROWS,), I32), pltpu.VMEM((SC_ROWS, d), t2.dtype)],
        name="dispatch",
    )
    def k(t_hbm, pos_hbm, o_hbm, idx_v, rows_v):
        wid, nw = _sc_worker()
        assert nw == n_workers

        @pl.loop(0, per_w // SC_ROWS)
        def _(c):
            base = pl.multiple_of(wid * per_w + c * SC_ROWS, SC_ROWS)
            pltpu.sync_copy(t_hbm.at[pl.ds(base, SC_ROWS)], rows_v)
            for kk in range(TOP_K):
                pltpu.sync_copy(pos_hbm.at[pl.ds(kk * n_tok + base, SC_ROWS)], idx_v)
                pltpu.sync_copy(rows_v, o_hbm.at[idx_v])

    return k(t2, pos_flat)


def _gather_back(sorted_out, pos_flat):
    n_rows = pos_flat.shape[0]
    d = sorted_out.shape[1]
    n_workers = 32
    per_w = n_rows // n_workers
    assert per_w % SC_ROWS == 0

    @functools.partial(
        pl.kernel, mesh=_sc_mesh(),
        out_type=jax.ShapeDtypeStruct((n_rows, d), sorted_out.dtype),
        scratch_types=[pltpu.VMEM((SC_ROWS,), I32), pltpu.VMEM((SC_ROWS, d), sorted_out.dtype)],
        name="gather_back",
    )
    def k(s_hbm, pos_hbm, o_hbm, idx_v, rows_v):
        wid, nw = _sc_worker()
        assert nw == n_workers

        @pl.loop(0, per_w // SC_ROWS)
        def _(c):
            base = pl.multiple_of(wid * per_w + c * SC_ROWS, SC_ROWS)
            pltpu.sync_copy(pos_hbm.at[pl.ds(base, SC_ROWS)], idx_v)
            pltpu.sync_copy(s_hbm.at[idx_v], rows_v)
            pltpu.sync_copy(rows_v, o_hbm.at[pl.ds(base, SC_ROWS)])

    return k(sorted_out, pos_flat)


def _weight_copies(w_hbm, wf32, sem, expert, slot):
    return [pltpu.make_async_copy(w.at[expert], wf32.at[slot, mtx], sem.at[slot])
            for mtx, w in enumerate(w_hbm)]


def _expert_kernel(texp_ref, tfirst_ref, tvalid_ref, tslot_ref, tnext_ref,
                   x_ref, wg_hbm, bg_ref, wu_hbm, bu_ref, wd_hbm, bd_ref,
                   o_ref, wf32, wbf, wsem):
    i = pl.program_id(0)
    w_hbm = (wg_hbm, wu_hbm, wd_hbm)

    @pl.when(tvalid_ref[i] == 0)
    def _():
        o_ref[...] = jnp.zeros_like(o_ref)

    @pl.when(tvalid_ref[i] == 1)
    def _():
        @pl.when(tfirst_ref[i] == 1)
        def _():
            slot = tslot_ref[i]
            expert = texp_ref[i]

            @pl.when(i == 0)
            def _():
                for cp in _weight_copies(w_hbm, wf32, wsem, expert, slot):
                    cp.start()

            for cp in _weight_copies(w_hbm, wf32, wsem, expert, slot):
                cp.wait()

            @pl.when(tnext_ref[i] >= 0)
            def _():
                for cp in _weight_copies(w_hbm, wf32, wsem, tnext_ref[i], 1 - slot):
                    cp.start()

            for mtx in range(3):
                _cast_rows(wf32.at[slot, mtx], wbf.at[mtx], D_MODEL)

        xs = x_ref[...].astype(BF16)
        acc = None
        cn = 256
        for c in range(D_MODEL // cn):
            sl = slice(c * cn, (c + 1) * cn)
            g = jnp.dot(xs, wbf[0, :, sl], preferred_element_type=F32) + bg_ref[:, sl]
            u = jnp.dot(xs, wbf[1, :, sl], preferred_element_type=F32) + bu_ref[:, sl]
            g = jnp.minimum(g, SWIGLU_LIMIT)
            u = jnp.clip(u, -SWIGLU_LIMIT, SWIGLU_LIMIT)
            hid = ((u + 1.0) * (g * jax.nn.sigmoid(SWIGLU_ALPHA * g))).astype(BF16)
            part = jnp.dot(hid, wbf[2, sl, :], preferred_element_type=F32)
            acc = part if acc is None else acc + part
        o_ref[...] = acc + bd_ref[...]


def _experts(sorted_t, tile_flags, w_gate, b_gate, w_up, b_up, w_down, b_down, n_tiles):
    row_blk = pl.BlockSpec((EXP_ROWS, D_MODEL), lambda i, *_: (i, 0))
    w_any = pl.BlockSpec(memory_space=pl.ANY)
    b_blk = pl.BlockSpec((None, 1, D_MODEL), lambda i, te, *_: (te[i], 0, 0))
    grid_spec = pltpu.PrefetchScalarGridSpec(
        num_scalar_prefetch=len(tile_flags),
        grid=(n_tiles,),
        in_specs=[row_blk, w_any, b_blk, w_any, b_blk, w_any, b_blk],
        out_specs=row_blk,
        scratch_shapes=[
            pltpu.VMEM((2, 3, D_MODEL, D_MODEL), F32),
            pltpu.VMEM((3, D_MODEL, D_MODEL), BF16),
            pltpu.SemaphoreType.DMA((2,)),
        ],
    )
    b3 = lambda b: b.reshape(N_EXPERTS, 1, D_MODEL)
    return pl.pallas_call(
        _expert_kernel,
        grid_spec=grid_spec,
        out_shape=jax.ShapeDtypeStruct((n_tiles * EXP_ROWS, D_MODEL), F32),
        compiler_params=pltpu.CompilerParams(
            dimension_semantics=("arbitrary",), vmem_limit_bytes=VMEM_LIMIT),
        name="experts",
    )(*tile_flags, sorted_t,
      w_gate, b3(b_gate), w_up, b3(b_up), w_down, b3(b_down))


def _combine_kernel(x1_ref, s0_ref, s1_ref, s2_ref, s3_ref, w_ref, g_ref, o_ref):
    w = w_ref[...]
    y = x1_ref[...]
    for k, s_ref in enumerate((s0_ref, s1_ref, s2_ref, s3_ref)):
        y = y + w[:, k:k + 1] * s_ref[...]
    ms = jnp.mean(y * y, axis=-1, keepdims=True)
    o_ref[...] = (y * lax.rsqrt(ms + RMS_EPS)) * g_ref[...]


def _combine(x1, slabs, wgt, norm_g):
    n_tok = x1.shape[0]
    rows = MIX_ROWS
    ntiles = n_tok // rows
    slab_spec = lambda k: pl.BlockSpec((rows, D_MODEL), lambda i, k=k: (k * ntiles + i, 0))
    return pl.pallas_call(
        _combine_kernel,
        grid=(ntiles,),
        in_specs=[pl.BlockSpec((rows, D_MODEL), lambda i: (i, 0))]
        + [slab_spec(k) for k in range(TOP_K)]
        + [pl.BlockSpec((rows, 8), lambda i: (i, 0)),
           pl.BlockSpec((1, D_MODEL), lambda i: (0, 0))],
        out_specs=pl.BlockSpec((rows, D_MODEL), lambda i: (i, 0)),
        out_shape=jax.ShapeDtypeStruct((n_tok, D_MODEL), F32),
        compiler_params=pltpu.CompilerParams(
            dimension_semantics=("arbitrary",), vmem_limit_bytes=VMEM_LIMIT),
        name="combine",
    )(x1, slabs, slabs, slabs, slabs, wgt, norm_g.reshape(1, D_MODEL))


def kernel(x, norm_mix_g, w_in, conv_dw_w, conv_dw_b, conv_ln_g, conv_ln_b, rel_bias, w_out,
           norm_ffn_g, router_w, router_b, exp_w_gate, exp_b_gate, exp_w_up, exp_b_up,
           exp_w_down, exp_b_down, norm_final_g):
    bsz, seq, _ = x.shape
    n_tok = bsz * seq
    assert norm_mix_g.shape[0] == 1, "single-layer block"
    assert seq % IN_ROWS == 0 and seq % MIX_ROWS == 0 and IN_ROWS == LEFT_PAD
    n_tiles = (TOP_K * n_tok) // EXP_ROWS + N_EXPERTS - 1

    q, kpad, vpad, hcv = _inproj(x, norm_mix_g[0], w_in[0])
    attn = _attention(q, kpad, vpad, _band_bias(rel_bias[0]))
    x1, t, idx, wgt, rank, cnt = _mix_route(
        x.reshape(n_tok, D_MODEL), attn.reshape(n_tok, D_ATTN), hcv.reshape(n_tok, D_CONV), seq,
        conv_dw_w[0], conv_dw_b[0], conv_ln_g[0], conv_ln_b[0], w_out[0], norm_ffn_g[0],
        router_w[0], router_b[0])
    tile_flags, group_start = _tile_plan(cnt[:, 0].astype(I32), n_tiles)
    pos_flat = _slots(group_start, idx, rank).reshape(TOP_K * n_tok)
    sorted_t = _dispatch(t, pos_flat, n_tiles * EXP_ROWS)
    sorted_out = _experts(sorted_t, tile_flags,
                          exp_w_gate[0], exp_b_gate[0], exp_w_up[0], exp_b_up[0],
                          exp_w_down[0], exp_b_down[0], n_tiles)
    slabs = _gather_back(sorted_out, pos_flat)
    out = _combine(x1, slabs, wgt, norm_final_g)
    return out.reshape(bsz, seq, D_MODEL)
```

```python
import functools

import jax
import jax.numpy as jnp
from jax import lax
from jax.experimental import pallas as pl
from jax.experimental.pallas import tpu as pltpu
from jax.experimental.pallas import tpu_sc as plsc

F32 = jnp.float32
BF16 = jnp.bfloat16
I32 = jnp.int32

D_MODEL = 1024
CHUNK = 64
N_HEADS = 8
HEAD_DIM = 64
D_ATTN = N_HEADS * HEAD_DIM
LEFT_CHUNKS = 8
REL_MAX = 128
REL_MIN = -(CHUNK - 1)
D_CONV = D_MODEL - D_ATTN
CONV_WIDTH = 31
N_EXPERTS = 32
TOP_K = 4
SWIGLU_ALPHA = 1.702
SWIGLU_LIMIT = 7.0
RMS_EPS = 1e-5
LN_EPS = 1e-5

LEFT_PAD = LEFT_CHUNKS * CHUNK
IN_ROWS = 512
Q_ROWS = 2 * CHUNK
BAND_ROWS = Q_ROWS + LEFT_PAD
HEADS_PER_STEP = 4
GROUP_LANES = HEADS_PER_STEP * HEAD_DIM
MIX_ROWS = 256
HALO_ROWS = 32
SHIFT_ROWS = HALO_ROWS + MIX_ROWS - 8
EXP_ROWS = 256
SC_ROWS = 128
NEG_BIG = -1e30
VMEM_LIMIT = 56 * 1024 * 1024


HALF = D_MODEL // 2
HI_MASK = -65536


def _pack_rows(x):
    bits = lax.bitcast_convert_type(x.astype(BF16).astype(F32), I32)
    return lax.shift_right_logical(bits[:, :HALF], 16) | (bits[:, HALF:] & HI_MASK)


def _unpack_rows(w):
    lo = lax.bitcast_convert_type(lax.shift_left(w, 16), F32)
    hi = lax.bitcast_convert_type(w & HI_MASK, F32)
    return lo, hi


def _cast_rows(src_ref, dst_ref, rows, step=128):
    def body(c, carry):
        r = pl.multiple_of(c * step, step)
        dst_ref[pl.ds(r, step), :] = src_ref[pl.ds(r, step), :].astype(dst_ref.dtype)
        return carry
    lax.fori_loop(0, rows // step, body, 0)


def _inproj_kernel(x_ref, g_ref, w_ref, q_ref, k_ref, v_ref, h_ref, wbf_ref):
    b = pl.program_id(0)
    j = pl.program_id(1)

    @pl.when((b == 0) & (j == 0))
    def _():
        _cast_rows(w_ref, wbf_ref, D_MODEL)

    @pl.when(j == 0)
    def _():
        k_ref[...] = jnp.zeros_like(k_ref)
        v_ref[...] = jnp.zeros_like(v_ref)

    @pl.when(j > 0)
    def _():
        x = x_ref[...]
        ms = jnp.mean(x * x, axis=-1, keepdims=True)
        hb = ((x * lax.rsqrt(ms + RMS_EPS)) * g_ref[...]).astype(BF16)

        def proj(c0, width):
            return jnp.dot(hb, wbf_ref[:, c0:c0 + width], preferred_element_type=F32)

        q_ref[...] = (proj(0, D_ATTN) * (HEAD_DIM ** -0.5)).astype(BF16)
        k_ref[...] = proj(D_ATTN, D_ATTN).astype(BF16)
        v_ref[...] = proj(2 * D_ATTN, D_ATTN).astype(BF16)
        a = proj(3 * D_ATTN, D_CONV)
        gate = proj(3 * D_ATTN + D_CONV, D_CONV)
        h_ref[...] = a * jax.nn.sigmoid(gate)


def _inproj(x, norm_g, w_in):
    bsz, seq, _ = x.shape
    nblk = seq // IN_ROWS
    d_cols = w_in.shape[1]
    row_blk = lambda b, j: (b, jnp.maximum(j - 1, 0), 0)
    return pl.pallas_call(
        _inproj_kernel,
        grid=(bsz, nblk + 1),
        in_specs=[
            pl.BlockSpec((None, IN_ROWS, D_MODEL), row_blk),
            pl.BlockSpec((1, D_MODEL), lambda b, j: (0, 0)),
            pl.BlockSpec((D_MODEL, d_cols), lambda b, j: (0, 0)),
        ],
        out_specs=[
            pl.BlockSpec((None, IN_ROWS, D_ATTN), row_blk),
            pl.BlockSpec((None, IN_ROWS, D_ATTN), lambda b, j: (b, j, 0)),
            pl.BlockSpec((None, IN_ROWS, D_ATTN), lambda b, j: (b, j, 0)),
            pl.BlockSpec((None, IN_ROWS, D_CONV), row_blk),
        ],
        out_shape=[
            jax.ShapeDtypeStruct((bsz, seq, D_ATTN), BF16),
            jax.ShapeDtypeStruct((bsz, seq + LEFT_PAD, D_ATTN), BF16),
            jax.ShapeDtypeStruct((bsz, seq + LEFT_PAD, D_ATTN), BF16),
            jax.ShapeDtypeStruct((bsz, seq, D_CONV), F32),
        ],
        scratch_shapes=[pltpu.VMEM((D_MODEL, d_cols), BF16)],
        compiler_params=pltpu.CompilerParams(
            dimension_semantics=("arbitrary", "arbitrary"), vmem_limit_bytes=VMEM_LIMIT),
        name="inproj",
    )(x, norm_g.reshape(1, D_MODEL), w_in)


def _attn_kernel(q_ref, k_ref, v_ref, bias_ref, o_ref):
    i = pl.program_id(1)
    start = pl.multiple_of(i * Q_ROWS, Q_ROWS)
    lane = lax.broadcasted_iota(I32, (Q_ROWS, GROUP_LANES), 1) // HEAD_DIM
    col = lax.broadcasted_iota(I32, (HEADS_PER_STEP * Q_ROWS, BAND_ROWS), 1)
    key_ok = col >= LEFT_PAD - i * Q_ROWS
    for g in range(N_HEADS // HEADS_PER_STEP):
        lanes = slice(g * GROUP_LANES, (g + 1) * GROUP_LANES)
        q = q_ref[:, lanes]
        qs = jnp.concatenate(
            [jnp.where(lane == h, q, jnp.zeros_like(q)) for h in range(HEADS_PER_STEP)], axis=0)
        kb = k_ref[pl.ds(start, BAND_ROWS), lanes]
        vb = v_ref[pl.ds(start, BAND_ROWS), lanes]
        s = lax.dot_general(qs, kb, (((1,), (1,)), ((), ())), preferred_element_type=F32)
        rows = slice(g * HEADS_PER_STEP * Q_ROWS, (g + 1) * HEADS_PER_STEP * Q_ROWS)
        s = jnp.where(key_ok, s + bias_ref[rows, :], NEG_BIG)
        m = jnp.max(s, axis=-1, keepdims=True)
        p = jnp.exp(s - m)
        l = jnp.sum(p, axis=-1, keepdims=True)
        o = jnp.dot(p.astype(BF16), vb, preferred_element_type=F32) / l
        out = o[0:Q_ROWS]
        for h in range(1, HEADS_PER_STEP):
            out = jnp.where(lane == h, o[h * Q_ROWS:(h + 1) * Q_ROWS], out)
        o_ref[:, lanes] = out.astype(o_ref.dtype)


def _attention(q, kpad, vpad, bias):
    bsz, seq, _ = q.shape
    nq = seq // Q_ROWS
    return pl.pallas_call(
        _attn_kernel,
        grid=(bsz, nq),
        in_specs=[
            pl.BlockSpec((None, Q_ROWS, D_ATTN), lambda b, i: (b, i, 0)),
            pl.BlockSpec((None, seq + LEFT_PAD, D_ATTN), lambda b, i: (b, 0, 0)),
            pl.BlockSpec((None, seq + LEFT_PAD, D_ATTN), lambda b, i: (b, 0, 0)),
            pl.BlockSpec((N_HEADS * Q_ROWS, BAND_ROWS), lambda b, i: (0, 0)),
        ],
        out_specs=pl.BlockSpec((None, Q_ROWS, D_ATTN), lambda b, i: (b, i, 0)),
        out_shape=jax.ShapeDtypeStruct((bsz, seq, D_ATTN), BF16),
        compiler_params=pltpu.CompilerParams(
            dimension_semantics=("arbitrary", "arbitrary"), vmem_limit_bytes=VMEM_LIMIT),
        name="chunk_attn",
    )(q, kpad, vpad, bias)


def _band_bias(rel_bias):
    n_rel = REL_MAX - REL_MIN + 1
    far = jnp.broadcast_to(rel_bias[:, n_rel - 1:n_rel], (N_HEADS, BAND_ROWS - 1 - REL_MAX))
    near = rel_bias[:, ::-1]
    ahead = jnp.broadcast_to(rel_bias[:, 0:1], (N_HEADS, Q_ROWS - 1 + REL_MIN))
    diag = jnp.concatenate([far, near, ahead], axis=1).astype(F32)
    bias = jnp.stack(
        [diag[:, Q_ROWS - 1 - r:Q_ROWS - 1 - r + BAND_ROWS] for r in range(Q_ROWS)], axis=1)
    r = jnp.arange(Q_ROWS)[:, None]
    m = jnp.arange(BAND_ROWS)[None, :]
    cq = r // CHUNK
    ck = m // CHUNK
    in_band = (ck >= cq) & (ck <= cq + LEFT_CHUNKS)
    bias = jnp.where(in_band[None], bias, NEG_BIG)
    return bias.reshape(N_HEADS * Q_ROWS, BAND_ROWS)


def _mix_route_kernel(tiles_per_seq,
                      x_ref, a_ref, hc_ref, hh_ref, cw_ref, cb_ref, lg_ref, lb_ref, wo_ref,
                      ng_ref, rwt_ref, rb_ref,
                      x1_ref, t_ref, idx_ref, wgt_ref, rank_ref, cnt_ref,
                      wobf_ref, hw_ref, sh_ref, cntacc_ref):
    i = pl.program_id(0)

    @pl.when(i == 0)
    def _():
        _cast_rows(wo_ref, wobf_ref, D_MODEL)
        cntacc_ref[...] = jnp.zeros_like(cntacc_ref)

    seq_start = (i % tiles_per_seq) == 0
    hw_ref[0:HALO_ROWS, :] = jnp.where(seq_start, 0.0, hh_ref[...])
    hw_ref[HALO_ROWS:HALO_ROWS + MIX_ROWS, :] = hc_ref[...]
    off = HALO_ROWS - (CONV_WIDTH - 1)
    for b in range(1, 8):
        sh_ref[b - 1] = hw_ref[pl.ds(b, SHIFT_ROWS), :]
    acc = None
    for j in range(CONV_WIDTH):
        a, b = divmod(off + j, 8)
        src = hw_ref if b == 0 else sh_ref.at[b - 1]
        term = src[pl.ds(8 * a, MIX_ROWS), :] * cw_ref[j:j + 1, :]
        acc = term if acc is None else acc + term
    acc = acc + cb_ref[...]
    mu = jnp.mean(acc, axis=-1, keepdims=True)
    d = acc - mu
    var = jnp.mean(d * d, axis=-1, keepdims=True)
    y = d * lax.rsqrt(var + LN_EPS) * lg_ref[...] + lb_ref[...]
    conv_out = (y * jax.nn.sigmoid(y)).astype(BF16)

    mixed = jnp.dot(a_ref[...], wobf_ref[0:D_ATTN, :], preferred_element_type=F32)
    mixed = mixed + jnp.dot(conv_out, wobf_ref[D_ATTN:D_MODEL, :], preferred_element_type=F32)
    x1 = x_ref[...] + mixed
    x1_ref[...] = x1
    ms = jnp.mean(x1 * x1, axis=-1, keepdims=True)
    t = (x1 * lax.rsqrt(ms + RMS_EPS)) * ng_ref[...]
    t_ref[...] = _pack_rows(t)

    logits = lax.dot_general(rwt_ref[...], t, (((1,), (1,)), ((), ())),
                             precision=lax.Precision.HIGHEST,
                             preferred_element_type=F32) + rb_ref[...]
    e_iota = lax.broadcasted_iota(I32, logits.shape, 0)
    vals, idxs, hots = [], [], []
    for _ in range(TOP_K):
        m = jnp.max(logits, axis=0, keepdims=True)
        am = jnp.min(jnp.where(logits == m, e_iota, N_EXPERTS), axis=0, keepdims=True)
        hot = e_iota == am
        vals.append(m)
        idxs.append(am)
        hots.append(hot)
        logits = jnp.where(hot, -jnp.inf, logits)
    exps = [jnp.exp(v - vals[0]) for v in vals]
    den = exps[0] + exps[1] + exps[2] + exps[3]
    wts = [e / den for e in exps]

    hot_sum = (hots[0] | hots[1] | hots[2] | hots[3])
    hot_f = hot_sum.astype(F32)
    ra = lax.broadcasted_iota(I32, (MIX_ROWS, MIX_ROWS), 0)
    rc = lax.broadcasted_iota(I32, (MIX_ROWS, MIX_ROWS), 1)
    upper = (ra < rc).astype(BF16)
    prefix = jnp.dot(hot_f.astype(BF16), upper, preferred_element_type=F32)
    base = prefix + cntacc_ref[...]
    ranks = [jnp.sum(jnp.where(h, base, 0.0), axis=0, keepdims=True) for h in hots]
    cntacc_ref[...] = cntacc_ref[...] + jnp.sum(hot_f, axis=1, keepdims=True)

    idx_ref[...] = jnp.concatenate(idxs, axis=0)
    rank_ref[...] = jnp.concatenate(ranks, axis=0).astype(I32)
    cnt_ref[...] = jnp.broadcast_to(cntacc_ref[...], cnt_ref.shape)
    w8 = jnp.concatenate(wts + [jnp.zeros((8 - TOP_K, MIX_ROWS), F32)], axis=0)
    wgt_ref[...] = w8.T


def _mix_route(x2, attn2, hcv2, seq, conv_w, conv_b, ln_g, ln_b, w_out, norm_g, router_w, router_b):
    n_tok = x2.shape[0]
    ntiles = n_tok // MIX_ROWS
    halo_per_tile = MIX_ROWS // HALO_ROWS
    row = lambda i: (i, 0)
    const = lambda i: (0, 0)
    vec = lambda v: v.reshape(1, -1)
    return pl.pallas_call(
        functools.partial(_mix_route_kernel, seq // MIX_ROWS),
        grid=(ntiles,),
        in_specs=[
            pl.BlockSpec((MIX_ROWS, D_MODEL), row),
            pl.BlockSpec((MIX_ROWS, D_ATTN), row),
            pl.BlockSpec((MIX_ROWS, D_CONV), row),
            pl.BlockSpec((HALO_ROWS, D_CONV), lambda i: (jnp.maximum(i * halo_per_tile - 1, 0), 0)),
            pl.BlockSpec((CONV_WIDTH, D_CONV), const),
            pl.BlockSpec((1, D_CONV), const),
            pl.BlockSpec((1, D_CONV), const),
            pl.BlockSpec((1, D_CONV), const),
            pl.BlockSpec((D_MODEL, D_MODEL), const),
            pl.BlockSpec((1, D_MODEL), const),
            pl.BlockSpec((N_EXPERTS, D_MODEL), const),
            pl.BlockSpec((N_EXPERTS, 1), const),
        ],
        out_specs=[
            pl.BlockSpec((MIX_ROWS, D_MODEL), row),
            pl.BlockSpec((MIX_ROWS, HALF), row),
            pl.BlockSpec((TOP_K, MIX_ROWS), lambda i: (0, i)),
            pl.BlockSpec((MIX_ROWS, 8), row),
            pl.BlockSpec((TOP_K, MIX_ROWS), lambda i: (0, i)),
            pl.BlockSpec((N_EXPERTS, 128), const),
        ],
        out_shape=[
            jax.ShapeDtypeStruct((n_tok, D_MODEL), F32),
            jax.ShapeDtypeStruct((n_tok, HALF), I32),
            jax.ShapeDtypeStruct((TOP_K, n_tok), I32),
            jax.ShapeDtypeStruct((n_tok, 8), F32),
            jax.ShapeDtypeStruct((TOP_K, n_tok), I32),
            jax.ShapeDtypeStruct((N_EXPERTS, 128), F32),
        ],
        scratch_shapes=[
            pltpu.VMEM((D_MODEL, D_MODEL), BF16),
            pltpu.VMEM((HALO_ROWS + MIX_ROWS, D_CONV), F32),
            pltpu.VMEM((7, SHIFT_ROWS, D_CONV), F32),
            pltpu.VMEM((N_EXPERTS, 1), F32),
        ],
        compiler_params=pltpu.CompilerParams(
            dimension_semantics=("arbitrary",), vmem_limit_bytes=VMEM_LIMIT),
        name="mix_route",
    )(x2, attn2, hcv2, hcv2, conv_w, vec(conv_b), vec(ln_g), vec(ln_b), w_out, vec(norm_g),
      router_w.T, router_b.reshape(N_EXPERTS, 1))


def _tile_plan(counts, n_tiles):
    tiles_per = (counts + EXP_ROWS - 1) // EXP_ROWS
    tile_end = jnp.cumsum(tiles_per)
    tile_begin = tile_end - tiles_per
    n_valid = tile_end[-1]
    tiles = jnp.arange(n_tiles, dtype=I32)
    tile_valid = tiles < n_valid
    capped = jnp.minimum(tiles, n_valid - 1)
    tile_expert = jnp.sum((capped[:, None] >= tile_end[None, :]).astype(I32), axis=1)
    tile_expert = jnp.minimum(tile_expert, N_EXPERTS - 1)
    is_first = (tiles[:, None] == tile_begin[None, :]) & (tiles_per[None, :] > 0)
    tile_first = tile_valid & jnp.any(is_first, axis=1)
    group_start = tile_begin * EXP_ROWS
    experts = jnp.arange(N_EXPERTS, dtype=I32)
    nonempty = tiles_per > 0
    parity = (jnp.cumsum(nonempty.astype(I32)) - 1) % 2
    later = nonempty[None, :] & (experts[None, :] > experts[:, None])
    nxt = jnp.min(jnp.where(later, experts[None, :], N_EXPERTS), axis=1)
    nxt = jnp.where(nxt == N_EXPERTS, -1, nxt)
    hot = (tile_expert[:, None] == experts[None, :]).astype(I32)
    tile_slot = jnp.sum(hot * parity[None, :], axis=1)
    tile_next = jnp.sum(hot * nxt[None, :], axis=1)
    flags = (tile_expert, tile_first.astype(I32), tile_valid.astype(I32),
             tile_slot.astype(I32), tile_next.astype(I32))
    return flags, group_start.astype(I32)


def _slot_kernel(gstart_ref, idx_ref, rank_ref, pos_ref):
    idx = idx_ref[...]
    pos = rank_ref[...]
    for e in range(N_EXPERTS):
        pos = pos + jnp.where(idx == e, gstart_ref[e], 0)
    pos_ref[...] = pos


def _slots(group_start, idx, rank):
    full = pl.BlockSpec(idx.shape, lambda i, gs: (0, 0))
    return pl.pallas_call(
        _slot_kernel,
        grid_spec=pltpu.PrefetchScalarGridSpec(
            num_scalar_prefetch=1, grid=(1,), in_specs=[full, full], out_specs=full),
        out_shape=jax.ShapeDtypeStruct(idx.shape, I32),
        name="slots",
    )(group_start, idx, rank)


def _sc_mesh():
    return plsc.VectorSubcoreMesh(core_axis_name="core", subcore_axis_name="subcore")


def _sc_worker():
    info = plsc.get_sparse_core_info()
    wid = lax.axis_index("subcore") * info.num_cores + lax.axis_index("core")
    return wid, info.num_cores * info.num_subcores


def _dispatch(t2, pos_flat, n_slots):
    n_tok, d = t2.shape
    n_workers = 32
    per_w = n_tok // n_workers
    assert per_w % SC_ROWS == 0

    @functools.partial(
        pl.kernel, mesh=_sc_mesh(),
        out_type=jax.ShapeDtypeStruct((n_slots, d), t2.dtype),
        scratch_types=[pltpu.VMEM((SC_ROWS,), I32), pltpu.VMEM((SC_ROWS, d), t2.dtype)],
        name="dispatch",
    )
    def k(t_hbm, pos_hbm, o_hbm, idx_v, rows_v):
        wid, nw = _sc_worker()
        assert nw == n_workers

        @pl.loop(0, per_w // SC_ROWS)
        def _(c):
            base = pl.multiple_of(wid * per_w + c * SC_ROWS, SC_ROWS)
            pltpu.sync_copy(t_hbm.at[pl.ds(base, SC_ROWS)], rows_v)
            for kk in range(TOP_K):
                pltpu.sync_copy(pos_hbm.at[pl.ds(kk * n_tok + base, SC_ROWS)], idx_v)
                pltpu.sync_copy(rows_v, o_hbm.at[idx_v])

    return k(t2, pos_flat)


def _gather_back(sorted_out, pos_flat):
    n_rows = pos_flat.shape[0]
    d = sorted_out.shape[1]
    n_workers = 32
    per_w = n_rows // n_workers
    assert per_w % SC_ROWS == 0

    @functools.partial(
        pl.kernel, mesh=_sc_mesh(),
        out_type=jax.ShapeDtypeStruct((n_rows, d), sorted_out.dtype),
        scratch_types=[pltpu.VMEM((SC_ROWS,), I32), pltpu.VMEM((SC_ROWS, d), sorted_out.dtype)],
        name="gather_back",
    )
    def k(s_hbm, pos_hbm, o_hbm, idx_v, rows_v):
        wid, nw = _sc_worker()
        assert nw == n_workers

        @pl.loop(0, per_w // SC_ROWS)
        def _(c):
            base = pl.multiple_of(wid * per_w + c * SC_ROWS, SC_ROWS)
            pltpu.sync_copy(pos_hbm.at[pl.ds(base, SC_ROWS)], idx_v)
            pltpu.sync_copy(s_hbm.at[idx_v], rows_v)
            pltpu.sync_copy(rows_v, o_hbm.at[pl.ds(base, SC_ROWS)])

    return k(sorted_out, pos_flat)


def _weight_copies(w_hbm, wf32, sem, expert, slot):
    return [pltpu.make_async_copy(w.at[expert], wf32.at[slot, mtx], sem.at[slot])
            for mtx, w in enumerate(w_hbm)]


def _expert_kernel(texp_ref, tfirst_ref, tvalid_ref, tslot_ref, tnext_ref,
                   x_ref, wg_hbm, bg_ref, wu_hbm, bu_ref, wd_hbm, bd_ref,
                   o_ref, wf32, wbf, wsem):
    i = pl.program_id(0)
    w_hbm = (wg_hbm, wu_hbm, wd_hbm)

    @pl.when(tvalid_ref[i] == 0)
    def _():
        o_ref[...] = jnp.zeros_like(o_ref)

    @pl.when(tvalid_ref[i] == 1)
    def _():
        @pl.when(tfirst_ref[i] == 1)
        def _():
            slot = tslot_ref[i]
            expert = texp_ref[i]

            @pl.when(i == 0)
            def _():
                for cp in _weight_copies(w_hbm, wf32, wsem, expert, slot):
                    cp.start()

            for cp in _weight_copies(w_hbm, wf32, wsem, expert, slot):
                cp.wait()

            @pl.when(tnext_ref[i] >= 0)
            def _():
                for cp in _weight_copies(w_hbm, wf32, wsem, tnext_ref[i], 1 - slot):
                    cp.start()

            for mtx in range(3):
                _cast_rows(wf32.at[slot, mtx], wbf.at[mtx], D_MODEL)

        xs = jnp.concatenate(_unpack_rows(x_ref[...]), axis=1).astype(BF16)
        acc = None
        cn = 256
        for c in range(D_MODEL // cn):
            sl = slice(c * cn, (c + 1) * cn)
            g = jnp.dot(xs, wbf[0, :, sl], preferred_element_type=F32) + bg_ref[:, sl]
            u = jnp.dot(xs, wbf[1, :, sl], preferred_element_type=F32) + bu_ref[:, sl]
            g = jnp.minimum(g, SWIGLU_LIMIT)
            u = jnp.clip(u, -SWIGLU_LIMIT, SWIGLU_LIMIT)
            hid = ((u + 1.0) * (g * jax.nn.sigmoid(SWIGLU_ALPHA * g))).astype(BF16)
            part = jnp.dot(hid, wbf[2, sl, :], preferred_element_type=F32)
            acc = part if acc is None else acc + part
        o_ref[...] = _pack_rows(acc + bd_ref[...])


def _experts(sorted_t, tile_flags, w_gate, b_gate, w_up, b_up, w_down, b_down, n_tiles):
    row_blk = pl.BlockSpec((EXP_ROWS, HALF), lambda i, *_: (i, 0))
    w_any = pl.BlockSpec(memory_space=pl.ANY)
    b_blk = pl.BlockSpec((None, 1, D_MODEL), lambda i, te, *_: (te[i], 0, 0))
    grid_spec = pltpu.PrefetchScalarGridSpec(
        num_scalar_prefetch=len(tile_flags),
        grid=(n_tiles,),
        in_specs=[row_blk, w_any, b_blk, w_any, b_blk, w_any, b_blk],
        out_specs=row_blk,
        scratch_shapes=[
            pltpu.VMEM((2, 3, D_MODEL, D_MODEL), F32),
            pltpu.VMEM((3, D_MODEL, D_MODEL), BF16),
            pltpu.SemaphoreType.DMA((2,)),
        ],
    )
    b3 = lambda b: b.reshape(N_EXPERTS, 1, D_MODEL)
    return pl.pallas_call(
        _expert_kernel,
        grid_spec=grid_spec,
        out_shape=jax.ShapeDtypeStruct((n_tiles * EXP_ROWS, HALF), I32),
        compiler_params=pltpu.CompilerParams(
            dimension_semantics=("arbitrary",), vmem_limit_bytes=VMEM_LIMIT),
        name="experts",
    )(*tile_flags, sorted_t,
      w_gate, b3(b_gate), w_up, b3(b_up), w_down, b3(b_down))


def _combine_kernel(x1_ref, s0_ref, s1_ref, s2_ref, s3_ref, w_ref, g_ref, o_ref):
    w = w_ref[...]
    y_lo = x1_ref[:, :HALF]
    y_hi = x1_ref[:, HALF:]
    for k, s_ref in enumerate((s0_ref, s1_ref, s2_ref, s3_ref)):
        lo, hi = _unpack_rows(s_ref[...])
        y_lo = y_lo + w[:, k:k + 1] * lo
        y_hi = y_hi + w[:, k:k + 1] * hi
    sq = jnp.sum(y_lo * y_lo, axis=-1, keepdims=True) + jnp.sum(y_hi * y_hi, axis=-1, keepdims=True)
    scale = lax.rsqrt(sq * (1.0 / D_MODEL) + RMS_EPS)
    o_ref[:, :HALF] = (y_lo * scale) * g_ref[:, :HALF]
    o_ref[:, HALF:] = (y_hi * scale) * g_ref[:, HALF:]


def _combine(x1, slabs, wgt, norm_g):
    n_tok = x1.shape[0]
    rows = MIX_ROWS
    ntiles = n_tok // rows
    slab_spec = lambda k: pl.BlockSpec((rows, HALF), lambda i, k=k: (k * ntiles + i, 0))
    return pl.pallas_call(
        _combine_kernel,
        grid=(ntiles,),
        in_specs=[pl.BlockSpec((rows, D_MODEL), lambda i: (i, 0))]
        + [slab_spec(k) for k in range(TOP_K)]
        + [pl.BlockSpec((rows, 8), lambda i: (i, 0)),
           pl.BlockSpec((1, D_MODEL), lambda i: (0, 0))],
        out_specs=pl.BlockSpec((rows, D_MODEL), lambda i: (i, 0)),
        out_shape=jax.ShapeDtypeStruct((n_tok, D_MODEL), F32),
        compiler_params=pltpu.CompilerParams(
            dimension_semantics=("arbitrary",), vmem_limit_bytes=VMEM_LIMIT),
        name="combine",
    )(x1, slabs, slabs, slabs, slabs, wgt, norm_g.reshape(1, D_MODEL))


def kernel(x, norm_mix_g, w_in, conv_dw_w, conv_dw_b, conv_ln_g, conv_ln_b, rel_bias, w_out,
           norm_ffn_g, router_w, router_b, exp_w_gate, exp_b_gate, exp_w_up, exp_b_up,
           exp_w_down, exp_b_down, norm_final_g):
    bsz, seq, _ = x.shape
    n_tok = bsz * seq
    assert norm_mix_g.shape[0] == 1, "single-layer block"
    assert seq % IN_ROWS == 0 and seq % MIX_ROWS == 0 and IN_ROWS == LEFT_PAD
    n_tiles = (TOP_K * n_tok) // EXP_ROWS + N_EXPERTS - 1

    q, kpad, vpad, hcv = _inproj(x, norm_mix_g[0], w_in[0])
    attn = _attention(q, kpad, vpad, _band_bias(rel_bias[0]))
    x1, t, idx, wgt, rank, cnt = _mix_route(
        x.reshape(n_tok, D_MODEL), attn.reshape(n_tok, D_ATTN), hcv.reshape(n_tok, D_CONV), seq,
        conv_dw_w[0], conv_dw_b[0], conv_ln_g[0], conv_ln_b[0], w_out[0], norm_ffn_g[0],
        router_w[0], router_b[0])
    tile_flags, group_start = _tile_plan(cnt[:, 0].astype(I32), n_tiles)
    pos_flat = _slots(group_start, idx, rank).reshape(TOP_K * n_tok)
    sorted_t = _dispatch(t, pos_flat, n_tiles * EXP_ROWS)
    sorted_out = _experts(sorted_t, tile_flags,
                          exp_w_gate[0], exp_b_gate[0], exp_w_up[0], exp_b_up[0],
                          exp_w_down[0], exp_b_down[0], n_tiles)
    slabs = _gather_back(sorted_out, pos_flat)
    out = _combine(x1, slabs, wgt, norm_final_g)
    return out.reshape(bsz, seq, D_MODEL)
```

```python
import functools

import jax
import jax.numpy as jnp
from jax import lax
from jax.experimental import pallas as pl
from jax.experimental.pallas import tpu as pltpu
from jax.experimental.pallas import tpu_sc as plsc

F32 = jnp.float32
BF16 = jnp.bfloat16
I32 = jnp.int32

D_MODEL = 1024
CHUNK = 64
N_HEADS = 8
HEAD_DIM = 64
D_ATTN = N_HEADS * HEAD_DIM
LEFT_CHUNKS = 8
REL_MAX = 128
REL_MIN = -(CHUNK - 1)
D_CONV = D_MODEL - D_ATTN
CONV_WIDTH = 31
N_EXPERTS = 32
TOP_K = 4
SWIGLU_ALPHA = 1.702
SWIGLU_LIMIT = 7.0
RMS_EPS = 1e-5
LN_EPS = 1e-5

LEFT_PAD = LEFT_CHUNKS * CHUNK
IN_ROWS = 512
Q_ROWS = 2 * CHUNK
BAND_ROWS = Q_ROWS + LEFT_PAD
HEADS_PER_STEP = 4
GROUP_LANES = HEADS_PER_STEP * HEAD_DIM
MIX_ROWS = 512
MIX_SUBTILES = 1
HALO_ROWS = 32
EXP_ROWS = 256
SC_ROWS = 128
NEG_BIG = -1e30
VMEM_LIMIT = 56 * 1024 * 1024


HALF = D_MODEL // 2
HI_MASK = -65536


def _pack_rows(x):
    bits = lax.bitcast_convert_type(x.astype(BF16).astype(F32), I32)
    return lax.shift_right_logical(bits[:, :HALF], 16) | (bits[:, HALF:] & HI_MASK)


def _unpack_rows(w):
    lo = lax.bitcast_convert_type(lax.shift_left(w, 16), F32)
    hi = lax.bitcast_convert_type(w & HI_MASK, F32)
    return lo, hi


def _cast_rows(src_ref, dst_ref, rows, step=128):
    def body(c, carry):
        r = pl.multiple_of(c * step, step)
        dst_ref[pl.ds(r, step), :] = src_ref[pl.ds(r, step), :].astype(dst_ref.dtype)
        return carry
    lax.fori_loop(0, rows // step, body, 0)


def _conv_branch(hw_ref, sh_ref, rows, cw_ref, cb_ref, lg_ref, lb_ref, out_ref):
    off = HALO_ROWS - (CONV_WIDTH - 1)
    shift_rows = HALO_ROWS + rows - 8
    for b in range(1, 8):
        sh_ref[b - 1] = hw_ref[pl.ds(b, shift_rows), :]
    acc = None
    for j in range(CONV_WIDTH):
        a, b = divmod(off + j, 8)
        src = hw_ref if b == 0 else sh_ref.at[b - 1]
        term = src[pl.ds(8 * a, rows), :] * cw_ref[j:j + 1, :]
        acc = term if acc is None else acc + term
    acc = acc + cb_ref[...]
    mu = jnp.mean(acc, axis=-1, keepdims=True)
    d = acc - mu
    var = jnp.mean(d * d, axis=-1, keepdims=True)
    y = d * lax.rsqrt(var + LN_EPS) * lg_ref[...] + lb_ref[...]
    out_ref[...] = (y * jax.nn.sigmoid(y)).astype(out_ref.dtype)


def _inproj_kernel(x_ref, g_ref, w_ref, cw_ref, cb_ref, lg_ref, lb_ref,
                   q_ref, k_ref, v_ref, c_ref, wbf_ref, hw_ref, sh_ref):
    b = pl.program_id(0)
    j = pl.program_id(1)

    @pl.when((b == 0) & (j == 0))
    def _():
        _cast_rows(w_ref, wbf_ref, D_MODEL)

    @pl.when(j == 0)
    def _():
        k_ref[...] = jnp.zeros_like(k_ref)
        v_ref[...] = jnp.zeros_like(v_ref)
        hw_ref[IN_ROWS:IN_ROWS + HALO_ROWS, :] = jnp.zeros((HALO_ROWS, D_CONV), F32)

    @pl.when(j > 0)
    def _():
        x = x_ref[...]
        ms = jnp.mean(x * x, axis=-1, keepdims=True)
        hb = ((x * lax.rsqrt(ms + RMS_EPS)) * g_ref[...]).astype(BF16)

        def proj(c0, width):
            return jnp.dot(hb, wbf_ref[:, c0:c0 + width], preferred_element_type=F32)

        a = proj(3 * D_ATTN, D_CONV)
        gate = proj(3 * D_ATTN + D_CONV, D_CONV)
        hw_ref[0:HALO_ROWS, :] = hw_ref[IN_ROWS:IN_ROWS + HALO_ROWS, :]
        hw_ref[HALO_ROWS:HALO_ROWS + IN_ROWS, :] = a * jax.nn.sigmoid(gate)
        _conv_branch(hw_ref, sh_ref, IN_ROWS, cw_ref, cb_ref, lg_ref, lb_ref, c_ref)

        q_ref[...] = (proj(0, D_ATTN) * (HEAD_DIM ** -0.5)).astype(BF16)
        k_ref[...] = proj(D_ATTN, D_ATTN).astype(BF16)
        v_ref[...] = proj(2 * D_ATTN, D_ATTN).astype(BF16)


def _inproj(x, norm_g, w_in, conv_w, conv_b, ln_g, ln_b):
    bsz, seq, _ = x.shape
    nblk = seq // IN_ROWS
    d_cols = w_in.shape[1]
    row_blk = lambda b, j: (b, jnp.maximum(j - 1, 0), 0)
    const = lambda b, j: (0, 0)
    vec = lambda v: v.reshape(1, -1)
    return pl.pallas_call(
        _inproj_kernel,
        grid=(bsz, nblk + 1),
        in_specs=[
            pl.BlockSpec((None, IN_ROWS, D_MODEL), row_blk),
            pl.BlockSpec((1, D_MODEL), const),
            pl.BlockSpec((D_MODEL, d_cols), const),
            pl.BlockSpec((CONV_WIDTH, D_CONV), const),
            pl.BlockSpec((1, D_CONV), const),
            pl.BlockSpec((1, D_CONV), const),
            pl.BlockSpec((1, D_CONV), const),
        ],
        out_specs=[
            pl.BlockSpec((None, IN_ROWS, D_ATTN), row_blk),
            pl.BlockSpec((None, IN_ROWS, D_ATTN), lambda b, j: (b, j, 0)),
            pl.BlockSpec((None, IN_ROWS, D_ATTN), lambda b, j: (b, j, 0)),
            pl.BlockSpec((None, IN_ROWS, D_CONV), row_blk),
        ],
        out_shape=[
            jax.ShapeDtypeStruct((bsz, seq, D_ATTN), BF16),
            jax.ShapeDtypeStruct((bsz, seq + LEFT_PAD, D_ATTN), BF16),
            jax.ShapeDtypeStruct((bsz, seq + LEFT_PAD, D_ATTN), BF16),
            jax.ShapeDtypeStruct((bsz, seq, D_CONV), BF16),
        ],
        scratch_shapes=[
            pltpu.VMEM((D_MODEL, d_cols), BF16),
            pltpu.VMEM((HALO_ROWS + IN_ROWS, D_CONV), F32),
            pltpu.VMEM((7, HALO_ROWS + IN_ROWS - 8, D_CONV), F32),
        ],
        compiler_params=pltpu.CompilerParams(
            dimension_semantics=("arbitrary", "arbitrary"), vmem_limit_bytes=VMEM_LIMIT),
        name="inproj",
    )(x, vec(norm_g), w_in, conv_w, vec(conv_b), vec(ln_g), vec(ln_b))


def _attn_kernel(q_ref, k_ref, v_ref, bias_ref, o_ref):
    i = pl.program_id(1)
    start = pl.multiple_of(i * Q_ROWS, Q_ROWS)
    lane = lax.broadcasted_iota(I32, (Q_ROWS, GROUP_LANES), 1) // HEAD_DIM
    col = lax.broadcasted_iota(I32, (HEADS_PER_STEP * Q_ROWS, BAND_ROWS), 1)
    key_ok = col >= LEFT_PAD - i * Q_ROWS
    for g in range(N_HEADS // HEADS_PER_STEP):
        lanes = slice(g * GROUP_LANES, (g + 1) * GROUP_LANES)
        q = q_ref[:, lanes]
        qs = jnp.concatenate(
            [jnp.where(lane == h, q, jnp.zeros_like(q)) for h in range(HEADS_PER_STEP)], axis=0)
        kb = k_ref[pl.ds(start, BAND_ROWS), lanes]
        vb = v_ref[pl.ds(start, BAND_ROWS), lanes]
        s = lax.dot_general(qs, kb, (((1,), (1,)), ((), ())), preferred_element_type=F32)
        rows = slice(g * HEADS_PER_STEP * Q_ROWS, (g + 1) * HEADS_PER_STEP * Q_ROWS)
        s = jnp.where(key_ok, s + bias_ref[rows, :], NEG_BIG)
        m = jnp.max(s, axis=-1, keepdims=True)
        p = jnp.exp(s - m)
        l = jnp.sum(p, axis=-1, keepdims=True)
        o = jnp.dot(p.astype(BF16), vb, preferred_element_type=F32) / l
        out = o[0:Q_ROWS]
        for h in range(1, HEADS_PER_STEP):
            out = jnp.where(lane == h, o[h * Q_ROWS:(h + 1) * Q_ROWS], out)
        o_ref[:, lanes] = out.astype(o_ref.dtype)


def _attention(q, kpad, vpad, bias):
    bsz, seq, _ = q.shape
    nq = seq // Q_ROWS
    return pl.pallas_call(
        _attn_kernel,
        grid=(bsz, nq),
        in_specs=[
            pl.BlockSpec((None, Q_ROWS, D_ATTN), lambda b, i: (b, i, 0)),
            pl.BlockSpec((None, seq + LEFT_PAD, D_ATTN), lambda b, i: (b, 0, 0)),
            pl.BlockSpec((None, seq + LEFT_PAD, D_ATTN), lambda b, i: (b, 0, 0)),
            pl.BlockSpec((N_HEADS * Q_ROWS, BAND_ROWS), lambda b, i: (0, 0)),
        ],
        out_specs=pl.BlockSpec((None, Q_ROWS, D_ATTN), lambda b, i: (b, i, 0)),
        out_shape=jax.ShapeDtypeStruct((bsz, seq, D_ATTN), BF16),
        compiler_params=pltpu.CompilerParams(
            dimension_semantics=("arbitrary", "arbitrary"), vmem_limit_bytes=VMEM_LIMIT),
        name="chunk_attn",
    )(q, kpad, vpad, bias)


def _band_bias(rel_bias):
    n_rel = REL_MAX - REL_MIN + 1
    far = jnp.broadcast_to(rel_bias[:, n_rel - 1:n_rel], (N_HEADS, BAND_ROWS - 1 - REL_MAX))
    near = rel_bias[:, ::-1]
    ahead = jnp.broadcast_to(rel_bias[:, 0:1], (N_HEADS, Q_ROWS - 1 + REL_MIN))
    diag = jnp.concatenate([far, near, ahead], axis=1).astype(F32)
    bias = jnp.stack(
        [diag[:, Q_ROWS - 1 - r:Q_ROWS - 1 - r + BAND_ROWS] for r in range(Q_ROWS)], axis=1)
    r = jnp.arange(Q_ROWS)[:, None]
    m = jnp.arange(BAND_ROWS)[None, :]
    cq = r // CHUNK
    ck = m // CHUNK
    in_band = (ck >= cq) & (ck <= cq + LEFT_CHUNKS)
    bias = jnp.where(in_band[None], bias, NEG_BIG)
    return bias.reshape(N_HEADS * Q_ROWS, BAND_ROWS)


def _split_bf16(v):
    hi = v.astype(BF16)
    lo = (v - hi.astype(F32)).astype(BF16)
    return hi, lo


def _mix_route_kernel(x_ref, a_ref, c_ref, wo_ref, ng_ref, rwt_ref, rb_ref,
                      x1_ref, t_ref, idx_ref, wgt_ref, rank_ref, cnt_ref,
                      wobf_ref, cntacc_ref):
    i = pl.program_id(0)

    @pl.when(i == 0)
    def _():
        _cast_rows(wo_ref, wobf_ref, D_MODEL)
        cntacc_ref[...] = jnp.zeros_like(cntacc_ref)

    nt = (((1,), (1,)), ((), ()))
    w_hi, w_lo = _split_bf16(rwt_ref[...])
    ra = lax.broadcasted_iota(I32, (MIX_ROWS, MIX_ROWS), 0)
    rc = lax.broadcasted_iota(I32, (MIX_ROWS, MIX_ROWS), 1)
    upper = (ra < rc).astype(BF16)
    e_iota = lax.broadcasted_iota(I32, (N_EXPERTS, MIX_ROWS), 0)
    counts = cntacc_ref[...]

    for s in range(MIX_SUBTILES):
        rows = pl.ds(s * MIX_ROWS, MIX_ROWS)
        mixed = jnp.dot(a_ref[rows, :], wobf_ref[0:D_ATTN, :], preferred_element_type=F32)
        mixed = mixed + jnp.dot(c_ref[rows, :], wobf_ref[D_ATTN:D_MODEL, :],
                                preferred_element_type=F32)
        x1 = x_ref[rows, :] + mixed
        x1_ref[rows, :] = x1
        ms = jnp.mean(x1 * x1, axis=-1, keepdims=True)
        t = (x1 * lax.rsqrt(ms + RMS_EPS)) * ng_ref[...]
        t_ref[rows, :] = _pack_rows(t)

        t_hi, t_lo = _split_bf16(t)
        logits = (lax.dot_general(w_hi, t_hi, nt, preferred_element_type=F32)
                  + lax.dot_general(w_hi, t_lo, nt, preferred_element_type=F32)
                  + lax.dot_general(w_lo, t_hi, nt, preferred_element_type=F32)) + rb_ref[...]
        vals, idxs, hots = [], [], []
        for _ in range(TOP_K):
            m = jnp.max(logits, axis=0, keepdims=True)
            am = jnp.min(jnp.where(logits == m, e_iota, N_EXPERTS), axis=0, keepdims=True)
            hot = e_iota == am
            vals.append(m)
            idxs.append(am)
            hots.append(hot)
            logits = jnp.where(hot, -jnp.inf, logits)
        exps = [jnp.exp(v - vals[0]) for v in vals]
        den = exps[0] + exps[1] + exps[2] + exps[3]
        wts = [e / den for e in exps]

        hot_f = (hots[0] | hots[1] | hots[2] | hots[3]).astype(F32)
        prefix = jnp.dot(hot_f.astype(BF16), upper, preferred_element_type=F32)
        base = prefix + counts
        ranks = [jnp.sum(jnp.where(h, base, 0.0), axis=0, keepdims=True) for h in hots]
        counts = counts + jnp.sum(hot_f, axis=1, keepdims=True)

        idx_ref[:, rows] = jnp.concatenate(idxs, axis=0)
        rank_ref[:, rows] = jnp.concatenate(ranks, axis=0).astype(I32)
        w8 = jnp.concatenate(wts + [jnp.zeros((8 - TOP_K, MIX_ROWS), F32)], axis=0)
        wgt_ref[rows, :] = w8.T

    cntacc_ref[...] = counts
    cnt_ref[...] = jnp.broadcast_to(counts, cnt_ref.shape)


def _mix_route(x2, attn2, conv2, w_out, norm_g, router_w, router_b):
    n_tok = x2.shape[0]
    step_rows = MIX_SUBTILES * MIX_ROWS
    row = lambda i: (i, 0)
    const = lambda i: (0, 0)
    vec = lambda v: v.reshape(1, -1)
    return pl.pallas_call(
        _mix_route_kernel,
        grid=(n_tok // step_rows,),
        in_specs=[
            pl.BlockSpec((step_rows, D_MODEL), row),
            pl.BlockSpec((step_rows, D_ATTN), row),
            pl.BlockSpec((step_rows, D_CONV), row),
            pl.BlockSpec((D_MODEL, D_MODEL), const),
            pl.BlockSpec((1, D_MODEL), const),
            pl.BlockSpec((N_EXPERTS, D_MODEL), const),
            pl.BlockSpec((N_EXPERTS, 1), const),
        ],
        out_specs=[
            pl.BlockSpec((step_rows, D_MODEL), row),
            pl.BlockSpec((step_rows, HALF), row),
            pl.BlockSpec((TOP_K, step_rows), lambda i: (0, i)),
            pl.BlockSpec((step_rows, 8), row),
            pl.BlockSpec((TOP_K, step_rows), lambda i: (0, i)),
            pl.BlockSpec((N_EXPERTS, 128), const),
        ],
        out_shape=[
            jax.ShapeDtypeStruct((n_tok, D_MODEL), F32),
            jax.ShapeDtypeStruct((n_tok, HALF), I32),
            jax.ShapeDtypeStruct((TOP_K, n_tok), I32),
            jax.ShapeDtypeStruct((n_tok, 8), F32),
            jax.ShapeDtypeStruct((TOP_K, n_tok), I32),
            jax.ShapeDtypeStruct((N_EXPERTS, 128), F32),
        ],
        scratch_shapes=[
            pltpu.VMEM((D_MODEL, D_MODEL), BF16),
            pltpu.VMEM((N_EXPERTS, 1), F32),
        ],
        compiler_params=pltpu.CompilerParams(
            dimension_semantics=("arbitrary",), vmem_limit_bytes=VMEM_LIMIT),
        name="mix_route",
    )(x2, attn2, conv2, w_out, vec(norm_g), router_w.T, router_b.reshape(N_EXPERTS, 1))


def _tile_plan(counts, n_tiles):
    tiles_per = (counts + EXP_ROWS - 1) // EXP_ROWS
    tile_end = jnp.cumsum(tiles_per)
    tile_begin = tile_end - tiles_per
    n_valid = tile_end[-1]
    tiles = jnp.arange(n_tiles, dtype=I32)
    tile_valid = tiles < n_valid
    capped = jnp.minimum(tiles, n_valid - 1)
    tile_expert = jnp.sum((capped[:, None] >= tile_end[None, :]).astype(I32), axis=1)
    tile_expert = jnp.minimum(tile_expert, N_EXPERTS - 1)
    is_first = (tiles[:, None] == tile_begin[None, :]) & (tiles_per[None, :] > 0)
    tile_first = tile_valid & jnp.any(is_first, axis=1)
    group_start = tile_begin * EXP_ROWS
    experts = jnp.arange(N_EXPERTS, dtype=I32)
    nonempty = tiles_per > 0
    parity = (jnp.cumsum(nonempty.astype(I32)) - 1) % 2
    later = nonempty[None, :] & (experts[None, :] > experts[:, None])
    nxt = jnp.min(jnp.where(later, experts[None, :], N_EXPERTS), axis=1)
    nxt = jnp.where(nxt == N_EXPERTS, -1, nxt)
    hot = (tile_expert[:, None] == experts[None, :]).astype(I32)
    tile_slot = jnp.sum(hot * parity[None, :], axis=1)
    tile_next = jnp.sum(hot * nxt[None, :], axis=1)
    flags = (tile_expert, tile_first.astype(I32), tile_valid.astype(I32),
             tile_slot.astype(I32), tile_next.astype(I32))
    return flags, group_start.astype(I32)


def _slot_kernel(gstart_ref, idx_ref, rank_ref, pos_ref):
    idx = idx_ref[...]
    pos = rank_ref[...]
    for e in range(N_EXPERTS):
        pos = pos + jnp.where(idx == e, gstart_ref[e], 0)
    pos_ref[...] = pos


def _slots(group_start, idx, rank):
    full = pl.BlockSpec(idx.shape, lambda i, gs: (0, 0))
    return pl.pallas_call(
        _slot_kernel,
        grid_spec=pltpu.PrefetchScalarGridSpec(
            num_scalar_prefetch=1, grid=(1,), in_specs=[full, full], out_specs=full),
        out_shape=jax.ShapeDtypeStruct(idx.shape, I32),
        name="slots",
    )(group_start, idx, rank)


def _sc_mesh():
    return plsc.VectorSubcoreMesh(core_axis_name="core", subcore_axis_name="subcore")


def _sc_worker():
    info = plsc.get_sparse_core_info()
    wid = lax.axis_index("subcore") * info.num_cores + lax.axis_index("core")
    return wid, info.num_cores * info.num_subcores


def _dispatch(t2, pos_flat, n_slots):
    n_tok, d = t2.shape
    n_workers = 32
    per_w = n_tok // n_workers
    assert per_w % SC_ROWS == 0

    @functools.partial(
        pl.kernel, mesh=_sc_mesh(),
        out_type=jax.ShapeDtypeStruct((n_slots, d), t2.dtype),
        scratch_types=[pltpu.VMEM((SC_ROWS,), I32), pltpu.VMEM((SC_ROWS, d), t2.dtype)],
        name="dispatch",
    )
    def k(t_hbm, pos_hbm, o_hbm, idx_v, rows_v):
        wid, nw = _sc_worker()
        assert nw == n_workers

        @pl.loop(0, per_w // SC_ROWS)
        def _(c):
            base = pl.multiple_of(wid * per_w + c * SC_ROWS, SC_ROWS)
            pltpu.sync_copy(t_hbm.at[pl.ds(base, SC_ROWS)], rows_v)
            for kk in range(TOP_K):
                pltpu.sync_copy(pos_hbm.at[pl.ds(kk * n_tok + base, SC_ROWS)], idx_v)
                pltpu.sync_copy(rows_v, o_hbm.at[idx_v])

    return k(t2, pos_flat)


def _gather_back(sorted_out, pos_flat):
    n_rows = pos_flat.shape[0]
    d = sorted_out.shape[1]
    n_workers = 32
    per_w = n_rows // n_workers
    assert per_w % SC_ROWS == 0

    @functools.partial(
        pl.kernel, mesh=_sc_mesh(),
        out_type=jax.ShapeDtypeStruct((n_rows, d), sorted_out.dtype),
        scratch_types=[pltpu.VMEM((SC_ROWS,), I32), pltpu.VMEM((SC_ROWS, d), sorted_out.dtype)],
        name="gather_back",
    )
    def k(s_hbm, pos_hbm, o_hbm, idx_v, rows_v):
        wid, nw = _sc_worker()
        assert nw == n_workers

        @pl.loop(0, per_w // SC_ROWS)
        def _(c):
            base = pl.multiple_of(wid * per_w + c * SC_ROWS, SC_ROWS)
            pltpu.sync_copy(pos_hbm.at[pl.ds(base, SC_ROWS)], idx_v)
            pltpu.sync_copy(s_hbm.at[idx_v], rows_v)
            pltpu.sync_copy(rows_v, o_hbm.at[pl.ds(base, SC_ROWS)])

    return k(sorted_out, pos_flat)


def _weight_copies(w_hbm, wf32, sem, expert, slot):
    return [pltpu.make_async_copy(w.at[expert], wf32.at[slot, mtx], sem.at[slot])
            for mtx, w in enumerate(w_hbm)]


def _expert_kernel(texp_ref, tfirst_ref, tvalid_ref, tslot_ref, tnext_ref,
                   x_ref, wg_hbm, bg_ref, wu_hbm, bu_ref, wd_hbm, bd_ref,
                   o_ref, wf32, wbf, wsem):
    i = pl.program_id(0)
    w_hbm = (wg_hbm, wu_hbm, wd_hbm)

    @pl.when(tvalid_ref[i] == 0)
    def _():
        o_ref[...] = jnp.zeros_like(o_ref)

    @pl.when(tvalid_ref[i] == 1)
    def _():
        @pl.when(tfirst_ref[i] == 1)
        def _():
            slot = tslot_ref[i]
            expert = texp_ref[i]

            @pl.when(i == 0)
            def _():
                for cp in _weight_copies(w_hbm, wf32, wsem, expert, slot):
                    cp.start()

            for cp in _weight_copies(w_hbm, wf32, wsem, expert, slot):
                cp.wait()

            @pl.when(tnext_ref[i] >= 0)
            def _():
                for cp in _weight_copies(w_hbm, wf32, wsem, tnext_ref[i], 1 - slot):
                    cp.start()

            for mtx in range(3):
                _cast_rows(wf32.at[slot, mtx], wbf.at[mtx], D_MODEL)

        xs = jnp.concatenate(_unpack_rows(x_ref[...]), axis=1).astype(BF16)
        acc = None
        cn = 256
        for c in range(D_MODEL // cn):
            sl = slice(c * cn, (c + 1) * cn)
            g = jnp.dot(xs, wbf[0, :, sl], preferred_element_type=F32) + bg_ref[:, sl]
            u = jnp.dot(xs, wbf[1, :, sl], preferred_element_type=F32) + bu_ref[:, sl]
            g = jnp.minimum(g, SWIGLU_LIMIT)
            u = jnp.clip(u, -SWIGLU_LIMIT, SWIGLU_LIMIT)
            hid = ((u + 1.0) * (g * jax.nn.sigmoid(SWIGLU_ALPHA * g))).astype(BF16)
            part = jnp.dot(hid, wbf[2, sl, :], preferred_element_type=F32)
            acc = part if acc is None else acc + part
        o_ref[...] = _pack_rows(acc + bd_ref[...])


def _experts(sorted_t, tile_flags, w_gate, b_gate, w_up, b_up, w_down, b_down, n_tiles):
    row_blk = pl.BlockSpec((EXP_ROWS, HALF), lambda i, *_: (i, 0))
    w_any = pl.BlockSpec(memory_space=pl.ANY)
    b_blk = pl.BlockSpec((None, 1, D_MODEL), lambda i, te, *_: (te[i], 0, 0))
    grid_spec = pltpu.PrefetchScalarGridSpec(
        num_scalar_prefetch=len(tile_flags),
        grid=(n_tiles,),
        in_specs=[row_blk, w_any, b_blk, w_any, b_blk, w_any, b_blk],
        out_specs=row_blk,
        scratch_shapes=[
            pltpu.VMEM((2, 3, D_MODEL, D_MODEL), F32),
            pltpu.VMEM((3, D_MODEL, D_MODEL), BF16),
            pltpu.SemaphoreType.DMA((2,)),
        ],
    )
    b3 = lambda b: b.reshape(N_EXPERTS, 1, D_MODEL)
    return pl.pallas_call(
        _expert_kernel,
        grid_spec=grid_spec,
        out_shape=jax.ShapeDtypeStruct((n_tiles * EXP_ROWS, HALF), I32),
        compiler_params=pltpu.CompilerParams(
            dimension_semantics=("arbitrary",), vmem_limit_bytes=VMEM_LIMIT),
        name="experts",
    )(*tile_flags, sorted_t,
      w_gate, b3(b_gate), w_up, b3(b_up), w_down, b3(b_down))


def _combine_kernel(x1_ref, s0_ref, s1_ref, s2_ref, s3_ref, w_ref, g_ref, o_ref):
    w = w_ref[...]
    y_lo = x1_ref[:, :HALF]
    y_hi = x1_ref[:, HALF:]
    for k, s_ref in enumerate((s0_ref, s1_ref, s2_ref, s3_ref)):
        lo, hi = _unpack_rows(s_ref[...])
        y_lo = y_lo + w[:, k:k + 1] * lo
        y_hi = y_hi + w[:, k:k + 1] * hi
    sq = jnp.sum(y_lo * y_lo, axis=-1, keepdims=True) + jnp.sum(y_hi * y_hi, axis=-1, keepdims=True)
    scale = lax.rsqrt(sq * (1.0 / D_MODEL) + RMS_EPS)
    o_ref[:, :HALF] = (y_lo * scale) * g_ref[:, :HALF]
    o_ref[:, HALF:] = (y_hi * scale) * g_ref[:, HALF:]


def _combine(x1, slabs, wgt, norm_g):
    n_tok = x1.shape[0]
    rows = MIX_ROWS
    ntiles = n_tok // rows
    slab_spec = lambda k: pl.BlockSpec((rows, HALF), lambda i, k=k: (k * ntiles + i, 0))
    return pl.pallas_call(
        _combine_kernel,
        grid=(ntiles,),
        in_specs=[pl.BlockSpec((rows, D_MODEL), lambda i: (i, 0))]
        + [slab_spec(k) for k in range(TOP_K)]
        + [pl.BlockSpec((rows, 8), lambda i: (i, 0)),
           pl.BlockSpec((1, D_MODEL), lambda i: (0, 0))],
        out_specs=pl.BlockSpec((rows, D_MODEL), lambda i: (i, 0)),
        out_shape=jax.ShapeDtypeStruct((n_tok, D_MODEL), F32),
        compiler_params=pltpu.CompilerParams(
            dimension_semantics=("arbitrary",), vmem_limit_bytes=VMEM_LIMIT),
        name="combine",
    )(x1, slabs, slabs, slabs, slabs, wgt, norm_g.reshape(1, D_MODEL))


def kernel(x, norm_mix_g, w_in, conv_dw_w, conv_dw_b, conv_ln_g, conv_ln_b, rel_bias, w_out,
           norm_ffn_g, router_w, router_b, exp_w_gate, exp_b_gate, exp_w_up, exp_b_up,
           exp_w_down, exp_b_down, norm_final_g):
    bsz, seq, _ = x.shape
    n_tok = bsz * seq
    assert norm_mix_g.shape[0] == 1, "single-layer block"
    assert seq % IN_ROWS == 0 and seq % MIX_ROWS == 0 and IN_ROWS == LEFT_PAD
    n_tiles = (TOP_K * n_tok) // EXP_ROWS + N_EXPERTS - 1

    q, kpad, vpad, conv = _inproj(x, norm_mix_g[0], w_in[0], conv_dw_w[0], conv_dw_b[0],
                                  conv_ln_g[0], conv_ln_b[0])
    attn = _attention(q, kpad, vpad, _band_bias(rel_bias[0]))
    x1, t, idx, wgt, rank, cnt = _mix_route(
        x.reshape(n_tok, D_MODEL), attn.reshape(n_tok, D_ATTN), conv.reshape(n_tok, D_CONV),
        w_out[0], norm_ffn_g[0], router_w[0], router_b[0])
    tile_flags, group_start = _tile_plan(cnt[:, 0].astype(I32), n_tiles)
    pos_flat = _slots(group_start, idx, rank).reshape(TOP_K * n_tok)
    sorted_t = _dispatch(t, pos_flat, n_tiles * EXP_ROWS)
    sorted_out = _experts(sorted_t, tile_flags,
                          exp_w_gate[0], exp_b_gate[0], exp_w_up[0], exp_b_up[0],
                          exp_w_down[0], exp_b_down[0], n_tiles)
    slabs = _gather_back(sorted_out, pos_flat)
    out = _combine(x1, slabs, wgt, norm_final_g)
    return out.reshape(bsz, seq, D_MODEL)
```

```python
import functools

import jax
import jax.numpy as jnp
from jax import lax
from jax.experimental import pallas as pl
from jax.experimental.pallas import tpu as pltpu
from jax.experimental.pallas import tpu_sc as plsc

F32 = jnp.float32
BF16 = jnp.bfloat16
I32 = jnp.int32

D_MODEL = 1024
CHUNK = 64
N_HEADS = 8
HEAD_DIM = 64
D_ATTN = N_HEADS * HEAD_DIM
LEFT_CHUNKS = 8
REL_MAX = 128
REL_MIN = -(CHUNK - 1)
D_CONV = D_MODEL - D_ATTN
CONV_WIDTH = 31
N_EXPERTS = 32
TOP_K = 4
SWIGLU_ALPHA = 1.702
SWIGLU_LIMIT = 7.0
RMS_EPS = 1e-5
LN_EPS = 1e-5

LEFT_PAD = LEFT_CHUNKS * CHUNK
IN_ROWS = 512
Q_ROWS = 2 * CHUNK
BAND_ROWS = Q_ROWS + LEFT_PAD
HEADS_PER_STEP = 4
GROUP_LANES = HEADS_PER_STEP * HEAD_DIM
MIX_ROWS = 512
MIX_SUBTILES = 1
HALO_ROWS = 32
EXP_ROWS = 512
SC_ROWS = 128
NEG_BIG = -1e30
VMEM_LIMIT = 56 * 1024 * 1024


HALF = D_MODEL // 2
HI_MASK = -65536


def _pack_rows(x):
    bits = lax.bitcast_convert_type(x.astype(BF16).astype(F32), I32)
    return lax.shift_right_logical(bits[:, :HALF], 16) | (bits[:, HALF:] & HI_MASK)


def _unpack_rows(w):
    lo = lax.bitcast_convert_type(lax.shift_left(w, 16), F32)
    hi = lax.bitcast_convert_type(w & HI_MASK, F32)
    return lo, hi


def _cast_rows(src_ref, dst_ref, rows, step=128):
    def body(c, carry):
        r = pl.multiple_of(c * step, step)
        dst_ref[pl.ds(r, step), :] = src_ref[pl.ds(r, step), :].astype(dst_ref.dtype)
        return carry
    lax.fori_loop(0, rows // step, body, 0)


def _conv_branch(hw_ref, sh_ref, rows, cw_ref, cb_ref, lg_ref, lb_ref, out_ref):
    off = HALO_ROWS - (CONV_WIDTH - 1)
    shift_rows = HALO_ROWS + rows - 8
    for b in range(1, 8):
        sh_ref[b - 1] = hw_ref[pl.ds(b, shift_rows), :]
    acc = None
    for j in range(CONV_WIDTH):
        a, b = divmod(off + j, 8)
        src = hw_ref if b == 0 else sh_ref.at[b - 1]
        term = src[pl.ds(8 * a, rows), :] * cw_ref[j:j + 1, :]
        acc = term if acc is None else acc + term
    acc = acc + cb_ref[...]
    mu = jnp.mean(acc, axis=-1, keepdims=True)
    d = acc - mu
    var = jnp.mean(d * d, axis=-1, keepdims=True)
    y = d * lax.rsqrt(var + LN_EPS) * lg_ref[...] + lb_ref[...]
    out_ref[...] = (y * jax.nn.sigmoid(y)).astype(out_ref.dtype)


def _inproj_kernel(x_ref, g_ref, w_ref, cw_ref, cb_ref, lg_ref, lb_ref,
                   q_ref, k_ref, v_ref, c_ref, wbf_ref, hw_ref, sh_ref):
    b = pl.program_id(0)
    j = pl.program_id(1)

    @pl.when((b == 0) & (j == 0))
    def _():
        _cast_rows(w_ref, wbf_ref, D_MODEL)

    @pl.when(j == 0)
    def _():
        k_ref[...] = jnp.zeros_like(k_ref)
        v_ref[...] = jnp.zeros_like(v_ref)
        hw_ref[IN_ROWS:IN_ROWS + HALO_ROWS, :] = jnp.zeros((HALO_ROWS, D_CONV), F32)

    @pl.when(j > 0)
    def _():
        x = x_ref[...]
        ms = jnp.mean(x * x, axis=-1, keepdims=True)
        hb = ((x * lax.rsqrt(ms + RMS_EPS)) * g_ref[...]).astype(BF16)

        def proj(c0, width):
            return jnp.dot(hb, wbf_ref[:, c0:c0 + width], preferred_element_type=F32)

        a = proj(3 * D_ATTN, D_CONV)
        gate = proj(3 * D_ATTN + D_CONV, D_CONV)
        hw_ref[0:HALO_ROWS, :] = hw_ref[IN_ROWS:IN_ROWS + HALO_ROWS, :]
        hw_ref[HALO_ROWS:HALO_ROWS + IN_ROWS, :] = a * jax.nn.sigmoid(gate)
        _conv_branch(hw_ref, sh_ref, IN_ROWS, cw_ref, cb_ref, lg_ref, lb_ref, c_ref)

        q_ref[...] = (proj(0, D_ATTN) * (HEAD_DIM ** -0.5)).astype(BF16)
        k_ref[...] = proj(D_ATTN, D_ATTN).astype(BF16)
        v_ref[...] = proj(2 * D_ATTN, D_ATTN).astype(BF16)


def _inproj(x, norm_g, w_in, conv_w, conv_b, ln_g, ln_b):
    bsz, seq, _ = x.shape
    nblk = seq // IN_ROWS
    d_cols = w_in.shape[1]
    row_blk = lambda b, j: (b, jnp.maximum(j - 1, 0), 0)
    const = lambda b, j: (0, 0)
    vec = lambda v: v.reshape(1, -1)
    return pl.pallas_call(
        _inproj_kernel,
        grid=(bsz, nblk + 1),
        in_specs=[
            pl.BlockSpec((None, IN_ROWS, D_MODEL), row_blk),
            pl.BlockSpec((1, D_MODEL), const),
            pl.BlockSpec((D_MODEL, d_cols), const),
            pl.BlockSpec((CONV_WIDTH, D_CONV), const),
            pl.BlockSpec((1, D_CONV), const),
            pl.BlockSpec((1, D_CONV), const),
            pl.BlockSpec((1, D_CONV), const),
        ],
        out_specs=[
            pl.BlockSpec((None, IN_ROWS, D_ATTN), row_blk),
            pl.BlockSpec((None, IN_ROWS, D_ATTN), lambda b, j: (b, j, 0)),
            pl.BlockSpec((None, IN_ROWS, D_ATTN), lambda b, j: (b, j, 0)),
            pl.BlockSpec((None, IN_ROWS, D_CONV), row_blk),
        ],
        out_shape=[
            jax.ShapeDtypeStruct((bsz, seq, D_ATTN), BF16),
            jax.ShapeDtypeStruct((bsz, seq + LEFT_PAD, D_ATTN), BF16),
            jax.ShapeDtypeStruct((bsz, seq + LEFT_PAD, D_ATTN), BF16),
            jax.ShapeDtypeStruct((bsz, seq, D_CONV), BF16),
        ],
        scratch_shapes=[
            pltpu.VMEM((D_MODEL, d_cols), BF16),
            pltpu.VMEM((HALO_ROWS + IN_ROWS, D_CONV), F32),
            pltpu.VMEM((7, HALO_ROWS + IN_ROWS - 8, D_CONV), F32),
        ],
        compiler_params=pltpu.CompilerParams(
            dimension_semantics=("arbitrary", "arbitrary"), vmem_limit_bytes=VMEM_LIMIT),
        name="inproj",
    )(x, vec(norm_g), w_in, conv_w, vec(conv_b), vec(ln_g), vec(ln_b))


def _attn_kernel(q_ref, k_ref, v_ref, bias_ref, o_ref):
    i = pl.program_id(1)
    start = pl.multiple_of(i * Q_ROWS, Q_ROWS)
    lane = lax.broadcasted_iota(I32, (Q_ROWS, GROUP_LANES), 1) // HEAD_DIM
    col = lax.broadcasted_iota(I32, (HEADS_PER_STEP * Q_ROWS, BAND_ROWS), 1)
    key_ok = col >= LEFT_PAD - i * Q_ROWS
    for g in range(N_HEADS // HEADS_PER_STEP):
        lanes = slice(g * GROUP_LANES, (g + 1) * GROUP_LANES)
        q = q_ref[:, lanes]
        qs = jnp.concatenate(
            [jnp.where(lane == h, q, jnp.zeros_like(q)) for h in range(HEADS_PER_STEP)], axis=0)
        kb = k_ref[pl.ds(start, BAND_ROWS), lanes]
        vb = v_ref[pl.ds(start, BAND_ROWS), lanes]
        s = lax.dot_general(qs, kb, (((1,), (1,)), ((), ())), preferred_element_type=F32)
        rows = slice(g * HEADS_PER_STEP * Q_ROWS, (g + 1) * HEADS_PER_STEP * Q_ROWS)
        s = jnp.where(key_ok, s + bias_ref[rows, :], NEG_BIG)
        m = jnp.max(s, axis=-1, keepdims=True)
        p = jnp.exp(s - m)
        l = jnp.sum(p, axis=-1, keepdims=True)
        o = jnp.dot(p.astype(BF16), vb, preferred_element_type=F32) / l
        out = o[0:Q_ROWS]
        for h in range(1, HEADS_PER_STEP):
            out = jnp.where(lane == h, o[h * Q_ROWS:(h + 1) * Q_ROWS], out)
        o_ref[:, lanes] = out.astype(o_ref.dtype)


def _attention(q, kpad, vpad, bias):
    bsz, seq, _ = q.shape
    nq = seq // Q_ROWS
    return pl.pallas_call(
        _attn_kernel,
        grid=(bsz, nq),
        in_specs=[
            pl.BlockSpec((None, Q_ROWS, D_ATTN), lambda b, i: (b, i, 0)),
            pl.BlockSpec((None, seq + LEFT_PAD, D_ATTN), lambda b, i: (b, 0, 0)),
            pl.BlockSpec((None, seq + LEFT_PAD, D_ATTN), lambda b, i: (b, 0, 0)),
            pl.BlockSpec((N_HEADS * Q_ROWS, BAND_ROWS), lambda b, i: (0, 0)),
        ],
        out_specs=pl.BlockSpec((None, Q_ROWS, D_ATTN), lambda b, i: (b, i, 0)),
        out_shape=jax.ShapeDtypeStruct((bsz, seq, D_ATTN), BF16),
        compiler_params=pltpu.CompilerParams(
            dimension_semantics=("arbitrary", "arbitrary"), vmem_limit_bytes=VMEM_LIMIT),
        name="chunk_attn",
    )(q, kpad, vpad, bias)


def _band_bias(rel_bias):
    n_rel = REL_MAX - REL_MIN + 1
    far = jnp.broadcast_to(rel_bias[:, n_rel - 1:n_rel], (N_HEADS, BAND_ROWS - 1 - REL_MAX))
    near = rel_bias[:, ::-1]
    ahead = jnp.broadcast_to(rel_bias[:, 0:1], (N_HEADS, Q_ROWS - 1 + REL_MIN))
    diag = jnp.concatenate([far, near, ahead], axis=1).astype(F32)
    bias = jnp.stack(
        [diag[:, Q_ROWS - 1 - r:Q_ROWS - 1 - r + BAND_ROWS] for r in range(Q_ROWS)], axis=1)
    r = jnp.arange(Q_ROWS)[:, None]
    m = jnp.arange(BAND_ROWS)[None, :]
    cq = r // CHUNK
    ck = m // CHUNK
    in_band = (ck >= cq) & (ck <= cq + LEFT_CHUNKS)
    bias = jnp.where(in_band[None], bias, NEG_BIG)
    return bias.reshape(N_HEADS * Q_ROWS, BAND_ROWS)


def _split_bf16(v):
    hi = v.astype(BF16)
    lo = (v - hi.astype(F32)).astype(BF16)
    return hi, lo


def _mix_route_kernel(x_ref, a_ref, c_ref, wo_ref, ng_ref, rwt_ref, rb_ref,
                      x1_ref, t_ref, idx_ref, wgt_ref, rank_ref, cnt_ref,
                      wobf_ref, cntacc_ref):
    i = pl.program_id(0)

    @pl.when(i == 0)
    def _():
        _cast_rows(wo_ref, wobf_ref, D_MODEL)
        cntacc_ref[...] = jnp.zeros_like(cntacc_ref)

    nt = (((1,), (1,)), ((), ()))
    w_hi, w_lo = _split_bf16(rwt_ref[...])
    ra = lax.broadcasted_iota(I32, (MIX_ROWS, MIX_ROWS), 0)
    rc = lax.broadcasted_iota(I32, (MIX_ROWS, MIX_ROWS), 1)
    upper = (ra < rc).astype(BF16)
    e_iota = lax.broadcasted_iota(I32, (N_EXPERTS, MIX_ROWS), 0)
    counts = cntacc_ref[...]

    for s in range(MIX_SUBTILES):
        rows = pl.ds(s * MIX_ROWS, MIX_ROWS)
        mixed = jnp.dot(a_ref[rows, :], wobf_ref[0:D_ATTN, :], preferred_element_type=F32)
        mixed = mixed + jnp.dot(c_ref[rows, :], wobf_ref[D_ATTN:D_MODEL, :],
                                preferred_element_type=F32)
        x1 = x_ref[rows, :] + mixed
        x1_ref[rows, :] = x1
        ms = jnp.mean(x1 * x1, axis=-1, keepdims=True)
        t = (x1 * lax.rsqrt(ms + RMS_EPS)) * ng_ref[...]
        t_ref[rows, :] = _pack_rows(t)

        t_hi, t_lo = _split_bf16(t)
        logits = (lax.dot_general(w_hi, t_hi, nt, preferred_element_type=F32)
                  + lax.dot_general(w_hi, t_lo, nt, preferred_element_type=F32)
                  + lax.dot_general(w_lo, t_hi, nt, preferred_element_type=F32)) + rb_ref[...]
        vals, idxs, hots = [], [], []
        for _ in range(TOP_K):
            m = jnp.max(logits, axis=0, keepdims=True)
            am = jnp.min(jnp.where(logits == m, e_iota, N_EXPERTS), axis=0, keepdims=True)
            hot = e_iota == am
            vals.append(m)
            idxs.append(am)
            hots.append(hot)
            logits = jnp.where(hot, -jnp.inf, logits)
        exps = [jnp.exp(v - vals[0]) for v in vals]
        den = exps[0] + exps[1] + exps[2] + exps[3]
        wts = [e / den for e in exps]

        hot_f = (hots[0] | hots[1] | hots[2] | hots[3]).astype(F32)
        prefix = jnp.dot(hot_f.astype(BF16), upper, preferred_element_type=F32)
        base = prefix + counts
        ranks = [jnp.sum(jnp.where(h, base, 0.0), axis=0, keepdims=True) for h in hots]
        counts = counts + jnp.sum(hot_f, axis=1, keepdims=True)

        idx_ref[:, rows] = jnp.concatenate(idxs, axis=0)
        rank_ref[:, rows] = jnp.concatenate(ranks, axis=0).astype(I32)
        w8 = jnp.concatenate(wts + [jnp.zeros((8 - TOP_K, MIX_ROWS), F32)], axis=0)
        wgt_ref[rows, :] = w8.T

    cntacc_ref[...] = counts
    cnt_ref[...] = jnp.broadcast_to(counts, cnt_ref.shape)


def _mix_route(x2, attn2, conv2, w_out, norm_g, router_w, router_b):
    n_tok = x2.shape[0]
    step_rows = MIX_SUBTILES * MIX_ROWS
    row = lambda i: (i, 0)
    const = lambda i: (0, 0)
    vec = lambda v: v.reshape(1, -1)
    return pl.pallas_call(
        _mix_route_kernel,
        grid=(n_tok // step_rows,),
        in_specs=[
            pl.BlockSpec((step_rows, D_MODEL), row),
            pl.BlockSpec((step_rows, D_ATTN), row),
            pl.BlockSpec((step_rows, D_CONV), row),
            pl.BlockSpec((D_MODEL, D_MODEL), const),
            pl.BlockSpec((1, D_MODEL), const),
            pl.BlockSpec((N_EXPERTS, D_MODEL), const),
            pl.BlockSpec((N_EXPERTS, 1), const),
        ],
        out_specs=[
            pl.BlockSpec((step_rows, D_MODEL), row),
            pl.BlockSpec((step_rows, HALF), row),
            pl.BlockSpec((TOP_K, step_rows), lambda i: (0, i)),
            pl.BlockSpec((step_rows, 8), row),
            pl.BlockSpec((TOP_K, step_rows), lambda i: (0, i)),
            pl.BlockSpec((N_EXPERTS, 128), const),
        ],
        out_shape=[
            jax.ShapeDtypeStruct((n_tok, D_MODEL), F32),
            jax.ShapeDtypeStruct((n_tok, HALF), I32),
            jax.ShapeDtypeStruct((TOP_K, n_tok), I32),
            jax.ShapeDtypeStruct((n_tok, 8), F32),
            jax.ShapeDtypeStruct((TOP_K, n_tok), I32),
            jax.ShapeDtypeStruct((N_EXPERTS, 128), F32),
        ],
        scratch_shapes=[
            pltpu.VMEM((D_MODEL, D_MODEL), BF16),
            pltpu.VMEM((N_EXPERTS, 1), F32),
        ],
        compiler_params=pltpu.CompilerParams(
            dimension_semantics=("arbitrary",), vmem_limit_bytes=VMEM_LIMIT),
        name="mix_route",
    )(x2, attn2, conv2, w_out, vec(norm_g), router_w.T, router_b.reshape(N_EXPERTS, 1))


def _tile_plan(counts, n_tiles):
    tiles_per = (counts + EXP_ROWS - 1) // EXP_ROWS
    tile_end = jnp.cumsum(tiles_per)
    tile_begin = tile_end - tiles_per
    n_valid = tile_end[-1]
    tiles = jnp.arange(n_tiles, dtype=I32)
    tile_valid = tiles < n_valid
    capped = jnp.minimum(tiles, n_valid - 1)
    tile_expert = jnp.sum((capped[:, None] >= tile_end[None, :]).astype(I32), axis=1)
    tile_expert = jnp.minimum(tile_expert, N_EXPERTS - 1)
    is_first = (tiles[:, None] == tile_begin[None, :]) & (tiles_per[None, :] > 0)
    tile_first = tile_valid & jnp.any(is_first, axis=1)
    group_start = tile_begin * EXP_ROWS
    experts = jnp.arange(N_EXPERTS, dtype=I32)
    nonempty = tiles_per > 0
    parity = (jnp.cumsum(nonempty.astype(I32)) - 1) % 2
    later = nonempty[None, :] & (experts[None, :] > experts[:, None])
    nxt = jnp.min(jnp.where(later, experts[None, :], N_EXPERTS), axis=1)
    nxt = jnp.where(nxt == N_EXPERTS, -1, nxt)
    hot = (tile_expert[:, None] == experts[None, :]).astype(I32)
    tile_slot = jnp.sum(hot * parity[None, :], axis=1)
    tile_next = jnp.sum(hot * nxt[None, :], axis=1)
    flags = (tile_expert, tile_first.astype(I32), tile_valid.astype(I32),
             tile_slot.astype(I32), tile_next.astype(I32))
    return flags, group_start.astype(I32)


def _slot_kernel(gstart_ref, idx_ref, rank_ref, pos_ref):
    idx = idx_ref[...]
    pos = rank_ref[...]
    for e in range(N_EXPERTS):
        pos = pos + jnp.where(idx == e, gstart_ref[e], 0)
    pos_ref[...] = pos


def _slots(group_start, idx, rank):
    full = pl.BlockSpec(idx.shape, lambda i, gs: (0, 0))
    return pl.pallas_call(
        _slot_kernel,
        grid_spec=pltpu.PrefetchScalarGridSpec(
            num_scalar_prefetch=1, grid=(1,), in_specs=[full, full], out_specs=full),
        out_shape=jax.ShapeDtypeStruct(idx.shape, I32),
        name="slots",
    )(group_start, idx, rank)


def _sc_mesh():
    return plsc.VectorSubcoreMesh(core_axis_name="core", subcore_axis_name="subcore")


def _sc_worker():
    info = plsc.get_sparse_core_info()
    wid = lax.axis_index("subcore") * info.num_cores + lax.axis_index("core")
    return wid, info.num_cores * info.num_subcores


def _dispatch(t2, pos_flat, n_slots):
    n_tok, d = t2.shape
    n_workers = 32
    per_w = n_tok // n_workers
    assert per_w % SC_ROWS == 0

    @functools.partial(
        pl.kernel, mesh=_sc_mesh(),
        out_type=jax.ShapeDtypeStruct((n_slots, d), t2.dtype),
        scratch_types=[pltpu.VMEM((SC_ROWS,), I32), pltpu.VMEM((SC_ROWS, d), t2.dtype)],
        name="dispatch",
    )
    def k(t_hbm, pos_hbm, o_hbm, idx_v, rows_v):
        wid, nw = _sc_worker()
        assert nw == n_workers

        @pl.loop(0, per_w // SC_ROWS)
        def _(c):
            base = pl.multiple_of(wid * per_w + c * SC_ROWS, SC_ROWS)
            pltpu.sync_copy(t_hbm.at[pl.ds(base, SC_ROWS)], rows_v)
            for kk in range(TOP_K):
                pltpu.sync_copy(pos_hbm.at[pl.ds(kk * n_tok + base, SC_ROWS)], idx_v)
                pltpu.sync_copy(rows_v, o_hbm.at[idx_v])

    return k(t2, pos_flat)


def _gather_back(sorted_out, pos_flat):
    n_rows = pos_flat.shape[0]
    d = sorted_out.shape[1]
    n_workers = 32
    per_w = n_rows // n_workers
    assert per_w % SC_ROWS == 0

    @functools.partial(
        pl.kernel, mesh=_sc_mesh(),
        out_type=jax.ShapeDtypeStruct((n_rows, d), sorted_out.dtype),
        scratch_types=[pltpu.VMEM((SC_ROWS,), I32), pltpu.VMEM((SC_ROWS, d), sorted_out.dtype)],
        name="gather_back",
    )
    def k(s_hbm, pos_hbm, o_hbm, idx_v, rows_v):
        wid, nw = _sc_worker()
        assert nw == n_workers

        @pl.loop(0, per_w // SC_ROWS)
        def _(c):
            base = pl.multiple_of(wid * per_w + c * SC_ROWS, SC_ROWS)
            pltpu.sync_copy(pos_hbm.at[pl.ds(base, SC_ROWS)], idx_v)
            pltpu.sync_copy(s_hbm.at[idx_v], rows_v)
            pltpu.sync_copy(rows_v, o_hbm.at[pl.ds(base, SC_ROWS)])

    return k(sorted_out, pos_flat)


def _weight_copies(w_hbm, wf32, sem, expert, slot):
    return [pltpu.make_async_copy(w.at[expert], wf32.at[slot, mtx], sem.at[slot])
            for mtx, w in enumerate(w_hbm)]


def _expert_kernel(texp_ref, tfirst_ref, tvalid_ref, tslot_ref, tnext_ref,
                   x_ref, wg_hbm, bg_ref, wu_hbm, bu_ref, wd_hbm, bd_ref,
                   o_ref, wf32, wbf, wsem):
    i = pl.program_id(0)
    w_hbm = (wg_hbm, wu_hbm, wd_hbm)

    @pl.when(tvalid_ref[i] == 0)
    def _():
        o_ref[...] = jnp.zeros_like(o_ref)

    @pl.when(tvalid_ref[i] == 1)
    def _():
        @pl.when(tfirst_ref[i] == 1)
        def _():
            slot = tslot_ref[i]
            expert = texp_ref[i]

            @pl.when(i == 0)
            def _():
                for cp in _weight_copies(w_hbm, wf32, wsem, expert, slot):
                    cp.start()

            for cp in _weight_copies(w_hbm, wf32, wsem, expert, slot):
                cp.wait()

            @pl.when(tnext_ref[i] >= 0)
            def _():
                for cp in _weight_copies(w_hbm, wf32, wsem, tnext_ref[i], 1 - slot):
                    cp.start()

            for mtx in range(3):
                _cast_rows(wf32.at[slot, mtx], wbf.at[mtx], D_MODEL)

        xs = jnp.concatenate(_unpack_rows(x_ref[...]), axis=1).astype(BF16)
        acc = None
        cn = 256
        for c in range(D_MODEL // cn):
            sl = slice(c * cn, (c + 1) * cn)
            g = jnp.dot(xs, wbf[0, :, sl], preferred_element_type=F32) + bg_ref[:, sl]
            u = jnp.dot(xs, wbf[1, :, sl], preferred_element_type=F32) + bu_ref[:, sl]
            g = jnp.minimum(g, SWIGLU_LIMIT)
            u = jnp.clip(u, -SWIGLU_LIMIT, SWIGLU_LIMIT)
            hid = ((u + 1.0) * (g * jax.nn.sigmoid(SWIGLU_ALPHA * g))).astype(BF16)
            part = jnp.dot(hid, wbf[2, sl, :], preferred_element_type=F32)
            acc = part if acc is None else acc + part
        o_ref[...] = _pack_rows(acc + bd_ref[...])


def _experts(sorted_t, tile_flags, w_gate, b_gate, w_up, b_up, w_down, b_down, n_tiles):
    row_blk = pl.BlockSpec((EXP_ROWS, HALF), lambda i, *_: (i, 0))
    w_any = pl.BlockSpec(memory_space=pl.ANY)
    b_blk = pl.BlockSpec((None, 1, D_MODEL), lambda i, te, *_: (te[i], 0, 0))
    grid_spec = pltpu.PrefetchScalarGridSpec(
        num_scalar_prefetch=len(tile_flags),
        grid=(n_tiles,),
        in_specs=[row_blk, w_any, b_blk, w_any, b_blk, w_any, b_blk],
        out_specs=row_blk,
        scratch_shapes=[
            pltpu.VMEM((2, 3, D_MODEL, D_MODEL), F32),
            pltpu.VMEM((3, D_MODEL, D_MODEL), BF16),
            pltpu.SemaphoreType.DMA((2,)),
        ],
    )
    b3 = lambda b: b.reshape(N_EXPERTS, 1, D_MODEL)
    return pl.pallas_call(
        _expert_kernel,
        grid_spec=grid_spec,
        out_shape=jax.ShapeDtypeStruct((n_tiles * EXP_ROWS, HALF), I32),
        compiler_params=pltpu.CompilerParams(
            dimension_semantics=("arbitrary",), vmem_limit_bytes=VMEM_LIMIT),
        name="experts",
    )(*tile_flags, sorted_t,
      w_gate, b3(b_gate), w_up, b3(b_up), w_down, b3(b_down))


def _combine_kernel(x1_ref, s0_ref, s1_ref, s2_ref, s3_ref, w_ref, g_ref, o_ref):
    w = w_ref[...]
    y_lo = x1_ref[:, :HALF]
    y_hi = x1_ref[:, HALF:]
    for k, s_ref in enumerate((s0_ref, s1_ref, s2_ref, s3_ref)):
        lo, hi = _unpack_rows(s_ref[...])
        y_lo = y_lo + w[:, k:k + 1] * lo
        y_hi = y_hi + w[:, k:k + 1] * hi
    sq = jnp.sum(y_lo * y_lo, axis=-1, keepdims=True) + jnp.sum(y_hi * y_hi, axis=-1, keepdims=True)
    scale = lax.rsqrt(sq * (1.0 / D_MODEL) + RMS_EPS)
    o_ref[:, :HALF] = (y_lo * scale) * g_ref[:, :HALF]
    o_ref[:, HALF:] = (y_hi * scale) * g_ref[:, HALF:]


def _combine(x1, slabs, wgt, norm_g):
    n_tok = x1.shape[0]
    rows = MIX_ROWS
    ntiles = n_tok // rows
    slab_spec = lambda k: pl.BlockSpec((rows, HALF), lambda i, k=k: (k * ntiles + i, 0))
    return pl.pallas_call(
        _combine_kernel,
        grid=(ntiles,),
        in_specs=[pl.BlockSpec((rows, D_MODEL), lambda i: (i, 0))]
        + [slab_spec(k) for k in range(TOP_K)]
        + [pl.BlockSpec((rows, 8), lambda i: (i, 0)),
           pl.BlockSpec((1, D_MODEL), lambda i: (0, 0))],
        out_specs=pl.BlockSpec((rows, D_MODEL), lambda i: (i, 0)),
        out_shape=jax.ShapeDtypeStruct((n_tok, D_MODEL), F32),
        compiler_params=pltpu.CompilerParams(
            dimension_semantics=("arbitrary",), vmem_limit_bytes=VMEM_LIMIT),
        name="combine",
    )(x1, slabs, slabs, slabs, slabs, wgt, norm_g.reshape(1, D_MODEL))


def kernel(x, norm_mix_g, w_in, conv_dw_w, conv_dw_b, conv_ln_g, conv_ln_b, rel_bias, w_out,
           norm_ffn_g, router_w, router_b, exp_w_gate, exp_b_gate, exp_w_up, exp_b_up,
           exp_w_down, exp_b_down, norm_final_g):
    bsz, seq, _ = x.shape
    n_tok = bsz * seq
    assert norm_mix_g.shape[0] == 1, "single-layer block"
    assert seq % IN_ROWS == 0 and seq % MIX_ROWS == 0 and IN_ROWS == LEFT_PAD
    n_tiles = (TOP_K * n_tok) // EXP_ROWS + N_EXPERTS - 1

    q, kpad, vpad, conv = _inproj(x, norm_mix_g[0], w_in[0], conv_dw_w[0], conv_dw_b[0],
                                  conv_ln_g[0], conv_ln_b[0])
    attn = _attention(q, kpad, vpad, _band_bias(rel_bias[0]))
    x1, t, idx, wgt, rank, cnt = _mix_route(
        x.reshape(n_tok, D_MODEL), attn.reshape(n_tok, D_ATTN), conv.reshape(n_tok, D_CONV),
        w_out[0], norm_ffn_g[0], router_w[0], router_b[0])
    tile_flags, group_start = _tile_plan(cnt[:, 0].astype(I32), n_tiles)
    pos_flat = _slots(group_start, idx, rank).reshape(TOP_K * n_tok)
    sorted_t = _dispatch(t, pos_flat, n_tiles * EXP_ROWS)
    sorted_out = _experts(sorted_t, tile_flags,
                          exp_w_gate[0], exp_b_gate[0], exp_w_up[0], exp_b_up[0],
                          exp_w_down[0], exp_b_down[0], n_tiles)
    slabs = _gather_back(sorted_out, pos_flat)
    out = _combine(x1, slabs, wgt, norm_final_g)
    return out.reshape(bsz, seq, D_MODEL)
```

```python
import functools

import jax
import jax.numpy as jnp
from jax import lax
from jax.experimental import pallas as pl
from jax.experimental.pallas import tpu as pltpu
from jax.experimental.pallas import tpu_sc as plsc

F32 = jnp.float32
BF16 = jnp.bfloat16
I32 = jnp.int32

D_MODEL = 1024
CHUNK = 64
N_HEADS = 8
HEAD_DIM = 64
D_ATTN = N_HEADS * HEAD_DIM
LEFT_CHUNKS = 8
REL_MAX = 128
REL_MIN = -(CHUNK - 1)
D_CONV = D_MODEL - D_ATTN
CONV_WIDTH = 31
N_EXPERTS = 32
TOP_K = 4
SWIGLU_ALPHA = 1.702
SWIGLU_LIMIT = 7.0
RMS_EPS = 1e-5
LN_EPS = 1e-5

LEFT_PAD = LEFT_CHUNKS * CHUNK
IN_ROWS = 512
Q_ROWS = 2 * CHUNK
ATT_BLOCKS = 4
BAND_ROWS = Q_ROWS + LEFT_PAD
HEADS_PER_STEP = 4
GROUP_LANES = HEADS_PER_STEP * HEAD_DIM
MIX_ROWS = 512
MIX_SUBTILES = 1
HALO_ROWS = 32
EXP_ROWS = 512
SC_ROWS = 128
NEG_BIG = -1e30
VMEM_LIMIT = 56 * 1024 * 1024


HALF = D_MODEL // 2
HI_MASK = -65536


def _pack_rows(x):
    bits = lax.bitcast_convert_type(x.astype(BF16).astype(F32), I32)
    return lax.shift_right_logical(bits[:, :HALF], 16) | (bits[:, HALF:] & HI_MASK)


def _unpack_rows(w):
    lo = lax.bitcast_convert_type(lax.shift_left(w, 16), F32)
    hi = lax.bitcast_convert_type(w & HI_MASK, F32)
    return lo, hi


def _cast_rows(src_ref, dst_ref, rows, step=128):
    def body(c, carry):
        r = pl.multiple_of(c * step, step)
        dst_ref[pl.ds(r, step), :] = src_ref[pl.ds(r, step), :].astype(dst_ref.dtype)
        return carry
    lax.fori_loop(0, rows // step, body, 0)


def _conv_branch(hw_ref, sh_ref, rows, cw_ref, cb_ref, lg_ref, lb_ref, out_ref):
    off = HALO_ROWS - (CONV_WIDTH - 1)
    shift_rows = HALO_ROWS + rows - 8
    for b in range(1, 8):
        sh_ref[b - 1] = hw_ref[pl.ds(b, shift_rows), :]
    acc = None
    for j in range(CONV_WIDTH):
        a, b = divmod(off + j, 8)
        src = hw_ref if b == 0 else sh_ref.at[b - 1]
        term = src[pl.ds(8 * a, rows), :] * cw_ref[j:j + 1, :]
        acc = term if acc is None else acc + term
    acc = acc + cb_ref[...]
    mu = jnp.mean(acc, axis=-1, keepdims=True)
    d = acc - mu
    var = jnp.mean(d * d, axis=-1, keepdims=True)
    y = d * lax.rsqrt(var + LN_EPS) * lg_ref[...] + lb_ref[...]
    out_ref[...] = (y * jax.nn.sigmoid(y)).astype(out_ref.dtype)


def _inproj_kernel(x_ref, g_ref, w_ref, cw_ref, cb_ref, lg_ref, lb_ref,
                   q_ref, k_ref, v_ref, c_ref, wbf_ref, hw_ref, sh_ref):
    b = pl.program_id(0)
    j = pl.program_id(1)

    @pl.when((b == 0) & (j == 0))
    def _():
        _cast_rows(w_ref, wbf_ref, D_MODEL)

    @pl.when(j == 0)
    def _():
        k_ref[...] = jnp.zeros_like(k_ref)
        v_ref[...] = jnp.zeros_like(v_ref)
        hw_ref[IN_ROWS:IN_ROWS + HALO_ROWS, :] = jnp.zeros((HALO_ROWS, D_CONV), F32)

    @pl.when(j > 0)
    def _():
        x = x_ref[...]
        ms = jnp.mean(x * x, axis=-1, keepdims=True)
        hb = ((x * lax.rsqrt(ms + RMS_EPS)) * g_ref[...]).astype(BF16)

        def proj(c0, width):
            return jnp.dot(hb, wbf_ref[:, c0:c0 + width], preferred_element_type=F32)

        a = proj(3 * D_ATTN, D_CONV)
        gate = proj(3 * D_ATTN + D_CONV, D_CONV)
        hw_ref[0:HALO_ROWS, :] = hw_ref[IN_ROWS:IN_ROWS + HALO_ROWS, :]
        hw_ref[HALO_ROWS:HALO_ROWS + IN_ROWS, :] = a * jax.nn.sigmoid(gate)
        _conv_branch(hw_ref, sh_ref, IN_ROWS, cw_ref, cb_ref, lg_ref, lb_ref, c_ref)

        q_ref[...] = (proj(0, D_ATTN) * (HEAD_DIM ** -0.5)).astype(BF16)
        k_ref[...] = proj(D_ATTN, D_ATTN).astype(BF16)
        v_ref[...] = proj(2 * D_ATTN, D_ATTN).astype(BF16)


def _inproj(x, norm_g, w_in, conv_w, conv_b, ln_g, ln_b):
    bsz, seq, _ = x.shape
    nblk = seq // IN_ROWS
    d_cols = w_in.shape[1]
    row_blk = lambda b, j: (b, jnp.maximum(j - 1, 0), 0)
    const = lambda b, j: (0, 0)
    vec = lambda v: v.reshape(1, -1)
    return pl.pallas_call(
        _inproj_kernel,
        grid=(bsz, nblk + 1),
        in_specs=[
            pl.BlockSpec((None, IN_ROWS, D_MODEL), row_blk),
            pl.BlockSpec((1, D_MODEL), const),
            pl.BlockSpec((D_MODEL, d_cols), const),
            pl.BlockSpec((CONV_WIDTH, D_CONV), const),
            pl.BlockSpec((1, D_CONV), const),
            pl.BlockSpec((1, D_CONV), const),
            pl.BlockSpec((1, D_CONV), const),
        ],
        out_specs=[
            pl.BlockSpec((None, IN_ROWS, D_ATTN), row_blk),
            pl.BlockSpec((None, IN_ROWS, D_ATTN), lambda b, j: (b, j, 0)),
            pl.BlockSpec((None, IN_ROWS, D_ATTN), lambda b, j: (b, j, 0)),
            pl.BlockSpec((None, IN_ROWS, D_CONV), row_blk),
        ],
        out_shape=[
            jax.ShapeDtypeStruct((bsz, seq, D_ATTN), BF16),
            jax.ShapeDtypeStruct((bsz, seq + LEFT_PAD, D_ATTN), BF16),
            jax.ShapeDtypeStruct((bsz, seq + LEFT_PAD, D_ATTN), BF16),
            jax.ShapeDtypeStruct((bsz, seq, D_CONV), BF16),
        ],
        scratch_shapes=[
            pltpu.VMEM((D_MODEL, d_cols), BF16),
            pltpu.VMEM((HALO_ROWS + IN_ROWS, D_CONV), F32),
            pltpu.VMEM((7, HALO_ROWS + IN_ROWS - 8, D_CONV), F32),
        ],
        compiler_params=pltpu.CompilerParams(
            dimension_semantics=("arbitrary", "arbitrary"), vmem_limit_bytes=VMEM_LIMIT),
        name="inproj",
    )(x, vec(norm_g), w_in, conv_w, vec(conv_b), vec(ln_g), vec(ln_b))


def _attn_kernel(q_ref, k_ref, v_ref, bias_ref, o_ref):
    i = pl.program_id(1)
    lane = lax.broadcasted_iota(I32, (Q_ROWS, GROUP_LANES), 1) // HEAD_DIM
    col = lax.broadcasted_iota(I32, (HEADS_PER_STEP * Q_ROWS, BAND_ROWS), 1)
    for qb in range(ATT_BLOCKS):
        blk = i * ATT_BLOCKS + qb
        start = pl.multiple_of(blk * Q_ROWS, Q_ROWS)
        qrows = slice(qb * Q_ROWS, (qb + 1) * Q_ROWS)
        key_ok = col >= LEFT_PAD - blk * Q_ROWS
        for g in range(N_HEADS // HEADS_PER_STEP):
            lanes = slice(g * GROUP_LANES, (g + 1) * GROUP_LANES)
            q = q_ref[qrows, lanes]
            qs = jnp.concatenate(
                [jnp.where(lane == h, q, jnp.zeros_like(q)) for h in range(HEADS_PER_STEP)], axis=0)
            kb = k_ref[pl.ds(start, BAND_ROWS), lanes]
            vb = v_ref[pl.ds(start, BAND_ROWS), lanes]
            s = lax.dot_general(qs, kb, (((1,), (1,)), ((), ())), preferred_element_type=F32)
            rows = slice(g * HEADS_PER_STEP * Q_ROWS, (g + 1) * HEADS_PER_STEP * Q_ROWS)
            s = jnp.where(key_ok, s + bias_ref[rows, :], NEG_BIG)
            m = jnp.max(s, axis=-1, keepdims=True)
            p = jnp.exp(s - m)
            l = jnp.sum(p, axis=-1, keepdims=True)
            o = jnp.dot(p.astype(BF16), vb, preferred_element_type=F32) / l
            out = o[0:Q_ROWS]
            for h in range(1, HEADS_PER_STEP):
                out = jnp.where(lane == h, o[h * Q_ROWS:(h + 1) * Q_ROWS], out)
            o_ref[qrows, lanes] = out.astype(o_ref.dtype)


def _attention(q, kpad, vpad, bias):
    bsz, seq, _ = q.shape
    step_rows = ATT_BLOCKS * Q_ROWS
    return pl.pallas_call(
        _attn_kernel,
        grid=(bsz, seq // step_rows),
        in_specs=[
            pl.BlockSpec((None, step_rows, D_ATTN), lambda b, i: (b, i, 0)),
            pl.BlockSpec((None, seq + LEFT_PAD, D_ATTN), lambda b, i: (b, 0, 0)),
            pl.BlockSpec((None, seq + LEFT_PAD, D_ATTN), lambda b, i: (b, 0, 0)),
            pl.BlockSpec((N_HEADS * Q_ROWS, BAND_ROWS), lambda b, i: (0, 0)),
        ],
        out_specs=pl.BlockSpec((None, step_rows, D_ATTN), lambda b, i: (b, i, 0)),
        out_shape=jax.ShapeDtypeStruct((bsz, seq, D_ATTN), BF16),
        compiler_params=pltpu.CompilerParams(
            dimension_semantics=("arbitrary", "arbitrary"), vmem_limit_bytes=VMEM_LIMIT),
        name="chunk_attn",
    )(q, kpad, vpad, bias)


def _band_bias(rel_bias):
    n_rel = REL_MAX - REL_MIN + 1
    far = jnp.broadcast_to(rel_bias[:, n_rel - 1:n_rel], (N_HEADS, BAND_ROWS - 1 - REL_MAX))
    near = rel_bias[:, ::-1]
    ahead = jnp.broadcast_to(rel_bias[:, 0:1], (N_HEADS, Q_ROWS - 1 + REL_MIN))
    diag = jnp.concatenate([far, near, ahead], axis=1).astype(F32)
    bias = jnp.stack(
        [diag[:, Q_ROWS - 1 - r:Q_ROWS - 1 - r + BAND_ROWS] for r in range(Q_ROWS)], axis=1)
    r = jnp.arange(Q_ROWS)[:, None]
    m = jnp.arange(BAND_ROWS)[None, :]
    cq = r // CHUNK
    ck = m // CHUNK
    in_band = (ck >= cq) & (ck <= cq + LEFT_CHUNKS)
    bias = jnp.where(in_band[None], bias, NEG_BIG)
    return bias.reshape(N_HEADS * Q_ROWS, BAND_ROWS)


def _split_bf16(v):
    hi = v.astype(BF16)
    lo = (v - hi.astype(F32)).astype(BF16)
    return hi, lo


def _mix_route_kernel(x_ref, a_ref, c_ref, wo_ref, ng_ref, rwt_ref, rb_ref,
                      x1_ref, t_ref, idx_ref, wgt_ref, rank_ref, cnt_ref,
                      wobf_ref, cntacc_ref):
    i = pl.program_id(0)

    @pl.when(i == 0)
    def _():
        _cast_rows(wo_ref, wobf_ref, D_MODEL)
        cntacc_ref[...] = jnp.zeros_like(cntacc_ref)

    nt = (((1,), (1,)), ((), ()))
    w_hi, w_lo = _split_bf16(rwt_ref[...])
    ra = lax.broadcasted_iota(I32, (MIX_ROWS, MIX_ROWS), 0)
    rc = lax.broadcasted_iota(I32, (MIX_ROWS, MIX_ROWS), 1)
    upper = (ra < rc).astype(BF16)
    e_iota = lax.broadcasted_iota(I32, (N_EXPERTS, MIX_ROWS), 0)
    counts = cntacc_ref[...]

    for s in range(MIX_SUBTILES):
        rows = pl.ds(s * MIX_ROWS, MIX_ROWS)
        mixed = jnp.dot(a_ref[rows, :], wobf_ref[0:D_ATTN, :], preferred_element_type=F32)
        mixed = mixed + jnp.dot(c_ref[rows, :], wobf_ref[D_ATTN:D_MODEL, :],
                                preferred_element_type=F32)
        x1 = x_ref[rows, :] + mixed
        x1_ref[rows, :] = x1
        ms = jnp.mean(x1 * x1, axis=-1, keepdims=True)
        t = (x1 * lax.rsqrt(ms + RMS_EPS)) * ng_ref[...]
        t_ref[rows, :] = _pack_rows(t)

        t_hi, t_lo = _split_bf16(t)
        logits = (lax.dot_general(w_hi, t_hi, nt, preferred_element_type=F32)
                  + lax.dot_general(w_hi, t_lo, nt, preferred_element_type=F32)
                  + lax.dot_general(w_lo, t_hi, nt, preferred_element_type=F32)) + rb_ref[...]
        vals, idxs, hots = [], [], []
        for _ in range(TOP_K):
            m = jnp.max(logits, axis=0, keepdims=True)
            am = jnp.min(jnp.where(logits == m, e_iota, N_EXPERTS), axis=0, keepdims=True)
            hot = e_iota == am
            vals.append(m)
            idxs.append(am)
            hots.append(hot)
            logits = jnp.where(hot, -jnp.inf, logits)
        exps = [jnp.exp(v - vals[0]) for v in vals]
        den = exps[0] + exps[1] + exps[2] + exps[3]
        wts = [e / den for e in exps]

        hot_f = (hots[0] | hots[1] | hots[2] | hots[3]).astype(F32)
        prefix = jnp.dot(hot_f.astype(BF16), upper, preferred_element_type=F32)
        base = prefix + counts
        ranks = [jnp.sum(jnp.where(h, base, 0.0), axis=0, keepdims=True) for h in hots]
        counts = counts + jnp.sum(hot_f, axis=1, keepdims=True)

        idx_ref[:, rows] = jnp.concatenate(idxs, axis=0)
        rank_ref[:, rows] = jnp.concatenate(ranks, axis=0).astype(I32)
        w8 = jnp.concatenate(wts + [jnp.zeros((8 - TOP_K, MIX_ROWS), F32)], axis=0)
        wgt_ref[rows, :] = w8.T

    cntacc_ref[...] = counts
    cnt_ref[...] = jnp.broadcast_to(counts, cnt_ref.shape)


def _mix_route(x2, attn2, conv2, w_out, norm_g, router_w, router_b):
    n_tok = x2.shape[0]
    step_rows = MIX_SUBTILES * MIX_ROWS
    row = lambda i: (i, 0)
    const = lambda i: (0, 0)
    vec = lambda v: v.reshape(1, -1)
    return pl.pallas_call(
        _mix_route_kernel,
        grid=(n_tok // step_rows,),
        in_specs=[
            pl.BlockSpec((step_rows, D_MODEL), row),
            pl.BlockSpec((step_rows, D_ATTN), row),
            pl.BlockSpec((step_rows, D_CONV), row),
            pl.BlockSpec((D_MODEL, D_MODEL), const),
            pl.BlockSpec((1, D_MODEL), const),
            pl.BlockSpec((N_EXPERTS, D_MODEL), const),
            pl.BlockSpec((N_EXPERTS, 1), const),
        ],
        out_specs=[
            pl.BlockSpec((step_rows, D_MODEL), row),
            pl.BlockSpec((step_rows, HALF), row),
            pl.BlockSpec((TOP_K, step_rows), lambda i: (0, i)),
            pl.BlockSpec((step_rows, 8), row),
            pl.BlockSpec((TOP_K, step_rows), lambda i: (0, i)),
            pl.BlockSpec((N_EXPERTS, 128), const),
        ],
        out_shape=[
            jax.ShapeDtypeStruct((n_tok, D_MODEL), F32),
            jax.ShapeDtypeStruct((n_tok, HALF), I32),
            jax.ShapeDtypeStruct((TOP_K, n_tok), I32),
            jax.ShapeDtypeStruct((n_tok, 8), F32),
            jax.ShapeDtypeStruct((TOP_K, n_tok), I32),
            jax.ShapeDtypeStruct((N_EXPERTS, 128), F32),
        ],
        scratch_shapes=[
            pltpu.VMEM((D_MODEL, D_MODEL), BF16),
            pltpu.VMEM((N_EXPERTS, 1), F32),
        ],
        compiler_params=pltpu.CompilerParams(
            dimension_semantics=("arbitrary",), vmem_limit_bytes=VMEM_LIMIT),
        name="mix_route",
    )(x2, attn2, conv2, w_out, vec(norm_g), router_w.T, router_b.reshape(N_EXPERTS, 1))


def _tile_plan(counts, n_tiles):
    tiles_per = (counts + EXP_ROWS - 1) // EXP_ROWS
    tile_end = jnp.cumsum(tiles_per)
    tile_begin = tile_end - tiles_per
    n_valid = tile_end[-1]
    tiles = jnp.arange(n_tiles, dtype=I32)
    tile_valid = tiles < n_valid
    capped = jnp.minimum(tiles, n_valid - 1)
    tile_expert = jnp.sum((capped[:, None] >= tile_end[None, :]).astype(I32), axis=1)
    tile_expert = jnp.minimum(tile_expert, N_EXPERTS - 1)
    is_first = (tiles[:, None] == tile_begin[None, :]) & (tiles_per[None, :] > 0)
    tile_first = tile_valid & jnp.any(is_first, axis=1)
    group_start = tile_begin * EXP_ROWS
    experts = jnp.arange(N_EXPERTS, dtype=I32)
    nonempty = tiles_per > 0
    parity = (jnp.cumsum(nonempty.astype(I32)) - 1) % 2
    later = nonempty[None, :] & (experts[None, :] > experts[:, None])
    nxt = jnp.min(jnp.where(later, experts[None, :], N_EXPERTS), axis=1)
    nxt = jnp.where(nxt == N_EXPERTS, -1, nxt)
    hot = (tile_expert[:, None] == experts[None, :]).astype(I32)
    tile_slot = jnp.sum(hot * parity[None, :], axis=1)
    tile_next = jnp.sum(hot * nxt[None, :], axis=1)
    flags = (tile_expert, tile_first.astype(I32), tile_valid.astype(I32),
             tile_slot.astype(I32), tile_next.astype(I32))
    return flags, group_start.astype(I32)


def _slot_kernel(gstart_ref, idx_ref, rank_ref, pos_ref):
    idx = idx_ref[...]
    pos = rank_ref[...]
    for e in range(N_EXPERTS):
        pos = pos + jnp.where(idx == e, gstart_ref[e], 0)
    pos_ref[...] = pos


def _slots(group_start, idx, rank):
    full = pl.BlockSpec(idx.shape, lambda i, gs: (0, 0))
    return pl.pallas_call(
        _slot_kernel,
        grid_spec=pltpu.PrefetchScalarGridSpec(
            num_scalar_prefetch=1, grid=(1,), in_specs=[full, full], out_specs=full),
        out_shape=jax.ShapeDtypeStruct(idx.shape, I32),
        name="slots",
    )(group_start, idx, rank)


def _sc_mesh():
    return plsc.VectorSubcoreMesh(core_axis_name="core", subcore_axis_name="subcore")


def _sc_worker():
    info = plsc.get_sparse_core_info()
    wid = lax.axis_index("subcore") * info.num_cores + lax.axis_index("core")
    return wid, info.num_cores * info.num_subcores


def _dispatch(t2, pos_flat, n_slots):
    n_tok, d = t2.shape
    n_workers = 32
    per_w = n_tok // n_workers
    assert per_w % SC_ROWS == 0

    @functools.partial(
        pl.kernel, mesh=_sc_mesh(),
        out_type=jax.ShapeDtypeStruct((n_slots, d), t2.dtype),
        scratch_types=[pltpu.VMEM((SC_ROWS,), I32), pltpu.VMEM((SC_ROWS, d), t2.dtype)],
        name="dispatch",
    )
    def k(t_hbm, pos_hbm, o_hbm, idx_v, rows_v):
        wid, nw = _sc_worker()
        assert nw == n_workers

        @pl.loop(0, per_w // SC_ROWS)
        def _(c):
            base = pl.multiple_of(wid * per_w + c * SC_ROWS, SC_ROWS)
            pltpu.sync_copy(t_hbm.at[pl.ds(base, SC_ROWS)], rows_v)
            for kk in range(TOP_K):
                pltpu.sync_copy(pos_hbm.at[pl.ds(kk * n_tok + base, SC_ROWS)], idx_v)
                pltpu.sync_copy(rows_v, o_hbm.at[idx_v])

    return k(t2, pos_flat)


def _gather_back(sorted_out, pos_flat):
    n_rows = pos_flat.shape[0]
    d = sorted_out.shape[1]
    n_workers = 32
    per_w = n_rows // n_workers
    assert per_w % SC_ROWS == 0

    @functools.partial(
        pl.kernel, mesh=_sc_mesh(),
        out_type=jax.ShapeDtypeStruct((n_rows, d), sorted_out.dtype),
        scratch_types=[pltpu.VMEM((SC_ROWS,), I32), pltpu.VMEM((SC_ROWS, d), sorted_out.dtype)],
        name="gather_back",
    )
    def k(s_hbm, pos_hbm, o_hbm, idx_v, rows_v):
        wid, nw = _sc_worker()
        assert nw == n_workers

        @pl.loop(0, per_w // SC_ROWS)
        def _(c):
            base = pl.multiple_of(wid * per_w + c * SC_ROWS, SC_ROWS)
            pltpu.sync_copy(pos_hbm.at[pl.ds(base, SC_ROWS)], idx_v)
            pltpu.sync_copy(s_hbm.at[idx_v], rows_v)
            pltpu.sync_copy(rows_v, o_hbm.at[pl.ds(base, SC_ROWS)])

    return k(sorted_out, pos_flat)


def _weight_copies(w_hbm, wf32, sem, expert, slot):
    return [pltpu.make_async_copy(w.at[expert], wf32.at[slot, mtx], sem.at[slot])
            for mtx, w in enumerate(w_hbm)]


def _expert_kernel(texp_ref, tfirst_ref, tvalid_ref, tslot_ref, tnext_ref,
                   x_ref, wg_hbm, bg_ref, wu_hbm, bu_ref, wd_hbm, bd_ref,
                   o_ref, wf32, wbf, wsem):
    i = pl.program_id(0)
    w_hbm = (wg_hbm, wu_hbm, wd_hbm)

    @pl.when(tvalid_ref[i] == 0)
    def _():
        o_ref[...] = jnp.zeros_like(o_ref)

    @pl.when(tvalid_ref[i] == 1)
    def _():
        @pl.when(tfirst_ref[i] == 1)
        def _():
            slot = tslot_ref[i]
            expert = texp_ref[i]

            @pl.when(i == 0)
            def _():
                for cp in _weight_copies(w_hbm, wf32, wsem, expert, slot):
                    cp.start()

            for cp in _weight_copies(w_hbm, wf32, wsem, expert, slot):
                cp.wait()

            @pl.when(tnext_ref[i] >= 0)
            def _():
                for cp in _weight_copies(w_hbm, wf32, wsem, tnext_ref[i], 1 - slot):
                    cp.start()

            for mtx in range(3):
                _cast_rows(wf32.at[slot, mtx], wbf.at[mtx], D_MODEL)

        xs = jnp.concatenate(_unpack_rows(x_ref[...]), axis=1).astype(BF16)
        acc = None
        cn = 256
        for c in range(D_MODEL // cn):
            sl = slice(c * cn, (c + 1) * cn)
            g = jnp.dot(xs, wbf[0, :, sl], preferred_element_type=F32) + bg_ref[:, sl]
            u = jnp.dot(xs, wbf[1, :, sl], preferred_element_type=F32) + bu_ref[:, sl]
            g = jnp.minimum(g, SWIGLU_LIMIT)
            u = jnp.clip(u, -SWIGLU_LIMIT, SWIGLU_LIMIT)
            hid = ((u + 1.0) * (g * jax.nn.sigmoid(SWIGLU_ALPHA * g))).astype(BF16)
            part = jnp.dot(hid, wbf[2, sl, :], preferred_element_type=F32)
            acc = part if acc is None else acc + part
        o_ref[...] = _pack_rows(acc + bd_ref[...])


def _experts(sorted_t, tile_flags, w_gate, b_gate, w_up, b_up, w_down, b_down, n_tiles):
    row_blk = pl.BlockSpec((EXP_ROWS, HALF), lambda i, *_: (i, 0))
    w_any = pl.BlockSpec(memory_space=pl.ANY)
    b_blk = pl.BlockSpec((None, 1, D_MODEL), lambda i, te, *_: (te[i], 0, 0))
    grid_spec = pltpu.PrefetchScalarGridSpec(
        num_scalar_prefetch=len(tile_flags),
        grid=(n_tiles,),
        in_specs=[row_blk, w_any, b_blk, w_any, b_blk, w_any, b_blk],
        out_specs=row_blk,
        scratch_shapes=[
            pltpu.VMEM((2, 3, D_MODEL, D_MODEL), F32),
            pltpu.VMEM((3, D_MODEL, D_MODEL), BF16),
            pltpu.SemaphoreType.DMA((2,)),
        ],
    )
    b3 = lambda b: b.reshape(N_EXPERTS, 1, D_MODEL)
    return pl.pallas_call(
        _expert_kernel,
        grid_spec=grid_spec,
        out_shape=jax.ShapeDtypeStruct((n_tiles * EXP_ROWS, HALF), I32),
        compiler_params=pltpu.CompilerParams(
            dimension_semantics=("arbitrary",), vmem_limit_bytes=VMEM_LIMIT),
        name="experts",
    )(*tile_flags, sorted_t,
      w_gate, b3(b_gate), w_up, b3(b_up), w_down, b3(b_down))


def _combine_kernel(x1_ref, s0_ref, s1_ref, s2_ref, s3_ref, w_ref, g_ref, o_ref):
    w = w_ref[...]
    y_lo = x1_ref[:, :HALF]
    y_hi = x1_ref[:, HALF:]
    for k, s_ref in enumerate((s0_ref, s1_ref, s2_ref, s3_ref)):
        lo, hi = _unpack_rows(s_ref[...])
        y_lo = y_lo + w[:, k:k + 1] * lo
        y_hi = y_hi + w[:, k:k + 1] * hi
    sq = jnp.sum(y_lo * y_lo, axis=-1, keepdims=True) + jnp.sum(y_hi * y_hi, axis=-1, keepdims=True)
    scale = lax.rsqrt(sq * (1.0 / D_MODEL) + RMS_EPS)
    o_ref[:, :HALF] = (y_lo * scale) * g_ref[:, :HALF]
    o_ref[:, HALF:] = (y_hi * scale) * g_ref[:, HALF:]


def _combine(x1, slabs, wgt, norm_g):
    n_tok = x1.shape[0]
    rows = MIX_ROWS
    ntiles = n_tok // rows
    slab_spec = lambda k: pl.BlockSpec((rows, HALF), lambda i, k=k: (k * ntiles + i, 0))
    return pl.pallas_call(
        _combine_kernel,
        grid=(ntiles,),
        in_specs=[pl.BlockSpec((rows, D_MODEL), lambda i: (i, 0))]
        + [slab_spec(k) for k in range(TOP_K)]
        + [pl.BlockSpec((rows, 8), lambda i: (i, 0)),
           pl.BlockSpec((1, D_MODEL), lambda i: (0, 0))],
        out_specs=pl.BlockSpec((rows, D_MODEL), lambda i: (i, 0)),
        out_shape=jax.ShapeDtypeStruct((n_tok, D_MODEL), F32),
        compiler_params=pltpu.CompilerParams(
            dimension_semantics=("arbitrary",), vmem_limit_bytes=VMEM_LIMIT),
        name="combine",
    )(x1, slabs, slabs, slabs, slabs, wgt, norm_g.reshape(1, D_MODEL))


def kernel(x, norm_mix_g, w_in, conv_dw_w, conv_dw_b, conv_ln_g, conv_ln_b, rel_bias, w_out,
           norm_ffn_g, router_w, router_b, exp_w_gate, exp_b_gate, exp_w_up, exp_b_up,
           exp_w_down, exp_b_down, norm_final_g):
    bsz, seq, _ = x.shape
    n_tok = bsz * seq
    assert norm_mix_g.shape[0] == 1, "single-layer block"
    assert seq % IN_ROWS == 0 and seq % MIX_ROWS == 0 and IN_ROWS == LEFT_PAD
    n_tiles = (TOP_K * n_tok) // EXP_ROWS + N_EXPERTS - 1

    q, kpad, vpad, conv = _inproj(x, norm_mix_g[0], w_in[0], conv_dw_w[0], conv_dw_b[0],
                                  conv_ln_g[0], conv_ln_b[0])
    attn = _attention(q, kpad, vpad, _band_bias(rel_bias[0]))
    x1, t, idx, wgt, rank, cnt = _mix_route(
        x.reshape(n_tok, D_MODEL), attn.reshape(n_tok, D_ATTN), conv.reshape(n_tok, D_CONV),
        w_out[0], norm_ffn_g[0], router_w[0], router_b[0])
    tile_flags, group_start = _tile_plan(cnt[:, 0].astype(I32), n_tiles)
    pos_flat = _slots(group_start, idx, rank).reshape(TOP_K * n_tok)
    sorted_t = _dispatch(t, pos_flat, n_tiles * EXP_ROWS)
    sorted_out = _experts(sorted_t, tile_flags,
                          exp_w_gate[0], exp_b_gate[0], exp_w_up[0], exp_b_up[0],
                          exp_w_down[0], exp_b_down[0], n_tiles)
    slabs = _gather_back(sorted_out, pos_flat)
    out = _combine(x1, slabs, wgt, norm_final_g)
    return out.reshape(bsz, seq, D_MODEL)
```

```python
import functools

import jax
import jax.numpy as jnp
from jax import lax
from jax.experimental import pallas as pl
from jax.experimental.pallas import tpu as pltpu
from jax.experimental.pallas import tpu_sc as plsc

F32 = jnp.float32
BF16 = jnp.bfloat16
I32 = jnp.int32

D_MODEL = 1024
CHUNK = 64
N_HEADS = 8
HEAD_DIM = 64
D_ATTN = N_HEADS * HEAD_DIM
LEFT_CHUNKS = 8
REL_MAX = 128
REL_MIN = -(CHUNK - 1)
D_CONV = D_MODEL - D_ATTN
CONV_WIDTH = 31
N_EXPERTS = 32
TOP_K = 4
SWIGLU_ALPHA = 1.702
SWIGLU_LIMIT = 7.0
RMS_EPS = 1e-5
LN_EPS = 1e-5

LEFT_PAD = LEFT_CHUNKS * CHUNK
IN_ROWS = 512
Q_ROWS = 2 * CHUNK
ATT_BLOCKS = 4
BAND_ROWS = Q_ROWS + LEFT_PAD
HEADS_PER_STEP = 4
GROUP_LANES = HEADS_PER_STEP * HEAD_DIM
MIX_ROWS = 512
MIX_SUBTILES = 1
HALO_ROWS = 32
EXP_ROWS = 512
EXP_GRANULE = 128
SC_ROWS = 128
NEG_BIG = -1e30
VMEM_LIMIT = 56 * 1024 * 1024


HALF = D_MODEL // 2
HI_MASK = -65536


def _pack_rows(x):
    bits = lax.bitcast_convert_type(x.astype(BF16).astype(F32), I32)
    return lax.shift_right_logical(bits[:, :HALF], 16) | (bits[:, HALF:] & HI_MASK)


def _unpack_rows(w):
    lo = lax.bitcast_convert_type(lax.shift_left(w, 16), F32)
    hi = lax.bitcast_convert_type(w & HI_MASK, F32)
    return lo, hi


def _cast_rows(src_ref, dst_ref, rows, step=128):
    def body(c, carry):
        r = pl.multiple_of(c * step, step)
        dst_ref[pl.ds(r, step), :] = src_ref[pl.ds(r, step), :].astype(dst_ref.dtype)
        return carry
    lax.fori_loop(0, rows // step, body, 0)


def _conv_branch(hw_ref, sh_ref, rows, cw_ref, cb_ref, lg_ref, lb_ref, out_ref):
    off = HALO_ROWS - (CONV_WIDTH - 1)
    shift_rows = HALO_ROWS + rows - 8
    for b in range(1, 8):
        sh_ref[b - 1] = hw_ref[pl.ds(b, shift_rows), :]
    acc = None
    for j in range(CONV_WIDTH):
        a, b = divmod(off + j, 8)
        src = hw_ref if b == 0 else sh_ref.at[b - 1]
        term = src[pl.ds(8 * a, rows), :] * cw_ref[j:j + 1, :]
        acc = term if acc is None else acc + term
    acc = acc + cb_ref[...]
    mu = jnp.mean(acc, axis=-1, keepdims=True)
    d = acc - mu
    var = jnp.mean(d * d, axis=-1, keepdims=True)
    y = d * lax.rsqrt(var + LN_EPS) * lg_ref[...] + lb_ref[...]
    out_ref[...] = (y * jax.nn.sigmoid(y)).astype(out_ref.dtype)


def _inproj_kernel(x_ref, g_ref, w_ref, cw_ref, cb_ref, lg_ref, lb_ref,
                   q_ref, k_ref, v_ref, c_ref, wbf_ref, hw_ref, sh_ref):
    b = pl.program_id(0)
    j = pl.program_id(1)

    @pl.when((b == 0) & (j == 0))
    def _():
        _cast_rows(w_ref, wbf_ref, D_MODEL)

    @pl.when(j == 0)
    def _():
        k_ref[...] = jnp.zeros_like(k_ref)
        v_ref[...] = jnp.zeros_like(v_ref)
        hw_ref[IN_ROWS:IN_ROWS + HALO_ROWS, :] = jnp.zeros((HALO_ROWS, D_CONV), F32)

    @pl.when(j > 0)
    def _():
        x = x_ref[...]
        ms = jnp.mean(x * x, axis=-1, keepdims=True)
        hb = ((x * lax.rsqrt(ms + RMS_EPS)) * g_ref[...]).astype(BF16)

        def proj(c0, width):
            return jnp.dot(hb, wbf_ref[:, c0:c0 + width], preferred_element_type=F32)

        a = proj(3 * D_ATTN, D_CONV)
        gate = proj(3 * D_ATTN + D_CONV, D_CONV)
        hw_ref[0:HALO_ROWS, :] = hw_ref[IN_ROWS:IN_ROWS + HALO_ROWS, :]
        hw_ref[HALO_ROWS:HALO_ROWS + IN_ROWS, :] = a * jax.nn.sigmoid(gate)
        _conv_branch(hw_ref, sh_ref, IN_ROWS, cw_ref, cb_ref, lg_ref, lb_ref, c_ref)

        q_ref[...] = (proj(0, D_ATTN) * (HEAD_DIM ** -0.5)).astype(BF16)
        k_ref[...] = proj(D_ATTN, D_ATTN).astype(BF16)
        v_ref[...] = proj(2 * D_ATTN, D_ATTN).astype(BF16)


def _inproj(x, norm_g, w_in, conv_w, conv_b, ln_g, ln_b):
    bsz, seq, _ = x.shape
    nblk = seq // IN_ROWS
    d_cols = w_in.shape[1]
    row_blk = lambda b, j: (b, jnp.maximum(j - 1, 0), 0)
    const = lambda b, j: (0, 0)
    vec = lambda v: v.reshape(1, -1)
    return pl.pallas_call(
        _inproj_kernel,
        grid=(bsz, nblk + 1),
        in_specs=[
            pl.BlockSpec((None, IN_ROWS, D_MODEL), row_blk),
            pl.BlockSpec((1, D_MODEL), const),
            pl.BlockSpec((D_MODEL, d_cols), const),
            pl.BlockSpec((CONV_WIDTH, D_CONV), const),
            pl.BlockSpec((1, D_CONV), const),
            pl.BlockSpec((1, D_CONV), const),
            pl.BlockSpec((1, D_CONV), const),
        ],
        out_specs=[
            pl.BlockSpec((None, IN_ROWS, D_ATTN), row_blk),
            pl.BlockSpec((None, IN_ROWS, D_ATTN), lambda b, j: (b, j, 0)),
            pl.BlockSpec((None, IN_ROWS, D_ATTN), lambda b, j: (b, j, 0)),
            pl.BlockSpec((None, IN_ROWS, D_CONV), row_blk),
        ],
        out_shape=[
            jax.ShapeDtypeStruct((bsz, seq, D_ATTN), BF16),
            jax.ShapeDtypeStruct((bsz, seq + LEFT_PAD, D_ATTN), BF16),
            jax.ShapeDtypeStruct((bsz, seq + LEFT_PAD, D_ATTN), BF16),
            jax.ShapeDtypeStruct((bsz, seq, D_CONV), BF16),
        ],
        scratch_shapes=[
            pltpu.VMEM((D_MODEL, d_cols), BF16),
            pltpu.VMEM((HALO_ROWS + IN_ROWS, D_CONV), F32),
            pltpu.VMEM((7, HALO_ROWS + IN_ROWS - 8, D_CONV), F32),
        ],
        compiler_params=pltpu.CompilerParams(
            dimension_semantics=("arbitrary", "arbitrary"), vmem_limit_bytes=VMEM_LIMIT),
        name="inproj",
    )(x, vec(norm_g), w_in, conv_w, vec(conv_b), vec(ln_g), vec(ln_b))


def _attn_kernel(q_ref, k_ref, v_ref, bias_ref, o_ref):
    i = pl.program_id(1)
    lane = lax.broadcasted_iota(I32, (Q_ROWS, GROUP_LANES), 1) // HEAD_DIM
    col = lax.broadcasted_iota(I32, (HEADS_PER_STEP * Q_ROWS, BAND_ROWS), 1)
    for qb in range(ATT_BLOCKS):
        blk = i * ATT_BLOCKS + qb
        start = pl.multiple_of(blk * Q_ROWS, Q_ROWS)
        qrows = slice(qb * Q_ROWS, (qb + 1) * Q_ROWS)
        key_ok = col >= LEFT_PAD - blk * Q_ROWS
        for g in range(N_HEADS // HEADS_PER_STEP):
            lanes = slice(g * GROUP_LANES, (g + 1) * GROUP_LANES)
            q = q_ref[qrows, lanes]
            qs = jnp.concatenate(
                [jnp.where(lane == h, q, jnp.zeros_like(q)) for h in range(HEADS_PER_STEP)], axis=0)
            kb = k_ref[pl.ds(start, BAND_ROWS), lanes]
            vb = v_ref[pl.ds(start, BAND_ROWS), lanes]
            s = lax.dot_general(qs, kb, (((1,), (1,)), ((), ())), preferred_element_type=F32)
            rows = slice(g * HEADS_PER_STEP * Q_ROWS, (g + 1) * HEADS_PER_STEP * Q_ROWS)
            s = jnp.where(key_ok, s + bias_ref[rows, :], NEG_BIG)
            m = jnp.max(s, axis=-1, keepdims=True)
            p = jnp.exp(s - m)
            l = jnp.sum(p, axis=-1, keepdims=True)
            o = jnp.dot(p.astype(BF16), vb, preferred_element_type=F32) / l
            out = o[0:Q_ROWS]
            for h in range(1, HEADS_PER_STEP):
                out = jnp.where(lane == h, o[h * Q_ROWS:(h + 1) * Q_ROWS], out)
            o_ref[qrows, lanes] = out.astype(o_ref.dtype)


def _attention(q, kpad, vpad, bias):
    bsz, seq, _ = q.shape
    step_rows = ATT_BLOCKS * Q_ROWS
    return pl.pallas_call(
        _attn_kernel,
        grid=(bsz, seq // step_rows),
        in_specs=[
            pl.BlockSpec((None, step_rows, D_ATTN), lambda b, i: (b, i, 0)),
            pl.BlockSpec((None, seq + LEFT_PAD, D_ATTN), lambda b, i: (b, 0, 0)),
            pl.BlockSpec((None, seq + LEFT_PAD, D_ATTN), lambda b, i: (b, 0, 0)),
            pl.BlockSpec((N_HEADS * Q_ROWS, BAND_ROWS), lambda b, i: (0, 0)),
        ],
        out_specs=pl.BlockSpec((None, step_rows, D_ATTN), lambda b, i: (b, i, 0)),
        out_shape=jax.ShapeDtypeStruct((bsz, seq, D_ATTN), BF16),
        compiler_params=pltpu.CompilerParams(
            dimension_semantics=("arbitrary", "arbitrary"), vmem_limit_bytes=VMEM_LIMIT),
        name="chunk_attn",
    )(q, kpad, vpad, bias)


def _band_bias(rel_bias):
    n_rel = REL_MAX - REL_MIN + 1
    far = jnp.broadcast_to(rel_bias[:, n_rel - 1:n_rel], (N_HEADS, BAND_ROWS - 1 - REL_MAX))
    near = rel_bias[:, ::-1]
    ahead = jnp.broadcast_to(rel_bias[:, 0:1], (N_HEADS, Q_ROWS - 1 + REL_MIN))
    diag = jnp.concatenate([far, near, ahead], axis=1).astype(F32)
    bias = jnp.stack(
        [diag[:, Q_ROWS - 1 - r:Q_ROWS - 1 - r + BAND_ROWS] for r in range(Q_ROWS)], axis=1)
    r = jnp.arange(Q_ROWS)[:, None]
    m = jnp.arange(BAND_ROWS)[None, :]
    cq = r // CHUNK
    ck = m // CHUNK
    in_band = (ck >= cq) & (ck <= cq + LEFT_CHUNKS)
    bias = jnp.where(in_band[None], bias, NEG_BIG)
    return bias.reshape(N_HEADS * Q_ROWS, BAND_ROWS)


def _split_bf16(v):
    hi = v.astype(BF16)
    lo = (v - hi.astype(F32)).astype(BF16)
    return hi, lo


def _mix_route_kernel(x_ref, a_ref, c_ref, wo_ref, ng_ref, rwt_ref, rb_ref,
                      x1_ref, t_ref, idx_ref, wgt_ref, rank_ref, cnt_ref,
                      wobf_ref, cntacc_ref):
    i = pl.program_id(0)

    @pl.when(i == 0)
    def _():
        _cast_rows(wo_ref, wobf_ref, D_MODEL)
        cntacc_ref[...] = jnp.zeros_like(cntacc_ref)

    nt = (((1,), (1,)), ((), ()))
    w_hi, w_lo = _split_bf16(rwt_ref[...])
    ra = lax.broadcasted_iota(I32, (MIX_ROWS, MIX_ROWS), 0)
    rc = lax.broadcasted_iota(I32, (MIX_ROWS, MIX_ROWS), 1)
    upper = (ra < rc).astype(BF16)
    e_iota = lax.broadcasted_iota(I32, (N_EXPERTS, MIX_ROWS), 0)
    counts = cntacc_ref[...]

    for s in range(MIX_SUBTILES):
        rows = pl.ds(s * MIX_ROWS, MIX_ROWS)
        mixed = jnp.dot(a_ref[rows, :], wobf_ref[0:D_ATTN, :], preferred_element_type=F32)
        mixed = mixed + jnp.dot(c_ref[rows, :], wobf_ref[D_ATTN:D_MODEL, :],
                                preferred_element_type=F32)
        x1 = x_ref[rows, :] + mixed
        x1_ref[rows, :] = x1
        ms = jnp.mean(x1 * x1, axis=-1, keepdims=True)
        t = (x1 * lax.rsqrt(ms + RMS_EPS)) * ng_ref[...]
        t_ref[rows, :] = _pack_rows(t)

        t_hi, t_lo = _split_bf16(t)
        logits = (lax.dot_general(w_hi, t_hi, nt, preferred_element_type=F32)
                  + lax.dot_general(w_hi, t_lo, nt, preferred_element_type=F32)
                  + lax.dot_general(w_lo, t_hi, nt, preferred_element_type=F32)) + rb_ref[...]
        vals, idxs, hots = [], [], []
        for _ in range(TOP_K):
            m = jnp.max(logits, axis=0, keepdims=True)
            am = jnp.min(jnp.where(logits == m, e_iota, N_EXPERTS), axis=0, keepdims=True)
            hot = e_iota == am
            vals.append(m)
            idxs.append(am)
            hots.append(hot)
            logits = jnp.where(hot, -jnp.inf, logits)
        exps = [jnp.exp(v - vals[0]) for v in vals]
        den = exps[0] + exps[1] + exps[2] + exps[3]
        wts = [e / den for e in exps]

        hot_f = (hots[0] | hots[1] | hots[2] | hots[3]).astype(F32)
        prefix = jnp.dot(hot_f.astype(BF16), upper, preferred_element_type=F32)
        base = prefix + counts
        ranks = [jnp.sum(jnp.where(h, base, 0.0), axis=0, keepdims=True) for h in hots]
        counts = counts + jnp.sum(hot_f, axis=1, keepdims=True)

        idx_ref[:, rows] = jnp.concatenate(idxs, axis=0)
        rank_ref[:, rows] = jnp.concatenate(ranks, axis=0).astype(I32)
        w8 = jnp.concatenate(wts + [jnp.zeros((8 - TOP_K, MIX_ROWS), F32)], axis=0)
        wgt_ref[rows, :] = w8.T

    cntacc_ref[...] = counts
    cnt_ref[...] = jnp.broadcast_to(counts, cnt_ref.shape)


def _mix_route(x2, attn2, conv2, w_out, norm_g, router_w, router_b):
    n_tok = x2.shape[0]
    step_rows = MIX_SUBTILES * MIX_ROWS
    row = lambda i: (i, 0)
    const = lambda i: (0, 0)
    vec = lambda v: v.reshape(1, -1)
    return pl.pallas_call(
        _mix_route_kernel,
        grid=(n_tok // step_rows,),
        in_specs=[
            pl.BlockSpec((step_rows, D_MODEL), row),
            pl.BlockSpec((step_rows, D_ATTN), row),
            pl.BlockSpec((step_rows, D_CONV), row),
            pl.BlockSpec((D_MODEL, D_MODEL), const),
            pl.BlockSpec((1, D_MODEL), const),
            pl.BlockSpec((N_EXPERTS, D_MODEL), const),
            pl.BlockSpec((N_EXPERTS, 1), const),
        ],
        out_specs=[
            pl.BlockSpec((step_rows, D_MODEL), row),
            pl.BlockSpec((step_rows, HALF), row),
            pl.BlockSpec((TOP_K, step_rows), lambda i: (0, i)),
            pl.BlockSpec((step_rows, 8), row),
            pl.BlockSpec((TOP_K, step_rows), lambda i: (0, i)),
            pl.BlockSpec((N_EXPERTS, 128), const),
        ],
        out_shape=[
            jax.ShapeDtypeStruct((n_tok, D_MODEL), F32),
            jax.ShapeDtypeStruct((n_tok, HALF), I32),
            jax.ShapeDtypeStruct((TOP_K, n_tok), I32),
            jax.ShapeDtypeStruct((n_tok, 8), F32),
            jax.ShapeDtypeStruct((TOP_K, n_tok), I32),
            jax.ShapeDtypeStruct((N_EXPERTS, 128), F32),
        ],
        scratch_shapes=[
            pltpu.VMEM((D_MODEL, D_MODEL), BF16),
            pltpu.VMEM((N_EXPERTS, 1), F32),
        ],
        compiler_params=pltpu.CompilerParams(
            dimension_semantics=("arbitrary",), vmem_limit_bytes=VMEM_LIMIT),
        name="mix_route",
    )(x2, attn2, conv2, w_out, vec(norm_g), router_w.T, router_b.reshape(N_EXPERTS, 1))


def _tile_plan(counts, n_tiles):
    tiles_per = (counts + EXP_ROWS - 1) // EXP_ROWS
    tile_end = jnp.cumsum(tiles_per)
    tile_begin = tile_end - tiles_per
    n_valid = tile_end[-1]
    tiles = jnp.arange(n_tiles, dtype=I32)
    tile_valid = tiles < n_valid
    capped = jnp.minimum(tiles, n_valid - 1)
    tile_expert = jnp.sum((capped[:, None] >= tile_end[None, :]).astype(I32), axis=1)
    tile_expert = jnp.minimum(tile_expert, N_EXPERTS - 1)
    is_first = (tiles[:, None] == tile_begin[None, :]) & (tiles_per[None, :] > 0)
    tile_first = tile_valid & jnp.any(is_first, axis=1)
    group_start = tile_begin * EXP_ROWS
    experts = jnp.arange(N_EXPERTS, dtype=I32)
    nonempty = tiles_per > 0
    parity = (jnp.cumsum(nonempty.astype(I32)) - 1) % 2
    later = nonempty[None, :] & (experts[None, :] > experts[:, None])
    nxt = jnp.min(jnp.where(later, experts[None, :], N_EXPERTS), axis=1)
    nxt = jnp.where(nxt == N_EXPERTS, -1, nxt)
    hot = (tile_expert[:, None] == experts[None, :]).astype(I32)
    tile_slot = jnp.sum(hot * parity[None, :], axis=1)
    tile_next = jnp.sum(hot * nxt[None, :], axis=1)
    filled = jnp.sum(hot * counts[None, :], axis=1) - (tiles - jnp.sum(hot * tile_begin[None, :], axis=1)) * EXP_ROWS
    filled = jnp.where(tile_valid, jnp.clip(filled, 0, EXP_ROWS), 0)
    tile_rows = (filled + EXP_GRANULE - 1) // EXP_GRANULE * EXP_GRANULE
    flags = (tile_expert, tile_first.astype(I32), tile_rows.astype(I32),
             tile_slot.astype(I32), tile_next.astype(I32))
    return flags, group_start.astype(I32)


def _slot_kernel(gstart_ref, idx_ref, rank_ref, pos_ref):
    idx = idx_ref[...]
    pos = rank_ref[...]
    for e in range(N_EXPERTS):
        pos = pos + jnp.where(idx == e, gstart_ref[e], 0)
    pos_ref[...] = pos


def _slots(group_start, idx, rank):
    full = pl.BlockSpec(idx.shape, lambda i, gs: (0, 0))
    return pl.pallas_call(
        _slot_kernel,
        grid_spec=pltpu.PrefetchScalarGridSpec(
            num_scalar_prefetch=1, grid=(1,), in_specs=[full, full], out_specs=full),
        out_shape=jax.ShapeDtypeStruct(idx.shape, I32),
        name="slots",
    )(group_start, idx, rank)


def _sc_mesh():
    return plsc.VectorSubcoreMesh(core_axis_name="core", subcore_axis_name="subcore")


def _sc_worker():
    info = plsc.get_sparse_core_info()
    wid = lax.axis_index("subcore") * info.num_cores + lax.axis_index("core")
    return wid, info.num_cores * info.num_subcores


def _dispatch(t2, pos_flat, n_slots):
    n_tok, d = t2.shape
    n_workers = 32
    per_w = n_tok // n_workers
    assert per_w % SC_ROWS == 0

    @functools.partial(
        pl.kernel, mesh=_sc_mesh(),
        out_type=jax.ShapeDtypeStruct((n_slots, d), t2.dtype),
        scratch_types=[pltpu.VMEM((SC_ROWS,), I32), pltpu.VMEM((SC_ROWS, d), t2.dtype)],
        name="dispatch",
    )
    def k(t_hbm, pos_hbm, o_hbm, idx_v, rows_v):
        wid, nw = _sc_worker()
        assert nw == n_workers

        @pl.loop(0, per_w // SC_ROWS)
        def _(c):
            base = pl.multiple_of(wid * per_w + c * SC_ROWS, SC_ROWS)
            pltpu.sync_copy(t_hbm.at[pl.ds(base, SC_ROWS)], rows_v)
            for kk in range(TOP_K):
                pltpu.sync_copy(pos_hbm.at[pl.ds(kk * n_tok + base, SC_ROWS)], idx_v)
                pltpu.sync_copy(rows_v, o_hbm.at[idx_v])

    return k(t2, pos_flat)


def _gather_back(sorted_out, pos_flat):
    n_rows = pos_flat.shape[0]
    d = sorted_out.shape[1]
    n_workers = 32
    per_w = n_rows // n_workers
    assert per_w % SC_ROWS == 0

    @functools.partial(
        pl.kernel, mesh=_sc_mesh(),
        out_type=jax.ShapeDtypeStruct((n_rows, d), sorted_out.dtype),
        scratch_types=[pltpu.VMEM((SC_ROWS,), I32), pltpu.VMEM((SC_ROWS, d), sorted_out.dtype)],
        name="gather_back",
    )
    def k(s_hbm, pos_hbm, o_hbm, idx_v, rows_v):
        wid, nw = _sc_worker()
        assert nw == n_workers

        @pl.loop(0, per_w // SC_ROWS)
        def _(c):
            base = pl.multiple_of(wid * per_w + c * SC_ROWS, SC_ROWS)
            pltpu.sync_copy(pos_hbm.at[pl.ds(base, SC_ROWS)], idx_v)
            pltpu.sync_copy(s_hbm.at[idx_v], rows_v)
            pltpu.sync_copy(rows_v, o_hbm.at[pl.ds(base, SC_ROWS)])

    return k(sorted_out, pos_flat)


def _weight_copies(w_hbm, wf32, sem, expert, slot):
    return [pltpu.make_async_copy(w.at[expert], wf32.at[slot, mtx], sem.at[slot])
            for mtx, w in enumerate(w_hbm)]


def _expert_rows(rows, x_ref, bg_ref, bu_ref, bd_ref, o_ref, wbf):
    xs = jnp.concatenate(_unpack_rows(x_ref[0:rows, :]), axis=1).astype(BF16)
    acc = None
    cn = 256
    for c in range(D_MODEL // cn):
        sl = slice(c * cn, (c + 1) * cn)
        g = jnp.dot(xs, wbf[0, :, sl], preferred_element_type=F32) + bg_ref[:, sl]
        u = jnp.dot(xs, wbf[1, :, sl], preferred_element_type=F32) + bu_ref[:, sl]
        g = jnp.minimum(g, SWIGLU_LIMIT)
        u = jnp.clip(u, -SWIGLU_LIMIT, SWIGLU_LIMIT)
        hid = ((u + 1.0) * (g * jax.nn.sigmoid(SWIGLU_ALPHA * g))).astype(BF16)
        part = jnp.dot(hid, wbf[2, sl, :], preferred_element_type=F32)
        acc = part if acc is None else acc + part
    o_ref[0:rows, :] = _pack_rows(acc + bd_ref[...])
    if rows < EXP_ROWS:
        o_ref[rows:EXP_ROWS, :] = jnp.zeros((EXP_ROWS - rows, HALF), I32)


def _expert_kernel(texp_ref, tfirst_ref, trows_ref, tslot_ref, tnext_ref,
                   x_ref, wg_hbm, bg_ref, wu_hbm, bu_ref, wd_hbm, bd_ref,
                   o_ref, wf32, wbf, wsem):
    i = pl.program_id(0)
    w_hbm = (wg_hbm, wu_hbm, wd_hbm)

    @pl.when(trows_ref[i] == 0)
    def _():
        o_ref[...] = jnp.zeros_like(o_ref)

    @pl.when(tfirst_ref[i] == 1)
    def _():
        slot = tslot_ref[i]
        expert = texp_ref[i]

        @pl.when(i == 0)
        def _():
            for cp in _weight_copies(w_hbm, wf32, wsem, expert, slot):
                cp.start()

        for cp in _weight_copies(w_hbm, wf32, wsem, expert, slot):
            cp.wait()

        @pl.when(tnext_ref[i] >= 0)
        def _():
            for cp in _weight_copies(w_hbm, wf32, wsem, tnext_ref[i], 1 - slot):
                cp.start()

        for mtx in range(3):
            _cast_rows(wf32.at[slot, mtx], wbf.at[mtx], D_MODEL)

    for rows in range(EXP_GRANULE, EXP_ROWS + 1, EXP_GRANULE):
        @pl.when(trows_ref[i] == rows)
        def _(rows=rows):
            _expert_rows(rows, x_ref, bg_ref, bu_ref, bd_ref, o_ref, wbf)


def _experts(sorted_t, tile_flags, w_gate, b_gate, w_up, b_up, w_down, b_down, n_tiles):
    row_blk = pl.BlockSpec((EXP_ROWS, HALF), lambda i, *_: (i, 0))
    w_any = pl.BlockSpec(memory_space=pl.ANY)
    b_blk = pl.BlockSpec((None, 1, D_MODEL), lambda i, te, *_: (te[i], 0, 0))
    grid_spec = pltpu.PrefetchScalarGridSpec(
        num_scalar_prefetch=len(tile_flags),
        grid=(n_tiles,),
        in_specs=[row_blk, w_any, b_blk, w_any, b_blk, w_any, b_blk],
        out_specs=row_blk,
        scratch_shapes=[
            pltpu.VMEM((2, 3, D_MODEL, D_MODEL), F32),
            pltpu.VMEM((3, D_MODEL, D_MODEL), BF16),
            pltpu.SemaphoreType.DMA((2,)),
        ],
    )
    b3 = lambda b: b.reshape(N_EXPERTS, 1, D_MODEL)
    return pl.pallas_call(
        _expert_kernel,
        grid_spec=grid_spec,
        out_shape=jax.ShapeDtypeStruct((n_tiles * EXP_ROWS, HALF), I32),
        compiler_params=pltpu.CompilerParams(
            dimension_semantics=("arbitrary",), vmem_limit_bytes=VMEM_LIMIT),
        name="experts",
    )(*tile_flags, sorted_t,
      w_gate, b3(b_gate), w_up, b3(b_up), w_down, b3(b_down))


def _combine_kernel(x1_ref, s0_ref, s1_ref, s2_ref, s3_ref, w_ref, g_ref, o_ref):
    w = w_ref[...]
    y_lo = x1_ref[:, :HALF]
    y_hi = x1_ref[:, HALF:]
    for k, s_ref in enumerate((s0_ref, s1_ref, s2_ref, s3_ref)):
        lo, hi = _unpack_rows(s_ref[...])
        y_lo = y_lo + w[:, k:k + 1] * lo
        y_hi = y_hi + w[:, k:k + 1] * hi
    sq = jnp.sum(y_lo * y_lo, axis=-1, keepdims=True) + jnp.sum(y_hi * y_hi, axis=-1, keepdims=True)
    scale = lax.rsqrt(sq * (1.0 / D_MODEL) + RMS_EPS)
    o_ref[:, :HALF] = (y_lo * scale) * g_ref[:, :HALF]
    o_ref[:, HALF:] = (y_hi * scale) * g_ref[:, HALF:]


def _combine(x1, slabs, wgt, norm_g):
    n_tok = x1.shape[0]
    rows = MIX_ROWS
    ntiles = n_tok // rows
    slab_spec = lambda k: pl.BlockSpec((rows, HALF), lambda i, k=k: (k * ntiles + i, 0))
    return pl.pallas_call(
        _combine_kernel,
        grid=(ntiles,),
        in_specs=[pl.BlockSpec((rows, D_MODEL), lambda i: (i, 0))]
        + [slab_spec(k) for k in range(TOP_K)]
        + [pl.BlockSpec((rows, 8), lambda i: (i, 0)),
           pl.BlockSpec((1, D_MODEL), lambda i: (0, 0))],
        out_specs=pl.BlockSpec((rows, D_MODEL), lambda i: (i, 0)),
        out_shape=jax.ShapeDtypeStruct((n_tok, D_MODEL), F32),
        compiler_params=pltpu.CompilerParams(
            dimension_semantics=("arbitrary",), vmem_limit_bytes=VMEM_LIMIT),
        name="combine",
    )(x1, slabs, slabs, slabs, slabs, wgt, norm_g.reshape(1, D_MODEL))


def kernel(x, norm_mix_g, w_in, conv_dw_w, conv_dw_b, conv_ln_g, conv_ln_b, rel_bias, w_out,
           norm_ffn_g, router_w, router_b, exp_w_gate, exp_b_gate, exp_w_up, exp_b_up,
           exp_w_down, exp_b_down, norm_final_g):
    bsz, seq, _ = x.shape
    n_tok = bsz * seq
    assert norm_mix_g.shape[0] == 1, "single-layer block"
    assert seq % IN_ROWS == 0 and seq % MIX_ROWS == 0 and IN_ROWS == LEFT_PAD
    n_tiles = (TOP_K * n_tok) // EXP_ROWS + N_EXPERTS - 1

    q, kpad, vpad, conv = _inproj(x, norm_mix_g[0], w_in[0], conv_dw_w[0], conv_dw_b[0],
                                  conv_ln_g[0], conv_ln_b[0])
    attn = _attention(q, kpad, vpad, _band_bias(rel_bias[0]))
    x1, t, idx, wgt, rank, cnt = _mix_route(
        x.reshape(n_tok, D_MODEL), attn.reshape(n_tok, D_ATTN), conv.reshape(n_tok, D_CONV),
        w_out[0], norm_ffn_g[0], router_w[0], router_b[0])
    tile_flags, group_start = _tile_plan(cnt[:, 0].astype(I32), n_tiles)
    pos_flat = _slots(group_start, idx, rank).reshape(TOP_K * n_tok)
    sorted_t = _dispatch(t, pos_flat, n_tiles * EXP_ROWS)
    sorted_out = _experts(sorted_t, tile_flags,
                          exp_w_gate[0], exp_b_gate[0], exp_w_up[0], exp_b_up[0],
                          exp_w_down[0], exp_b_down[0], n_tiles)
    slabs = _gather_back(sorted_out, pos_flat)
    out = _combine(x1, slabs, wgt, norm_final_g)
    return out.reshape(bsz, seq, D_MODEL)
```

```python
import functools

import jax
import jax.numpy as jnp
from jax import lax
from jax.experimental import pallas as pl
from jax.experimental.pallas import tpu as pltpu
from jax.experimental.pallas import tpu_sc as plsc

F32 = jnp.float32
BF16 = jnp.bfloat16
I32 = jnp.int32

D_MODEL = 1024
CHUNK = 64
N_HEADS = 8
HEAD_DIM = 64
D_ATTN = N_HEADS * HEAD_DIM
LEFT_CHUNKS = 8
REL_MAX = 128
REL_MIN = -(CHUNK - 1)
D_CONV = D_MODEL - D_ATTN
CONV_WIDTH = 31
N_EXPERTS = 32
TOP_K = 4
SWIGLU_ALPHA = 1.702
SWIGLU_LIMIT = 7.0
RMS_EPS = 1e-5
LN_EPS = 1e-5

LEFT_PAD = LEFT_CHUNKS * CHUNK
IN_ROWS = 512
Q_ROWS = 2 * CHUNK
ATT_BLOCKS = 4
BAND_ROWS = Q_ROWS + LEFT_PAD
HEADS_PER_STEP = 4
GROUP_LANES = HEADS_PER_STEP * HEAD_DIM
MIX_ROWS = 512
HALO_ROWS = 32
EXP_ROWS = 512
EXP_GRANULE = 128
SC_ROWS = 128
NEG_BIG = -1e30
VMEM_LIMIT = 56 * 1024 * 1024


HALF = D_MODEL // 2
HI_MASK = -65536


def _pack_rows(x):
    bits = lax.bitcast_convert_type(x.astype(BF16).astype(F32), I32)
    return lax.shift_right_logical(bits[:, :HALF], 16) | (bits[:, HALF:] & HI_MASK)


def _unpack_rows(w):
    lo = lax.bitcast_convert_type(lax.shift_left(w, 16), F32)
    hi = lax.bitcast_convert_type(w & HI_MASK, F32)
    return lo, hi


def _cast_rows(src_ref, dst_ref, rows, step=128):
    def body(c, carry):
        r = pl.multiple_of(c * step, step)
        dst_ref[pl.ds(r, step), :] = src_ref[pl.ds(r, step), :].astype(dst_ref.dtype)
        return carry
    lax.fori_loop(0, rows // step, body, 0)


def _conv_branch(hw_ref, sh_ref, rows, cw_ref, cb_ref, lg_ref, lb_ref, out_ref):
    off = HALO_ROWS - (CONV_WIDTH - 1)
    shift_rows = HALO_ROWS + rows - 8
    for b in range(1, 8):
        sh_ref[b - 1] = hw_ref[pl.ds(b, shift_rows), :]
    acc = None
    for j in range(CONV_WIDTH):
        a, b = divmod(off + j, 8)
        src = hw_ref if b == 0 else sh_ref.at[b - 1]
        term = src[pl.ds(8 * a, rows), :] * cw_ref[j:j + 1, :]
        acc = term if acc is None else acc + term
    acc = acc + cb_ref[...]
    mu = jnp.mean(acc, axis=-1, keepdims=True)
    d = acc - mu
    var = jnp.mean(d * d, axis=-1, keepdims=True)
    y = d * lax.rsqrt(var + LN_EPS) * lg_ref[...] + lb_ref[...]
    out_ref[...] = (y * jax.nn.sigmoid(y)).astype(out_ref.dtype)


def _inproj_kernel(x_ref, g_ref, w_ref, cw_ref, cb_ref, lg_ref, lb_ref,
                   q_ref, k_ref, v_ref, c_ref, wbf_ref, hw_ref, sh_ref):
    b = pl.program_id(0)
    j = pl.program_id(1)

    @pl.when((b == 0) & (j == 0))
    def _():
        _cast_rows(w_ref, wbf_ref, D_MODEL)

    @pl.when(j == 0)
    def _():
        k_ref[...] = jnp.zeros_like(k_ref)
        v_ref[...] = jnp.zeros_like(v_ref)
        hw_ref[IN_ROWS:IN_ROWS + HALO_ROWS, :] = jnp.zeros((HALO_ROWS, D_CONV), F32)

    @pl.when(j > 0)
    def _():
        x = x_ref[...]
        ms = jnp.mean(x * x, axis=-1, keepdims=True)
        hb = ((x * lax.rsqrt(ms + RMS_EPS)) * g_ref[...]).astype(BF16)

        def proj(c0, width):
            return jnp.dot(hb, wbf_ref[:, c0:c0 + width], preferred_element_type=F32)

        a = proj(3 * D_ATTN, D_CONV)
        gate = proj(3 * D_ATTN + D_CONV, D_CONV)
        hw_ref[0:HALO_ROWS, :] = hw_ref[IN_ROWS:IN_ROWS + HALO_ROWS, :]
        hw_ref[HALO_ROWS:HALO_ROWS + IN_ROWS, :] = a * jax.nn.sigmoid(gate)
        _conv_branch(hw_ref, sh_ref, IN_ROWS, cw_ref, cb_ref, lg_ref, lb_ref, c_ref)

        q_ref[...] = (proj(0, D_ATTN) * (HEAD_DIM ** -0.5)).astype(BF16)
        k_ref[...] = proj(D_ATTN, D_ATTN).astype(BF16)
        v_ref[...] = proj(2 * D_ATTN, D_ATTN).astype(BF16)


def _inproj(x, norm_g, w_in, conv_w, conv_b, ln_g, ln_b):
    bsz, seq, _ = x.shape
    nblk = seq // IN_ROWS
    d_cols = w_in.shape[1]
    row_blk = lambda b, j: (b, jnp.maximum(j - 1, 0), 0)
    const = lambda b, j: (0, 0)
    vec = lambda v: v.reshape(1, -1)
    return pl.pallas_call(
        _inproj_kernel,
        grid=(bsz, nblk + 1),
        in_specs=[
            pl.BlockSpec((None, IN_ROWS, D_MODEL), row_blk),
            pl.BlockSpec((1, D_MODEL), const),
            pl.BlockSpec((D_MODEL, d_cols), const),
            pl.BlockSpec((CONV_WIDTH, D_CONV), const),
            pl.BlockSpec((1, D_CONV), const),
            pl.BlockSpec((1, D_CONV), const),
            pl.BlockSpec((1, D_CONV), const),
        ],
        out_specs=[
            pl.BlockSpec((None, IN_ROWS, D_ATTN), row_blk),
            pl.BlockSpec((None, IN_ROWS, D_ATTN), lambda b, j: (b, j, 0)),
            pl.BlockSpec((None, IN_ROWS, D_ATTN), lambda b, j: (b, j, 0)),
            pl.BlockSpec((None, IN_ROWS, D_CONV), row_blk),
        ],
        out_shape=[
            jax.ShapeDtypeStruct((bsz, seq, D_ATTN), BF16),
            jax.ShapeDtypeStruct((bsz, seq + LEFT_PAD, D_ATTN), BF16),
            jax.ShapeDtypeStruct((bsz, seq + LEFT_PAD, D_ATTN), BF16),
            jax.ShapeDtypeStruct((bsz, seq, D_CONV), BF16),
        ],
        scratch_shapes=[
            pltpu.VMEM((D_MODEL, d_cols), BF16),
            pltpu.VMEM((HALO_ROWS + IN_ROWS, D_CONV), F32),
            pltpu.VMEM((7, HALO_ROWS + IN_ROWS - 8, D_CONV), F32),
        ],
        compiler_params=pltpu.CompilerParams(
            dimension_semantics=("arbitrary", "arbitrary"), vmem_limit_bytes=VMEM_LIMIT),
        name="inproj",
    )(x, vec(norm_g), w_in, conv_w, vec(conv_b), vec(ln_g), vec(ln_b))


def _attn_kernel(q_ref, k_ref, v_ref, bias_ref, o_ref):
    i = pl.program_id(1)
    lane = lax.broadcasted_iota(I32, (Q_ROWS, GROUP_LANES), 1) // HEAD_DIM
    col = lax.broadcasted_iota(I32, (HEADS_PER_STEP * Q_ROWS, BAND_ROWS), 1)
    for qb in range(ATT_BLOCKS):
        blk = i * ATT_BLOCKS + qb
        start = pl.multiple_of(blk * Q_ROWS, Q_ROWS)
        qrows = slice(qb * Q_ROWS, (qb + 1) * Q_ROWS)
        key_ok = col >= LEFT_PAD - blk * Q_ROWS
        for g in range(N_HEADS // HEADS_PER_STEP):
            lanes = slice(g * GROUP_LANES, (g + 1) * GROUP_LANES)
            q = q_ref[qrows, lanes]
            qs = jnp.concatenate(
                [jnp.where(lane == h, q, jnp.zeros_like(q)) for h in range(HEADS_PER_STEP)], axis=0)
            kb = k_ref[pl.ds(start, BAND_ROWS), lanes]
            vb = v_ref[pl.ds(start, BAND_ROWS), lanes]
            s = lax.dot_general(qs, kb, (((1,), (1,)), ((), ())), preferred_element_type=F32)
            rows = slice(g * HEADS_PER_STEP * Q_ROWS, (g + 1) * HEADS_PER_STEP * Q_ROWS)
            s = jnp.where(key_ok, s + bias_ref[rows, :], NEG_BIG)
            m = jnp.max(s, axis=-1, keepdims=True)
            p = jnp.exp(s - m)
            l = jnp.sum(p, axis=-1, keepdims=True)
            o = jnp.dot(p.astype(BF16), vb, preferred_element_type=F32) / l
            out = o[0:Q_ROWS]
            for h in range(1, HEADS_PER_STEP):
                out = jnp.where(lane == h, o[h * Q_ROWS:(h + 1) * Q_ROWS], out)
            o_ref[qrows, lanes] = out.astype(o_ref.dtype)


def _attention(q, kpad, vpad, bias):
    bsz, seq, _ = q.shape
    step_rows = ATT_BLOCKS * Q_ROWS
    return pl.pallas_call(
        _attn_kernel,
        grid=(bsz, seq // step_rows),
        in_specs=[
            pl.BlockSpec((None, step_rows, D_ATTN), lambda b, i: (b, i, 0)),
            pl.BlockSpec((None, seq + LEFT_PAD, D_ATTN), lambda b, i: (b, 0, 0)),
            pl.BlockSpec((None, seq + LEFT_PAD, D_ATTN), lambda b, i: (b, 0, 0)),
            pl.BlockSpec((N_HEADS * Q_ROWS, BAND_ROWS), lambda b, i: (0, 0)),
        ],
        out_specs=pl.BlockSpec((None, step_rows, D_ATTN), lambda b, i: (b, i, 0)),
        out_shape=jax.ShapeDtypeStruct((bsz, seq, D_ATTN), BF16),
        compiler_params=pltpu.CompilerParams(
            dimension_semantics=("arbitrary", "arbitrary"), vmem_limit_bytes=VMEM_LIMIT),
        name="chunk_attn",
    )(q, kpad, vpad, bias)


def _band_bias(rel_bias):
    n_rel = REL_MAX - REL_MIN + 1
    far = jnp.broadcast_to(rel_bias[:, n_rel - 1:n_rel], (N_HEADS, BAND_ROWS - 1 - REL_MAX))
    near = rel_bias[:, ::-1]
    ahead = jnp.broadcast_to(rel_bias[:, 0:1], (N_HEADS, Q_ROWS - 1 + REL_MIN))
    diag = jnp.concatenate([far, near, ahead], axis=1).astype(F32)
    bias = jnp.stack(
        [diag[:, Q_ROWS - 1 - r:Q_ROWS - 1 - r + BAND_ROWS] for r in range(Q_ROWS)], axis=1)
    r = jnp.arange(Q_ROWS)[:, None]
    m = jnp.arange(BAND_ROWS)[None, :]
    cq = r // CHUNK
    ck = m // CHUNK
    in_band = (ck >= cq) & (ck <= cq + LEFT_CHUNKS)
    bias = jnp.where(in_band[None], bias, NEG_BIG)
    return bias.reshape(N_HEADS * Q_ROWS, BAND_ROWS)


def _split_bf16(v):
    hi = v.astype(BF16)
    lo = (v - hi.astype(F32)).astype(BF16)
    return hi, lo


def _mix_route_kernel(x_ref, a_ref, c_ref, wo_ref, ng_ref, rwt_ref, rb_ref,
                      x1_ref, t_ref, idx_ref, wgt_ref, rank_ref, cnt_ref,
                      wobf_ref, cntacc_ref):
    i = pl.program_id(0)

    @pl.when(i == 0)
    def _():
        _cast_rows(wo_ref, wobf_ref, D_MODEL)
        cntacc_ref[...] = jnp.zeros_like(cntacc_ref)

    mixed = jnp.dot(a_ref[...], wobf_ref[0:D_ATTN, :], preferred_element_type=F32)
    mixed = mixed + jnp.dot(c_ref[...], wobf_ref[D_ATTN:D_MODEL, :], preferred_element_type=F32)
    x1 = x_ref[...] + mixed
    x1_ref[...] = x1
    ms = jnp.mean(x1 * x1, axis=-1, keepdims=True)
    t = (x1 * lax.rsqrt(ms + RMS_EPS)) * ng_ref[...]
    t_ref[...] = _pack_rows(t)

    nt = (((1,), (1,)), ((), ()))
    w_hi, w_lo = _split_bf16(rwt_ref[...])
    t_hi, t_lo = _split_bf16(t)
    logits = (lax.dot_general(w_hi, t_hi, nt, preferred_element_type=F32)
              + lax.dot_general(w_hi, t_lo, nt, preferred_element_type=F32)
              + lax.dot_general(w_lo, t_hi, nt, preferred_element_type=F32)) + rb_ref[...]
    e_iota = lax.broadcasted_iota(I32, (N_EXPERTS, MIX_ROWS), 0)
    vals, idxs, hots = [], [], []
    for _ in range(TOP_K):
        m = jnp.max(logits, axis=0, keepdims=True)
        am = jnp.min(jnp.where(logits == m, e_iota, N_EXPERTS), axis=0, keepdims=True)
        hot = e_iota == am
        vals.append(m)
        idxs.append(am)
        hots.append(hot)
        logits = jnp.where(hot, -jnp.inf, logits)
    exps = [jnp.exp(v - vals[0]) for v in vals]
    den = exps[0] + exps[1] + exps[2] + exps[3]
    wts = [e / den for e in exps]

    hot_f = (hots[0] | hots[1] | hots[2] | hots[3]).astype(F32)
    ra = lax.broadcasted_iota(I32, (MIX_ROWS, MIX_ROWS), 0)
    rc = lax.broadcasted_iota(I32, (MIX_ROWS, MIX_ROWS), 1)
    upper = (ra < rc).astype(BF16)
    prefix = jnp.dot(hot_f.astype(BF16), upper, preferred_element_type=F32)
    base = prefix + cntacc_ref[...]
    ranks = [jnp.sum(jnp.where(h, base, 0.0), axis=0, keepdims=True) for h in hots]
    counts = cntacc_ref[...] + jnp.sum(hot_f, axis=1, keepdims=True)
    cntacc_ref[...] = counts

    idx_ref[...] = jnp.concatenate(idxs, axis=0)
    rank_ref[...] = jnp.concatenate(ranks, axis=0).astype(I32)
    cnt_ref[...] = jnp.broadcast_to(counts, cnt_ref.shape)
    w8 = jnp.concatenate(wts + [jnp.zeros((8 - TOP_K, MIX_ROWS), F32)], axis=0)
    wgt_ref[...] = w8.T


def _mix_route(x2, attn2, conv2, w_out, norm_g, router_w, router_b):
    n_tok = x2.shape[0]
    row = lambda i: (i, 0)
    const = lambda i: (0, 0)
    vec = lambda v: v.reshape(1, -1)
    return pl.pallas_call(
        _mix_route_kernel,
        grid=(n_tok // MIX_ROWS,),
        in_specs=[
            pl.BlockSpec((MIX_ROWS, D_MODEL), row),
            pl.BlockSpec((MIX_ROWS, D_ATTN), row),
            pl.BlockSpec((MIX_ROWS, D_CONV), row),
            pl.BlockSpec((D_MODEL, D_MODEL), const),
            pl.BlockSpec((1, D_MODEL), const),
            pl.BlockSpec((N_EXPERTS, D_MODEL), const),
            pl.BlockSpec((N_EXPERTS, 1), const),
        ],
        out_specs=[
            pl.BlockSpec((MIX_ROWS, D_MODEL), row),
            pl.BlockSpec((MIX_ROWS, HALF), row),
            pl.BlockSpec((TOP_K, MIX_ROWS), lambda i: (0, i)),
            pl.BlockSpec((MIX_ROWS, 8), row),
            pl.BlockSpec((TOP_K, MIX_ROWS), lambda i: (0, i)),
            pl.BlockSpec((N_EXPERTS, 128), const),
        ],
        out_shape=[
            jax.ShapeDtypeStruct((n_tok, D_MODEL), F32),
            jax.ShapeDtypeStruct((n_tok, HALF), I32),
            jax.ShapeDtypeStruct((TOP_K, n_tok), I32),
            jax.ShapeDtypeStruct((n_tok, 8), F32),
            jax.ShapeDtypeStruct((TOP_K, n_tok), I32),
            jax.ShapeDtypeStruct((N_EXPERTS, 128), F32),
        ],
        scratch_shapes=[
            pltpu.VMEM((D_MODEL, D_MODEL), BF16),
            pltpu.VMEM((N_EXPERTS, 1), F32),
        ],
        compiler_params=pltpu.CompilerParams(
            dimension_semantics=("arbitrary",), vmem_limit_bytes=VMEM_LIMIT),
        name="mix_route",
    )(x2, attn2, conv2, w_out, vec(norm_g), router_w.T, router_b.reshape(N_EXPERTS, 1))


def _tile_plan(counts, n_tiles):
    tiles_per = (counts + EXP_ROWS - 1) // EXP_ROWS
    tile_end = jnp.cumsum(tiles_per)
    tile_begin = tile_end - tiles_per
    n_valid = tile_end[-1]
    tiles = jnp.arange(n_tiles, dtype=I32)
    tile_valid = tiles < n_valid
    capped = jnp.minimum(tiles, n_valid - 1)
    tile_expert = jnp.sum((capped[:, None] >= tile_end[None, :]).astype(I32), axis=1)
    tile_expert = jnp.minimum(tile_expert, N_EXPERTS - 1)
    is_first = (tiles[:, None] == tile_begin[None, :]) & (tiles_per[None, :] > 0)
    tile_first = tile_valid & jnp.any(is_first, axis=1)
    group_start = tile_begin * EXP_ROWS
    experts = jnp.arange(N_EXPERTS, dtype=I32)
    nonempty = tiles_per > 0
    parity = (jnp.cumsum(nonempty.astype(I32)) - 1) % 2
    later = nonempty[None, :] & (experts[None, :] > experts[:, None])
    nxt = jnp.min(jnp.where(later, experts[None, :], N_EXPERTS), axis=1)
    nxt = jnp.where(nxt == N_EXPERTS, -1, nxt)
    hot = (tile_expert[:, None] == experts[None, :]).astype(I32)
    tile_slot = jnp.sum(hot * parity[None, :], axis=1)
    tile_next = jnp.sum(hot * nxt[None, :], axis=1)
    filled = jnp.sum(hot * counts[None, :], axis=1) - (tiles - jnp.sum(hot * tile_begin[None, :], axis=1)) * EXP_ROWS
    filled = jnp.where(tile_valid, jnp.clip(filled, 0, EXP_ROWS), 0)
    tile_rows = (filled + EXP_GRANULE - 1) // EXP_GRANULE * EXP_GRANULE
    flags = (tile_expert, tile_first.astype(I32), tile_rows.astype(I32),
             tile_slot.astype(I32), tile_next.astype(I32))
    return flags, group_start.astype(I32)


def _slot_kernel(gstart_ref, idx_ref, rank_ref, pos_ref):
    idx = idx_ref[...]
    pos = rank_ref[...]
    for e in range(N_EXPERTS):
        pos = pos + jnp.where(idx == e, gstart_ref[e], 0)
    pos_ref[...] = pos


def _slots(group_start, idx, rank):
    full = pl.BlockSpec(idx.shape, lambda i, gs: (0, 0))
    return pl.pallas_call(
        _slot_kernel,
        grid_spec=pltpu.PrefetchScalarGridSpec(
            num_scalar_prefetch=1, grid=(1,), in_specs=[full, full], out_specs=full),
        out_shape=jax.ShapeDtypeStruct(idx.shape, I32),
        name="slots",
    )(group_start, idx, rank)


def _sc_mesh():
    return plsc.VectorSubcoreMesh(core_axis_name="core", subcore_axis_name="subcore")


def _sc_worker():
    info = plsc.get_sparse_core_info()
    wid = lax.axis_index("subcore") * info.num_cores + lax.axis_index("core")
    return wid, info.num_cores * info.num_subcores


def _dispatch(t2, pos_flat, n_slots):
    n_tok, d = t2.shape
    n_workers = 32
    per_w = n_tok // n_workers
    assert per_w % SC_ROWS == 0

    @functools.partial(
        pl.kernel, mesh=_sc_mesh(),
        out_type=jax.ShapeDtypeStruct((n_slots, d), t2.dtype),
        scratch_types=[pltpu.VMEM((SC_ROWS,), I32), pltpu.VMEM((SC_ROWS, d), t2.dtype)],
        name="dispatch",
    )
    def k(t_hbm, pos_hbm, o_hbm, idx_v, rows_v):
        wid, nw = _sc_worker()
        assert nw == n_workers

        @pl.loop(0, per_w // SC_ROWS)
        def _(c):
            base = pl.multiple_of(wid * per_w + c * SC_ROWS, SC_ROWS)
            pltpu.sync_copy(t_hbm.at[pl.ds(base, SC_ROWS)], rows_v)
            for kk in range(TOP_K):
                pltpu.sync_copy(pos_hbm.at[pl.ds(kk * n_tok + base, SC_ROWS)], idx_v)
                pltpu.sync_copy(rows_v, o_hbm.at[idx_v])

    return k(t2, pos_flat)


def _gather_back(sorted_out, pos_flat):
    n_rows = pos_flat.shape[0]
    d = sorted_out.shape[1]
    n_workers = 32
    per_w = n_rows // n_workers
    assert per_w % SC_ROWS == 0

    @functools.partial(
        pl.kernel, mesh=_sc_mesh(),
        out_type=jax.ShapeDtypeStruct((n_rows, d), sorted_out.dtype),
        scratch_types=[pltpu.VMEM((SC_ROWS,), I32), pltpu.VMEM((SC_ROWS, d), sorted_out.dtype)],
        name="gather_back",
    )
    def k(s_hbm, pos_hbm, o_hbm, idx_v, rows_v):
        wid, nw = _sc_worker()
        assert nw == n_workers

        @pl.loop(0, per_w // SC_ROWS)
        def _(c):
            base = pl.multiple_of(wid * per_w + c * SC_ROWS, SC_ROWS)
            pltpu.sync_copy(pos_hbm.at[pl.ds(base, SC_ROWS)], idx_v)
            pltpu.sync_copy(s_hbm.at[idx_v], rows_v)
            pltpu.sync_copy(rows_v, o_hbm.at[pl.ds(base, SC_ROWS)])

    return k(sorted_out, pos_flat)


def _weight_copies(w_hbm, wf32, sem, expert, slot):
    return [pltpu.make_async_copy(w.at[expert], wf32.at[slot, mtx], sem.at[slot])
            for mtx, w in enumerate(w_hbm)]


def _expert_rows(rows, x_ref, bg_ref, bu_ref, bd_ref, o_ref, wbf):
    xs = jnp.concatenate(_unpack_rows(x_ref[0:rows, :]), axis=1).astype(BF16)
    cn = 256
    hids = []
    for c in range(D_MODEL // cn):
        sl = slice(c * cn, (c + 1) * cn)
        g = jnp.dot(xs, wbf[0, :, sl], preferred_element_type=F32) + bg_ref[:, sl]
        u = jnp.dot(xs, wbf[1, :, sl], preferred_element_type=F32) + bu_ref[:, sl]
        g = jnp.minimum(g, SWIGLU_LIMIT)
        u = jnp.clip(u, -SWIGLU_LIMIT, SWIGLU_LIMIT)
        hids.append(((u + 1.0) * (g * jax.nn.sigmoid(SWIGLU_ALPHA * g))).astype(BF16))
    acc = jnp.dot(jnp.concatenate(hids, axis=1), wbf[2], preferred_element_type=F32)
    o_ref[0:rows, :] = _pack_rows(acc + bd_ref[...])
    if rows < EXP_ROWS:
        o_ref[rows:EXP_ROWS, :] = jnp.zeros((EXP_ROWS - rows, HALF), I32)


def _expert_kernel(texp_ref, tfirst_ref, trows_ref, tslot_ref, tnext_ref,
                   x_ref, wg_hbm, bg_ref, wu_hbm, bu_ref, wd_hbm, bd_ref,
                   o_ref, wf32, wbf, wsem):
    i = pl.program_id(0)
    w_hbm = (wg_hbm, wu_hbm, wd_hbm)

    @pl.when(trows_ref[i] == 0)
    def _():
        o_ref[...] = jnp.zeros_like(o_ref)

    @pl.when(tfirst_ref[i] == 1)
    def _():
        slot = tslot_ref[i]
        expert = texp_ref[i]

        @pl.when(i == 0)
        def _():
            for cp in _weight_copies(w_hbm, wf32, wsem, expert, slot):
                cp.start()

        for cp in _weight_copies(w_hbm, wf32, wsem, expert, slot):
            cp.wait()

        @pl.when(tnext_ref[i] >= 0)
        def _():
            for cp in _weight_copies(w_hbm, wf32, wsem, tnext_ref[i], 1 - slot):
                cp.start()

        for mtx in range(3):
            _cast_rows(wf32.at[slot, mtx], wbf.at[mtx], D_MODEL)

    for rows in range(EXP_GRANULE, EXP_ROWS + 1, EXP_GRANULE):
        @pl.when(trows_ref[i] == rows)
        def _(rows=rows):
            _expert_rows(rows, x_ref, bg_ref, bu_ref, bd_ref, o_ref, wbf)


def _experts(sorted_t, tile_flags, w_gate, b_gate, w_up, b_up, w_down, b_down, n_tiles):
    row_blk = pl.BlockSpec((EXP_ROWS, HALF), lambda i, *_: (i, 0))
    w_any = pl.BlockSpec(memory_space=pl.ANY)
    b_blk = pl.BlockSpec((None, 1, D_MODEL), lambda i, te, *_: (te[i], 0, 0))
    grid_spec = pltpu.PrefetchScalarGridSpec(
        num_scalar_prefetch=len(tile_flags),
        grid=(n_tiles,),
        in_specs=[row_blk, w_any, b_blk, w_any, b_blk, w_any, b_blk],
        out_specs=row_blk,
        scratch_shapes=[
            pltpu.VMEM((2, 3, D_MODEL, D_MODEL), F32),
            pltpu.VMEM((3, D_MODEL, D_MODEL), BF16),
            pltpu.SemaphoreType.DMA((2,)),
        ],
    )
    b3 = lambda b: b.reshape(N_EXPERTS, 1, D_MODEL)
    return pl.pallas_call(
        _expert_kernel,
        grid_spec=grid_spec,
        out_shape=jax.ShapeDtypeStruct((n_tiles * EXP_ROWS, HALF), I32),
        compiler_params=pltpu.CompilerParams(
            dimension_semantics=("arbitrary",), vmem_limit_bytes=VMEM_LIMIT),
        name="experts",
    )(*tile_flags, sorted_t,
      w_gate, b3(b_gate), w_up, b3(b_up), w_down, b3(b_down))


def _combine_kernel(x1_ref, s0_ref, s1_ref, s2_ref, s3_ref, w_ref, g_ref, o_ref):
    w = w_ref[...]
    y_lo = x1_ref[:, :HALF]
    y_hi = x1_ref[:, HALF:]
    for k, s_ref in enumerate((s0_ref, s1_ref, s2_ref, s3_ref)):
        lo, hi = _unpack_rows(s_ref[...])
        y_lo = y_lo + w[:, k:k + 1] * lo
        y_hi = y_hi + w[:, k:k + 1] * hi
    sq = jnp.sum(y_lo * y_lo, axis=-1, keepdims=True) + jnp.sum(y_hi * y_hi, axis=-1, keepdims=True)
    scale = lax.rsqrt(sq * (1.0 / D_MODEL) + RMS_EPS)
    o_ref[:, :HALF] = (y_lo * scale) * g_ref[:, :HALF]
    o_ref[:, HALF:] = (y_hi * scale) * g_ref[:, HALF:]


def _combine(x1, slabs, wgt, norm_g):
    n_tok = x1.shape[0]
    rows = MIX_ROWS
    ntiles = n_tok // rows
    slab_spec = lambda k: pl.BlockSpec((rows, HALF), lambda i, k=k: (k * ntiles + i, 0))
    return pl.pallas_call(
        _combine_kernel,
        grid=(ntiles,),
        in_specs=[pl.BlockSpec((rows, D_MODEL), lambda i: (i, 0))]
        + [slab_spec(k) for k in range(TOP_K)]
        + [pl.BlockSpec((rows, 8), lambda i: (i, 0)),
           pl.BlockSpec((1, D_MODEL), lambda i: (0, 0))],
        out_specs=pl.BlockSpec((rows, D_MODEL), lambda i: (i, 0)),
        out_shape=jax.ShapeDtypeStruct((n_tok, D_MODEL), F32),
        compiler_params=pltpu.CompilerParams(
            dimension_semantics=("arbitrary",), vmem_limit_bytes=VMEM_LIMIT),
        name="combine",
    )(x1, slabs, slabs, slabs, slabs, wgt, norm_g.reshape(1, D_MODEL))


def kernel(x, norm_mix_g, w_in, conv_dw_w, conv_dw_b, conv_ln_g, conv_ln_b, rel_bias, w_out,
           norm_ffn_g, router_w, router_b, exp_w_gate, exp_b_gate, exp_w_up, exp_b_up,
           exp_w_down, exp_b_down, norm_final_g):
    bsz, seq, _ = x.shape
    n_tok = bsz * seq
    assert norm_mix_g.shape[0] == 1, "single-layer block"
    assert seq % IN_ROWS == 0 and seq % MIX_ROWS == 0 and IN_ROWS == LEFT_PAD
    n_tiles = (TOP_K * n_tok) // EXP_ROWS + N_EXPERTS - 1

    q, kpad, vpad, conv = _inproj(x, norm_mix_g[0], w_in[0], conv_dw_w[0], conv_dw_b[0],
                                  conv_ln_g[0], conv_ln_b[0])
    attn = _attention(q, kpad, vpad, _band_bias(rel_bias[0]))
    x1, t, idx, wgt, rank, cnt = _mix_route(
        x.reshape(n_tok, D_MODEL), attn.reshape(n_tok, D_ATTN), conv.reshape(n_tok, D_CONV),
        w_out[0], norm_ffn_g[0], router_w[0], router_b[0])
    tile_flags, group_start = _tile_plan(cnt[:, 0].astype(I32), n_tiles)
    pos_flat = _slots(group_start, idx, rank).reshape(TOP_K * n_tok)
    sorted_t = _dispatch(t, pos_flat, n_tiles * EXP_ROWS)
    sorted_out = _experts(sorted_t, tile_flags,
                          exp_w_gate[0], exp_b_gate[0], exp_w_up[0], exp_b_up[0],
                          exp_w_down[0], exp_b_down[0], n_tiles)
    slabs = _gather_back(sorted_out, pos_flat)
    out = _combine(x1, slabs, wgt, norm_final_g)
    return out.reshape(bsz, seq, D_MODEL)
```

```python
import functools

import jax
import jax.numpy as jnp
from jax import lax
from jax.experimental import pallas as pl
from jax.experimental.pallas import tpu as pltpu
from jax.experimental.pallas import tpu_sc as plsc

F32 = jnp.float32
BF16 = jnp.bfloat16
I32 = jnp.int32

D_MODEL = 1024
CHUNK = 64
N_HEADS = 8
HEAD_DIM = 64
D_ATTN = N_HEADS * HEAD_DIM
LEFT_CHUNKS = 8
REL_MAX = 128
REL_MIN = -(CHUNK - 1)
D_CONV = D_MODEL - D_ATTN
CONV_WIDTH = 31
N_EXPERTS = 32
TOP_K = 4
SWIGLU_ALPHA = 1.702
SWIGLU_LIMIT = 7.0
RMS_EPS = 1e-5
LN_EPS = 1e-5

LEFT_PAD = LEFT_CHUNKS * CHUNK
IN_ROWS = 512
Q_ROWS = 2 * CHUNK
ATT_BLOCKS = 4
BAND_ROWS = Q_ROWS + LEFT_PAD
HEADS_PER_STEP = 4
GROUP_LANES = HEADS_PER_STEP * HEAD_DIM
MIX_ROWS = 512
HALO_ROWS = 32
EXP_ROWS = 512
EXP_GRANULE = 128
SC_ROWS = 128
RETURN_PARTS = 2
NEG_BIG = -1e30
VMEM_LIMIT = 56 * 1024 * 1024


HALF = D_MODEL // 2
HI_MASK = -65536


def _pack_rows(x):
    bits = lax.bitcast_convert_type(x.astype(BF16).astype(F32), I32)
    return lax.shift_right_logical(bits[:, :HALF], 16) | (bits[:, HALF:] & HI_MASK)


def _unpack_rows(w):
    lo = lax.bitcast_convert_type(lax.shift_left(w, 16), F32)
    hi = lax.bitcast_convert_type(w & HI_MASK, F32)
    return lo, hi


def _cast_rows(src_ref, dst_ref, rows, step=128):
    def body(c, carry):
        r = pl.multiple_of(c * step, step)
        dst_ref[pl.ds(r, step), :] = src_ref[pl.ds(r, step), :].astype(dst_ref.dtype)
        return carry
    lax.fori_loop(0, rows // step, body, 0)


def _conv_branch(hw_ref, sh_ref, rows, cw_ref, cb_ref, lg_ref, lb_ref, out_ref):
    off = HALO_ROWS - (CONV_WIDTH - 1)
    shift_rows = HALO_ROWS + rows - 8
    for b in range(1, 8):
        sh_ref[b - 1] = hw_ref[pl.ds(b, shift_rows), :]
    acc = None
    for j in range(CONV_WIDTH):
        a, b = divmod(off + j, 8)
        src = hw_ref if b == 0 else sh_ref.at[b - 1]
        term = src[pl.ds(8 * a, rows), :] * cw_ref[j:j + 1, :]
        acc = term if acc is None else acc + term
    acc = acc + cb_ref[...]
    mu = jnp.mean(acc, axis=-1, keepdims=True)
    d = acc - mu
    var = jnp.mean(d * d, axis=-1, keepdims=True)
    y = d * lax.rsqrt(var + LN_EPS) * lg_ref[...] + lb_ref[...]
    out_ref[...] = (y * jax.nn.sigmoid(y)).astype(out_ref.dtype)


def _inproj_kernel(x_ref, g_ref, w_ref, cw_ref, cb_ref, lg_ref, lb_ref,
                   q_ref, k_ref, v_ref, c_ref, wbf_ref, hw_ref, sh_ref):
    b = pl.program_id(0)
    j = pl.program_id(1)

    @pl.when((b == 0) & (j == 0))
    def _():
        _cast_rows(w_ref, wbf_ref, D_MODEL)

    @pl.when(j == 0)
    def _():
        k_ref[...] = jnp.zeros_like(k_ref)
        v_ref[...] = jnp.zeros_like(v_ref)
        hw_ref[IN_ROWS:IN_ROWS + HALO_ROWS, :] = jnp.zeros((HALO_ROWS, D_CONV), F32)

    @pl.when(j > 0)
    def _():
        x = x_ref[...]
        ms = jnp.mean(x * x, axis=-1, keepdims=True)
        hb = ((x * lax.rsqrt(ms + RMS_EPS)) * g_ref[...]).astype(BF16)

        def proj(c0, width):
            return jnp.dot(hb, wbf_ref[:, c0:c0 + width], preferred_element_type=F32)

        a = proj(3 * D_ATTN, D_CONV)
        gate = proj(3 * D_ATTN + D_CONV, D_CONV)
        hw_ref[0:HALO_ROWS, :] = hw_ref[IN_ROWS:IN_ROWS + HALO_ROWS, :]
        hw_ref[HALO_ROWS:HALO_ROWS + IN_ROWS, :] = a * jax.nn.sigmoid(gate)
        _conv_branch(hw_ref, sh_ref, IN_ROWS, cw_ref, cb_ref, lg_ref, lb_ref, c_ref)

        q_ref[...] = (proj(0, D_ATTN) * (HEAD_DIM ** -0.5)).astype(BF16)
        k_ref[...] = proj(D_ATTN, D_ATTN).astype(BF16)
        v_ref[...] = proj(2 * D_ATTN, D_ATTN).astype(BF16)


def _inproj(x, norm_g, w_in, conv_w, conv_b, ln_g, ln_b):
    bsz, seq, _ = x.shape
    nblk = seq // IN_ROWS
    d_cols = w_in.shape[1]
    row_blk = lambda b, j: (b, jnp.maximum(j - 1, 0), 0)
    const = lambda b, j: (0, 0)
    vec = lambda v: v.reshape(1, -1)
    return pl.pallas_call(
        _inproj_kernel,
        grid=(bsz, nblk + 1),
        in_specs=[
            pl.BlockSpec((None, IN_ROWS, D_MODEL), row_blk),
            pl.BlockSpec((1, D_MODEL), const),
            pl.BlockSpec((D_MODEL, d_cols), const),
            pl.BlockSpec((CONV_WIDTH, D_CONV), const),
            pl.BlockSpec((1, D_CONV), const),
            pl.BlockSpec((1, D_CONV), const),
            pl.BlockSpec((1, D_CONV), const),
        ],
        out_specs=[
            pl.BlockSpec((None, IN_ROWS, D_ATTN), row_blk),
            pl.BlockSpec((None, IN_ROWS, D_ATTN), lambda b, j: (b, j, 0)),
            pl.BlockSpec((None, IN_ROWS, D_ATTN), lambda b, j: (b, j, 0)),
            pl.BlockSpec((None, IN_ROWS, D_CONV), row_blk),
        ],
        out_shape=[
            jax.ShapeDtypeStruct((bsz, seq, D_ATTN), BF16),
            jax.ShapeDtypeStruct((bsz, seq + LEFT_PAD, D_ATTN), BF16),
            jax.ShapeDtypeStruct((bsz, seq + LEFT_PAD, D_ATTN), BF16),
            jax.ShapeDtypeStruct((bsz, seq, D_CONV), BF16),
        ],
        scratch_shapes=[
            pltpu.VMEM((D_MODEL, d_cols), BF16),
            pltpu.VMEM((HALO_ROWS + IN_ROWS, D_CONV), F32),
            pltpu.VMEM((7, HALO_ROWS + IN_ROWS - 8, D_CONV), F32),
        ],
        compiler_params=pltpu.CompilerParams(
            dimension_semantics=("arbitrary", "arbitrary"), vmem_limit_bytes=VMEM_LIMIT),
        name="inproj",
    )(x, vec(norm_g), w_in, conv_w, vec(conv_b), vec(ln_g), vec(ln_b))


def _attn_kernel(q_ref, k_ref, v_ref, bias_ref, o_ref):
    i = pl.program_id(1)
    lane = lax.broadcasted_iota(I32, (Q_ROWS, GROUP_LANES), 1) // HEAD_DIM
    col = lax.broadcasted_iota(I32, (HEADS_PER_STEP * Q_ROWS, BAND_ROWS), 1)
    for qb in range(ATT_BLOCKS):
        blk = i * ATT_BLOCKS + qb
        start = pl.multiple_of(blk * Q_ROWS, Q_ROWS)
        qrows = slice(qb * Q_ROWS, (qb + 1) * Q_ROWS)
        key_ok = col >= LEFT_PAD - blk * Q_ROWS
        for g in range(N_HEADS // HEADS_PER_STEP):
            lanes = slice(g * GROUP_LANES, (g + 1) * GROUP_LANES)
            q = q_ref[qrows, lanes]
            qs = jnp.concatenate(
                [jnp.where(lane == h, q, jnp.zeros_like(q)) for h in range(HEADS_PER_STEP)], axis=0)
            kb = k_ref[pl.ds(start, BAND_ROWS), lanes]
            vb = v_ref[pl.ds(start, BAND_ROWS), lanes]
            s = lax.dot_general(qs, kb, (((1,), (1,)), ((), ())), preferred_element_type=F32)
            rows = slice(g * HEADS_PER_STEP * Q_ROWS, (g + 1) * HEADS_PER_STEP * Q_ROWS)
            s = jnp.where(key_ok, s + bias_ref[rows, :], NEG_BIG)
            m = jnp.max(s, axis=-1, keepdims=True)
            p = jnp.exp(s - m)
            l = jnp.sum(p, axis=-1, keepdims=True)
            o = jnp.dot(p.astype(BF16), vb, preferred_element_type=F32) / l
            out = o[0:Q_ROWS]
            for h in range(1, HEADS_PER_STEP):
                out = jnp.where(lane == h, o[h * Q_ROWS:(h + 1) * Q_ROWS], out)
            o_ref[qrows, lanes] = out.astype(o_ref.dtype)


def _attention(q, kpad, vpad, bias):
    bsz, seq, _ = q.shape
    step_rows = ATT_BLOCKS * Q_ROWS
    return pl.pallas_call(
        _attn_kernel,
        grid=(bsz, seq // step_rows),
        in_specs=[
            pl.BlockSpec((None, step_rows, D_ATTN), lambda b, i: (b, i, 0)),
            pl.BlockSpec((None, seq + LEFT_PAD, D_ATTN), lambda b, i: (b, 0, 0)),
            pl.BlockSpec((None, seq + LEFT_PAD, D_ATTN), lambda b, i: (b, 0, 0)),
            pl.BlockSpec((N_HEADS * Q_ROWS, BAND_ROWS), lambda b, i: (0, 0)),
        ],
        out_specs=pl.BlockSpec((None, step_rows, D_ATTN), lambda b, i: (b, i, 0)),
        out_shape=jax.ShapeDtypeStruct((bsz, seq, D_ATTN), BF16),
        compiler_params=pltpu.CompilerParams(
            dimension_semantics=("arbitrary", "arbitrary"), vmem_limit_bytes=VMEM_LIMIT),
        name="chunk_attn",
    )(q, kpad, vpad, bias)


def _band_bias(rel_bias):
    n_rel = REL_MAX - REL_MIN + 1
    far = jnp.broadcast_to(rel_bias[:, n_rel - 1:n_rel], (N_HEADS, BAND_ROWS - 1 - REL_MAX))
    near = rel_bias[:, ::-1]
    ahead = jnp.broadcast_to(rel_bias[:, 0:1], (N_HEADS, Q_ROWS - 1 + REL_MIN))
    diag = jnp.concatenate([far, near, ahead], axis=1).astype(F32)
    bias = jnp.stack(
        [diag[:, Q_ROWS - 1 - r:Q_ROWS - 1 - r + BAND_ROWS] for r in range(Q_ROWS)], axis=1)
    r = jnp.arange(Q_ROWS)[:, None]
    m = jnp.arange(BAND_ROWS)[None, :]
    cq = r // CHUNK
    ck = m // CHUNK
    in_band = (ck >= cq) & (ck <= cq + LEFT_CHUNKS)
    bias = jnp.where(in_band[None], bias, NEG_BIG)
    return bias.reshape(N_HEADS * Q_ROWS, BAND_ROWS)


def _split_bf16(v):
    hi = v.astype(BF16)
    lo = (v - hi.astype(F32)).astype(BF16)
    return hi, lo


def _mix_route_kernel(x_ref, a_ref, c_ref, wo_ref, ng_ref, rwt_ref, rb_ref,
                      x1_ref, t_ref, idx_ref, wgt_ref, rank_ref, cnt_ref,
                      wobf_ref, cntacc_ref):
    i = pl.program_id(0)

    @pl.when(i == 0)
    def _():
        _cast_rows(wo_ref, wobf_ref, D_MODEL)
        cntacc_ref[...] = jnp.zeros_like(cntacc_ref)

    mixed = jnp.dot(a_ref[...], wobf_ref[0:D_ATTN, :], preferred_element_type=F32)
    mixed = mixed + jnp.dot(c_ref[...], wobf_ref[D_ATTN:D_MODEL, :], preferred_element_type=F32)
    x1 = x_ref[...] + mixed
    x1_ref[...] = x1
    ms = jnp.mean(x1 * x1, axis=-1, keepdims=True)
    t = (x1 * lax.rsqrt(ms + RMS_EPS)) * ng_ref[...]
    t_ref[...] = _pack_rows(t)

    nt = (((1,), (1,)), ((), ()))
    w_hi, w_lo = _split_bf16(rwt_ref[...])
    t_hi, t_lo = _split_bf16(t)
    logits = (lax.dot_general(w_hi, t_hi, nt, preferred_element_type=F32)
              + lax.dot_general(w_hi, t_lo, nt, preferred_element_type=F32)
              + lax.dot_general(w_lo, t_hi, nt, preferred_element_type=F32)) + rb_ref[...]
    e_iota = lax.broadcasted_iota(I32, (N_EXPERTS, MIX_ROWS), 0)
    vals, idxs, hots = [], [], []
    for _ in range(TOP_K):
        m = jnp.max(logits, axis=0, keepdims=True)
        am = jnp.min(jnp.where(logits == m, e_iota, N_EXPERTS), axis=0, keepdims=True)
        hot = e_iota == am
        vals.append(m)
        idxs.append(am)
        hots.append(hot)
        logits = jnp.where(hot, -jnp.inf, logits)
    exps = [jnp.exp(v - vals[0]) for v in vals]
    den = exps[0] + exps[1] + exps[2] + exps[3]
    wts = [e / den for e in exps]

    hot_f = (hots[0] | hots[1] | hots[2] | hots[3]).astype(F32)
    ra = lax.broadcasted_iota(I32, (MIX_ROWS, MIX_ROWS), 0)
    rc = lax.broadcasted_iota(I32, (MIX_ROWS, MIX_ROWS), 1)
    upper = (ra < rc).astype(BF16)
    prefix = jnp.dot(hot_f.astype(BF16), upper, preferred_element_type=F32)
    base = prefix + cntacc_ref[...]
    ranks = [jnp.sum(jnp.where(h, base, 0.0), axis=0, keepdims=True) for h in hots]
    counts = cntacc_ref[...] + jnp.sum(hot_f, axis=1, keepdims=True)
    cntacc_ref[...] = counts

    idx_ref[...] = jnp.concatenate(idxs, axis=0)
    rank_ref[...] = jnp.concatenate(ranks, axis=0).astype(I32)
    cnt_ref[...] = jnp.broadcast_to(counts, cnt_ref.shape)
    w8 = jnp.concatenate(wts + [jnp.zeros((8 - TOP_K, MIX_ROWS), F32)], axis=0)
    wgt_ref[...] = w8.T


def _mix_route(x2, attn2, conv2, w_out, norm_g, router_w, router_b):
    n_tok = x2.shape[0]
    row = lambda i: (i, 0)
    const = lambda i: (0, 0)
    vec = lambda v: v.reshape(1, -1)
    return pl.pallas_call(
        _mix_route_kernel,
        grid=(n_tok // MIX_ROWS,),
        in_specs=[
            pl.BlockSpec((MIX_ROWS, D_MODEL), row),
            pl.BlockSpec((MIX_ROWS, D_ATTN), row),
            pl.BlockSpec((MIX_ROWS, D_CONV), row),
            pl.BlockSpec((D_MODEL, D_MODEL), const),
            pl.BlockSpec((1, D_MODEL), const),
            pl.BlockSpec((N_EXPERTS, D_MODEL), const),
            pl.BlockSpec((N_EXPERTS, 1), const),
        ],
        out_specs=[
            pl.BlockSpec((MIX_ROWS, D_MODEL), row),
            pl.BlockSpec((MIX_ROWS, HALF), row),
            pl.BlockSpec((TOP_K, MIX_ROWS), lambda i: (0, i)),
            pl.BlockSpec((MIX_ROWS, 8), row),
            pl.BlockSpec((TOP_K, MIX_ROWS), lambda i: (0, i)),
            pl.BlockSpec((N_EXPERTS, 128), const),
        ],
        out_shape=[
            jax.ShapeDtypeStruct((n_tok, D_MODEL), F32),
            jax.ShapeDtypeStruct((n_tok, HALF), I32),
            jax.ShapeDtypeStruct((TOP_K, n_tok), I32),
            jax.ShapeDtypeStruct((n_tok, 8), F32),
            jax.ShapeDtypeStruct((TOP_K, n_tok), I32),
            jax.ShapeDtypeStruct((N_EXPERTS, 128), F32),
        ],
        scratch_shapes=[
            pltpu.VMEM((D_MODEL, D_MODEL), BF16),
            pltpu.VMEM((N_EXPERTS, 1), F32),
        ],
        compiler_params=pltpu.CompilerParams(
            dimension_semantics=("arbitrary",), vmem_limit_bytes=VMEM_LIMIT),
        name="mix_route",
    )(x2, attn2, conv2, w_out, vec(norm_g), router_w.T, router_b.reshape(N_EXPERTS, 1))


def _tile_plan(counts, n_tiles):
    tiles_per = (counts + EXP_ROWS - 1) // EXP_ROWS
    tile_end = jnp.cumsum(tiles_per)
    tile_begin = tile_end - tiles_per
    n_valid = tile_end[-1]
    tiles = jnp.arange(n_tiles, dtype=I32)
    tile_valid = tiles < n_valid
    capped = jnp.minimum(tiles, n_valid - 1)
    tile_expert = jnp.sum((capped[:, None] >= tile_end[None, :]).astype(I32), axis=1)
    tile_expert = jnp.minimum(tile_expert, N_EXPERTS - 1)
    is_first = (tiles[:, None] == tile_begin[None, :]) & (tiles_per[None, :] > 0)
    tile_first = tile_valid & jnp.any(is_first, axis=1)
    group_start = tile_begin * EXP_ROWS
    experts = jnp.arange(N_EXPERTS, dtype=I32)
    nonempty = tiles_per > 0
    parity = (jnp.cumsum(nonempty.astype(I32)) - 1) % 2
    later = nonempty[None, :] & (experts[None, :] > experts[:, None])
    nxt = jnp.min(jnp.where(later, experts[None, :], N_EXPERTS), axis=1)
    nxt = jnp.where(nxt == N_EXPERTS, -1, nxt)
    hot = (tile_expert[:, None] == experts[None, :]).astype(I32)
    tile_slot = jnp.sum(hot * parity[None, :], axis=1)
    tile_next = jnp.sum(hot * nxt[None, :], axis=1)
    filled = jnp.sum(hot * counts[None, :], axis=1) - (tiles - jnp.sum(hot * tile_begin[None, :], axis=1)) * EXP_ROWS
    filled = jnp.where(tile_valid, jnp.clip(filled, 0, EXP_ROWS), 0)
    tile_rows = (filled + EXP_GRANULE - 1) // EXP_GRANULE * EXP_GRANULE
    flags = (tile_expert, tile_first.astype(I32), tile_rows.astype(I32),
             tile_slot.astype(I32), tile_next.astype(I32))
    return flags, group_start.astype(I32)


def _slot_kernel(gstart_ref, idx_ref, rank_ref, pos_ref):
    idx = idx_ref[...]
    pos = rank_ref[...]
    for e in range(N_EXPERTS):
        pos = pos + jnp.where(idx == e, gstart_ref[e], 0)
    pos_ref[...] = pos


def _slots(group_start, idx, rank):
    full = pl.BlockSpec(idx.shape, lambda i, gs: (0, 0))
    return pl.pallas_call(
        _slot_kernel,
        grid_spec=pltpu.PrefetchScalarGridSpec(
            num_scalar_prefetch=1, grid=(1,), in_specs=[full, full], out_specs=full),
        out_shape=jax.ShapeDtypeStruct(idx.shape, I32),
        name="slots",
    )(group_start, idx, rank)


def _sc_mesh():
    return plsc.VectorSubcoreMesh(core_axis_name="core", subcore_axis_name="subcore")


def _sc_worker():
    info = plsc.get_sparse_core_info()
    wid = lax.axis_index("subcore") * info.num_cores + lax.axis_index("core")
    return wid, info.num_cores * info.num_subcores


def _dispatch(t2, pos_flat, n_slots):
    n_tok, d = t2.shape
    n_workers = 32
    per_w = n_tok // n_workers
    assert per_w % SC_ROWS == 0

    @functools.partial(
        pl.kernel, mesh=_sc_mesh(),
        out_type=jax.ShapeDtypeStruct((n_slots, d), t2.dtype),
        scratch_types=[pltpu.VMEM((SC_ROWS,), I32), pltpu.VMEM((SC_ROWS, d), t2.dtype)],
        name="dispatch",
    )
    def k(t_hbm, pos_hbm, o_hbm, idx_v, rows_v):
        wid, nw = _sc_worker()
        assert nw == n_workers

        @pl.loop(0, per_w // SC_ROWS)
        def _(c):
            base = pl.multiple_of(wid * per_w + c * SC_ROWS, SC_ROWS)
            pltpu.sync_copy(t_hbm.at[pl.ds(base, SC_ROWS)], rows_v)
            for kk in range(TOP_K):
                pltpu.sync_copy(pos_hbm.at[pl.ds(kk * n_tok + base, SC_ROWS)], idx_v)
                pltpu.sync_copy(rows_v, o_hbm.at[idx_v])

    return k(t2, pos_flat)


def _gather_back(sorted_out, pos_flat):
    n_rows = pos_flat.shape[0]
    d = sorted_out.shape[1]
    n_workers = 32
    per_w = n_rows // n_workers
    assert per_w % SC_ROWS == 0

    @functools.partial(
        pl.kernel, mesh=_sc_mesh(),
        out_type=jax.ShapeDtypeStruct((n_rows, d), sorted_out.dtype),
        scratch_types=[pltpu.VMEM((SC_ROWS,), I32), pltpu.VMEM((SC_ROWS, d), sorted_out.dtype)],
        name="gather_back",
    )
    def k(s_hbm, pos_hbm, o_hbm, idx_v, rows_v):
        wid, nw = _sc_worker()
        assert nw == n_workers

        @pl.loop(0, per_w // SC_ROWS)
        def _(c):
            base = pl.multiple_of(wid * per_w + c * SC_ROWS, SC_ROWS)
            pltpu.sync_copy(pos_hbm.at[pl.ds(base, SC_ROWS)], idx_v)
            pltpu.sync_copy(s_hbm.at[idx_v], rows_v)
            pltpu.sync_copy(rows_v, o_hbm.at[pl.ds(base, SC_ROWS)])

    return k(sorted_out, pos_flat)


def _weight_copies(w_hbm, wf32, sem, expert, slot):
    return [pltpu.make_async_copy(w.at[expert], wf32.at[slot, mtx], sem.at[slot])
            for mtx, w in enumerate(w_hbm)]


def _expert_rows(rows, x_ref, bg_ref, bu_ref, bd_ref, o_ref, wbf):
    xs = jnp.concatenate(_unpack_rows(x_ref[0:rows, :]), axis=1).astype(BF16)
    cn = 256
    hids = []
    for c in range(D_MODEL // cn):
        sl = slice(c * cn, (c + 1) * cn)
        g = jnp.dot(xs, wbf[0, :, sl], preferred_element_type=F32) + bg_ref[:, sl]
        u = jnp.dot(xs, wbf[1, :, sl], preferred_element_type=F32) + bu_ref[:, sl]
        g = jnp.minimum(g, SWIGLU_LIMIT)
        u = jnp.clip(u, -SWIGLU_LIMIT, SWIGLU_LIMIT)
        hids.append(((u + 1.0) * (g * jax.nn.sigmoid(SWIGLU_ALPHA * g))).astype(BF16))
    acc = jnp.dot(jnp.concatenate(hids, axis=1), wbf[2], preferred_element_type=F32)
    o_ref[0:rows, :] = _pack_rows(acc + bd_ref[...])
    if rows < EXP_ROWS:
        o_ref[rows:EXP_ROWS, :] = jnp.zeros((EXP_ROWS - rows, HALF), I32)


def _expert_kernel(texp_ref, tfirst_ref, trows_ref, tslot_ref, tnext_ref,
                   x_ref, wg_hbm, bg_ref, wu_hbm, bu_ref, wd_hbm, bd_ref,
                   o_ref, wf32, wbf, wsem):
    i = pl.program_id(0)
    w_hbm = (wg_hbm, wu_hbm, wd_hbm)

    @pl.when(trows_ref[i] == 0)
    def _():
        o_ref[...] = jnp.zeros_like(o_ref)

    @pl.when(tfirst_ref[i] == 1)
    def _():
        slot = tslot_ref[i]
        expert = texp_ref[i]

        @pl.when(i == 0)
        def _():
            for cp in _weight_copies(w_hbm, wf32, wsem, expert, slot):
                cp.start()

        for cp in _weight_copies(w_hbm, wf32, wsem, expert, slot):
            cp.wait()

        @pl.when(tnext_ref[i] >= 0)
        def _():
            for cp in _weight_copies(w_hbm, wf32, wsem, tnext_ref[i], 1 - slot):
                cp.start()

        for mtx in range(3):
            _cast_rows(wf32.at[slot, mtx], wbf.at[mtx], D_MODEL)

    for rows in range(EXP_GRANULE, EXP_ROWS + 1, EXP_GRANULE):
        @pl.when(trows_ref[i] == rows)
        def _(rows=rows):
            _expert_rows(rows, x_ref, bg_ref, bu_ref, bd_ref, o_ref, wbf)


def _experts(sorted_t, tile_flags, w_gate, b_gate, w_up, b_up, w_down, b_down, n_tiles):
    row_blk = pl.BlockSpec((EXP_ROWS, HALF), lambda i, *_: (i, 0))
    w_any = pl.BlockSpec(memory_space=pl.ANY)
    b_blk = pl.BlockSpec((None, 1, D_MODEL), lambda i, te, *_: (te[i], 0, 0))
    grid_spec = pltpu.PrefetchScalarGridSpec(
        num_scalar_prefetch=len(tile_flags),
        grid=(n_tiles,),
        in_specs=[row_blk, w_any, b_blk, w_any, b_blk, w_any, b_blk],
        out_specs=row_blk,
        scratch_shapes=[
            pltpu.VMEM((2, 3, D_MODEL, D_MODEL), F32),
            pltpu.VMEM((3, D_MODEL, D_MODEL), BF16),
            pltpu.SemaphoreType.DMA((2,)),
        ],
    )
    b3 = lambda b: b.reshape(N_EXPERTS, 1, D_MODEL)
    return pl.pallas_call(
        _expert_kernel,
        grid_spec=grid_spec,
        out_shape=jax.ShapeDtypeStruct((n_tiles * EXP_ROWS, HALF), I32),
        compiler_params=pltpu.CompilerParams(
            dimension_semantics=("arbitrary",), vmem_limit_bytes=VMEM_LIMIT),
        name="experts",
    )(*tile_flags, sorted_t,
      w_gate, b3(b_gate), w_up, b3(b_up), w_down, b3(b_down))


def _combine_kernel(x1_ref, s0_ref, s1_ref, s2_ref, s3_ref, w_ref, g_ref, o_ref):
    w = w_ref[...]
    y_lo = x1_ref[:, :HALF]
    y_hi = x1_ref[:, HALF:]
    for k, s_ref in enumerate((s0_ref, s1_ref, s2_ref, s3_ref)):
        lo, hi = _unpack_rows(s_ref[...])
        y_lo = y_lo + w[:, k:k + 1] * lo
        y_hi = y_hi + w[:, k:k + 1] * hi
    sq = jnp.sum(y_lo * y_lo, axis=-1, keepdims=True) + jnp.sum(y_hi * y_hi, axis=-1, keepdims=True)
    scale = lax.rsqrt(sq * (1.0 / D_MODEL) + RMS_EPS)
    o_ref[:, :HALF] = (y_lo * scale) * g_ref[:, :HALF]
    o_ref[:, HALF:] = (y_hi * scale) * g_ref[:, HALF:]


def _combine_part_kernel(x1_ref, s0_ref, s1_ref, s2_ref, s3_ref, w_ref, g_ref, prev_ref, o_ref):
    del prev_ref
    _combine_kernel(x1_ref, s0_ref, s1_ref, s2_ref, s3_ref, w_ref, g_ref, o_ref)


def _combine(x1, slabs, wgt, norm_g, part, n_parts, prev):
    n_tok = x1.shape[0]
    rows = MIX_ROWS
    ntiles = n_tok // rows // n_parts
    first = part * ntiles
    slab_spec = lambda k: pl.BlockSpec((rows, HALF), lambda i, k=k: (k * ntiles + i, 0))
    in_specs = ([pl.BlockSpec((rows, D_MODEL), lambda i: (first + i, 0))]
                + [slab_spec(k) for k in range(TOP_K)]
                + [pl.BlockSpec((rows, 8), lambda i: (first + i, 0)),
                   pl.BlockSpec((1, D_MODEL), lambda i: (0, 0))])
    args = [x1, slabs, slabs, slabs, slabs, wgt, norm_g.reshape(1, D_MODEL)]
    aliases = {}
    body = _combine_kernel
    if prev is not None:
        in_specs.append(pl.BlockSpec(memory_space=pl.ANY))
        args.append(prev)
        aliases = {len(args) - 1: 0}
        body = _combine_part_kernel
    return pl.pallas_call(
        body,
        grid=(ntiles,),
        in_specs=in_specs,
        out_specs=pl.BlockSpec((rows, D_MODEL), lambda i: (first + i, 0)),
        out_shape=jax.ShapeDtypeStruct((n_tok, D_MODEL), F32),
        input_output_aliases=aliases,
        compiler_params=pltpu.CompilerParams(
            dimension_semantics=("arbitrary",), vmem_limit_bytes=VMEM_LIMIT),
        name="combine",
    )(*args)


def kernel(x, norm_mix_g, w_in, conv_dw_w, conv_dw_b, conv_ln_g, conv_ln_b, rel_bias, w_out,
           norm_ffn_g, router_w, router_b, exp_w_gate, exp_b_gate, exp_w_up, exp_b_up,
           exp_w_down, exp_b_down, norm_final_g):
    bsz, seq, _ = x.shape
    n_tok = bsz * seq
    assert norm_mix_g.shape[0] == 1, "single-layer block"
    assert seq % IN_ROWS == 0 and seq % MIX_ROWS == 0 and IN_ROWS == LEFT_PAD
    n_tiles = (TOP_K * n_tok) // EXP_ROWS + N_EXPERTS - 1

    q, kpad, vpad, conv = _inproj(x, norm_mix_g[0], w_in[0], conv_dw_w[0], conv_dw_b[0],
                                  conv_ln_g[0], conv_ln_b[0])
    attn = _attention(q, kpad, vpad, _band_bias(rel_bias[0]))
    x1, t, idx, wgt, rank, cnt = _mix_route(
        x.reshape(n_tok, D_MODEL), attn.reshape(n_tok, D_ATTN), conv.reshape(n_tok, D_CONV),
        w_out[0], norm_ffn_g[0], router_w[0], router_b[0])
    tile_flags, group_start = _tile_plan(cnt[:, 0].astype(I32), n_tiles)
    pos = _slots(group_start, idx, rank)
    sorted_t = _dispatch(t, pos.reshape(TOP_K * n_tok), n_tiles * EXP_ROWS)
    sorted_out = _experts(sorted_t, tile_flags,
                          exp_w_gate[0], exp_b_gate[0], exp_w_up[0], exp_b_up[0],
                          exp_w_down[0], exp_b_down[0], n_tiles)
    out = None
    part_tok = n_tok // RETURN_PARTS
    for part in range(RETURN_PARTS):
        pos_part = pos[:, part * part_tok:(part + 1) * part_tok].reshape(TOP_K * part_tok)
        slabs = _gather_back(sorted_out, pos_part)
        out = _combine(x1, slabs, wgt, norm_final_g, part, RETURN_PARTS, out)
    return out.reshape(bsz, seq, D_MODEL)
```

```python
import functools

import jax
import jax.numpy as jnp
from jax import lax
from jax.experimental import pallas as pl
from jax.experimental.pallas import tpu as pltpu
from jax.experimental.pallas import tpu_sc as plsc

F32 = jnp.float32
BF16 = jnp.bfloat16
I32 = jnp.int32

D_MODEL = 1024
CHUNK = 64
N_HEADS = 8
HEAD_DIM = 64
D_ATTN = N_HEADS * HEAD_DIM
LEFT_CHUNKS = 8
REL_MAX = 128
REL_MIN = -(CHUNK - 1)
D_CONV = D_MODEL - D_ATTN
CONV_WIDTH = 31
N_EXPERTS = 32
TOP_K = 4
SWIGLU_ALPHA = 1.702
SWIGLU_LIMIT = 7.0
RMS_EPS = 1e-5
LN_EPS = 1e-5

LEFT_PAD = LEFT_CHUNKS * CHUNK
IN_ROWS = 512
Q_ROWS = 2 * CHUNK
ATT_BLOCKS = 4
BAND_ROWS = Q_ROWS + LEFT_PAD
HEADS_PER_STEP = 4
GROUP_LANES = HEADS_PER_STEP * HEAD_DIM
MIX_ROWS = 512
HALO_ROWS = 32
EXP_ROWS = 512
EXP_GRANULE = 128
EXP_TILES_PER_STEP = 2
SC_ROWS = 128
NEG_BIG = -1e30
VMEM_LIMIT = 56 * 1024 * 1024


HALF = D_MODEL // 2
HI_MASK = -65536


def _pack_rows(x):
    bits = lax.bitcast_convert_type(x.astype(BF16).astype(F32), I32)
    return lax.shift_right_logical(bits[:, :HALF], 16) | (bits[:, HALF:] & HI_MASK)


def _unpack_rows(w):
    lo = lax.bitcast_convert_type(lax.shift_left(w, 16), F32)
    hi = lax.bitcast_convert_type(w & HI_MASK, F32)
    return lo, hi


def _cast_rows(src_ref, dst_ref, rows, step=128):
    def body(c, carry):
        r = pl.multiple_of(c * step, step)
        dst_ref[pl.ds(r, step), :] = src_ref[pl.ds(r, step), :].astype(dst_ref.dtype)
        return carry
    lax.fori_loop(0, rows // step, body, 0)


def _conv_branch(hw_ref, sh_ref, rows, cw_ref, cb_ref, lg_ref, lb_ref, out_ref):
    off = HALO_ROWS - (CONV_WIDTH - 1)
    shift_rows = HALO_ROWS + rows - 8
    for b in range(1, 8):
        sh_ref[b - 1] = hw_ref[pl.ds(b, shift_rows), :]
    acc = None
    for j in range(CONV_WIDTH):
        a, b = divmod(off + j, 8)
        src = hw_ref if b == 0 else sh_ref.at[b - 1]
        term = src[pl.ds(8 * a, rows), :] * cw_ref[j:j + 1, :]
        acc = term if acc is None else acc + term
    acc = acc + cb_ref[...]
    mu = jnp.mean(acc, axis=-1, keepdims=True)
    d = acc - mu
    var = jnp.mean(d * d, axis=-1, keepdims=True)
    y = d * lax.rsqrt(var + LN_EPS) * lg_ref[...] + lb_ref[...]
    out_ref[...] = (y * jax.nn.sigmoid(y)).astype(out_ref.dtype)


def _inproj_kernel(x_ref, g_ref, w_ref, cw_ref, cb_ref, lg_ref, lb_ref,
                   q_ref, k_ref, v_ref, c_ref, wbf_ref, hw_ref, sh_ref):
    b = pl.program_id(0)
    j = pl.program_id(1)

    @pl.when((b == 0) & (j == 0))
    def _():
        _cast_rows(w_ref, wbf_ref, D_MODEL)

    @pl.when(j == 0)
    def _():
        k_ref[...] = jnp.zeros_like(k_ref)
        v_ref[...] = jnp.zeros_like(v_ref)
        hw_ref[IN_ROWS:IN_ROWS + HALO_ROWS, :] = jnp.zeros((HALO_ROWS, D_CONV), F32)

    @pl.when(j > 0)
    def _():
        x = x_ref[...]
        ms = jnp.mean(x * x, axis=-1, keepdims=True)
        hb = ((x * lax.rsqrt(ms + RMS_EPS)) * g_ref[...]).astype(BF16)

        def proj(c0, width):
            return jnp.dot(hb, wbf_ref[:, c0:c0 + width], preferred_element_type=F32)

        a = proj(3 * D_ATTN, D_CONV)
        gate = proj(3 * D_ATTN + D_CONV, D_CONV)
        hw_ref[0:HALO_ROWS, :] = hw_ref[IN_ROWS:IN_ROWS + HALO_ROWS, :]
        hw_ref[HALO_ROWS:HALO_ROWS + IN_ROWS, :] = a * jax.nn.sigmoid(gate)
        _conv_branch(hw_ref, sh_ref, IN_ROWS, cw_ref, cb_ref, lg_ref, lb_ref, c_ref)

        q_ref[...] = (proj(0, D_ATTN) * (HEAD_DIM ** -0.5)).astype(BF16)
        k_ref[...] = proj(D_ATTN, D_ATTN).astype(BF16)
        v_ref[...] = proj(2 * D_ATTN, D_ATTN).astype(BF16)


def _inproj(x, norm_g, w_in, conv_w, conv_b, ln_g, ln_b):
    bsz, seq, _ = x.shape
    nblk = seq // IN_ROWS
    d_cols = w_in.shape[1]
    row_blk = lambda b, j: (b, jnp.maximum(j - 1, 0), 0)
    const = lambda b, j: (0, 0)
    vec = lambda v: v.reshape(1, -1)
    return pl.pallas_call(
        _inproj_kernel,
        grid=(bsz, nblk + 1),
        in_specs=[
            pl.BlockSpec((None, IN_ROWS, D_MODEL), row_blk),
            pl.BlockSpec((1, D_MODEL), const),
            pl.BlockSpec((D_MODEL, d_cols), const),
            pl.BlockSpec((CONV_WIDTH, D_CONV), const),
            pl.BlockSpec((1, D_CONV), const),
            pl.BlockSpec((1, D_CONV), const),
            pl.BlockSpec((1, D_CONV), const),
        ],
        out_specs=[
            pl.BlockSpec((None, IN_ROWS, D_ATTN), row_blk),
            pl.BlockSpec((None, IN_ROWS, D_ATTN), lambda b, j: (b, j, 0)),
            pl.BlockSpec((None, IN_ROWS, D_ATTN), lambda b, j: (b, j, 0)),
            pl.BlockSpec((None, IN_ROWS, D_CONV), row_blk),
        ],
        out_shape=[
            jax.ShapeDtypeStruct((bsz, seq, D_ATTN), BF16),
            jax.ShapeDtypeStruct((bsz, seq + LEFT_PAD, D_ATTN), BF16),
            jax.ShapeDtypeStruct((bsz, seq + LEFT_PAD, D_ATTN), BF16),
            jax.ShapeDtypeStruct((bsz, seq, D_CONV), BF16),
        ],
        scratch_shapes=[
            pltpu.VMEM((D_MODEL, d_cols), BF16),
            pltpu.VMEM((HALO_ROWS + IN_ROWS, D_CONV), F32),
            pltpu.VMEM((7, HALO_ROWS + IN_ROWS - 8, D_CONV), F32),
        ],
        compiler_params=pltpu.CompilerParams(
            dimension_semantics=("arbitrary", "arbitrary"), vmem_limit_bytes=VMEM_LIMIT),
        name="inproj",
    )(x, vec(norm_g), w_in, conv_w, vec(conv_b), vec(ln_g), vec(ln_b))


def _attn_kernel(q_ref, k_ref, v_ref, bias_ref, o_ref):
    i = pl.program_id(1)
    lane = lax.broadcasted_iota(I32, (Q_ROWS, GROUP_LANES), 1) // HEAD_DIM
    col = lax.broadcasted_iota(I32, (HEADS_PER_STEP * Q_ROWS, BAND_ROWS), 1)
    for qb in range(ATT_BLOCKS):
        blk = i * ATT_BLOCKS + qb
        start = pl.multiple_of(blk * Q_ROWS, Q_ROWS)
        qrows = slice(qb * Q_ROWS, (qb + 1) * Q_ROWS)
        key_ok = col >= LEFT_PAD - blk * Q_ROWS
        for g in range(N_HEADS // HEADS_PER_STEP):
            lanes = slice(g * GROUP_LANES, (g + 1) * GROUP_LANES)
            q = q_ref[qrows, lanes]
            qs = jnp.concatenate(
                [jnp.where(lane == h, q, jnp.zeros_like(q)) for h in range(HEADS_PER_STEP)], axis=0)
            kb = k_ref[pl.ds(start, BAND_ROWS), lanes]
            vb = v_ref[pl.ds(start, BAND_ROWS), lanes]
            s = lax.dot_general(qs, kb, (((1,), (1,)), ((), ())), preferred_element_type=F32)
            rows = slice(g * HEADS_PER_STEP * Q_ROWS, (g + 1) * HEADS_PER_STEP * Q_ROWS)
            s = jnp.where(key_ok, s + bias_ref[rows, :], NEG_BIG)
            m = jnp.max(s, axis=-1, keepdims=True)
            p = jnp.exp(s - m)
            l = jnp.sum(p, axis=-1, keepdims=True)
            o = jnp.dot(p.astype(BF16), vb, preferred_element_type=F32) / l
            out = o[0:Q_ROWS]
            for h in range(1, HEADS_PER_STEP):
                out = jnp.where(lane == h, o[h * Q_ROWS:(h + 1) * Q_ROWS], out)
            o_ref[qrows, lanes] = out.astype(o_ref.dtype)


def _attention(q, kpad, vpad, bias):
    bsz, seq, _ = q.shape
    step_rows = ATT_BLOCKS * Q_ROWS
    return pl.pallas_call(
        _attn_kernel,
        grid=(bsz, seq // step_rows),
        in_specs=[
            pl.BlockSpec((None, step_rows, D_ATTN), lambda b, i: (b, i, 0)),
            pl.BlockSpec((None, seq + LEFT_PAD, D_ATTN), lambda b, i: (b, 0, 0)),
            pl.BlockSpec((None, seq + LEFT_PAD, D_ATTN), lambda b, i: (b, 0, 0)),
            pl.BlockSpec((N_HEADS * Q_ROWS, BAND_ROWS), lambda b, i: (0, 0)),
        ],
        out_specs=pl.BlockSpec((None, step_rows, D_ATTN), lambda b, i: (b, i, 0)),
        out_shape=jax.ShapeDtypeStruct((bsz, seq, D_ATTN), BF16),
        compiler_params=pltpu.CompilerParams(
            dimension_semantics=("arbitrary", "arbitrary"), vmem_limit_bytes=VMEM_LIMIT),
        name="chunk_attn",
    )(q, kpad, vpad, bias)


def _band_bias(rel_bias):
    n_rel = REL_MAX - REL_MIN + 1
    far = jnp.broadcast_to(rel_bias[:, n_rel - 1:n_rel], (N_HEADS, BAND_ROWS - 1 - REL_MAX))
    near = rel_bias[:, ::-1]
    ahead = jnp.broadcast_to(rel_bias[:, 0:1], (N_HEADS, Q_ROWS - 1 + REL_MIN))
    diag = jnp.concatenate([far, near, ahead], axis=1).astype(F32)
    bias = jnp.stack(
        [diag[:, Q_ROWS - 1 - r:Q_ROWS - 1 - r + BAND_ROWS] for r in range(Q_ROWS)], axis=1)
    r = jnp.arange(Q_ROWS)[:, None]
    m = jnp.arange(BAND_ROWS)[None, :]
    cq = r // CHUNK
    ck = m // CHUNK
    in_band = (ck >= cq) & (ck <= cq + LEFT_CHUNKS)
    bias = jnp.where(in_band[None], bias, NEG_BIG)
    return bias.reshape(N_HEADS * Q_ROWS, BAND_ROWS)


def _split_bf16(v):
    hi = v.astype(BF16)
    lo = (v - hi.astype(F32)).astype(BF16)
    return hi, lo


def _mix_route_kernel(x_ref, a_ref, c_ref, wo_ref, ng_ref, rwt_ref, rb_ref,
                      x1_ref, t_ref, idx_ref, wgt_ref, rank_ref, cnt_ref,
                      wobf_ref, cntacc_ref):
    i = pl.program_id(0)

    @pl.when(i == 0)
    def _():
        _cast_rows(wo_ref, wobf_ref, D_MODEL)
        cntacc_ref[...] = jnp.zeros_like(cntacc_ref)

    mixed = jnp.dot(a_ref[...], wobf_ref[0:D_ATTN, :], preferred_element_type=F32)
    mixed = mixed + jnp.dot(c_ref[...], wobf_ref[D_ATTN:D_MODEL, :], preferred_element_type=F32)
    x1 = x_ref[...] + mixed
    x1_ref[...] = x1
    ms = jnp.mean(x1 * x1, axis=-1, keepdims=True)
    t = (x1 * lax.rsqrt(ms + RMS_EPS)) * ng_ref[...]
    t_ref[...] = _pack_rows(t)

    nt = (((1,), (1,)), ((), ()))
    w_hi, w_lo = _split_bf16(rwt_ref[...])
    t_hi, t_lo = _split_bf16(t)
    logits = (lax.dot_general(w_hi, t_hi, nt, preferred_element_type=F32)
              + lax.dot_general(w_hi, t_lo, nt, preferred_element_type=F32)
              + lax.dot_general(w_lo, t_hi, nt, preferred_element_type=F32)) + rb_ref[...]
    e_iota = lax.broadcasted_iota(I32, (N_EXPERTS, MIX_ROWS), 0)
    vals, idxs, hots = [], [], []
    for _ in range(TOP_K):
        m = jnp.max(logits, axis=0, keepdims=True)
        am = jnp.min(jnp.where(logits == m, e_iota, N_EXPERTS), axis=0, keepdims=True)
        hot = e_iota == am
        vals.append(m)
        idxs.append(am)
        hots.append(hot)
        logits = jnp.where(hot, -jnp.inf, logits)
    exps = [jnp.exp(v - vals[0]) for v in vals]
    den = exps[0] + exps[1] + exps[2] + exps[3]
    wts = [e / den for e in exps]

    hot_f = (hots[0] | hots[1] | hots[2] | hots[3]).astype(F32)
    ra = lax.broadcasted_iota(I32, (MIX_ROWS, MIX_ROWS), 0)
    rc = lax.broadcasted_iota(I32, (MIX_ROWS, MIX_ROWS), 1)
    upper = (ra < rc).astype(BF16)
    prefix = jnp.dot(hot_f.astype(BF16), upper, preferred_element_type=F32)
    base = prefix + cntacc_ref[...]
    ranks = [jnp.sum(jnp.where(h, base, 0.0), axis=0, keepdims=True) for h in hots]
    counts = cntacc_ref[...] + jnp.sum(hot_f, axis=1, keepdims=True)
    cntacc_ref[...] = counts

    idx_ref[...] = jnp.concatenate(idxs, axis=0)
    rank_ref[...] = jnp.concatenate(ranks, axis=0).astype(I32)
    cnt_ref[...] = jnp.broadcast_to(counts, cnt_ref.shape)
    w8 = jnp.concatenate(wts + [jnp.zeros((8 - TOP_K, MIX_ROWS), F32)], axis=0)
    wgt_ref[...] = w8.T


def _mix_route(x2, attn2, conv2, w_out, norm_g, router_w, router_b):
    n_tok = x2.shape[0]
    row = lambda i: (i, 0)
    const = lambda i: (0, 0)
    vec = lambda v: v.reshape(1, -1)
    return pl.pallas_call(
        _mix_route_kernel,
        grid=(n_tok // MIX_ROWS,),
        in_specs=[
            pl.BlockSpec((MIX_ROWS, D_MODEL), row),
            pl.BlockSpec((MIX_ROWS, D_ATTN), row),
            pl.BlockSpec((MIX_ROWS, D_CONV), row),
            pl.BlockSpec((D_MODEL, D_MODEL), const),
            pl.BlockSpec((1, D_MODEL), const),
            pl.BlockSpec((N_EXPERTS, D_MODEL), const),
            pl.BlockSpec((N_EXPERTS, 1), const),
        ],
        out_specs=[
            pl.BlockSpec((MIX_ROWS, D_MODEL), row),
            pl.BlockSpec((MIX_ROWS, HALF), row),
            pl.BlockSpec((TOP_K, MIX_ROWS), lambda i: (0, i)),
            pl.BlockSpec((MIX_ROWS, 8), row),
            pl.BlockSpec((TOP_K, MIX_ROWS), lambda i: (0, i)),
            pl.BlockSpec((N_EXPERTS, 128), const),
        ],
        out_shape=[
            jax.ShapeDtypeStruct((n_tok, D_MODEL), F32),
            jax.ShapeDtypeStruct((n_tok, HALF), I32),
            jax.ShapeDtypeStruct((TOP_K, n_tok), I32),
            jax.ShapeDtypeStruct((n_tok, 8), F32),
            jax.ShapeDtypeStruct((TOP_K, n_tok), I32),
            jax.ShapeDtypeStruct((N_EXPERTS, 128), F32),
        ],
        scratch_shapes=[
            pltpu.VMEM((D_MODEL, D_MODEL), BF16),
            pltpu.VMEM((N_EXPERTS, 1), F32),
        ],
        compiler_params=pltpu.CompilerParams(
            dimension_semantics=("arbitrary",), vmem_limit_bytes=VMEM_LIMIT),
        name="mix_route",
    )(x2, attn2, conv2, w_out, vec(norm_g), router_w.T, router_b.reshape(N_EXPERTS, 1))


def _tile_plan(counts, n_tiles):
    tiles_per = (counts + EXP_ROWS - 1) // EXP_ROWS
    tile_end = jnp.cumsum(tiles_per)
    tile_begin = tile_end - tiles_per
    n_valid = tile_end[-1]
    tiles = jnp.arange(n_tiles, dtype=I32)
    tile_valid = tiles < n_valid
    capped = jnp.minimum(tiles, n_valid - 1)
    tile_expert = jnp.sum((capped[:, None] >= tile_end[None, :]).astype(I32), axis=1)
    tile_expert = jnp.minimum(tile_expert, N_EXPERTS - 1)
    is_first = (tiles[:, None] == tile_begin[None, :]) & (tiles_per[None, :] > 0)
    tile_first = tile_valid & jnp.any(is_first, axis=1)
    group_start = tile_begin * EXP_ROWS
    experts = jnp.arange(N_EXPERTS, dtype=I32)
    nonempty = tiles_per > 0
    parity = (jnp.cumsum(nonempty.astype(I32)) - 1) % 2
    later = nonempty[None, :] & (experts[None, :] > experts[:, None])
    nxt = jnp.min(jnp.where(later, experts[None, :], N_EXPERTS), axis=1)
    nxt = jnp.where(nxt == N_EXPERTS, -1, nxt)
    hot = (tile_expert[:, None] == experts[None, :]).astype(I32)
    tile_slot = jnp.sum(hot * parity[None, :], axis=1)
    tile_next = jnp.sum(hot * nxt[None, :], axis=1)
    filled = jnp.sum(hot * counts[None, :], axis=1) - (tiles - jnp.sum(hot * tile_begin[None, :], axis=1)) * EXP_ROWS
    filled = jnp.where(tile_valid, jnp.clip(filled, 0, EXP_ROWS), 0)
    tile_rows = (filled + EXP_GRANULE - 1) // EXP_GRANULE * EXP_GRANULE
    flags = (tile_expert, tile_first.astype(I32), tile_rows.astype(I32),
             tile_slot.astype(I32), tile_next.astype(I32))
    return flags, group_start.astype(I32)


def _slot_kernel(gstart_ref, idx_ref, rank_ref, pos_ref):
    idx = idx_ref[...]
    pos = rank_ref[...]
    for e in range(N_EXPERTS):
        pos = pos + jnp.where(idx == e, gstart_ref[e], 0)
    pos_ref[...] = pos


def _slots(group_start, idx, rank):
    full = pl.BlockSpec(idx.shape, lambda i, gs: (0, 0))
    return pl.pallas_call(
        _slot_kernel,
        grid_spec=pltpu.PrefetchScalarGridSpec(
            num_scalar_prefetch=1, grid=(1,), in_specs=[full, full], out_specs=full),
        out_shape=jax.ShapeDtypeStruct(idx.shape, I32),
        name="slots",
    )(group_start, idx, rank)


def _sc_mesh():
    return plsc.VectorSubcoreMesh(core_axis_name="core", subcore_axis_name="subcore")


def _sc_worker():
    info = plsc.get_sparse_core_info()
    wid = lax.axis_index("subcore") * info.num_cores + lax.axis_index("core")
    return wid, info.num_cores * info.num_subcores


def _dispatch(t2, pos_flat, n_slots):
    n_tok, d = t2.shape
    n_workers = 32
    per_w = n_tok // n_workers
    assert per_w % SC_ROWS == 0

    @functools.partial(
        pl.kernel, mesh=_sc_mesh(),
        out_type=jax.ShapeDtypeStruct((n_slots, d), t2.dtype),
        scratch_types=[pltpu.VMEM((SC_ROWS,), I32), pltpu.VMEM((SC_ROWS, d), t2.dtype)],
        name="dispatch",
    )
    def k(t_hbm, pos_hbm, o_hbm, idx_v, rows_v):
        wid, nw = _sc_worker()
        assert nw == n_workers

        @pl.loop(0, per_w // SC_ROWS)
        def _(c):
            base = pl.multiple_of(wid * per_w + c * SC_ROWS, SC_ROWS)
            pltpu.sync_copy(t_hbm.at[pl.ds(base, SC_ROWS)], rows_v)
            for kk in range(TOP_K):
                pltpu.sync_copy(pos_hbm.at[pl.ds(kk * n_tok + base, SC_ROWS)], idx_v)
                pltpu.sync_copy(rows_v, o_hbm.at[idx_v])

    return k(t2, pos_flat)


def _gather_back(sorted_out, pos_flat):
    n_rows = pos_flat.shape[0]
    d = sorted_out.shape[1]
    n_workers = 32
    per_w = n_rows // n_workers
    assert per_w % SC_ROWS == 0

    @functools.partial(
        pl.kernel, mesh=_sc_mesh(),
        out_type=jax.ShapeDtypeStruct((n_rows, d), sorted_out.dtype),
        scratch_types=[pltpu.VMEM((SC_ROWS,), I32), pltpu.VMEM((SC_ROWS, d), sorted_out.dtype)],
        name="gather_back",
    )
    def k(s_hbm, pos_hbm, o_hbm, idx_v, rows_v):
        wid, nw = _sc_worker()
        assert nw == n_workers

        @pl.loop(0, per_w // SC_ROWS)
        def _(c):
            base = pl.multiple_of(wid * per_w + c * SC_ROWS, SC_ROWS)
            pltpu.sync_copy(pos_hbm.at[pl.ds(base, SC_ROWS)], idx_v)
            pltpu.sync_copy(s_hbm.at[idx_v], rows_v)
            pltpu.sync_copy(rows_v, o_hbm.at[pl.ds(base, SC_ROWS)])

    return k(sorted_out, pos_flat)


def _weight_copies(w_hbm, wf32, sem, expert, slot):
    return [pltpu.make_async_copy(w.at[expert], wf32.at[slot, mtx], sem.at[slot])
            for mtx, w in enumerate(w_hbm)]


def _expert_rows(rows, x_ref, bg_ref, bu_ref, bd_ref, o_ref, wbf):
    xs = jnp.concatenate(_unpack_rows(x_ref[0:rows, :]), axis=1).astype(BF16)
    cn = 256
    hids = []
    for c in range(D_MODEL // cn):
        sl = slice(c * cn, (c + 1) * cn)
        g = jnp.dot(xs, wbf[0, :, sl], preferred_element_type=F32) + bg_ref[:, sl]
        u = jnp.dot(xs, wbf[1, :, sl], preferred_element_type=F32) + bu_ref[:, sl]
        g = jnp.minimum(g, SWIGLU_LIMIT)
        u = jnp.clip(u, -SWIGLU_LIMIT, SWIGLU_LIMIT)
        hids.append(((u + 1.0) * (g * jax.nn.sigmoid(SWIGLU_ALPHA * g))).astype(BF16))
    acc = jnp.dot(jnp.concatenate(hids, axis=1), wbf[2], preferred_element_type=F32)
    o_ref[0:rows, :] = _pack_rows(acc + bd_ref[...])
    if rows < EXP_ROWS:
        o_ref[rows:EXP_ROWS, :] = jnp.zeros((EXP_ROWS - rows, HALF), I32)


def _expert_kernel(texp_ref, tfirst_ref, trows_ref, tslot_ref, tnext_ref,
                   x_ref, wg_hbm, bg_ref, wu_hbm, bu_ref, wd_hbm, bd_ref,
                   o_ref, wf32, wbf, wsem):
    w_hbm = (wg_hbm, wu_hbm, wd_hbm)
    for sub in range(EXP_TILES_PER_STEP):
        i = pl.program_id(0) * EXP_TILES_PER_STEP + sub
        x_tile = x_ref.at[sub * EXP_ROWS:(sub + 1) * EXP_ROWS]
        o_tile = o_ref.at[sub * EXP_ROWS:(sub + 1) * EXP_ROWS]
        _expert_tile(i, texp_ref, tfirst_ref, trows_ref, tslot_ref, tnext_ref,
                     x_tile, w_hbm, bg_ref, bu_ref, bd_ref, o_tile, wf32, wbf, wsem)


def _expert_tile(i, texp_ref, tfirst_ref, trows_ref, tslot_ref, tnext_ref,
                 x_ref, w_hbm, bg_ref, bu_ref, bd_ref, o_ref, wf32, wbf, wsem):
    @pl.when(trows_ref[i] == 0)
    def _():
        o_ref[...] = jnp.zeros(o_ref.shape, o_ref.dtype)

    @pl.when(tfirst_ref[i] == 1)
    def _():
        slot = tslot_ref[i]
        expert = texp_ref[i]

        @pl.when(i == 0)
        def _():
            for cp in _weight_copies(w_hbm, wf32, wsem, expert, slot):
                cp.start()

        for cp in _weight_copies(w_hbm, wf32, wsem, expert, slot):
            cp.wait()

        @pl.when(tnext_ref[i] >= 0)
        def _():
            for cp in _weight_copies(w_hbm, wf32, wsem, tnext_ref[i], 1 - slot):
                cp.start()

        for mtx in range(3):
            _cast_rows(wf32.at[slot, mtx], wbf.at[mtx], D_MODEL)

    for rows in range(EXP_GRANULE, EXP_ROWS + 1, EXP_GRANULE):
        @pl.when(trows_ref[i] == rows)
        def _(rows=rows):
            expert = texp_ref[i]
            _expert_rows(rows, x_ref, bg_ref.at[expert], bu_ref.at[expert], bd_ref.at[expert],
                         o_ref, wbf)


def _experts(sorted_t, tile_flags, w_gate, b_gate, w_up, b_up, w_down, b_down, n_tiles):
    row_blk = pl.BlockSpec((EXP_TILES_PER_STEP * EXP_ROWS, HALF), lambda i, *_: (i, 0))
    w_any = pl.BlockSpec(memory_space=pl.ANY)
    b_blk = pl.BlockSpec((N_EXPERTS, 1, D_MODEL), lambda i, *_: (0, 0, 0))
    grid_spec = pltpu.PrefetchScalarGridSpec(
        num_scalar_prefetch=len(tile_flags),
        grid=(n_tiles // EXP_TILES_PER_STEP,),
        in_specs=[row_blk, w_any, b_blk, w_any, b_blk, w_any, b_blk],
        out_specs=row_blk,
        scratch_shapes=[
            pltpu.VMEM((2, 3, D_MODEL, D_MODEL), F32),
            pltpu.VMEM((3, D_MODEL, D_MODEL), BF16),
            pltpu.SemaphoreType.DMA((2,)),
        ],
    )
    b3 = lambda b: b.reshape(N_EXPERTS, 1, D_MODEL)
    return pl.pallas_call(
        _expert_kernel,
        grid_spec=grid_spec,
        out_shape=jax.ShapeDtypeStruct((n_tiles * EXP_ROWS, HALF), I32),
        compiler_params=pltpu.CompilerParams(
            dimension_semantics=("arbitrary",), vmem_limit_bytes=VMEM_LIMIT),
        name="experts",
    )(*tile_flags, sorted_t,
      w_gate, b3(b_gate), w_up, b3(b_up), w_down, b3(b_down))


def _combine_kernel(x1_ref, s0_ref, s1_ref, s2_ref, s3_ref, w_ref, g_ref, o_ref):
    w = w_ref[...]
    y_lo = x1_ref[:, :HALF]
    y_hi = x1_ref[:, HALF:]
    for k, s_ref in enumerate((s0_ref, s1_ref, s2_ref, s3_ref)):
        lo, hi = _unpack_rows(s_ref[...])
        y_lo = y_lo + w[:, k:k + 1] * lo
        y_hi = y_hi + w[:, k:k + 1] * hi
    sq = jnp.sum(y_lo * y_lo, axis=-1, keepdims=True) + jnp.sum(y_hi * y_hi, axis=-1, keepdims=True)
    scale = lax.rsqrt(sq * (1.0 / D_MODEL) + RMS_EPS)
    o_ref[:, :HALF] = (y_lo * scale) * g_ref[:, :HALF]
    o_ref[:, HALF:] = (y_hi * scale) * g_ref[:, HALF:]


def _combine(x1, slabs, wgt, norm_g):
    n_tok = x1.shape[0]
    rows = MIX_ROWS
    ntiles = n_tok // rows
    slab_spec = lambda k: pl.BlockSpec((rows, HALF), lambda i, k=k: (k * ntiles + i, 0))
    return pl.pallas_call(
        _combine_kernel,
        grid=(ntiles,),
        in_specs=[pl.BlockSpec((rows, D_MODEL), lambda i: (i, 0))]
        + [slab_spec(k) for k in range(TOP_K)]
        + [pl.BlockSpec((rows, 8), lambda i: (i, 0)),
           pl.BlockSpec((1, D_MODEL), lambda i: (0, 0))],
        out_specs=pl.BlockSpec((rows, D_MODEL), lambda i: (i, 0)),
        out_shape=jax.ShapeDtypeStruct((n_tok, D_MODEL), F32),
        compiler_params=pltpu.CompilerParams(
            dimension_semantics=("arbitrary",), vmem_limit_bytes=VMEM_LIMIT),
        name="combine",
    )(x1, slabs, slabs, slabs, slabs, wgt, norm_g.reshape(1, D_MODEL))


def kernel(x, norm_mix_g, w_in, conv_dw_w, conv_dw_b, conv_ln_g, conv_ln_b, rel_bias, w_out,
           norm_ffn_g, router_w, router_b, exp_w_gate, exp_b_gate, exp_w_up, exp_b_up,
           exp_w_down, exp_b_down, norm_final_g):
    bsz, seq, _ = x.shape
    n_tok = bsz * seq
    assert norm_mix_g.shape[0] == 1, "single-layer block"
    assert seq % IN_ROWS == 0 and seq % MIX_ROWS == 0 and IN_ROWS == LEFT_PAD
    n_tiles = (TOP_K * n_tok) // EXP_ROWS + N_EXPERTS - 1
    n_tiles = -(-n_tiles // EXP_TILES_PER_STEP) * EXP_TILES_PER_STEP

    q, kpad, vpad, conv = _inproj(x, norm_mix_g[0], w_in[0], conv_dw_w[0], conv_dw_b[0],
                                  conv_ln_g[0], conv_ln_b[0])
    attn = _attention(q, kpad, vpad, _band_bias(rel_bias[0]))
    x1, t, idx, wgt, rank, cnt = _mix_route(
        x.reshape(n_tok, D_MODEL), attn.reshape(n_tok, D_ATTN), conv.reshape(n_tok, D_CONV),
        w_out[0], norm_ffn_g[0], router_w[0], router_b[0])
    tile_flags, group_start = _tile_plan(cnt[:, 0].astype(I32), n_tiles)
    pos_flat = _slots(group_start, idx, rank).reshape(TOP_K * n_tok)
    sorted_t = _dispatch(t, pos_flat, n_tiles * EXP_ROWS)
    sorted_out = _experts(sorted_t, tile_flags,
                          exp_w_gate[0], exp_b_gate[0], exp_w_up[0], exp_b_up[0],
                          exp_w_down[0], exp_b_down[0], n_tiles)
    slabs = _gather_back(sorted_out, pos_flat)
    out = _combine(x1, slabs, wgt, norm_final_g)
    return out.reshape(bsz, seq, D_MODEL)
```

```python
import functools

import jax
import jax.numpy as jnp
from jax import lax
from jax.experimental import pallas as pl
from jax.experimental.pallas import tpu as pltpu
from jax.experimental.pallas import tpu_sc as plsc

F32 = jnp.float32
BF16 = jnp.bfloat16
I32 = jnp.int32

D_MODEL = 1024
CHUNK = 64
N_HEADS = 8
HEAD_DIM = 64
D_ATTN = N_HEADS * HEAD_DIM
LEFT_CHUNKS = 8
REL_MAX = 128
REL_MIN = -(CHUNK - 1)
D_CONV = D_MODEL - D_ATTN
CONV_WIDTH = 31
N_EXPERTS = 32
TOP_K = 4
SWIGLU_ALPHA = 1.702
SWIGLU_LIMIT = 7.0
RMS_EPS = 1e-5
LN_EPS = 1e-5

LEFT_PAD = LEFT_CHUNKS * CHUNK
IN_ROWS = 512
PAD_BLOCKS = LEFT_PAD // IN_ROWS
Q_ROWS = 2 * CHUNK
ATT_BLOCKS = 4
BAND_ROWS = Q_ROWS + LEFT_PAD
HEADS_PER_STEP = 4
GROUP_LANES = HEADS_PER_STEP * HEAD_DIM
MIX_ROWS = 1024
HALO_ROWS = 32
EXP_ROWS = 512
EXP_GRANULE = 128
EXP_TILES_PER_STEP = 2
SC_ROWS = 128
NEG_BIG = -1e30
VMEM_LIMIT = 56 * 1024 * 1024


HALF = D_MODEL // 2
HI_MASK = -65536


def _pack_rows(x):
    bits = lax.bitcast_convert_type(x.astype(BF16).astype(F32), I32)
    return lax.shift_right_logical(bits[:, :HALF], 16) | (bits[:, HALF:] & HI_MASK)


def _unpack_rows(w):
    lo = lax.bitcast_convert_type(lax.shift_left(w, 16), F32)
    hi = lax.bitcast_convert_type(w & HI_MASK, F32)
    return lo, hi


def _cast_rows(src_ref, dst_ref, rows, step=128):
    def body(c, carry):
        r = pl.multiple_of(c * step, step)
        dst_ref[pl.ds(r, step), :] = src_ref[pl.ds(r, step), :].astype(dst_ref.dtype)
        return carry
    lax.fori_loop(0, rows // step, body, 0)


def _conv_branch(hw_ref, sh_ref, rows, cw_ref, cb_ref, lg_ref, lb_ref, out_ref):
    off = HALO_ROWS - (CONV_WIDTH - 1)
    shift_rows = HALO_ROWS + rows - 8
    for b in range(1, 8):
        sh_ref[b - 1] = hw_ref[pl.ds(b, shift_rows), :]
    acc = None
    for j in range(CONV_WIDTH):
        a, b = divmod(off + j, 8)
        src = hw_ref if b == 0 else sh_ref.at[b - 1]
        term = src[pl.ds(8 * a, rows), :] * cw_ref[j:j + 1, :]
        acc = term if acc is None else acc + term
    acc = acc + cb_ref[...]
    mu = jnp.mean(acc, axis=-1, keepdims=True)
    d = acc - mu
    var = jnp.mean(d * d, axis=-1, keepdims=True)
    y = d * lax.rsqrt(var + LN_EPS) * lg_ref[...] + lb_ref[...]
    out_ref[...] = (y * jax.nn.sigmoid(y)).astype(out_ref.dtype)


def _inproj_kernel(x_ref, g_ref, w_ref, cw_ref, cb_ref, lg_ref, lb_ref,
                   q_ref, k_ref, v_ref, c_ref, wbf_ref, hw_ref, sh_ref):
    b = pl.program_id(0)
    j = pl.program_id(1)

    @pl.when((b == 0) & (j == 0))
    def _():
        _cast_rows(w_ref, wbf_ref, D_MODEL)

    @pl.when(j < PAD_BLOCKS)
    def _():
        k_ref[...] = jnp.zeros_like(k_ref)
        v_ref[...] = jnp.zeros_like(v_ref)
        hw_ref[IN_ROWS:IN_ROWS + HALO_ROWS, :] = jnp.zeros((HALO_ROWS, D_CONV), F32)

    @pl.when(j >= PAD_BLOCKS)
    def _():
        x = x_ref[...]
        ms = jnp.mean(x * x, axis=-1, keepdims=True)
        hb = ((x * lax.rsqrt(ms + RMS_EPS)) * g_ref[...]).astype(BF16)

        def proj(c0, width):
            return jnp.dot(hb, wbf_ref[:, c0:c0 + width], preferred_element_type=F32)

        a = proj(3 * D_ATTN, D_CONV)
        gate = proj(3 * D_ATTN + D_CONV, D_CONV)
        hw_ref[0:HALO_ROWS, :] = hw_ref[IN_ROWS:IN_ROWS + HALO_ROWS, :]
        hw_ref[HALO_ROWS:HALO_ROWS + IN_ROWS, :] = a * jax.nn.sigmoid(gate)
        _conv_branch(hw_ref, sh_ref, IN_ROWS, cw_ref, cb_ref, lg_ref, lb_ref, c_ref)

        q_ref[...] = (proj(0, D_ATTN) * (HEAD_DIM ** -0.5)).astype(BF16)
        k_ref[...] = proj(D_ATTN, D_ATTN).astype(BF16)
        v_ref[...] = proj(2 * D_ATTN, D_ATTN).astype(BF16)


def _inproj(x, norm_g, w_in, conv_w, conv_b, ln_g, ln_b):
    bsz, seq, _ = x.shape
    nblk = seq // IN_ROWS
    d_cols = w_in.shape[1]
    row_blk = lambda b, j: (b, jnp.maximum(j - PAD_BLOCKS, 0), 0)
    const = lambda b, j: (0, 0)
    vec = lambda v: v.reshape(1, -1)
    return pl.pallas_call(
        _inproj_kernel,
        grid=(bsz, nblk + PAD_BLOCKS),
        in_specs=[
            pl.BlockSpec((None, IN_ROWS, D_MODEL), row_blk),
            pl.BlockSpec((1, D_MODEL), const),
            pl.BlockSpec((D_MODEL, d_cols), const),
            pl.BlockSpec((CONV_WIDTH, D_CONV), const),
            pl.BlockSpec((1, D_CONV), const),
            pl.BlockSpec((1, D_CONV), const),
            pl.BlockSpec((1, D_CONV), const),
        ],
        out_specs=[
            pl.BlockSpec((None, IN_ROWS, D_ATTN), row_blk),
            pl.BlockSpec((None, IN_ROWS, D_ATTN), lambda b, j: (b, j, 0)),
            pl.BlockSpec((None, IN_ROWS, D_ATTN), lambda b, j: (b, j, 0)),
            pl.BlockSpec((None, IN_ROWS, D_CONV), row_blk),
        ],
        out_shape=[
            jax.ShapeDtypeStruct((bsz, seq, D_ATTN), BF16),
            jax.ShapeDtypeStruct((bsz, seq + LEFT_PAD, D_ATTN), BF16),
            jax.ShapeDtypeStruct((bsz, seq + LEFT_PAD, D_ATTN), BF16),
            jax.ShapeDtypeStruct((bsz, seq, D_CONV), BF16),
        ],
        scratch_shapes=[
            pltpu.VMEM((D_MODEL, d_cols), BF16),
            pltpu.VMEM((HALO_ROWS + IN_ROWS, D_CONV), F32),
            pltpu.VMEM((7, HALO_ROWS + IN_ROWS - 8, D_CONV), F32),
        ],
        compiler_params=pltpu.CompilerParams(
            dimension_semantics=("arbitrary", "arbitrary"), vmem_limit_bytes=VMEM_LIMIT),
        name="inproj",
    )(x, vec(norm_g), w_in, conv_w, vec(conv_b), vec(ln_g), vec(ln_b))


def _attn_kernel(q_ref, k_ref, v_ref, bias_ref, o_ref):
    i = pl.program_id(1)
    lane = lax.broadcasted_iota(I32, (Q_ROWS, GROUP_LANES), 1) // HEAD_DIM
    col = lax.broadcasted_iota(I32, (HEADS_PER_STEP * Q_ROWS, BAND_ROWS), 1)
    for qb in range(ATT_BLOCKS):
        blk = i * ATT_BLOCKS + qb
        start = pl.multiple_of(blk * Q_ROWS, Q_ROWS)
        qrows = slice(qb * Q_ROWS, (qb + 1) * Q_ROWS)
        key_ok = col >= LEFT_PAD - blk * Q_ROWS
        for g in range(N_HEADS // HEADS_PER_STEP):
            lanes = slice(g * GROUP_LANES, (g + 1) * GROUP_LANES)
            q = q_ref[qrows, lanes]
            qs = jnp.concatenate(
                [jnp.where(lane == h, q, jnp.zeros_like(q)) for h in range(HEADS_PER_STEP)], axis=0)
            kb = k_ref[pl.ds(start, BAND_ROWS), lanes]
            vb = v_ref[pl.ds(start, BAND_ROWS), lanes]
            s = lax.dot_general(qs, kb, (((1,), (1,)), ((), ())), preferred_element_type=F32)
            rows = slice(g * HEADS_PER_STEP * Q_ROWS, (g + 1) * HEADS_PER_STEP * Q_ROWS)
            s = jnp.where(key_ok, s + bias_ref[rows, :], NEG_BIG)
            m = jnp.max(s, axis=-1, keepdims=True)
            p = jnp.exp(s - m)
            l = jnp.sum(p, axis=-1, keepdims=True)
            o = jnp.dot(p.astype(BF16), vb, preferred_element_type=F32) / l
            out = o[0:Q_ROWS]
            for h in range(1, HEADS_PER_STEP):
                out = jnp.where(lane == h, o[h * Q_ROWS:(h + 1) * Q_ROWS], out)
            o_ref[qrows, lanes] = out.astype(o_ref.dtype)


def _attention(q, kpad, vpad, bias):
    bsz, seq, _ = q.shape
    step_rows = ATT_BLOCKS * Q_ROWS
    return pl.pallas_call(
        _attn_kernel,
        grid=(bsz, seq // step_rows),
        in_specs=[
            pl.BlockSpec((None, step_rows, D_ATTN), lambda b, i: (b, i, 0)),
            pl.BlockSpec((None, seq + LEFT_PAD, D_ATTN), lambda b, i: (b, 0, 0)),
            pl.BlockSpec((None, seq + LEFT_PAD, D_ATTN), lambda b, i: (b, 0, 0)),
            pl.BlockSpec((N_HEADS * Q_ROWS, BAND_ROWS), lambda b, i: (0, 0)),
        ],
        out_specs=pl.BlockSpec((None, step_rows, D_ATTN), lambda b, i: (b, i, 0)),
        out_shape=jax.ShapeDtypeStruct((bsz, seq, D_ATTN), BF16),
        compiler_params=pltpu.CompilerParams(
            dimension_semantics=("arbitrary", "arbitrary"), vmem_limit_bytes=VMEM_LIMIT),
        name="chunk_attn",
    )(q, kpad, vpad, bias)


def _band_bias(rel_bias):
    n_rel = REL_MAX - REL_MIN + 1
    far = jnp.broadcast_to(rel_bias[:, n_rel - 1:n_rel], (N_HEADS, BAND_ROWS - 1 - REL_MAX))
    near = rel_bias[:, ::-1]
    ahead = jnp.broadcast_to(rel_bias[:, 0:1], (N_HEADS, Q_ROWS - 1 + REL_MIN))
    diag = jnp.concatenate([far, near, ahead], axis=1).astype(F32)
    bias = jnp.stack(
        [diag[:, Q_ROWS - 1 - r:Q_ROWS - 1 - r + BAND_ROWS] for r in range(Q_ROWS)], axis=1)
    r = jnp.arange(Q_ROWS)[:, None]
    m = jnp.arange(BAND_ROWS)[None, :]
    cq = r // CHUNK
    ck = m // CHUNK
    in_band = (ck >= cq) & (ck <= cq + LEFT_CHUNKS)
    bias = jnp.where(in_band[None], bias, NEG_BIG)
    return bias.reshape(N_HEADS * Q_ROWS, BAND_ROWS)


def _split_bf16(v):
    hi = v.astype(BF16)
    lo = (v - hi.astype(F32)).astype(BF16)
    return hi, lo


def _mix_route_kernel(x_ref, a_ref, c_ref, wo_ref, ng_ref, rwt_ref, rb_ref,
                      x1_ref, t_ref, idx_ref, wgt_ref, rank_ref, cnt_ref,
                      wobf_ref, cntacc_ref):
    i = pl.program_id(0)

    @pl.when(i == 0)
    def _():
        _cast_rows(wo_ref, wobf_ref, D_MODEL)
        cntacc_ref[...] = jnp.zeros_like(cntacc_ref)

    mixed = jnp.dot(a_ref[...], wobf_ref[0:D_ATTN, :], preferred_element_type=F32)
    mixed = mixed + jnp.dot(c_ref[...], wobf_ref[D_ATTN:D_MODEL, :], preferred_element_type=F32)
    x1 = x_ref[...] + mixed
    x1_ref[...] = x1
    ms = jnp.mean(x1 * x1, axis=-1, keepdims=True)
    t = (x1 * lax.rsqrt(ms + RMS_EPS)) * ng_ref[...]
    t_ref[...] = _pack_rows(t)

    nt = (((1,), (1,)), ((), ()))
    w_hi, w_lo = _split_bf16(rwt_ref[...])
    t_hi, t_lo = _split_bf16(t)
    logits = (lax.dot_general(w_hi, t_hi, nt, preferred_element_type=F32)
              + lax.dot_general(w_hi, t_lo, nt, preferred_element_type=F32)
              + lax.dot_general(w_lo, t_hi, nt, preferred_element_type=F32)) + rb_ref[...]
    e_iota = lax.broadcasted_iota(I32, (N_EXPERTS, MIX_ROWS), 0)
    vals, idxs, hots = [], [], []
    for _ in range(TOP_K):
        m = jnp.max(logits, axis=0, keepdims=True)
        am = jnp.min(jnp.where(logits == m, e_iota, N_EXPERTS), axis=0, keepdims=True)
        hot = e_iota == am
        vals.append(m)
        idxs.append(am)
        hots.append(hot)
        logits = jnp.where(hot, -jnp.inf, logits)
    exps = [jnp.exp(v - vals[0]) for v in vals]
    den = exps[0] + exps[1] + exps[2] + exps[3]
    wts = [e / den for e in exps]

    hot_f = (hots[0] | hots[1] | hots[2] | hots[3]).astype(F32)
    ra = lax.broadcasted_iota(I32, (MIX_ROWS, MIX_ROWS), 0)
    rc = lax.broadcasted_iota(I32, (MIX_ROWS, MIX_ROWS), 1)
    upper = (ra < rc).astype(BF16)
    prefix = jnp.dot(hot_f.astype(BF16), upper, preferred_element_type=F32)
    base = prefix + cntacc_ref[...]
    ranks = [jnp.sum(jnp.where(h, base, 0.0), axis=0, keepdims=True) for h in hots]
    counts = cntacc_ref[...] + jnp.sum(hot_f, axis=1, keepdims=True)
    cntacc_ref[...] = counts

    idx_ref[...] = jnp.concatenate(idxs, axis=0)
    rank_ref[...] = jnp.concatenate(ranks, axis=0).astype(I32)
    cnt_ref[...] = jnp.broadcast_to(counts, cnt_ref.shape)
    w8 = jnp.concatenate(wts + [jnp.zeros((8 - TOP_K, MIX_ROWS), F32)], axis=0)
    wgt_ref[...] = w8.T


def _mix_route(x2, attn2, conv2, w_out, norm_g, router_w, router_b):
    n_tok = x2.shape[0]
    row = lambda i: (i, 0)
    const = lambda i: (0, 0)
    vec = lambda v: v.reshape(1, -1)
    return pl.pallas_call(
        _mix_route_kernel,
        grid=(n_tok // MIX_ROWS,),
        in_specs=[
            pl.BlockSpec((MIX_ROWS, D_MODEL), row),
            pl.BlockSpec((MIX_ROWS, D_ATTN), row),
            pl.BlockSpec((MIX_ROWS, D_CONV), row),
            pl.BlockSpec((D_MODEL, D_MODEL), const),
            pl.BlockSpec((1, D_MODEL), const),
            pl.BlockSpec((N_EXPERTS, D_MODEL), const),
            pl.BlockSpec((N_EXPERTS, 1), const),
        ],
        out_specs=[
            pl.BlockSpec((MIX_ROWS, D_MODEL), row),
            pl.BlockSpec((MIX_ROWS, HALF), row),
            pl.BlockSpec((TOP_K, MIX_ROWS), lambda i: (0, i)),
            pl.BlockSpec((MIX_ROWS, 8), row),
            pl.BlockSpec((TOP_K, MIX_ROWS), lambda i: (0, i)),
            pl.BlockSpec((N_EXPERTS, 128), const),
        ],
        out_shape=[
            jax.ShapeDtypeStruct((n_tok, D_MODEL), F32),
            jax.ShapeDtypeStruct((n_tok, HALF), I32),
            jax.ShapeDtypeStruct((TOP_K, n_tok), I32),
            jax.ShapeDtypeStruct((n_tok, 8), F32),
            jax.ShapeDtypeStruct((TOP_K, n_tok), I32),
            jax.ShapeDtypeStruct((N_EXPERTS, 128), F32),
        ],
        scratch_shapes=[
            pltpu.VMEM((D_MODEL, D_MODEL), BF16),
            pltpu.VMEM((N_EXPERTS, 1), F32),
        ],
        compiler_params=pltpu.CompilerParams(
            dimension_semantics=("arbitrary",), vmem_limit_bytes=VMEM_LIMIT),
        name="mix_route",
    )(x2, attn2, conv2, w_out, vec(norm_g), router_w.T, router_b.reshape(N_EXPERTS, 1))


def _tile_plan(counts, n_tiles):
    tiles_per = (counts + EXP_ROWS - 1) // EXP_ROWS
    tile_end = jnp.cumsum(tiles_per)
    tile_begin = tile_end - tiles_per
    n_valid = tile_end[-1]
    tiles = jnp.arange(n_tiles, dtype=I32)
    tile_valid = tiles < n_valid
    capped = jnp.minimum(tiles, n_valid - 1)
    tile_expert = jnp.sum((capped[:, None] >= tile_end[None, :]).astype(I32), axis=1)
    tile_expert = jnp.minimum(tile_expert, N_EXPERTS - 1)
    is_first = (tiles[:, None] == tile_begin[None, :]) & (tiles_per[None, :] > 0)
    tile_first = tile_valid & jnp.any(is_first, axis=1)
    group_start = tile_begin * EXP_ROWS
    experts = jnp.arange(N_EXPERTS, dtype=I32)
    nonempty = tiles_per > 0
    parity = (jnp.cumsum(nonempty.astype(I32)) - 1) % 2
    later = nonempty[None, :] & (experts[None, :] > experts[:, None])
    nxt = jnp.min(jnp.where(later, experts[None, :], N_EXPERTS), axis=1)
    nxt = jnp.where(nxt == N_EXPERTS, -1, nxt)
    hot = (tile_expert[:, None] == experts[None, :]).astype(I32)
    tile_slot = jnp.sum(hot * parity[None, :], axis=1)
    tile_next = jnp.sum(hot * nxt[None, :], axis=1)
    filled = jnp.sum(hot * counts[None, :], axis=1) - (tiles - jnp.sum(hot * tile_begin[None, :], axis=1)) * EXP_ROWS
    filled = jnp.where(tile_valid, jnp.clip(filled, 0, EXP_ROWS), 0)
    tile_rows = (filled + EXP_GRANULE - 1) // EXP_GRANULE * EXP_GRANULE
    flags = (tile_expert, tile_first.astype(I32), tile_rows.astype(I32),
             tile_slot.astype(I32), tile_next.astype(I32))
    return flags, group_start.astype(I32)


def _slot_kernel(gstart_ref, idx_ref, rank_ref, pos_ref):
    idx = idx_ref[...]
    pos = rank_ref[...]
    for e in range(N_EXPERTS):
        pos = pos + jnp.where(idx == e, gstart_ref[e], 0)
    pos_ref[...] = pos


def _slots(group_start, idx, rank):
    full = pl.BlockSpec(idx.shape, lambda i, gs: (0, 0))
    return pl.pallas_call(
        _slot_kernel,
        grid_spec=pltpu.PrefetchScalarGridSpec(
            num_scalar_prefetch=1, grid=(1,), in_specs=[full, full], out_specs=full),
        out_shape=jax.ShapeDtypeStruct(idx.shape, I32),
        name="slots",
    )(group_start, idx, rank)


def _sc_mesh():
    return plsc.VectorSubcoreMesh(core_axis_name="core", subcore_axis_name="subcore")


def _sc_worker():
    info = plsc.get_sparse_core_info()
    wid = lax.axis_index("subcore") * info.num_cores + lax.axis_index("core")
    return wid, info.num_cores * info.num_subcores


def _dispatch(t2, pos_flat, n_slots):
    n_tok, d = t2.shape
    n_workers = 32
    per_w = n_tok // n_workers
    assert per_w % SC_ROWS == 0

    @functools.partial(
        pl.kernel, mesh=_sc_mesh(),
        out_type=jax.ShapeDtypeStruct((n_slots, d), t2.dtype),
        scratch_types=[pltpu.VMEM((SC_ROWS,), I32), pltpu.VMEM((SC_ROWS, d), t2.dtype)],
        name="dispatch",
    )
    def k(t_hbm, pos_hbm, o_hbm, idx_v, rows_v):
        wid, nw = _sc_worker()
        assert nw == n_workers

        @pl.loop(0, per_w // SC_ROWS)
        def _(c):
            base = pl.multiple_of(wid * per_w + c * SC_ROWS, SC_ROWS)
            pltpu.sync_copy(t_hbm.at[pl.ds(base, SC_ROWS)], rows_v)
            for kk in range(TOP_K):
                pltpu.sync_copy(pos_hbm.at[pl.ds(kk * n_tok + base, SC_ROWS)], idx_v)
                pltpu.sync_copy(rows_v, o_hbm.at[idx_v])

    return k(t2, pos_flat)


def _gather_back(sorted_out, pos_flat):
    n_rows = pos_flat.shape[0]
    d = sorted_out.shape[1]
    n_workers = 32
    per_w = n_rows // n_workers
    assert per_w % SC_ROWS == 0

    @functools.partial(
        pl.kernel, mesh=_sc_mesh(),
        out_type=jax.ShapeDtypeStruct((n_rows, d), sorted_out.dtype),
        scratch_types=[pltpu.VMEM((SC_ROWS,), I32), pltpu.VMEM((SC_ROWS, d), sorted_out.dtype)],
        name="gather_back",
    )
    def k(s_hbm, pos_hbm, o_hbm, idx_v, rows_v):
        wid, nw = _sc_worker()
        assert nw == n_workers

        @pl.loop(0, per_w // SC_ROWS)
        def _(c):
            base = pl.multiple_of(wid * per_w + c * SC_ROWS, SC_ROWS)
            pltpu.sync_copy(pos_hbm.at[pl.ds(base, SC_ROWS)], idx_v)
            pltpu.sync_copy(s_hbm.at[idx_v], rows_v)
            pltpu.sync_copy(rows_v, o_hbm.at[pl.ds(base, SC_ROWS)])

    return k(sorted_out, pos_flat)


def _weight_copies(w_hbm, wf32, sem, expert, slot):
    return [pltpu.make_async_copy(w.at[expert], wf32.at[slot, mtx], sem.at[slot])
            for mtx, w in enumerate(w_hbm)]


def _expert_rows(rows, x_ref, bg_ref, bu_ref, bd_ref, o_ref, wbf):
    xs = jnp.concatenate(_unpack_rows(x_ref[0:rows, :]), axis=1).astype(BF16)
    cn = 256
    hids = []
    for c in range(D_MODEL // cn):
        sl = slice(c * cn, (c + 1) * cn)
        g = jnp.dot(xs, wbf[0, :, sl], preferred_element_type=F32) + bg_ref[:, sl]
        u = jnp.dot(xs, wbf[1, :, sl], preferred_element_type=F32) + bu_ref[:, sl]
        g = jnp.minimum(g, SWIGLU_LIMIT)
        u = jnp.clip(u, -SWIGLU_LIMIT, SWIGLU_LIMIT)
        hids.append(((u + 1.0) * (g * jax.nn.sigmoid(SWIGLU_ALPHA * g))).astype(BF16))
    acc = jnp.dot(jnp.concatenate(hids, axis=1), wbf[2], preferred_element_type=F32)
    o_ref[0:rows, :] = _pack_rows(acc + bd_ref[...])
    if rows < EXP_ROWS:
        o_ref[rows:EXP_ROWS, :] = jnp.zeros((EXP_ROWS - rows, HALF), I32)


def _expert_kernel(texp_ref, tfirst_ref, trows_ref, tslot_ref, tnext_ref,
                   x_ref, wg_hbm, bg_ref, wu_hbm, bu_ref, wd_hbm, bd_ref,
                   o_ref, wf32, wbf, wsem):
    w_hbm = (wg_hbm, wu_hbm, wd_hbm)
    for sub in range(EXP_TILES_PER_STEP):
        i = pl.program_id(0) * EXP_TILES_PER_STEP + sub
        x_tile = x_ref.at[sub * EXP_ROWS:(sub + 1) * EXP_ROWS]
        o_tile = o_ref.at[sub * EXP_ROWS:(sub + 1) * EXP_ROWS]
        _expert_tile(i, texp_ref, tfirst_ref, trows_ref, tslot_ref, tnext_ref,
                     x_tile, w_hbm, bg_ref, bu_ref, bd_ref, o_tile, wf32, wbf, wsem)


def _expert_tile(i, texp_ref, tfirst_ref, trows_ref, tslot_ref, tnext_ref,
                 x_ref, w_hbm, bg_ref, bu_ref, bd_ref, o_ref, wf32, wbf, wsem):
    @pl.when(trows_ref[i] == 0)
    def _():
        o_ref[...] = jnp.zeros(o_ref.shape, o_ref.dtype)

    @pl.when(tfirst_ref[i] == 1)
    def _():
        slot = tslot_ref[i]
        expert = texp_ref[i]

        @pl.when(i == 0)
        def _():
            for cp in _weight_copies(w_hbm, wf32, wsem, expert, slot):
                cp.start()

        for cp in _weight_copies(w_hbm, wf32, wsem, expert, slot):
            cp.wait()

        @pl.when(tnext_ref[i] >= 0)
        def _():
            for cp in _weight_copies(w_hbm, wf32, wsem, tnext_ref[i], 1 - slot):
                cp.start()

        for mtx in range(3):
            _cast_rows(wf32.at[slot, mtx], wbf.at[mtx], D_MODEL)

    for rows in range(EXP_GRANULE, EXP_ROWS + 1, EXP_GRANULE):
        @pl.when(trows_ref[i] == rows)
        def _(rows=rows):
            expert = texp_ref[i]
            _expert_rows(rows, x_ref, bg_ref.at[expert], bu_ref.at[expert], bd_ref.at[expert],
                         o_ref, wbf)


def _experts(sorted_t, tile_flags, w_gate, b_gate, w_up, b_up, w_down, b_down, n_tiles):
    row_blk = pl.BlockSpec((EXP_TILES_PER_STEP * EXP_ROWS, HALF), lambda i, *_: (i, 0))
    w_any = pl.BlockSpec(memory_space=pl.ANY)
    b_blk = pl.BlockSpec((N_EXPERTS, 1, D_MODEL), lambda i, *_: (0, 0, 0))
    grid_spec = pltpu.PrefetchScalarGridSpec(
        num_scalar_prefetch=len(tile_flags),
        grid=(n_tiles // EXP_TILES_PER_STEP,),
        in_specs=[row_blk, w_any, b_blk, w_any, b_blk, w_any, b_blk],
        out_specs=row_blk,
        scratch_shapes=[
            pltpu.VMEM((2, 3, D_MODEL, D_MODEL), F32),
            pltpu.VMEM((3, D_MODEL, D_MODEL), BF16),
            pltpu.SemaphoreType.DMA((2,)),
        ],
    )
    b3 = lambda b: b.reshape(N_EXPERTS, 1, D_MODEL)
    return pl.pallas_call(
        _expert_kernel,
        grid_spec=grid_spec,
        out_shape=jax.ShapeDtypeStruct((n_tiles * EXP_ROWS, HALF), I32),
        compiler_params=pltpu.CompilerParams(
            dimension_semantics=("arbitrary",), vmem_limit_bytes=VMEM_LIMIT),
        name="experts",
    )(*tile_flags, sorted_t,
      w_gate, b3(b_gate), w_up, b3(b_up), w_down, b3(b_down))


def _combine_kernel(x1_ref, s0_ref, s1_ref, s2_ref, s3_ref, w_ref, g_ref, o_ref):
    w = w_ref[...]
    y_lo = x1_ref[:, :HALF]
    y_hi = x1_ref[:, HALF:]
    for k, s_ref in enumerate((s0_ref, s1_ref, s2_ref, s3_ref)):
        lo, hi = _unpack_rows(s_ref[...])
        y_lo = y_lo + w[:, k:k + 1] * lo
        y_hi = y_hi + w[:, k:k + 1] * hi
    sq = jnp.sum(y_lo * y_lo, axis=-1, keepdims=True) + jnp.sum(y_hi * y_hi, axis=-1, keepdims=True)
    scale = lax.rsqrt(sq * (1.0 / D_MODEL) + RMS_EPS)
    o_ref[:, :HALF] = (y_lo * scale) * g_ref[:, :HALF]
    o_ref[:, HALF:] = (y_hi * scale) * g_ref[:, HALF:]


def _combine(x1, slabs, wgt, norm_g):
    n_tok = x1.shape[0]
    rows = MIX_ROWS
    ntiles = n_tok // rows
    slab_spec = lambda k: pl.BlockSpec((rows, HALF), lambda i, k=k: (k * ntiles + i, 0))
    return pl.pallas_call(
        _combine_kernel,
        grid=(ntiles,),
        in_specs=[pl.BlockSpec((rows, D_MODEL), lambda i: (i, 0))]
        + [slab_spec(k) for k in range(TOP_K)]
        + [pl.BlockSpec((rows, 8), lambda i: (i, 0)),
           pl.BlockSpec((1, D_MODEL), lambda i: (0, 0))],
        out_specs=pl.BlockSpec((rows, D_MODEL), lambda i: (i, 0)),
        out_shape=jax.ShapeDtypeStruct((n_tok, D_MODEL), F32),
        compiler_params=pltpu.CompilerParams(
            dimension_semantics=("arbitrary",), vmem_limit_bytes=VMEM_LIMIT),
        name="combine",
    )(x1, slabs, slabs, slabs, slabs, wgt, norm_g.reshape(1, D_MODEL))


def kernel(x, norm_mix_g, w_in, conv_dw_w, conv_dw_b, conv_ln_g, conv_ln_b, rel_bias, w_out,
           norm_ffn_g, router_w, router_b, exp_w_gate, exp_b_gate, exp_w_up, exp_b_up,
           exp_w_down, exp_b_down, norm_final_g):
    bsz, seq, _ = x.shape
    n_tok = bsz * seq
    assert norm_mix_g.shape[0] == 1, "single-layer block"
    assert seq % IN_ROWS == 0 and seq % MIX_ROWS == 0 and LEFT_PAD % IN_ROWS == 0
    n_tiles = (TOP_K * n_tok) // EXP_ROWS + N_EXPERTS - 1
    n_tiles = -(-n_tiles // EXP_TILES_PER_STEP) * EXP_TILES_PER_STEP

    q, kpad, vpad, conv = _inproj(x, norm_mix_g[0], w_in[0], conv_dw_w[0], conv_dw_b[0],
                                  conv_ln_g[0], conv_ln_b[0])
    attn = _attention(q, kpad, vpad, _band_bias(rel_bias[0]))
    x1, t, idx, wgt, rank, cnt = _mix_route(
        x.reshape(n_tok, D_MODEL), attn.reshape(n_tok, D_ATTN), conv.reshape(n_tok, D_CONV),
        w_out[0], norm_ffn_g[0], router_w[0], router_b[0])
    tile_flags, group_start = _tile_plan(cnt[:, 0].astype(I32), n_tiles)
    pos_flat = _slots(group_start, idx, rank).reshape(TOP_K * n_tok)
    sorted_t = _dispatch(t, pos_flat, n_tiles * EXP_ROWS)
    sorted_out = _experts(sorted_t, tile_flags,
                          exp_w_gate[0], exp_b_gate[0], exp_w_up[0], exp_b_up[0],
                          exp_w_down[0], exp_b_down[0], n_tiles)
    slabs = _gather_back(sorted_out, pos_flat)
    out = _combine(x1, slabs, wgt, norm_final_g)
    return out.reshape(bsz, seq, D_MODEL)
```

```python
import functools

import jax
import jax.numpy as jnp
from jax import lax
from jax.experimental import pallas as pl
from jax.experimental.pallas import tpu as pltpu
from jax.experimental.pallas import tpu_sc as plsc

F32 = jnp.float32
BF16 = jnp.bfloat16
I32 = jnp.int32

D_MODEL = 1024
CHUNK = 64
N_HEADS = 8
HEAD_DIM = 64
D_ATTN = N_HEADS * HEAD_DIM
LEFT_CHUNKS = 8
REL_MAX = 128
REL_MIN = -(CHUNK - 1)
D_CONV = D_MODEL - D_ATTN
CONV_WIDTH = 31
N_EXPERTS = 32
TOP_K = 4
SWIGLU_ALPHA = 1.702
SWIGLU_LIMIT = 7.0
RMS_EPS = 1e-5
LN_EPS = 1e-5

LEFT_PAD = LEFT_CHUNKS * CHUNK
IN_ROWS = 512
PAD_BLOCKS = LEFT_PAD // IN_ROWS
Q_ROWS = 2 * CHUNK
ATT_BLOCKS = 4
BAND_ROWS = Q_ROWS + LEFT_PAD
HEADS_PER_STEP = 4
GROUP_LANES = HEADS_PER_STEP * HEAD_DIM
MIX_ROWS = 1024
HALO_ROWS = 32
EXP_ROWS = 512
EXP_GRANULE = 128
EXP_TILES_PER_STEP = 4
SC_ROWS = 128
NEG_BIG = -1e30
V7X_VMEM_BYTES = 64 * 1024 * 1024
VMEM_LIMIT = V7X_VMEM_BYTES - 8 * 1024 * 1024
V7X_SC_WORKERS = 32


HALF = D_MODEL // 2
HI_MASK = -65536


def _pack_rows(x):
    bits = lax.bitcast_convert_type(x.astype(BF16).astype(F32), I32)
    return lax.shift_right_logical(bits[:, :HALF], 16) | (bits[:, HALF:] & HI_MASK)


def _unpack_rows(w):
    lo = lax.bitcast_convert_type(lax.shift_left(w, 16), F32)
    hi = lax.bitcast_convert_type(w & HI_MASK, F32)
    return lo, hi


def _cast_rows(src_ref, dst_ref, rows, step=128):
    def body(c, carry):
        r = pl.multiple_of(c * step, step)
        dst_ref[pl.ds(r, step), :] = src_ref[pl.ds(r, step), :].astype(dst_ref.dtype)
        return carry
    lax.fori_loop(0, rows // step, body, 0)


def _conv_branch(hw_ref, sh_ref, rows, cw_ref, cb_ref, lg_ref, lb_ref, out_ref):
    off = HALO_ROWS - (CONV_WIDTH - 1)
    shift_rows = HALO_ROWS + rows - 8
    for b in range(1, 8):
        sh_ref[b - 1] = hw_ref[pl.ds(b, shift_rows), :]
    acc = None
    for j in range(CONV_WIDTH):
        a, b = divmod(off + j, 8)
        src = hw_ref if b == 0 else sh_ref.at[b - 1]
        term = src[pl.ds(8 * a, rows), :] * cw_ref[j:j + 1, :]
        acc = term if acc is None else acc + term
    acc = acc + cb_ref[...]
    mu = jnp.mean(acc, axis=-1, keepdims=True)
    d = acc - mu
    var = jnp.mean(d * d, axis=-1, keepdims=True)
    y = d * lax.rsqrt(var + LN_EPS) * lg_ref[...] + lb_ref[...]
    out_ref[...] = (y * jax.nn.sigmoid(y)).astype(out_ref.dtype)


def _inproj_kernel(x_ref, g_ref, w_ref, cw_ref, cb_ref, lg_ref, lb_ref,
                   q_ref, k_ref, v_ref, c_ref, wbf_ref, hw_ref, sh_ref):
    b = pl.program_id(0)
    j = pl.program_id(1)

    @pl.when((b == 0) & (j == 0))
    def _():
        _cast_rows(w_ref, wbf_ref, D_MODEL)

    @pl.when(j < PAD_BLOCKS)
    def _():
        k_ref[...] = jnp.zeros_like(k_ref)
        v_ref[...] = jnp.zeros_like(v_ref)
        hw_ref[IN_ROWS:IN_ROWS + HALO_ROWS, :] = jnp.zeros((HALO_ROWS, D_CONV), F32)

    @pl.when(j >= PAD_BLOCKS)
    def _():
        x = x_ref[...]
        ms = jnp.mean(x * x, axis=-1, keepdims=True)
        hb = ((x * lax.rsqrt(ms + RMS_EPS)) * g_ref[...]).astype(BF16)

        def proj(c0, width):
            return jnp.dot(hb, wbf_ref[:, c0:c0 + width], preferred_element_type=F32)

        a = proj(3 * D_ATTN, D_CONV)
        gate = proj(3 * D_ATTN + D_CONV, D_CONV)
        hw_ref[0:HALO_ROWS, :] = hw_ref[IN_ROWS:IN_ROWS + HALO_ROWS, :]
        hw_ref[HALO_ROWS:HALO_ROWS + IN_ROWS, :] = a * jax.nn.sigmoid(gate)
        _conv_branch(hw_ref, sh_ref, IN_ROWS, cw_ref, cb_ref, lg_ref, lb_ref, c_ref)

        q_ref[...] = (proj(0, D_ATTN) * (HEAD_DIM ** -0.5)).astype(BF16)
        k_ref[...] = proj(D_ATTN, D_ATTN).astype(BF16)
        v_ref[...] = proj(2 * D_ATTN, D_ATTN).astype(BF16)


def _inproj(x, norm_g, w_in, conv_w, conv_b, ln_g, ln_b):
    bsz, seq, _ = x.shape
    nblk = seq // IN_ROWS
    d_cols = w_in.shape[1]
    row_blk = lambda b, j: (b, jnp.maximum(j - PAD_BLOCKS, 0), 0)
    const = lambda b, j: (0, 0)
    vec = lambda v: v.reshape(1, -1)
    return pl.pallas_call(
        _inproj_kernel,
        grid=(bsz, nblk + PAD_BLOCKS),
        in_specs=[
            pl.BlockSpec((None, IN_ROWS, D_MODEL), row_blk),
            pl.BlockSpec((1, D_MODEL), const),
            pl.BlockSpec((D_MODEL, d_cols), const),
            pl.BlockSpec((CONV_WIDTH, D_CONV), const),
            pl.BlockSpec((1, D_CONV), const),
            pl.BlockSpec((1, D_CONV), const),
            pl.BlockSpec((1, D_CONV), const),
        ],
        out_specs=[
            pl.BlockSpec((None, IN_ROWS, D_ATTN), row_blk),
            pl.BlockSpec((None, IN_ROWS, D_ATTN), lambda b, j: (b, j, 0)),
            pl.BlockSpec((None, IN_ROWS, D_ATTN), lambda b, j: (b, j, 0)),
            pl.BlockSpec((None, IN_ROWS, D_CONV), row_blk),
        ],
        out_shape=[
            jax.ShapeDtypeStruct((bsz, seq, D_ATTN), BF16),
            jax.ShapeDtypeStruct((bsz, seq + LEFT_PAD, D_ATTN), BF16),
            jax.ShapeDtypeStruct((bsz, seq + LEFT_PAD, D_ATTN), BF16),
            jax.ShapeDtypeStruct((bsz, seq, D_CONV), BF16),
        ],
        scratch_shapes=[
            pltpu.VMEM((D_MODEL, d_cols), BF16),
            pltpu.VMEM((HALO_ROWS + IN_ROWS, D_CONV), F32),
            pltpu.VMEM((7, HALO_ROWS + IN_ROWS - 8, D_CONV), F32),
        ],
        compiler_params=pltpu.CompilerParams(
            dimension_semantics=("arbitrary", "arbitrary"), vmem_limit_bytes=VMEM_LIMIT),
        name="inproj",
    )(x, vec(norm_g), w_in, conv_w, vec(conv_b), vec(ln_g), vec(ln_b))


def _attn_kernel(q_ref, k_ref, v_ref, bias_ref, o_ref):
    i = pl.program_id(1)
    first_steps = -(-LEFT_PAD // (ATT_BLOCKS * Q_ROWS))

    @pl.when(i < first_steps)
    def _():
        _attn_step(i, True, q_ref, k_ref, v_ref, bias_ref, o_ref)

    @pl.when(i >= first_steps)
    def _():
        _attn_step(i, False, q_ref, k_ref, v_ref, bias_ref, o_ref)


def _attn_step(i, mask_start, q_ref, k_ref, v_ref, bias_ref, o_ref):
    lane = lax.broadcasted_iota(I32, (Q_ROWS, GROUP_LANES), 1) // HEAD_DIM
    col = lax.broadcasted_iota(I32, (HEADS_PER_STEP * Q_ROWS, BAND_ROWS), 1)
    for qb in range(ATT_BLOCKS):
        blk = i * ATT_BLOCKS + qb
        start = pl.multiple_of(blk * Q_ROWS, Q_ROWS)
        qrows = slice(qb * Q_ROWS, (qb + 1) * Q_ROWS)
        key_ok = col >= LEFT_PAD - blk * Q_ROWS
        for g in range(N_HEADS // HEADS_PER_STEP):
            lanes = slice(g * GROUP_LANES, (g + 1) * GROUP_LANES)
            q = q_ref[qrows, lanes]
            qs = jnp.concatenate(
                [jnp.where(lane == h, q, jnp.zeros_like(q)) for h in range(HEADS_PER_STEP)], axis=0)
            kb = k_ref[pl.ds(start, BAND_ROWS), lanes]
            vb = v_ref[pl.ds(start, BAND_ROWS), lanes]
            s = lax.dot_general(qs, kb, (((1,), (1,)), ((), ())), preferred_element_type=F32)
            rows = slice(g * HEADS_PER_STEP * Q_ROWS, (g + 1) * HEADS_PER_STEP * Q_ROWS)
            s = s + bias_ref[rows, :]
            if mask_start:
                s = jnp.where(key_ok, s, NEG_BIG)
            m = jnp.max(s, axis=-1, keepdims=True)
            p = jnp.exp(s - m)
            l = jnp.sum(p, axis=-1, keepdims=True)
            o = jnp.dot(p.astype(BF16), vb, preferred_element_type=F32) / l
            out = o[0:Q_ROWS]
            for h in range(1, HEADS_PER_STEP):
                out = jnp.where(lane == h, o[h * Q_ROWS:(h + 1) * Q_ROWS], out)
            o_ref[qrows, lanes] = out.astype(o_ref.dtype)


def _attention(q, kpad, vpad, bias):
    bsz, seq, _ = q.shape
    step_rows = ATT_BLOCKS * Q_ROWS
    return pl.pallas_call(
        _attn_kernel,
        grid=(bsz, seq // step_rows),
        in_specs=[
            pl.BlockSpec((None, step_rows, D_ATTN), lambda b, i: (b, i, 0)),
            pl.BlockSpec((None, seq + LEFT_PAD, D_ATTN), lambda b, i: (b, 0, 0)),
            pl.BlockSpec((None, seq + LEFT_PAD, D_ATTN), lambda b, i: (b, 0, 0)),
            pl.BlockSpec((N_HEADS * Q_ROWS, BAND_ROWS), lambda b, i: (0, 0)),
        ],
        out_specs=pl.BlockSpec((None, step_rows, D_ATTN), lambda b, i: (b, i, 0)),
        out_shape=jax.ShapeDtypeStruct((bsz, seq, D_ATTN), BF16),
        compiler_params=pltpu.CompilerParams(
            dimension_semantics=("arbitrary", "arbitrary"), vmem_limit_bytes=VMEM_LIMIT),
        name="chunk_attn",
    )(q, kpad, vpad, bias)


def _band_bias(rel_bias):
    n_rel = REL_MAX - REL_MIN + 1
    far = jnp.broadcast_to(rel_bias[:, n_rel - 1:n_rel], (N_HEADS, BAND_ROWS - 1 - REL_MAX))
    near = rel_bias[:, ::-1]
    ahead = jnp.broadcast_to(rel_bias[:, 0:1], (N_HEADS, Q_ROWS - 1 + REL_MIN))
    diag = jnp.concatenate([far, near, ahead], axis=1).astype(F32)
    bias = jnp.stack(
        [diag[:, Q_ROWS - 1 - r:Q_ROWS - 1 - r + BAND_ROWS] for r in range(Q_ROWS)], axis=1)
    r = jnp.arange(Q_ROWS)[:, None]
    m = jnp.arange(BAND_ROWS)[None, :]
    cq = r // CHUNK
    ck = m // CHUNK
    in_band = (ck >= cq) & (ck <= cq + LEFT_CHUNKS)
    bias = jnp.where(in_band[None], bias, NEG_BIG)
    return bias.reshape(N_HEADS * Q_ROWS, BAND_ROWS)


def _split_bf16(v):
    hi = v.astype(BF16)
    lo = (v - hi.astype(F32)).astype(BF16)
    return hi, lo


def _mix_route_kernel(x_ref, a_ref, c_ref, wo_ref, ng_ref, rwt_ref, rb_ref,
                      x1_ref, t_ref, idx_ref, wgt_ref, rank_ref, cnt_ref,
                      wobf_ref, cntacc_ref):
    i = pl.program_id(0)

    @pl.when(i == 0)
    def _():
        _cast_rows(wo_ref, wobf_ref, D_MODEL)
        cntacc_ref[...] = jnp.zeros_like(cntacc_ref)

    mixed = jnp.dot(a_ref[...], wobf_ref[0:D_ATTN, :], preferred_element_type=F32)
    mixed = mixed + jnp.dot(c_ref[...], wobf_ref[D_ATTN:D_MODEL, :], preferred_element_type=F32)
    x1 = x_ref[...] + mixed
    x1_ref[...] = x1
    ms = jnp.mean(x1 * x1, axis=-1, keepdims=True)
    t = (x1 * lax.rsqrt(ms + RMS_EPS)) * ng_ref[...]
    t_ref[...] = _pack_rows(t)

    nt = (((1,), (1,)), ((), ()))
    w_hi, w_lo = _split_bf16(rwt_ref[...])
    t_hi, t_lo = _split_bf16(t)
    logits = (lax.dot_general(w_hi, t_hi, nt, preferred_element_type=F32)
              + lax.dot_general(w_hi, t_lo, nt, preferred_element_type=F32)
              + lax.dot_general(w_lo, t_hi, nt, preferred_element_type=F32)) + rb_ref[...]
    e_iota = lax.broadcasted_iota(I32, (N_EXPERTS, MIX_ROWS), 0)
    vals, idxs, hots = [], [], []
    for _ in range(TOP_K):
        m = jnp.max(logits, axis=0, keepdims=True)
        am = jnp.min(jnp.where(logits == m, e_iota, N_EXPERTS), axis=0, keepdims=True)
        hot = e_iota == am
        vals.append(m)
        idxs.append(am)
        hots.append(hot)
        logits = jnp.where(hot, -jnp.inf, logits)
    exps = [jnp.exp(v - vals[0]) for v in vals]
    den = exps[0] + exps[1] + exps[2] + exps[3]
    wts = [e / den for e in exps]

    hot_f = (hots[0] | hots[1] | hots[2] | hots[3]).astype(F32)
    ra = lax.broadcasted_iota(I32, (MIX_ROWS, MIX_ROWS), 0)
    rc = lax.broadcasted_iota(I32, (MIX_ROWS, MIX_ROWS), 1)
    upper = (ra < rc).astype(BF16)
    prefix = jnp.dot(hot_f.astype(BF16), upper, preferred_element_type=F32)
    base = prefix + cntacc_ref[...]
    ranks = [jnp.sum(jnp.where(h, base, 0.0), axis=0, keepdims=True) for h in hots]
    counts = cntacc_ref[...] + jnp.sum(hot_f, axis=1, keepdims=True)
    cntacc_ref[...] = counts

    idx_ref[...] = jnp.concatenate(idxs, axis=0)
    rank_ref[...] = jnp.concatenate(ranks, axis=0).astype(I32)
    cnt_ref[...] = jnp.broadcast_to(counts, cnt_ref.shape)
    w8 = jnp.concatenate(wts + [jnp.zeros((8 - TOP_K, MIX_ROWS), F32)], axis=0)
    wgt_ref[...] = w8.T


def _mix_route(x2, attn2, conv2, w_out, norm_g, router_w, router_b):
    n_tok = x2.shape[0]
    row = lambda i: (i, 0)
    const = lambda i: (0, 0)
    vec = lambda v: v.reshape(1, -1)
    return pl.pallas_call(
        _mix_route_kernel,
        grid=(n_tok // MIX_ROWS,),
        in_specs=[
            pl.BlockSpec((MIX_ROWS, D_MODEL), row),
            pl.BlockSpec((MIX_ROWS, D_ATTN), row),
            pl.BlockSpec((MIX_ROWS, D_CONV), row),
            pl.BlockSpec((D_MODEL, D_MODEL), const),
            pl.BlockSpec((1, D_MODEL), const),
            pl.BlockSpec((N_EXPERTS, D_MODEL), const),
            pl.BlockSpec((N_EXPERTS, 1), const),
        ],
        out_specs=[
            pl.BlockSpec((MIX_ROWS, D_MODEL), row),
            pl.BlockSpec((MIX_ROWS, HALF), row),
            pl.BlockSpec((TOP_K, MIX_ROWS), lambda i: (0, i)),
            pl.BlockSpec((MIX_ROWS, 8), row),
            pl.BlockSpec((TOP_K, MIX_ROWS), lambda i: (0, i)),
            pl.BlockSpec((N_EXPERTS, 128), const),
        ],
        out_shape=[
            jax.ShapeDtypeStruct((n_tok, D_MODEL), F32),
            jax.ShapeDtypeStruct((n_tok, HALF), I32),
            jax.ShapeDtypeStruct((TOP_K, n_tok), I32),
            jax.ShapeDtypeStruct((n_tok, 8), F32),
            jax.ShapeDtypeStruct((TOP_K, n_tok), I32),
            jax.ShapeDtypeStruct((N_EXPERTS, 128), F32),
        ],
        scratch_shapes=[
            pltpu.VMEM((D_MODEL, D_MODEL), BF16),
            pltpu.VMEM((N_EXPERTS, 1), F32),
        ],
        compiler_params=pltpu.CompilerParams(
            dimension_semantics=("arbitrary",), vmem_limit_bytes=VMEM_LIMIT),
        name="mix_route",
    )(x2, attn2, conv2, w_out, vec(norm_g), router_w.T, router_b.reshape(N_EXPERTS, 1))


def _tile_plan(counts, n_tiles):
    tiles_per = (counts + EXP_ROWS - 1) // EXP_ROWS
    tile_end = jnp.cumsum(tiles_per)
    tile_begin = tile_end - tiles_per
    n_valid = tile_end[-1]
    tiles = jnp.arange(n_tiles, dtype=I32)
    tile_valid = tiles < n_valid
    capped = jnp.minimum(tiles, n_valid - 1)
    tile_expert = jnp.sum((capped[:, None] >= tile_end[None, :]).astype(I32), axis=1)
    tile_expert = jnp.minimum(tile_expert, N_EXPERTS - 1)
    is_first = (tiles[:, None] == tile_begin[None, :]) & (tiles_per[None, :] > 0)
    tile_first = tile_valid & jnp.any(is_first, axis=1)
    group_start = tile_begin * EXP_ROWS
    experts = jnp.arange(N_EXPERTS, dtype=I32)
    nonempty = tiles_per > 0
    parity = (jnp.cumsum(nonempty.astype(I32)) - 1) % 2
    later = nonempty[None, :] & (experts[None, :] > experts[:, None])
    nxt = jnp.min(jnp.where(later, experts[None, :], N_EXPERTS), axis=1)
    nxt = jnp.where(nxt == N_EXPERTS, -1, nxt)
    hot = (tile_expert[:, None] == experts[None, :]).astype(I32)
    tile_slot = jnp.sum(hot * parity[None, :], axis=1)
    tile_next = jnp.sum(hot * nxt[None, :], axis=1)
    filled = jnp.sum(hot * counts[None, :], axis=1) - (tiles - jnp.sum(hot * tile_begin[None, :], axis=1)) * EXP_ROWS
    filled = jnp.where(tile_valid, jnp.clip(filled, 0, EXP_ROWS), 0)
    tile_rows = (filled + EXP_GRANULE - 1) // EXP_GRANULE * EXP_GRANULE
    flags = (tile_expert, tile_first.astype(I32), tile_rows.astype(I32),
             tile_slot.astype(I32), tile_next.astype(I32))
    return flags, group_start.astype(I32)


def _slot_kernel(gstart_ref, idx_ref, rank_ref, pos_ref):
    idx = idx_ref[...]
    pos = rank_ref[...]
    for e in range(N_EXPERTS):
        pos = pos + jnp.where(idx == e, gstart_ref[e], 0)
    pos_ref[...] = pos


def _slots(group_start, idx, rank):
    full = pl.BlockSpec(idx.shape, lambda i, gs: (0, 0))
    return pl.pallas_call(
        _slot_kernel,
        grid_spec=pltpu.PrefetchScalarGridSpec(
            num_scalar_prefetch=1, grid=(1,), in_specs=[full, full], out_specs=full),
        out_shape=jax.ShapeDtypeStruct(idx.shape, I32),
        name="slots",
    )(group_start, idx, rank)


def _sc_mesh():
    return plsc.VectorSubcoreMesh(core_axis_name="core", subcore_axis_name="subcore")


def _sc_worker():
    info = plsc.get_sparse_core_info()
    wid = lax.axis_index("subcore") * info.num_cores + lax.axis_index("core")
    return wid, info.num_cores * info.num_subcores


def _dispatch(t2, pos_flat, n_slots):
    n_tok, d = t2.shape
    n_workers = V7X_SC_WORKERS
    per_w = n_tok // n_workers
    assert per_w % SC_ROWS == 0

    @functools.partial(
        pl.kernel, mesh=_sc_mesh(),
        out_type=jax.ShapeDtypeStruct((n_slots, d), t2.dtype),
        scratch_types=[pltpu.VMEM((SC_ROWS,), I32), pltpu.VMEM((SC_ROWS, d), t2.dtype)],
        name="dispatch",
    )
    def k(t_hbm, pos_hbm, o_hbm, idx_v, rows_v):
        wid, nw = _sc_worker()
        assert nw == n_workers

        @pl.loop(0, per_w // SC_ROWS)
        def _(c):
            base = pl.multiple_of(wid * per_w + c * SC_ROWS, SC_ROWS)
            pltpu.sync_copy(t_hbm.at[pl.ds(base, SC_ROWS)], rows_v)
            for kk in range(TOP_K):
                pltpu.sync_copy(pos_hbm.at[pl.ds(kk * n_tok + base, SC_ROWS)], idx_v)
                pltpu.sync_copy(rows_v, o_hbm.at[idx_v])

    return k(t2, pos_flat)


def _gather_back(sorted_out, pos_flat):
    n_rows = pos_flat.shape[0]
    d = sorted_out.shape[1]
    n_workers = V7X_SC_WORKERS
    per_w = n_rows // n_workers
    assert per_w % SC_ROWS == 0

    @functools.partial(
        pl.kernel, mesh=_sc_mesh(),
        out_type=jax.ShapeDtypeStruct((n_rows, d), sorted_out.dtype),
        scratch_types=[pltpu.VMEM((SC_ROWS,), I32), pltpu.VMEM((SC_ROWS, d), sorted_out.dtype)],
        name="gather_back",
    )
    def k(s_hbm, pos_hbm, o_hbm, idx_v, rows_v):
        wid, nw = _sc_worker()
        assert nw == n_workers

        @pl.loop(0, per_w // SC_ROWS)
        def _(c):
            base = pl.multiple_of(wid * per_w + c * SC_ROWS, SC_ROWS)
            pltpu.sync_copy(pos_hbm.at[pl.ds(base, SC_ROWS)], idx_v)
            pltpu.sync_copy(s_hbm.at[idx_v], rows_v)
            pltpu.sync_copy(rows_v, o_hbm.at[pl.ds(base, SC_ROWS)])

    return k(sorted_out, pos_flat)


def _weight_copies(w_hbm, wf32, sem, expert, slot):
    return [pltpu.make_async_copy(w.at[expert], wf32.at[slot, mtx], sem.at[slot])
            for mtx, w in enumerate(w_hbm)]


def _expert_rows(rows, x_ref, bg_ref, bu_ref, bd_ref, o_ref, wbf):
    xs = jnp.concatenate(_unpack_rows(x_ref[0:rows, :]), axis=1).astype(BF16)
    cn = 256
    hids = []
    for c in range(D_MODEL // cn):
        sl = slice(c * cn, (c + 1) * cn)
        g = jnp.dot(xs, wbf[0, :, sl], preferred_element_type=F32) + bg_ref[:, sl]
        u = jnp.dot(xs, wbf[1, :, sl], preferred_element_type=F32) + bu_ref[:, sl]
        g = jnp.minimum(g, SWIGLU_LIMIT)
        u = jnp.clip(u, -SWIGLU_LIMIT, SWIGLU_LIMIT)
        hids.append(((u + 1.0) * (g * jax.nn.sigmoid(SWIGLU_ALPHA * g))).astype(BF16))
    acc = jnp.dot(jnp.concatenate(hids, axis=1), wbf[2], preferred_element_type=F32)
    o_ref[0:rows, :] = _pack_rows(acc + bd_ref[...])
    if rows < EXP_ROWS:
        o_ref[rows:EXP_ROWS, :] = jnp.zeros((EXP_ROWS - rows, HALF), I32)


def _expert_kernel(texp_ref, tfirst_ref, trows_ref, tslot_ref, tnext_ref,
                   x_ref, wg_hbm, bg_ref, wu_hbm, bu_ref, wd_hbm, bd_ref,
                   o_ref, wf32, wbf, wsem):
    w_hbm = (wg_hbm, wu_hbm, wd_hbm)
    for sub in range(EXP_TILES_PER_STEP):
        i = pl.program_id(0) * EXP_TILES_PER_STEP + sub
        x_tile = x_ref.at[sub * EXP_ROWS:(sub + 1) * EXP_ROWS]
        o_tile = o_ref.at[sub * EXP_ROWS:(sub + 1) * EXP_ROWS]
        _expert_tile(i, texp_ref, tfirst_ref, trows_ref, tslot_ref, tnext_ref,
                     x_tile, w_hbm, bg_ref, bu_ref, bd_ref, o_tile, wf32, wbf, wsem)


def _expert_tile(i, texp_ref, tfirst_ref, trows_ref, tslot_ref, tnext_ref,
                 x_ref, w_hbm, bg_ref, bu_ref, bd_ref, o_ref, wf32, wbf, wsem):
    @pl.when(trows_ref[i] == 0)
    def _():
        o_ref[...] = jnp.zeros(o_ref.shape, o_ref.dtype)

    @pl.when(tfirst_ref[i] == 1)
    def _():
        slot = tslot_ref[i]
        expert = texp_ref[i]

        @pl.when(i == 0)
        def _():
            for cp in _weight_copies(w_hbm, wf32, wsem, expert, slot):
                cp.start()

        for cp in _weight_copies(w_hbm, wf32, wsem, expert, slot):
            cp.wait()

        @pl.when(tnext_ref[i] >= 0)
        def _():
            for cp in _weight_copies(w_hbm, wf32, wsem, tnext_ref[i], 1 - slot):
                cp.start()

        for mtx in range(3):
            _cast_rows(wf32.at[slot, mtx], wbf.at[mtx], D_MODEL)

    for rows in range(EXP_GRANULE, EXP_ROWS + 1, EXP_GRANULE):
        @pl.when(trows_ref[i] == rows)
        def _(rows=rows):
            expert = texp_ref[i]
            _expert_rows(rows, x_ref, bg_ref.at[expert], bu_ref.at[expert], bd_ref.at[expert],
                         o_ref, wbf)


def _experts(sorted_t, tile_flags, w_gate, b_gate, w_up, b_up, w_down, b_down, n_tiles):
    row_blk = pl.BlockSpec((EXP_TILES_PER_STEP * EXP_ROWS, HALF), lambda i, *_: (i, 0))
    w_any = pl.BlockSpec(memory_space=pl.ANY)
    b_blk = pl.BlockSpec((N_EXPERTS, 1, D_MODEL), lambda i, *_: (0, 0, 0))
    grid_spec = pltpu.PrefetchScalarGridSpec(
        num_scalar_prefetch=len(tile_flags),
        grid=(n_tiles // EXP_TILES_PER_STEP,),
        in_specs=[row_blk, w_any, b_blk, w_any, b_blk, w_any, b_blk],
        out_specs=row_blk,
        scratch_shapes=[
            pltpu.VMEM((2, 3, D_MODEL, D_MODEL), F32),
            pltpu.VMEM((3, D_MODEL, D_MODEL), BF16),
            pltpu.SemaphoreType.DMA((2,)),
        ],
    )
    b3 = lambda b: b.reshape(N_EXPERTS, 1, D_MODEL)
    return pl.pallas_call(
        _expert_kernel,
        grid_spec=grid_spec,
        out_shape=jax.ShapeDtypeStruct((n_tiles * EXP_ROWS, HALF), I32),
        compiler_params=pltpu.CompilerParams(
            dimension_semantics=("arbitrary",), vmem_limit_bytes=VMEM_LIMIT),
        name="experts",
    )(*tile_flags, sorted_t,
      w_gate, b3(b_gate), w_up, b3(b_up), w_down, b3(b_down))


def _combine_kernel(x1_ref, s0_ref, s1_ref, s2_ref, s3_ref, w_ref, g_ref, o_ref):
    w = w_ref[...]
    y_lo = x1_ref[:, :HALF]
    y_hi = x1_ref[:, HALF:]
    for k, s_ref in enumerate((s0_ref, s1_ref, s2_ref, s3_ref)):
        lo, hi = _unpack_rows(s_ref[...])
        y_lo = y_lo + w[:, k:k + 1] * lo
        y_hi = y_hi + w[:, k:k + 1] * hi
    sq = jnp.sum(y_lo * y_lo, axis=-1, keepdims=True) + jnp.sum(y_hi * y_hi, axis=-1, keepdims=True)
    scale = lax.rsqrt(sq * (1.0 / D_MODEL) + RMS_EPS)
    o_ref[:, :HALF] = (y_lo * scale) * g_ref[:, :HALF]
    o_ref[:, HALF:] = (y_hi * scale) * g_ref[:, HALF:]


def _combine(x1, slabs, wgt, norm_g):
    n_tok = x1.shape[0]
    rows = MIX_ROWS
    ntiles = n_tok // rows
    slab_spec = lambda k: pl.BlockSpec((rows, HALF), lambda i, k=k: (k * ntiles + i, 0))
    return pl.pallas_call(
        _combine_kernel,
        grid=(ntiles,),
        in_specs=[pl.BlockSpec((rows, D_MODEL), lambda i: (i, 0))]
        + [slab_spec(k) for k in range(TOP_K)]
        + [pl.BlockSpec((rows, 8), lambda i: (i, 0)),
           pl.BlockSpec((1, D_MODEL), lambda i: (0, 0))],
        out_specs=pl.BlockSpec((rows, D_MODEL), lambda i: (i, 0)),
        out_shape=jax.ShapeDtypeStruct((n_tok, D_MODEL), F32),
        compiler_params=pltpu.CompilerParams(
            dimension_semantics=("arbitrary",), vmem_limit_bytes=VMEM_LIMIT),
        name="combine",
    )(x1, slabs, slabs, slabs, slabs, wgt, norm_g.reshape(1, D_MODEL))


def kernel(x, norm_mix_g, w_in, conv_dw_w, conv_dw_b, conv_ln_g, conv_ln_b, rel_bias, w_out,
           norm_ffn_g, router_w, router_b, exp_w_gate, exp_b_gate, exp_w_up, exp_b_up,
           exp_w_down, exp_b_down, norm_final_g):
    bsz, seq, _ = x.shape
    n_tok = bsz * seq
    assert norm_mix_g.shape[0] == 1, "single-layer block"
    assert seq % IN_ROWS == 0 and seq % MIX_ROWS == 0 and LEFT_PAD % IN_ROWS == 0
    n_tiles = (TOP_K * n_tok) // EXP_ROWS + N_EXPERTS - 1
    n_tiles = -(-n_tiles // EXP_TILES_PER_STEP) * EXP_TILES_PER_STEP

    q, kpad, vpad, conv = _inproj(x, norm_mix_g[0], w_in[0], conv_dw_w[0], conv_dw_b[0],
                                  conv_ln_g[0], conv_ln_b[0])
    attn = _attention(q, kpad, vpad, _band_bias(rel_bias[0]))
    x1, t, idx, wgt, rank, cnt = _mix_route(
        x.reshape(n_tok, D_MODEL), attn.reshape(n_tok, D_ATTN), conv.reshape(n_tok, D_CONV),
        w_out[0], norm_ffn_g[0], router_w[0], router_b[0])
    tile_flags, group_start = _tile_plan(cnt[:, 0].astype(I32), n_tiles)
    pos_flat = _slots(group_start, idx, rank).reshape(TOP_K * n_tok)
    sorted_t = _dispatch(t, pos_flat, n_tiles * EXP_ROWS)
    sorted_out = _experts(sorted_t, tile_flags,
                          exp_w_gate[0], exp_b_gate[0], exp_w_up[0], exp_b_up[0],
                          exp_w_down[0], exp_b_down[0], n_tiles)
    slabs = _gather_back(sorted_out, pos_flat)
    out = _combine(x1, slabs, wgt, norm_final_g)
    return out.reshape(bsz, seq, D_MODEL)
```

```python
import functools

import jax
import jax.numpy as jnp
from jax import lax
from jax.experimental import pallas as pl
from jax.experimental.pallas import tpu as pltpu
from jax.experimental.pallas import tpu_sc as plsc

F32 = jnp.float32
BF16 = jnp.bfloat16
I32 = jnp.int32

D_MODEL = 1024
CHUNK = 64
N_HEADS = 8
HEAD_DIM = 64
D_ATTN = N_HEADS * HEAD_DIM
LEFT_CHUNKS = 8
REL_MAX = 128
REL_MIN = -(CHUNK - 1)
D_CONV = D_MODEL - D_ATTN
CONV_WIDTH = 31
N_EXPERTS = 32
TOP_K = 4
SWIGLU_ALPHA = 1.702
SWIGLU_LIMIT = 7.0
RMS_EPS = 1e-5
LN_EPS = 1e-5

LEFT_PAD = LEFT_CHUNKS * CHUNK
IN_ROWS = 512
PAD_BLOCKS = LEFT_PAD // IN_ROWS
Q_ROWS = 2 * CHUNK
ATT_BLOCKS = 4
BAND_ROWS = Q_ROWS + LEFT_PAD
HEADS_PER_STEP = 4
GROUP_LANES = HEADS_PER_STEP * HEAD_DIM
MIX_ROWS = 1024
HALO_ROWS = 32
EXP_ROWS = 512
EXP_GRANULE = 128
EXP_TILES_PER_STEP = 2
SC_ROWS = 128
NEG_BIG = -1e30
V7X_VMEM_BYTES = 64 * 1024 * 1024
VMEM_LIMIT = V7X_VMEM_BYTES - 8 * 1024 * 1024
V7X_SC_WORKERS = 32


HALF = D_MODEL // 2
HI_MASK = -65536


def _pack_rows(x):
    bits = lax.bitcast_convert_type(x.astype(BF16).astype(F32), I32)
    return lax.shift_right_logical(bits[:, :HALF], 16) | (bits[:, HALF:] & HI_MASK)


def _unpack_rows(w):
    lo = lax.bitcast_convert_type(lax.shift_left(w, 16), F32)
    hi = lax.bitcast_convert_type(w & HI_MASK, F32)
    return lo, hi


def _cast_rows(src_ref, dst_ref, rows, step=128):
    def body(c, carry):
        r = pl.multiple_of(c * step, step)
        dst_ref[pl.ds(r, step), :] = src_ref[pl.ds(r, step), :].astype(dst_ref.dtype)
        return carry
    lax.fori_loop(0, rows // step, body, 0)


def _conv_branch(hw_ref, sh_ref, rows, cw_ref, cb_ref, lg_ref, lb_ref, out_ref):
    off = HALO_ROWS - (CONV_WIDTH - 1)
    shift_rows = HALO_ROWS + rows - 8
    for b in range(1, 8):
        sh_ref[b - 1] = hw_ref[pl.ds(b, shift_rows), :]
    acc = None
    for j in range(CONV_WIDTH):
        a, b = divmod(off + j, 8)
        src = hw_ref if b == 0 else sh_ref.at[b - 1]
        term = src[pl.ds(8 * a, rows), :] * cw_ref[j:j + 1, :]
        acc = term if acc is None else acc + term
    acc = acc + cb_ref[...]
    mu = jnp.mean(acc, axis=-1, keepdims=True)
    d = acc - mu
    var = jnp.mean(d * d, axis=-1, keepdims=True)
    y = d * lax.rsqrt(var + LN_EPS) * lg_ref[...] + lb_ref[...]
    out_ref[...] = (y * jax.nn.sigmoid(y)).astype(out_ref.dtype)


def _inproj_kernel(x_ref, g_ref, w_ref, cw_ref, cb_ref, lg_ref, lb_ref,
                   q_ref, k_ref, v_ref, c_ref, wbf_ref, hw_ref, sh_ref):
    b = pl.program_id(0)
    j = pl.program_id(1)

    @pl.when((b == 0) & (j == 0))
    def _():
        _cast_rows(w_ref, wbf_ref, D_MODEL)

    @pl.when(j < PAD_BLOCKS)
    def _():
        k_ref[...] = jnp.zeros_like(k_ref)
        v_ref[...] = jnp.zeros_like(v_ref)
        hw_ref[IN_ROWS:IN_ROWS + HALO_ROWS, :] = jnp.zeros((HALO_ROWS, D_CONV), F32)

    @pl.when(j >= PAD_BLOCKS)
    def _():
        x = x_ref[...]
        ms = jnp.mean(x * x, axis=-1, keepdims=True)
        hb = ((x * lax.rsqrt(ms + RMS_EPS)) * g_ref[...]).astype(BF16)

        def proj(c0, width):
            return jnp.dot(hb, wbf_ref[:, c0:c0 + width], preferred_element_type=F32)

        a = proj(3 * D_ATTN, D_CONV)
        gate = proj(3 * D_ATTN + D_CONV, D_CONV)
        hw_ref[0:HALO_ROWS, :] = hw_ref[IN_ROWS:IN_ROWS + HALO_ROWS, :]
        hw_ref[HALO_ROWS:HALO_ROWS + IN_ROWS, :] = a * jax.nn.sigmoid(gate)
        _conv_branch(hw_ref, sh_ref, IN_ROWS, cw_ref, cb_ref, lg_ref, lb_ref, c_ref)

        q_ref[...] = (proj(0, D_ATTN) * (HEAD_DIM ** -0.5)).astype(BF16)
        k_ref[...] = proj(D_ATTN, D_ATTN).astype(BF16)
        v_ref[...] = proj(2 * D_ATTN, D_ATTN).astype(BF16)


def _inproj(x, norm_g, w_in, conv_w, conv_b, ln_g, ln_b):
    bsz, seq, _ = x.shape
    nblk = seq // IN_ROWS
    d_cols = w_in.shape[1]
    row_blk = lambda b, j: (b, jnp.maximum(j - PAD_BLOCKS, 0), 0)
    const = lambda b, j: (0, 0)
    vec = lambda v: v.reshape(1, -1)
    return pl.pallas_call(
        _inproj_kernel,
        grid=(bsz, nblk + PAD_BLOCKS),
        in_specs=[
            pl.BlockSpec((None, IN_ROWS, D_MODEL), row_blk),
            pl.BlockSpec((1, D_MODEL), const),
            pl.BlockSpec((D_MODEL, d_cols), const),
            pl.BlockSpec((CONV_WIDTH, D_CONV), const),
            pl.BlockSpec((1, D_CONV), const),
            pl.BlockSpec((1, D_CONV), const),
            pl.BlockSpec((1, D_CONV), const),
        ],
        out_specs=[
            pl.BlockSpec((None, IN_ROWS, D_ATTN), row_blk),
            pl.BlockSpec((None, IN_ROWS, D_ATTN), lambda b, j: (b, j, 0)),
            pl.BlockSpec((None, IN_ROWS, D_ATTN), lambda b, j: (b, j, 0)),
            pl.BlockSpec((None, IN_ROWS, D_CONV), row_blk),
        ],
        out_shape=[
            jax.ShapeDtypeStruct((bsz, seq, D_ATTN), BF16),
            jax.ShapeDtypeStruct((bsz, seq + LEFT_PAD, D_ATTN), BF16),
            jax.ShapeDtypeStruct((bsz, seq + LEFT_PAD, D_ATTN), BF16),
            jax.ShapeDtypeStruct((bsz, seq, D_CONV), BF16),
        ],
        scratch_shapes=[
            pltpu.VMEM((D_MODEL, d_cols), BF16),
            pltpu.VMEM((HALO_ROWS + IN_ROWS, D_CONV), F32),
            pltpu.VMEM((7, HALO_ROWS + IN_ROWS - 8, D_CONV), F32),
        ],
        compiler_params=pltpu.CompilerParams(
            dimension_semantics=("arbitrary", "arbitrary"), vmem_limit_bytes=VMEM_LIMIT),
        name="inproj",
    )(x, vec(norm_g), w_in, conv_w, vec(conv_b), vec(ln_g), vec(ln_b))


def _attn_kernel(q_ref, k_ref, v_ref, bias_ref, o_ref):
    i = pl.program_id(1)
    first_steps = -(-LEFT_PAD // (ATT_BLOCKS * Q_ROWS))

    @pl.when(i < first_steps)
    def _():
        _attn_step(i, True, q_ref, k_ref, v_ref, bias_ref, o_ref)

    @pl.when(i >= first_steps)
    def _():
        _attn_step(i, False, q_ref, k_ref, v_ref, bias_ref, o_ref)


def _attn_step(i, mask_start, q_ref, k_ref, v_ref, bias_ref, o_ref):
    lane = lax.broadcasted_iota(I32, (Q_ROWS, GROUP_LANES), 1) // HEAD_DIM
    col = lax.broadcasted_iota(I32, (HEADS_PER_STEP * Q_ROWS, BAND_ROWS), 1)
    for qb in range(ATT_BLOCKS):
        blk = i * ATT_BLOCKS + qb
        start = pl.multiple_of(blk * Q_ROWS, Q_ROWS)
        qrows = slice(qb * Q_ROWS, (qb + 1) * Q_ROWS)
        key_ok = col >= LEFT_PAD - blk * Q_ROWS
        for g in range(N_HEADS // HEADS_PER_STEP):
            lanes = slice(g * GROUP_LANES, (g + 1) * GROUP_LANES)
            q = q_ref[qrows, lanes]
            qs = jnp.concatenate(
                [jnp.where(lane == h, q, jnp.zeros_like(q)) for h in range(HEADS_PER_STEP)], axis=0)
            kb = k_ref[pl.ds(start, BAND_ROWS), lanes]
            vb = v_ref[pl.ds(start, BAND_ROWS), lanes]
            s = lax.dot_general(qs, kb, (((1,), (1,)), ((), ())), preferred_element_type=F32)
            rows = slice(g * HEADS_PER_STEP * Q_ROWS, (g + 1) * HEADS_PER_STEP * Q_ROWS)
            s = s + bias_ref[rows, :]
            if mask_start:
                s = jnp.where(key_ok, s, NEG_BIG)
            m = jnp.max(s, axis=-1, keepdims=True)
            p = jnp.exp(s - m)
            l = jnp.sum(p, axis=-1, keepdims=True)
            o = jnp.dot(p.astype(BF16), vb, preferred_element_type=F32) / l
            out = o[0:Q_ROWS]
            for h in range(1, HEADS_PER_STEP):
                out = jnp.where(lane == h, o[h * Q_ROWS:(h + 1) * Q_ROWS], out)
            o_ref[qrows, lanes] = out.astype(o_ref.dtype)


def _attention(q, kpad, vpad, bias):
    bsz, seq, _ = q.shape
    step_rows = ATT_BLOCKS * Q_ROWS
    return pl.pallas_call(
        _attn_kernel,
        grid=(bsz, seq // step_rows),
        in_specs=[
            pl.BlockSpec((None, step_rows, D_ATTN), lambda b, i: (b, i, 0)),
            pl.BlockSpec((None, seq + LEFT_PAD, D_ATTN), lambda b, i: (b, 0, 0)),
            pl.BlockSpec((None, seq + LEFT_PAD, D_ATTN), lambda b, i: (b, 0, 0)),
            pl.BlockSpec((N_HEADS * Q_ROWS, BAND_ROWS), lambda b, i: (0, 0)),
        ],
        out_specs=pl.BlockSpec((None, step_rows, D_ATTN), lambda b, i: (b, i, 0)),
        out_shape=jax.ShapeDtypeStruct((bsz, seq, D_ATTN), BF16),
        compiler_params=pltpu.CompilerParams(
            dimension_semantics=("arbitrary", "arbitrary"), vmem_limit_bytes=VMEM_LIMIT),
        name="chunk_attn",
    )(q, kpad, vpad, bias)


def _band_bias(rel_bias):
    n_rel = REL_MAX - REL_MIN + 1
    far = jnp.broadcast_to(rel_bias[:, n_rel - 1:n_rel], (N_HEADS, BAND_ROWS - 1 - REL_MAX))
    near = rel_bias[:, ::-1]
    ahead = jnp.broadcast_to(rel_bias[:, 0:1], (N_HEADS, Q_ROWS - 1 + REL_MIN))
    diag = jnp.concatenate([far, near, ahead], axis=1).astype(F32)
    bias = jnp.stack(
        [diag[:, Q_ROWS - 1 - r:Q_ROWS - 1 - r + BAND_ROWS] for r in range(Q_ROWS)], axis=1)
    r = jnp.arange(Q_ROWS)[:, None]
    m = jnp.arange(BAND_ROWS)[None, :]
    cq = r // CHUNK
    ck = m // CHUNK
    in_band = (ck >= cq) & (ck <= cq + LEFT_CHUNKS)
    bias = jnp.where(in_band[None], bias, NEG_BIG)
    return bias.reshape(N_HEADS * Q_ROWS, BAND_ROWS)


def _split_bf16(v):
    hi = v.astype(BF16)
    lo = (v - hi.astype(F32)).astype(BF16)
    return hi, lo


def _mix_route_kernel(x_ref, a_ref, c_ref, wo_ref, ng_ref, rwt_ref, rb_ref,
                      x1_ref, t_ref, idx_ref, wgt_ref, rank_ref, cnt_ref,
                      wobf_ref, cntacc_ref):
    i = pl.program_id(0)

    @pl.when(i == 0)
    def _():
        _cast_rows(wo_ref, wobf_ref, D_MODEL)
        cntacc_ref[...] = jnp.zeros_like(cntacc_ref)

    mix_in = jnp.concatenate([a_ref[...], c_ref[...]], axis=1)
    x1 = x_ref[...] + jnp.dot(mix_in, wobf_ref[...], preferred_element_type=F32)
    x1_ref[...] = x1
    ms = jnp.mean(x1 * x1, axis=-1, keepdims=True)
    t = (x1 * lax.rsqrt(ms + RMS_EPS)) * ng_ref[...]
    t_ref[...] = _pack_rows(t)

    nt = (((1,), (1,)), ((), ()))
    w_hi, w_lo = _split_bf16(rwt_ref[...])
    t_hi, t_lo = _split_bf16(t)
    logits = (lax.dot_general(w_hi, t_hi, nt, preferred_element_type=F32)
              + lax.dot_general(w_hi, t_lo, nt, preferred_element_type=F32)
              + lax.dot_general(w_lo, t_hi, nt, preferred_element_type=F32)) + rb_ref[...]
    e_iota = lax.broadcasted_iota(I32, (N_EXPERTS, MIX_ROWS), 0)
    vals, idxs, hots = [], [], []
    for _ in range(TOP_K):
        m = jnp.max(logits, axis=0, keepdims=True)
        am = jnp.min(jnp.where(logits == m, e_iota, N_EXPERTS), axis=0, keepdims=True)
        hot = e_iota == am
        vals.append(m)
        idxs.append(am)
        hots.append(hot)
        logits = jnp.where(hot, -jnp.inf, logits)
    exps = [jnp.exp(v - vals[0]) for v in vals]
    den = exps[0] + exps[1] + exps[2] + exps[3]
    wts = [e / den for e in exps]

    hot_f = (hots[0] | hots[1] | hots[2] | hots[3]).astype(F32)
    ra = lax.broadcasted_iota(I32, (MIX_ROWS, MIX_ROWS), 0)
    rc = lax.broadcasted_iota(I32, (MIX_ROWS, MIX_ROWS), 1)
    upper = (ra < rc).astype(BF16)
    prefix = jnp.dot(hot_f.astype(BF16), upper, preferred_element_type=F32)
    base = prefix + cntacc_ref[...]
    ranks = [jnp.sum(jnp.where(h, base, 0.0), axis=0, keepdims=True) for h in hots]
    counts = cntacc_ref[...] + jnp.sum(hot_f, axis=1, keepdims=True)
    cntacc_ref[...] = counts

    idx_ref[...] = jnp.concatenate(idxs, axis=0)
    rank_ref[...] = jnp.concatenate(ranks, axis=0).astype(I32)
    cnt_ref[...] = jnp.broadcast_to(counts, cnt_ref.shape)
    w8 = jnp.concatenate(wts + [jnp.zeros((8 - TOP_K, MIX_ROWS), F32)], axis=0)
    wgt_ref[...] = w8.T


def _mix_route(x2, attn2, conv2, w_out, norm_g, router_w, router_b):
    n_tok = x2.shape[0]
    row = lambda i: (i, 0)
    const = lambda i: (0, 0)
    vec = lambda v: v.reshape(1, -1)
    return pl.pallas_call(
        _mix_route_kernel,
        grid=(n_tok // MIX_ROWS,),
        in_specs=[
            pl.BlockSpec((MIX_ROWS, D_MODEL), row),
            pl.BlockSpec((MIX_ROWS, D_ATTN), row),
            pl.BlockSpec((MIX_ROWS, D_CONV), row),
            pl.BlockSpec((D_MODEL, D_MODEL), const),
            pl.BlockSpec((1, D_MODEL), const),
            pl.BlockSpec((N_EXPERTS, D_MODEL), const),
            pl.BlockSpec((N_EXPERTS, 1), const),
        ],
        out_specs=[
            pl.BlockSpec((MIX_ROWS, D_MODEL), row),
            pl.BlockSpec((MIX_ROWS, HALF), row),
            pl.BlockSpec((TOP_K, MIX_ROWS), lambda i: (0, i)),
            pl.BlockSpec((MIX_ROWS, 8), row),
            pl.BlockSpec((TOP_K, MIX_ROWS), lambda i: (0, i)),
            pl.BlockSpec((N_EXPERTS, 128), const),
        ],
        out_shape=[
            jax.ShapeDtypeStruct((n_tok, D_MODEL), F32),
            jax.ShapeDtypeStruct((n_tok, HALF), I32),
            jax.ShapeDtypeStruct((TOP_K, n_tok), I32),
            jax.ShapeDtypeStruct((n_tok, 8), F32),
            jax.ShapeDtypeStruct((TOP_K, n_tok), I32),
            jax.ShapeDtypeStruct((N_EXPERTS, 128), F32),
        ],
        scratch_shapes=[
            pltpu.VMEM((D_MODEL, D_MODEL), BF16),
            pltpu.VMEM((N_EXPERTS, 1), F32),
        ],
        compiler_params=pltpu.CompilerParams(
            dimension_semantics=("arbitrary",), vmem_limit_bytes=VMEM_LIMIT),
        name="mix_route",
    )(x2, attn2, conv2, w_out, vec(norm_g), router_w.T, router_b.reshape(N_EXPERTS, 1))


def _tile_plan(counts, n_tiles):
    tiles_per = (counts + EXP_ROWS - 1) // EXP_ROWS
    tile_end = jnp.cumsum(tiles_per)
    tile_begin = tile_end - tiles_per
    n_valid = tile_end[-1]
    tiles = jnp.arange(n_tiles, dtype=I32)
    tile_valid = tiles < n_valid
    capped = jnp.minimum(tiles, n_valid - 1)
    tile_expert = jnp.sum((capped[:, None] >= tile_end[None, :]).astype(I32), axis=1)
    tile_expert = jnp.minimum(tile_expert, N_EXPERTS - 1)
    is_first = (tiles[:, None] == tile_begin[None, :]) & (tiles_per[None, :] > 0)
    tile_first = tile_valid & jnp.any(is_first, axis=1)
    group_start = tile_begin * EXP_ROWS
    experts = jnp.arange(N_EXPERTS, dtype=I32)
    nonempty = tiles_per > 0
    parity = (jnp.cumsum(nonempty.astype(I32)) - 1) % 2
    later = nonempty[None, :] & (experts[None, :] > experts[:, None])
    nxt = jnp.min(jnp.where(later, experts[None, :], N_EXPERTS), axis=1)
    nxt = jnp.where(nxt == N_EXPERTS, -1, nxt)
    hot = (tile_expert[:, None] == experts[None, :]).astype(I32)
    tile_slot = jnp.sum(hot * parity[None, :], axis=1)
    tile_next = jnp.sum(hot * nxt[None, :], axis=1)
    filled = jnp.sum(hot * counts[None, :], axis=1) - (tiles - jnp.sum(hot * tile_begin[None, :], axis=1)) * EXP_ROWS
    filled = jnp.where(tile_valid, jnp.clip(filled, 0, EXP_ROWS), 0)
    tile_rows = (filled + EXP_GRANULE - 1) // EXP_GRANULE * EXP_GRANULE
    flags = (tile_expert, tile_first.astype(I32), tile_rows.astype(I32),
             tile_slot.astype(I32), tile_next.astype(I32))
    return flags, group_start.astype(I32)


def _slot_kernel(gstart_ref, idx_ref, rank_ref, pos_ref):
    idx = idx_ref[...]
    pos = rank_ref[...]
    for e in range(N_EXPERTS):
        pos = pos + jnp.where(idx == e, gstart_ref[e], 0)
    pos_ref[...] = pos


def _slots(group_start, idx, rank):
    full = pl.BlockSpec(idx.shape, lambda i, gs: (0, 0))
    return pl.pallas_call(
        _slot_kernel,
        grid_spec=pltpu.PrefetchScalarGridSpec(
            num_scalar_prefetch=1, grid=(1,), in_specs=[full, full], out_specs=full),
        out_shape=jax.ShapeDtypeStruct(idx.shape, I32),
        name="slots",
    )(group_start, idx, rank)


def _sc_mesh():
    return plsc.VectorSubcoreMesh(core_axis_name="core", subcore_axis_name="subcore")


def _sc_worker():
    info = plsc.get_sparse_core_info()
    wid = lax.axis_index("subcore") * info.num_cores + lax.axis_index("core")
    return wid, info.num_cores * info.num_subcores


def _dispatch(t2, pos_flat, n_slots):
    n_tok, d = t2.shape
    n_workers = V7X_SC_WORKERS
    per_w = n_tok // n_workers
    assert per_w % SC_ROWS == 0

    @functools.partial(
        pl.kernel, mesh=_sc_mesh(),
        out_type=jax.ShapeDtypeStruct((n_slots, d), t2.dtype),
        scratch_types=[pltpu.VMEM((SC_ROWS,), I32), pltpu.VMEM((SC_ROWS, d), t2.dtype)],
        name="dispatch",
    )
    def k(t_hbm, pos_hbm, o_hbm, idx_v, rows_v):
        wid, nw = _sc_worker()
        assert nw == n_workers

        @pl.loop(0, per_w // SC_ROWS)
        def _(c):
            base = pl.multiple_of(wid * per_w + c * SC_ROWS, SC_ROWS)
            pltpu.sync_copy(t_hbm.at[pl.ds(base, SC_ROWS)], rows_v)
            for kk in range(TOP_K):
                pltpu.sync_copy(pos_hbm.at[pl.ds(kk * n_tok + base, SC_ROWS)], idx_v)
                pltpu.sync_copy(rows_v, o_hbm.at[idx_v])

    return k(t2, pos_flat)


SC_LANES = 16


def _invert(pos_flat, n_slots):
    n_pairs = pos_flat.shape[0]
    n_workers = V7X_SC_WORKERS
    per_w = n_pairs // n_workers
    assert per_w % SC_ROWS == 0

    @functools.partial(
        pl.kernel, mesh=_sc_mesh(),
        out_type=jax.ShapeDtypeStruct((n_slots,), I32),
        scratch_types=[pltpu.VMEM((SC_ROWS,), I32), pltpu.VMEM((SC_ROWS,), I32)],
        compiler_params=pltpu.CompilerParams(needs_layout_passes=False),
        name="invert",
    )
    def k(pos_hbm, o_hbm, idx_v, val_v):
        wid, nw = _sc_worker()
        assert nw == n_workers

        @pl.loop(0, per_w // SC_ROWS)
        def _(c):
            base = pl.multiple_of(wid * per_w + c * SC_ROWS, SC_ROWS)
            pltpu.sync_copy(pos_hbm.at[pl.ds(base, SC_ROWS)], idx_v)
            for v in range(SC_ROWS // SC_LANES):
                val_v[pl.ds(v * SC_LANES, SC_LANES)] = (
                    lax.iota(I32, SC_LANES) + (base + v * SC_LANES))
            pltpu.sync_copy(val_v, o_hbm.at[idx_v])

    return k(pos_flat)


def _dispatch_sorted(t2, pair_of_slot, first_slot, n_rows):
    n_tok, d = t2.shape
    assert n_tok & (n_tok - 1) == 0, "token count must be a power of two (pair id -> token by masking)"
    n_workers = V7X_SC_WORKERS
    per_w = n_rows // n_workers
    assert per_w % SC_ROWS == 0

    @functools.partial(
        pl.kernel, mesh=_sc_mesh(),
        out_type=jax.ShapeDtypeStruct((n_rows, d), t2.dtype),
        scratch_types=[pltpu.VMEM((SC_ROWS,), I32), pltpu.VMEM((SC_ROWS, d), t2.dtype)],
        compiler_params=pltpu.CompilerParams(needs_layout_passes=False),
        name="dispatch",
    )
    def k(t_hbm, pair_hbm, o_hbm, idx_v, rows_v):
        wid, nw = _sc_worker()
        assert nw == n_workers

        @pl.loop(0, per_w // SC_ROWS)
        def _(c):
            base = pl.multiple_of(wid * per_w + c * SC_ROWS, SC_ROWS)
            pltpu.sync_copy(pair_hbm.at[pl.ds(first_slot + base, SC_ROWS)], idx_v)
            for v in range(SC_ROWS // SC_LANES):
                lanes = pl.ds(v * SC_LANES, SC_LANES)
                idx_v[lanes] = idx_v[lanes] & (n_tok - 1)
            pltpu.sync_copy(t_hbm.at[idx_v], rows_v)
            pltpu.sync_copy(rows_v, o_hbm.at[pl.ds(base, SC_ROWS)])

    return k(t2, pair_of_slot)


def _gather_back(sorted_out, pos_flat):
    n_rows = pos_flat.shape[0]
    d = sorted_out.shape[1]
    n_workers = V7X_SC_WORKERS
    per_w = n_rows // n_workers
    assert per_w % SC_ROWS == 0

    @functools.partial(
        pl.kernel, mesh=_sc_mesh(),
        out_type=jax.ShapeDtypeStruct((n_rows, d), sorted_out.dtype),
        scratch_types=[pltpu.VMEM((SC_ROWS,), I32), pltpu.VMEM((SC_ROWS, d), sorted_out.dtype)],
        name="gather_back",
    )
    def k(s_hbm, pos_hbm, o_hbm, idx_v, rows_v):
        wid, nw = _sc_worker()
        assert nw == n_workers

        @pl.loop(0, per_w // SC_ROWS)
        def _(c):
            base = pl.multiple_of(wid * per_w + c * SC_ROWS, SC_ROWS)
            pltpu.sync_copy(pos_hbm.at[pl.ds(base, SC_ROWS)], idx_v)
            pltpu.sync_copy(s_hbm.at[idx_v], rows_v)
            pltpu.sync_copy(rows_v, o_hbm.at[pl.ds(base, SC_ROWS)])

    return k(sorted_out, pos_flat)


def _weight_copies(w_hbm, wf32, sem, expert, slot):
    return [pltpu.make_async_copy(w.at[expert], wf32.at[slot, mtx], sem.at[slot])
            for mtx, w in enumerate(w_hbm)]


def _expert_rows(rows, x_ref, bg_ref, bu_ref, bd_ref, o_ref, wbf):
    xs = jnp.concatenate(_unpack_rows(x_ref[0:rows, :]), axis=1).astype(BF16)
    cn = 256
    hids = []
    for c in range(D_MODEL // cn):
        sl = slice(c * cn, (c + 1) * cn)
        g = jnp.dot(xs, wbf[0, :, sl], preferred_element_type=F32) + bg_ref[:, sl]
        u = jnp.dot(xs, wbf[1, :, sl], preferred_element_type=F32) + bu_ref[:, sl]
        g = jnp.minimum(g, SWIGLU_LIMIT)
        u = jnp.clip(u, -SWIGLU_LIMIT, SWIGLU_LIMIT)
        hids.append(((u + 1.0) * (g * jax.nn.sigmoid(SWIGLU_ALPHA * g))).astype(BF16))
    acc = jnp.dot(jnp.concatenate(hids, axis=1), wbf[2], preferred_element_type=F32)
    o_ref[0:rows, :] = _pack_rows(acc + bd_ref[...])
    if rows < EXP_ROWS:
        o_ref[rows:EXP_ROWS, :] = jnp.zeros((EXP_ROWS - rows, HALF), I32)


def _expert_kernel(texp_ref, tfirst_ref, trows_ref, tslot_ref, tnext_ref,
                   x_ref, wg_hbm, bg_ref, wu_hbm, bu_ref, wd_hbm, bd_ref,
                   o_ref, wf32, wbf, wsem):
    w_hbm = (wg_hbm, wu_hbm, wd_hbm)
    for sub in range(EXP_TILES_PER_STEP):
        i = pl.program_id(0) * EXP_TILES_PER_STEP + sub
        x_tile = x_ref.at[sub * EXP_ROWS:(sub + 1) * EXP_ROWS]
        o_tile = o_ref.at[sub * EXP_ROWS:(sub + 1) * EXP_ROWS]
        _expert_tile(i, texp_ref, tfirst_ref, trows_ref, tslot_ref, tnext_ref,
                     x_tile, w_hbm, bg_ref, bu_ref, bd_ref, o_tile, wf32, wbf, wsem)


def _expert_tile(i, texp_ref, tfirst_ref, trows_ref, tslot_ref, tnext_ref,
                 x_ref, w_hbm, bg_ref, bu_ref, bd_ref, o_ref, wf32, wbf, wsem):
    @pl.when(trows_ref[i] == 0)
    def _():
        o_ref[...] = jnp.zeros(o_ref.shape, o_ref.dtype)

    @pl.when(tfirst_ref[i] == 1)
    def _():
        slot = tslot_ref[i]
        expert = texp_ref[i]

        @pl.when(i == 0)
        def _():
            for cp in _weight_copies(w_hbm, wf32, wsem, expert, slot):
                cp.start()

        for cp in _weight_copies(w_hbm, wf32, wsem, expert, slot):
            cp.wait()

        @pl.when(tnext_ref[i] >= 0)
        def _():
            for cp in _weight_copies(w_hbm, wf32, wsem, tnext_ref[i], 1 - slot):
                cp.start()

        for mtx in range(3):
            _cast_rows(wf32.at[slot, mtx], wbf.at[mtx], D_MODEL)

    for rows in range(EXP_GRANULE, EXP_ROWS + 1, EXP_GRANULE):
        @pl.when(trows_ref[i] == rows)
        def _(rows=rows):
            expert = texp_ref[i]
            _expert_rows(rows, x_ref, bg_ref.at[expert], bu_ref.at[expert], bd_ref.at[expert],
                         o_ref, wbf)


def _experts(sorted_t, tile_flags, w_gate, b_gate, w_up, b_up, w_down, b_down, n_tiles):
    row_blk = pl.BlockSpec((EXP_TILES_PER_STEP * EXP_ROWS, HALF), lambda i, *_: (i, 0))
    w_any = pl.BlockSpec(memory_space=pl.ANY)
    b_blk = pl.BlockSpec((N_EXPERTS, 1, D_MODEL), lambda i, *_: (0, 0, 0))
    grid_spec = pltpu.PrefetchScalarGridSpec(
        num_scalar_prefetch=len(tile_flags),
        grid=(n_tiles // EXP_TILES_PER_STEP,),
        in_specs=[row_blk, w_any, b_blk, w_any, b_blk, w_any, b_blk],
        out_specs=row_blk,
        scratch_shapes=[
            pltpu.VMEM((2, 3, D_MODEL, D_MODEL), F32),
            pltpu.VMEM((3, D_MODEL, D_MODEL), BF16),
            pltpu.SemaphoreType.DMA((2,)),
        ],
    )
    b3 = lambda b: b.reshape(N_EXPERTS, 1, D_MODEL)
    return pl.pallas_call(
        _expert_kernel,
        grid_spec=grid_spec,
        out_shape=jax.ShapeDtypeStruct((n_tiles * EXP_ROWS, HALF), I32),
        compiler_params=pltpu.CompilerParams(
            dimension_semantics=("arbitrary",), vmem_limit_bytes=VMEM_LIMIT),
        name="experts",
    )(*tile_flags, sorted_t,
      w_gate, b3(b_gate), w_up, b3(b_up), w_down, b3(b_down))


def _combine_kernel(x1_ref, s0_ref, s1_ref, s2_ref, s3_ref, w_ref, g_ref, o_ref):
    w = w_ref[...]
    y_lo = x1_ref[:, :HALF]
    y_hi = x1_ref[:, HALF:]
    for k, s_ref in enumerate((s0_ref, s1_ref, s2_ref, s3_ref)):
        lo, hi = _unpack_rows(s_ref[...])
        y_lo = y_lo + w[:, k:k + 1] * lo
        y_hi = y_hi + w[:, k:k + 1] * hi
    sq = jnp.sum(y_lo * y_lo, axis=-1, keepdims=True) + jnp.sum(y_hi * y_hi, axis=-1, keepdims=True)
    scale = lax.rsqrt(sq * (1.0 / D_MODEL) + RMS_EPS)
    o_ref[:, :HALF] = (y_lo * scale) * g_ref[:, :HALF]
    o_ref[:, HALF:] = (y_hi * scale) * g_ref[:, HALF:]


def _combine(x1, slabs, wgt, norm_g):
    n_tok = x1.shape[0]
    rows = MIX_ROWS
    ntiles = n_tok // rows
    slab_spec = lambda k: pl.BlockSpec((rows, HALF), lambda i, k=k: (k * ntiles + i, 0))
    return pl.pallas_call(
        _combine_kernel,
        grid=(ntiles,),
        in_specs=[pl.BlockSpec((rows, D_MODEL), lambda i: (i, 0))]
        + [slab_spec(k) for k in range(TOP_K)]
        + [pl.BlockSpec((rows, 8), lambda i: (i, 0)),
           pl.BlockSpec((1, D_MODEL), lambda i: (0, 0))],
        out_specs=pl.BlockSpec((rows, D_MODEL), lambda i: (i, 0)),
        out_shape=jax.ShapeDtypeStruct((n_tok, D_MODEL), F32),
        compiler_params=pltpu.CompilerParams(
            dimension_semantics=("arbitrary",), vmem_limit_bytes=VMEM_LIMIT),
        name="combine",
    )(x1, slabs, slabs, slabs, slabs, wgt, norm_g.reshape(1, D_MODEL))


def kernel(x, norm_mix_g, w_in, conv_dw_w, conv_dw_b, conv_ln_g, conv_ln_b, rel_bias, w_out,
           norm_ffn_g, router_w, router_b, exp_w_gate, exp_b_gate, exp_w_up, exp_b_up,
           exp_w_down, exp_b_down, norm_final_g):
    bsz, seq, _ = x.shape
    n_tok = bsz * seq
    assert norm_mix_g.shape[0] == 1, "single-layer block"
    assert seq % IN_ROWS == 0 and seq % MIX_ROWS == 0 and LEFT_PAD % IN_ROWS == 0
    n_tiles = (TOP_K * n_tok) // EXP_ROWS + N_EXPERTS - 1
    n_tiles = -(-n_tiles // EXP_TILES_PER_STEP) * EXP_TILES_PER_STEP

    q, kpad, vpad, conv = _inproj(x, norm_mix_g[0], w_in[0], conv_dw_w[0], conv_dw_b[0],
                                  conv_ln_g[0], conv_ln_b[0])
    attn = _attention(q, kpad, vpad, _band_bias(rel_bias[0]))
    x1, t, idx, wgt, rank, cnt = _mix_route(
        x.reshape(n_tok, D_MODEL), attn.reshape(n_tok, D_ATTN), conv.reshape(n_tok, D_CONV),
        w_out[0], norm_ffn_g[0], router_w[0], router_b[0])
    tile_flags, group_start = _tile_plan(cnt[:, 0].astype(I32), n_tiles)
    pos_flat = _slots(group_start, idx, rank).reshape(TOP_K * n_tok)
    pair_of_slot = _invert(pos_flat, n_tiles * EXP_ROWS)
    sorted_t = _dispatch_sorted(t, pair_of_slot, 0, n_tiles * EXP_ROWS)
    sorted_out = _experts(sorted_t, tile_flags,
                          exp_w_gate[0], exp_b_gate[0], exp_w_up[0], exp_b_up[0],
                          exp_w_down[0], exp_b_down[0], n_tiles)
    slabs = _gather_back(sorted_out, pos_flat)
    out = _combine(x1, slabs, wgt, norm_final_g)
    return out.reshape(bsz, seq, D_MODEL)
```

```python
import functools

import jax
import jax.numpy as jnp
from jax import lax
from jax.experimental import pallas as pl
from jax.experimental.pallas import tpu as pltpu
from jax.experimental.pallas import tpu_sc as plsc

F32 = jnp.float32
BF16 = jnp.bfloat16
I32 = jnp.int32

D_MODEL = 1024
CHUNK = 64
N_HEADS = 8
HEAD_DIM = 64
D_ATTN = N_HEADS * HEAD_DIM
LEFT_CHUNKS = 8
REL_MAX = 128
REL_MIN = -(CHUNK - 1)
D_CONV = D_MODEL - D_ATTN
CONV_WIDTH = 31
N_EXPERTS = 32
TOP_K = 4
SWIGLU_ALPHA = 1.702
SWIGLU_LIMIT = 7.0
RMS_EPS = 1e-5
LN_EPS = 1e-5

LEFT_PAD = LEFT_CHUNKS * CHUNK
IN_ROWS = 512
PAD_BLOCKS = LEFT_PAD // IN_ROWS
Q_ROWS = 2 * CHUNK
ATT_BLOCKS = 4
BAND_ROWS = Q_ROWS + LEFT_PAD
HEADS_PER_STEP = 4
GROUP_LANES = HEADS_PER_STEP * HEAD_DIM
MIX_ROWS = 1024
HALO_ROWS = 32
EXP_ROWS = 512
EXP_GRANULE = 128
EXP_TILES_PER_STEP = 2
DISPATCH_ROWS = 64
DISPATCH_BUFS = 2
GATHER_ROWS = 32
GATHER_BUFS = 4
NEG_BIG = -1e30
V7X_VMEM_BYTES = 64 * 1024 * 1024
VMEM_LIMIT = V7X_VMEM_BYTES - 8 * 1024 * 1024
V7X_SC_WORKERS = 32


HALF = D_MODEL // 2
HI_MASK = -65536


def _pack_rows(x):
    bits = lax.bitcast_convert_type(x.astype(BF16).astype(F32), I32)
    return lax.shift_right_logical(bits[:, :HALF], 16) | (bits[:, HALF:] & HI_MASK)


def _unpack_rows(w):
    lo = lax.bitcast_convert_type(lax.shift_left(w, 16), F32)
    hi = lax.bitcast_convert_type(w & HI_MASK, F32)
    return lo, hi


def _cast_rows(src_ref, dst_ref, rows, step=128):
    def body(c, carry):
        r = pl.multiple_of(c * step, step)
        dst_ref[pl.ds(r, step), :] = src_ref[pl.ds(r, step), :].astype(dst_ref.dtype)
        return carry
    lax.fori_loop(0, rows // step, body, 0)


def _conv_branch(hw_ref, sh_ref, rows, cw_ref, cb_ref, lg_ref, lb_ref, out_ref):
    off = HALO_ROWS - (CONV_WIDTH - 1)
    shift_rows = HALO_ROWS + rows - 8
    for b in range(1, 8):
        sh_ref[b - 1] = hw_ref[pl.ds(b, shift_rows), :]
    acc = None
    for j in range(CONV_WIDTH):
        a, b = divmod(off + j, 8)
        src = hw_ref if b == 0 else sh_ref.at[b - 1]
        term = src[pl.ds(8 * a, rows), :] * cw_ref[j:j + 1, :]
        acc = term if acc is None else acc + term
    acc = acc + cb_ref[...]
    mu = jnp.mean(acc, axis=-1, keepdims=True)
    d = acc - mu
    var = jnp.mean(d * d, axis=-1, keepdims=True)
    y = d * lax.rsqrt(var + LN_EPS) * lg_ref[...] + lb_ref[...]
    out_ref[...] = (y * jax.nn.sigmoid(y)).astype(out_ref.dtype)


def _inproj_kernel(x_ref, g_ref, w_ref, cw_ref, cb_ref, lg_ref, lb_ref,
                   q_ref, k_ref, v_ref, c_ref, wbf_ref, hw_ref, sh_ref):
    b = pl.program_id(0)
    j = pl.program_id(1)

    @pl.when((b == 0) & (j == 0))
    def _():
        _cast_rows(w_ref, wbf_ref, D_MODEL)

    @pl.when(j < PAD_BLOCKS)
    def _():
        k_ref[...] = jnp.zeros_like(k_ref)
        v_ref[...] = jnp.zeros_like(v_ref)
        hw_ref[IN_ROWS:IN_ROWS + HALO_ROWS, :] = jnp.zeros((HALO_ROWS, D_CONV), F32)

    @pl.when(j >= PAD_BLOCKS)
    def _():
        x = x_ref[...]
        ms = jnp.mean(x * x, axis=-1, keepdims=True)
        hb = ((x * lax.rsqrt(ms + RMS_EPS)) * g_ref[...]).astype(BF16)

        def proj(c0, width):
            return jnp.dot(hb, wbf_ref[:, c0:c0 + width], preferred_element_type=F32)

        a = proj(3 * D_ATTN, D_CONV)
        gate = proj(3 * D_ATTN + D_CONV, D_CONV)
        hw_ref[0:HALO_ROWS, :] = hw_ref[IN_ROWS:IN_ROWS + HALO_ROWS, :]
        hw_ref[HALO_ROWS:HALO_ROWS + IN_ROWS, :] = a * jax.nn.sigmoid(gate)
        _conv_branch(hw_ref, sh_ref, IN_ROWS, cw_ref, cb_ref, lg_ref, lb_ref, c_ref)

        q_ref[...] = (proj(0, D_ATTN) * (HEAD_DIM ** -0.5)).astype(BF16)
        k_ref[...] = proj(D_ATTN, D_ATTN).astype(BF16)
        v_ref[...] = proj(2 * D_ATTN, D_ATTN).astype(BF16)


def _inproj(x, norm_g, w_in, conv_w, conv_b, ln_g, ln_b):
    bsz, seq, _ = x.shape
    nblk = seq // IN_ROWS
    d_cols = w_in.shape[1]
    row_blk = lambda b, j: (b, jnp.maximum(j - PAD_BLOCKS, 0), 0)
    const = lambda b, j: (0, 0)
    vec = lambda v: v.reshape(1, -1)
    return pl.pallas_call(
        _inproj_kernel,
        grid=(bsz, nblk + PAD_BLOCKS),
        in_specs=[
            pl.BlockSpec((None, IN_ROWS, D_MODEL), row_blk),
            pl.BlockSpec((1, D_MODEL), const),
            pl.BlockSpec((D_MODEL, d_cols), const),
            pl.BlockSpec((CONV_WIDTH, D_CONV), const),
            pl.BlockSpec((1, D_CONV), const),
            pl.BlockSpec((1, D_CONV), const),
            pl.BlockSpec((1, D_CONV), const),
        ],
        out_specs=[
            pl.BlockSpec((None, IN_ROWS, D_ATTN), row_blk),
            pl.BlockSpec((None, IN_ROWS, D_ATTN), lambda b, j: (b, j, 0)),
            pl.BlockSpec((None, IN_ROWS, D_ATTN), lambda b, j: (b, j, 0)),
            pl.BlockSpec((None, IN_ROWS, D_CONV), row_blk),
        ],
        out_shape=[
            jax.ShapeDtypeStruct((bsz, seq, D_ATTN), BF16),
            jax.ShapeDtypeStruct((bsz, seq + LEFT_PAD, D_ATTN), BF16),
            jax.ShapeDtypeStruct((bsz, seq + LEFT_PAD, D_ATTN), BF16),
            jax.ShapeDtypeStruct((bsz, seq, D_CONV), BF16),
        ],
        scratch_shapes=[
            pltpu.VMEM((D_MODEL, d_cols), BF16),
            pltpu.VMEM((HALO_ROWS + IN_ROWS, D_CONV), F32),
            pltpu.VMEM((7, HALO_ROWS + IN_ROWS - 8, D_CONV), F32),
        ],
        compiler_params=pltpu.CompilerParams(
            dimension_semantics=("arbitrary", "arbitrary"), vmem_limit_bytes=VMEM_LIMIT),
        name="inproj",
    )(x, vec(norm_g), w_in, conv_w, vec(conv_b), vec(ln_g), vec(ln_b))


def _attn_kernel(q_ref, k_ref, v_ref, bias_ref, o_ref):
    i = pl.program_id(1)
    first_steps = -(-LEFT_PAD // (ATT_BLOCKS * Q_ROWS))

    @pl.when(i < first_steps)
    def _():
        _attn_step(i, True, q_ref, k_ref, v_ref, bias_ref, o_ref)

    @pl.when(i >= first_steps)
    def _():
        _attn_step(i, False, q_ref, k_ref, v_ref, bias_ref, o_ref)


def _attn_step(i, mask_start, q_ref, k_ref, v_ref, bias_ref, o_ref):
    lane = lax.broadcasted_iota(I32, (Q_ROWS, GROUP_LANES), 1) // HEAD_DIM
    col = lax.broadcasted_iota(I32, (HEADS_PER_STEP * Q_ROWS, BAND_ROWS), 1)
    for qb in range(ATT_BLOCKS):
        blk = i * ATT_BLOCKS + qb
        start = pl.multiple_of(blk * Q_ROWS, Q_ROWS)
        qrows = slice(qb * Q_ROWS, (qb + 1) * Q_ROWS)
        key_ok = col >= LEFT_PAD - blk * Q_ROWS
        for g in range(N_HEADS // HEADS_PER_STEP):
            lanes = slice(g * GROUP_LANES, (g + 1) * GROUP_LANES)
            q = q_ref[qrows, lanes]
            qs = jnp.concatenate(
                [jnp.where(lane == h, q, jnp.zeros_like(q)) for h in range(HEADS_PER_STEP)], axis=0)
            kb = k_ref[pl.ds(start, BAND_ROWS), lanes]
            vb = v_ref[pl.ds(start, BAND_ROWS), lanes]
            s = lax.dot_general(qs, kb, (((1,), (1,)), ((), ())), preferred_element_type=F32)
            rows = slice(g * HEADS_PER_STEP * Q_ROWS, (g + 1) * HEADS_PER_STEP * Q_ROWS)
            s = s + bias_ref[rows, :]
            if mask_start:
                s = jnp.where(key_ok, s, NEG_BIG)
            m = jnp.max(s, axis=-1, keepdims=True)
            p = jnp.exp(s - m)
            l = jnp.sum(p, axis=-1, keepdims=True)
            o = jnp.dot(p.astype(BF16), vb, preferred_element_type=F32) / l
            out = o[0:Q_ROWS]
            for h in range(1, HEADS_PER_STEP):
                out = jnp.where(lane == h, o[h * Q_ROWS:(h + 1) * Q_ROWS], out)
            o_ref[qrows, lanes] = out.astype(o_ref.dtype)


def _attention(q, kpad, vpad, bias):
    bsz, seq, _ = q.shape
    step_rows = ATT_BLOCKS * Q_ROWS
    return pl.pallas_call(
        _attn_kernel,
        grid=(bsz, seq // step_rows),
        in_specs=[
            pl.BlockSpec((None, step_rows, D_ATTN), lambda b, i: (b, i, 0)),
            pl.BlockSpec((None, seq + LEFT_PAD, D_ATTN), lambda b, i: (b, 0, 0)),
            pl.BlockSpec((None, seq + LEFT_PAD, D_ATTN), lambda b, i: (b, 0, 0)),
            pl.BlockSpec((N_HEADS * Q_ROWS, BAND_ROWS), lambda b, i: (0, 0)),
        ],
        out_specs=pl.BlockSpec((None, step_rows, D_ATTN), lambda b, i: (b, i, 0)),
        out_shape=jax.ShapeDtypeStruct((bsz, seq, D_ATTN), BF16),
        compiler_params=pltpu.CompilerParams(
            dimension_semantics=("arbitrary", "arbitrary"), vmem_limit_bytes=VMEM_LIMIT),
        name="chunk_attn",
    )(q, kpad, vpad, bias)


def _band_bias(rel_bias):
    n_rel = REL_MAX - REL_MIN + 1
    far = jnp.broadcast_to(rel_bias[:, n_rel - 1:n_rel], (N_HEADS, BAND_ROWS - 1 - REL_MAX))
    near = rel_bias[:, ::-1]
    ahead = jnp.broadcast_to(rel_bias[:, 0:1], (N_HEADS, Q_ROWS - 1 + REL_MIN))
    diag = jnp.concatenate([far, near, ahead], axis=1).astype(F32)
    bias = jnp.stack(
        [diag[:, Q_ROWS - 1 - r:Q_ROWS - 1 - r + BAND_ROWS] for r in range(Q_ROWS)], axis=1)
    r = jnp.arange(Q_ROWS)[:, None]
    m = jnp.arange(BAND_ROWS)[None, :]
    cq = r // CHUNK
    ck = m // CHUNK
    in_band = (ck >= cq) & (ck <= cq + LEFT_CHUNKS)
    bias = jnp.where(in_band[None], bias, NEG_BIG)
    return bias.reshape(N_HEADS * Q_ROWS, BAND_ROWS)


def _split_bf16(v):
    hi = v.astype(BF16)
    lo = (v - hi.astype(F32)).astype(BF16)
    return hi, lo


def _mix_route_kernel(x_ref, a_ref, c_ref, wo_ref, ng_ref, rwt_ref, rb_ref,
                      x1_ref, t_ref, idx_ref, wgt_ref, rank_ref, cnt_ref,
                      wobf_ref, cntacc_ref):
    i = pl.program_id(0)

    @pl.when(i == 0)
    def _():
        _cast_rows(wo_ref, wobf_ref, D_MODEL)
        cntacc_ref[...] = jnp.zeros_like(cntacc_ref)

    mix_in = jnp.concatenate([a_ref[...], c_ref[...]], axis=1)
    x1 = x_ref[...] + jnp.dot(mix_in, wobf_ref[...], preferred_element_type=F32)
    x1_ref[...] = x1
    ms = jnp.mean(x1 * x1, axis=-1, keepdims=True)
    t = (x1 * lax.rsqrt(ms + RMS_EPS)) * ng_ref[...]
    t_ref[...] = _pack_rows(t)

    nt = (((1,), (1,)), ((), ()))
    w_hi, w_lo = _split_bf16(rwt_ref[...])
    t_hi, t_lo = _split_bf16(t)
    logits = (lax.dot_general(w_hi, t_hi, nt, preferred_element_type=F32)
              + lax.dot_general(w_hi, t_lo, nt, preferred_element_type=F32)
              + lax.dot_general(w_lo, t_hi, nt, preferred_element_type=F32)) + rb_ref[...]
    e_iota = lax.broadcasted_iota(I32, (N_EXPERTS, MIX_ROWS), 0)
    vals, idxs, hots = [], [], []
    for _ in range(TOP_K):
        m = jnp.max(logits, axis=0, keepdims=True)
        am = jnp.min(jnp.where(logits == m, e_iota, N_EXPERTS), axis=0, keepdims=True)
        hot = e_iota == am
        vals.append(m)
        idxs.append(am)
        hots.append(hot)
        logits = jnp.where(hot, -jnp.inf, logits)
    exps = [jnp.exp(v - vals[0]) for v in vals]
    den = exps[0] + exps[1] + exps[2] + exps[3]
    wts = [e / den for e in exps]

    hot_f = (hots[0] | hots[1] | hots[2] | hots[3]).astype(F32)
    ra = lax.broadcasted_iota(I32, (MIX_ROWS, MIX_ROWS), 0)
    rc = lax.broadcasted_iota(I32, (MIX_ROWS, MIX_ROWS), 1)
    upper = (ra < rc).astype(BF16)
    prefix = jnp.dot(hot_f.astype(BF16), upper, preferred_element_type=F32)
    base = prefix + cntacc_ref[...]
    ranks = [jnp.sum(jnp.where(h, base, 0.0), axis=0, keepdims=True) for h in hots]
    counts = cntacc_ref[...] + jnp.sum(hot_f, axis=1, keepdims=True)
    cntacc_ref[...] = counts

    idx_ref[...] = jnp.concatenate(idxs, axis=0)
    rank_ref[...] = jnp.concatenate(ranks, axis=0).astype(I32)
    cnt_ref[...] = jnp.broadcast_to(counts, cnt_ref.shape)
    w8 = jnp.concatenate(wts + [jnp.zeros((8 - TOP_K, MIX_ROWS), F32)], axis=0)
    wgt_ref[...] = w8.T


def _mix_route(x2, attn2, conv2, w_out, norm_g, router_w, router_b):
    n_tok = x2.shape[0]
    row = lambda i: (i, 0)
    const = lambda i: (0, 0)
    vec = lambda v: v.reshape(1, -1)
    return pl.pallas_call(
        _mix_route_kernel,
        grid=(n_tok // MIX_ROWS,),
        in_specs=[
            pl.BlockSpec((MIX_ROWS, D_MODEL), row),
            pl.BlockSpec((MIX_ROWS, D_ATTN), row),
            pl.BlockSpec((MIX_ROWS, D_CONV), row),
            pl.BlockSpec((D_MODEL, D_MODEL), const),
            pl.BlockSpec((1, D_MODEL), const),
            pl.BlockSpec((N_EXPERTS, D_MODEL), const),
            pl.BlockSpec((N_EXPERTS, 1), const),
        ],
        out_specs=[
            pl.BlockSpec((MIX_ROWS, D_MODEL), row),
            pl.BlockSpec((MIX_ROWS, HALF), row),
            pl.BlockSpec((TOP_K, MIX_ROWS), lambda i: (0, i)),
            pl.BlockSpec((MIX_ROWS, 8), row),
            pl.BlockSpec((TOP_K, MIX_ROWS), lambda i: (0, i)),
            pl.BlockSpec((N_EXPERTS, 128), const),
        ],
        out_shape=[
            jax.ShapeDtypeStruct((n_tok, D_MODEL), F32),
            jax.ShapeDtypeStruct((n_tok, HALF), I32),
            jax.ShapeDtypeStruct((TOP_K, n_tok), I32),
            jax.ShapeDtypeStruct((n_tok, 8), F32),
            jax.ShapeDtypeStruct((TOP_K, n_tok), I32),
            jax.ShapeDtypeStruct((N_EXPERTS, 128), F32),
        ],
        scratch_shapes=[
            pltpu.VMEM((D_MODEL, D_MODEL), BF16),
            pltpu.VMEM((N_EXPERTS, 1), F32),
        ],
        compiler_params=pltpu.CompilerParams(
            dimension_semantics=("arbitrary",), vmem_limit_bytes=VMEM_LIMIT),
        name="mix_route",
    )(x2, attn2, conv2, w_out, vec(norm_g), router_w.T, router_b.reshape(N_EXPERTS, 1))


def _tile_plan(counts, n_tiles):
    tiles_per = (counts + EXP_ROWS - 1) // EXP_ROWS
    tile_end = jnp.cumsum(tiles_per)
    tile_begin = tile_end - tiles_per
    n_valid = tile_end[-1]
    tiles = jnp.arange(n_tiles, dtype=I32)
    tile_valid = tiles < n_valid
    capped = jnp.minimum(tiles, n_valid - 1)
    tile_expert = jnp.sum((capped[:, None] >= tile_end[None, :]).astype(I32), axis=1)
    tile_expert = jnp.minimum(tile_expert, N_EXPERTS - 1)
    is_first = (tiles[:, None] == tile_begin[None, :]) & (tiles_per[None, :] > 0)
    tile_first = tile_valid & jnp.any(is_first, axis=1)
    group_start = tile_begin * EXP_ROWS
    experts = jnp.arange(N_EXPERTS, dtype=I32)
    nonempty = tiles_per > 0
    parity = (jnp.cumsum(nonempty.astype(I32)) - 1) % 2
    later = nonempty[None, :] & (experts[None, :] > experts[:, None])
    nxt = jnp.min(jnp.where(later, experts[None, :], N_EXPERTS), axis=1)
    nxt = jnp.where(nxt == N_EXPERTS, -1, nxt)
    hot = (tile_expert[:, None] == experts[None, :]).astype(I32)
    tile_slot = jnp.sum(hot * parity[None, :], axis=1)
    tile_next = jnp.sum(hot * nxt[None, :], axis=1)
    filled = jnp.sum(hot * counts[None, :], axis=1) - (tiles - jnp.sum(hot * tile_begin[None, :], axis=1)) * EXP_ROWS
    filled = jnp.where(tile_valid, jnp.clip(filled, 0, EXP_ROWS), 0)
    tile_rows = (filled + EXP_GRANULE - 1) // EXP_GRANULE * EXP_GRANULE
    flags = (tile_expert, tile_first.astype(I32), tile_rows.astype(I32),
             tile_slot.astype(I32), tile_next.astype(I32))
    return flags, group_start.astype(I32)


def _slot_kernel(gstart_ref, idx_ref, rank_ref, pos_ref):
    idx = idx_ref[...]
    pos = rank_ref[...]
    for e in range(N_EXPERTS):
        pos = pos + jnp.where(idx == e, gstart_ref[e], 0)
    pos_ref[...] = pos


def _slots(group_start, idx, rank):
    full = pl.BlockSpec(idx.shape, lambda i, gs: (0, 0))
    return pl.pallas_call(
        _slot_kernel,
        grid_spec=pltpu.PrefetchScalarGridSpec(
            num_scalar_prefetch=1, grid=(1,), in_specs=[full, full], out_specs=full),
        out_shape=jax.ShapeDtypeStruct(idx.shape, I32),
        name="slots",
    )(group_start, idx, rank)


def _sc_mesh():
    return plsc.VectorSubcoreMesh(core_axis_name="core", subcore_axis_name="subcore")


def _sc_worker():
    info = plsc.get_sparse_core_info()
    wid = lax.axis_index("subcore") * info.num_cores + lax.axis_index("core")
    return wid, info.num_cores * info.num_subcores


def _dispatch(t2, pos_flat, n_slots):
    n_tok, d = t2.shape
    n_workers = V7X_SC_WORKERS
    per_w = n_tok // n_workers
    group = DISPATCH_BUFS * DISPATCH_ROWS
    assert per_w % group == 0

    @functools.partial(
        pl.kernel, mesh=_sc_mesh(),
        out_type=jax.ShapeDtypeStruct((n_slots, d), t2.dtype),
        scratch_types=(
            [pltpu.VMEM((DISPATCH_ROWS,), I32) for _ in range(DISPATCH_BUFS * TOP_K)]
            + [pltpu.VMEM((DISPATCH_ROWS, d), t2.dtype) for _ in range(DISPATCH_BUFS)]
            + [pltpu.SemaphoreType.DMA for _ in range(DISPATCH_BUFS * (TOP_K + 1))]),
        name="dispatch",
    )
    def k(t_hbm, pos_hbm, o_hbm, *scratch):
        n_idx = DISPATCH_BUFS * TOP_K
        idx_v = scratch[:n_idx]
        rows_v = scratch[n_idx:n_idx + DISPATCH_BUFS]
        rsem = scratch[n_idx + DISPATCH_BUFS:n_idx + 2 * DISPATCH_BUFS]
        ssem = scratch[n_idx + 2 * DISPATCH_BUFS:]
        wid, nw = _sc_worker()
        assert nw == n_workers

        @pl.loop(0, per_w // group)
        def _(c):
            base = pl.multiple_of(wid * per_w + c * group, group)
            reads, scatters = [], []
            for b in range(DISPATCH_BUFS):
                rows = pl.ds(base + b * DISPATCH_ROWS, DISPATCH_ROWS)
                reads.append(pltpu.async_copy(t_hbm.at[rows], rows_v[b], rsem[b]))
            for b in range(DISPATCH_BUFS):
                for kk in range(TOP_K):
                    pltpu.sync_copy(
                        pos_hbm.at[pl.ds(kk * n_tok + base + b * DISPATCH_ROWS, DISPATCH_ROWS)],
                        idx_v[b * TOP_K + kk])
            for b in range(DISPATCH_BUFS):
                reads[b].wait()
                for kk in range(TOP_K):
                    i = b * TOP_K + kk
                    scatters.append(pltpu.async_copy(rows_v[b], o_hbm.at[idx_v[i]], ssem[i]))
            for cp in scatters:
                cp.wait()

    return k(t2, pos_flat)


def _gather_back(sorted_out, pos_flat):
    n_rows = pos_flat.shape[0]
    d = sorted_out.shape[1]
    n_workers = V7X_SC_WORKERS
    per_w = n_rows // n_workers
    group = GATHER_BUFS * GATHER_ROWS
    assert per_w % group == 0

    @functools.partial(
        pl.kernel, mesh=_sc_mesh(),
        out_type=jax.ShapeDtypeStruct((n_rows, d), sorted_out.dtype),
        scratch_types=(
            [pltpu.VMEM((GATHER_ROWS,), I32) for _ in range(GATHER_BUFS)]
            + [pltpu.VMEM((GATHER_ROWS, d), sorted_out.dtype) for _ in range(GATHER_BUFS)]
            + [pltpu.SemaphoreType.DMA for _ in range(2 * GATHER_BUFS)]),
        name="gather_back",
    )
    def k(s_hbm, pos_hbm, o_hbm, *scratch):
        idx_v = scratch[:GATHER_BUFS]
        rows_v = scratch[GATHER_BUFS:2 * GATHER_BUFS]
        gsem = scratch[2 * GATHER_BUFS:3 * GATHER_BUFS]
        wsem = scratch[3 * GATHER_BUFS:]
        wid, nw = _sc_worker()
        assert nw == n_workers

        @pl.loop(0, per_w // group)
        def _(c):
            base = pl.multiple_of(wid * per_w + c * group, group)
            gathers, writes = [], []
            for b in range(GATHER_BUFS):
                rows = pl.ds(base + b * GATHER_ROWS, GATHER_ROWS)
                pltpu.sync_copy(pos_hbm.at[rows], idx_v[b])
                gathers.append(pltpu.async_copy(s_hbm.at[idx_v[b]], rows_v[b], gsem[b]))
            for b in range(GATHER_BUFS):
                rows = pl.ds(base + b * GATHER_ROWS, GATHER_ROWS)
                gathers[b].wait()
                writes.append(pltpu.async_copy(rows_v[b], o_hbm.at[rows], wsem[b]))
            for b in range(GATHER_BUFS):
                writes[b].wait()

    return k(sorted_out, pos_flat)


def _weight_copies(w_hbm, wf32, sem, expert, slot):
    return [pltpu.make_async_copy(w.at[expert], wf32.at[slot, mtx], sem.at[slot])
            for mtx, w in enumerate(w_hbm)]


def _expert_rows(rows, x_ref, bg_ref, bu_ref, bd_ref, o_ref, wbf):
    xs = jnp.concatenate(_unpack_rows(x_ref[0:rows, :]), axis=1).astype(BF16)
    cn = 256
    hids = []
    for c in range(D_MODEL // cn):
        sl = slice(c * cn, (c + 1) * cn)
        g = jnp.dot(xs, wbf[0, :, sl], preferred_element_type=F32) + bg_ref[:, sl]
        u = jnp.dot(xs, wbf[1, :, sl], preferred_element_type=F32) + bu_ref[:, sl]
        g = jnp.minimum(g, SWIGLU_LIMIT)
        u = jnp.clip(u, -SWIGLU_LIMIT, SWIGLU_LIMIT)
        hids.append(((u + 1.0) * (g * jax.nn.sigmoid(SWIGLU_ALPHA * g))).astype(BF16))
    acc = jnp.dot(jnp.concatenate(hids, axis=1), wbf[2], preferred_element_type=F32)
    o_ref[0:rows, :] = _pack_rows(acc + bd_ref[...])
    if rows < EXP_ROWS:
        o_ref[rows:EXP_ROWS, :] = jnp.zeros((EXP_ROWS - rows, HALF), I32)


def _expert_kernel(texp_ref, tfirst_ref, trows_ref, tslot_ref, tnext_ref,
                   x_ref, wg_hbm, bg_ref, wu_hbm, bu_ref, wd_hbm, bd_ref,
                   o_ref, wf32, wbf, wsem):
    w_hbm = (wg_hbm, wu_hbm, wd_hbm)
    for sub in range(EXP_TILES_PER_STEP):
        i = pl.program_id(0) * EXP_TILES_PER_STEP + sub
        x_tile = x_ref.at[sub * EXP_ROWS:(sub + 1) * EXP_ROWS]
        o_tile = o_ref.at[sub * EXP_ROWS:(sub + 1) * EXP_ROWS]
        _expert_tile(i, texp_ref, tfirst_ref, trows_ref, tslot_ref, tnext_ref,
                     x_tile, w_hbm, bg_ref, bu_ref, bd_ref, o_tile, wf32, wbf, wsem)


def _expert_tile(i, texp_ref, tfirst_ref, trows_ref, tslot_ref, tnext_ref,
                 x_ref, w_hbm, bg_ref, bu_ref, bd_ref, o_ref, wf32, wbf, wsem):
    @pl.when(trows_ref[i] == 0)
    def _():
        o_ref[...] = jnp.zeros(o_ref.shape, o_ref.dtype)

    @pl.when(tfirst_ref[i] == 1)
    def _():
        slot = tslot_ref[i]
        expert = texp_ref[i]

        @pl.when(i == 0)
        def _():
            for cp in _weight_copies(w_hbm, wf32, wsem, expert, slot):
                cp.start()

        for cp in _weight_copies(w_hbm, wf32, wsem, expert, slot):
            cp.wait()

        @pl.when(tnext_ref[i] >= 0)
        def _():
            for cp in _weight_copies(w_hbm, wf32, wsem, tnext_ref[i], 1 - slot):
                cp.start()

        for mtx in range(3):
            _cast_rows(wf32.at[slot, mtx], wbf.at[mtx], D_MODEL)

    for rows in range(EXP_GRANULE, EXP_ROWS + 1, EXP_GRANULE):
        @pl.when(trows_ref[i] == rows)
        def _(rows=rows):
            expert = texp_ref[i]
            _expert_rows(rows, x_ref, bg_ref.at[expert], bu_ref.at[expert], bd_ref.at[expert],
                         o_ref, wbf)


def _experts(sorted_t, tile_flags, w_gate, b_gate, w_up, b_up, w_down, b_down, n_tiles):
    row_blk = pl.BlockSpec((EXP_TILES_PER_STEP * EXP_ROWS, HALF), lambda i, *_: (i, 0))
    w_any = pl.BlockSpec(memory_space=pl.ANY)
    b_blk = pl.BlockSpec((N_EXPERTS, 1, D_MODEL), lambda i, *_: (0, 0, 0))
    grid_spec = pltpu.PrefetchScalarGridSpec(
        num_scalar_prefetch=len(tile_flags),
        grid=(n_tiles // EXP_TILES_PER_STEP,),
        in_specs=[row_blk, w_any, b_blk, w_any, b_blk, w_any, b_blk],
        out_specs=row_blk,
        scratch_shapes=[
            pltpu.VMEM((2, 3, D_MODEL, D_MODEL), F32),
            pltpu.VMEM((3, D_MODEL, D_MODEL), BF16),
            pltpu.SemaphoreType.DMA((2,)),
        ],
    )
    b3 = lambda b: b.reshape(N_EXPERTS, 1, D_MODEL)
    return pl.pallas_call(
        _expert_kernel,
        grid_spec=grid_spec,
        out_shape=jax.ShapeDtypeStruct((n_tiles * EXP_ROWS, HALF), I32),
        compiler_params=pltpu.CompilerParams(
            dimension_semantics=("arbitrary",), vmem_limit_bytes=VMEM_LIMIT),
        name="experts",
    )(*tile_flags, sorted_t,
      w_gate, b3(b_gate), w_up, b3(b_up), w_down, b3(b_down))


def _combine_kernel(x1_ref, s0_ref, s1_ref, s2_ref, s3_ref, w_ref, g_ref, o_ref):
    w = w_ref[...]
    y_lo = x1_ref[:, :HALF]
    y_hi = x1_ref[:, HALF:]
    for k, s_ref in enumerate((s0_ref, s1_ref, s2_ref, s3_ref)):
        lo, hi = _unpack_rows(s_ref[...])
        y_lo = y_lo + w[:, k:k + 1] * lo
        y_hi = y_hi + w[:, k:k + 1] * hi
    sq = jnp.sum(y_lo * y_lo, axis=-1, keepdims=True) + jnp.sum(y_hi * y_hi, axis=-1, keepdims=True)
    scale = lax.rsqrt(sq * (1.0 / D_MODEL) + RMS_EPS)
    o_ref[:, :HALF] = (y_lo * scale) * g_ref[:, :HALF]
    o_ref[:, HALF:] = (y_hi * scale) * g_ref[:, HALF:]


def _combine(x1, slabs, wgt, norm_g):
    n_tok = x1.shape[0]
    rows = MIX_ROWS
    ntiles = n_tok // rows
    slab_spec = lambda k: pl.BlockSpec((rows, HALF), lambda i, k=k: (k * ntiles + i, 0))
    return pl.pallas_call(
        _combine_kernel,
        grid=(ntiles,),
        in_specs=[pl.BlockSpec((rows, D_MODEL), lambda i: (i, 0))]
        + [slab_spec(k) for k in range(TOP_K)]
        + [pl.BlockSpec((rows, 8), lambda i: (i, 0)),
           pl.BlockSpec((1, D_MODEL), lambda i: (0, 0))],
        out_specs=pl.BlockSpec((rows, D_MODEL), lambda i: (i, 0)),
        out_shape=jax.ShapeDtypeStruct((n_tok, D_MODEL), F32),
        compiler_params=pltpu.CompilerParams(
            dimension_semantics=("arbitrary",), vmem_limit_bytes=VMEM_LIMIT),
        name="combine",
    )(x1, slabs, slabs, slabs, slabs, wgt, norm_g.reshape(1, D_MODEL))


def kernel(x, norm_mix_g, w_in, conv_dw_w, conv_dw_b, conv_ln_g, conv_ln_b, rel_bias, w_out,
           norm_ffn_g, router_w, router_b, exp_w_gate, exp_b_gate, exp_w_up, exp_b_up,
           exp_w_down, exp_b_down, norm_final_g):
    bsz, seq, _ = x.shape
    n_tok = bsz * seq
    assert norm_mix_g.shape[0] == 1, "single-layer block"
    assert seq % IN_ROWS == 0 and seq % MIX_ROWS == 0 and LEFT_PAD % IN_ROWS == 0
    n_tiles = (TOP_K * n_tok) // EXP_ROWS + N_EXPERTS - 1
    n_tiles = -(-n_tiles // EXP_TILES_PER_STEP) * EXP_TILES_PER_STEP

    q, kpad, vpad, conv = _inproj(x, norm_mix_g[0], w_in[0], conv_dw_w[0], conv_dw_b[0],
                                  conv_ln_g[0], conv_ln_b[0])
    attn = _attention(q, kpad, vpad, _band_bias(rel_bias[0]))
    x1, t, idx, wgt, rank, cnt = _mix_route(
        x.reshape(n_tok, D_MODEL), attn.reshape(n_tok, D_ATTN), conv.reshape(n_tok, D_CONV),
        w_out[0], norm_ffn_g[0], router_w[0], router_b[0])
    tile_flags, group_start = _tile_plan(cnt[:, 0].astype(I32), n_tiles)
    pos_flat = _slots(group_start, idx, rank).reshape(TOP_K * n_tok)
    sorted_t = _dispatch(t, pos_flat, n_tiles * EXP_ROWS)
    sorted_out = _experts(sorted_t, tile_flags,
                          exp_w_gate[0], exp_b_gate[0], exp_w_up[0], exp_b_up[0],
                          exp_w_down[0], exp_b_down[0], n_tiles)
    slabs = _gather_back(sorted_out, pos_flat)
    out = _combine(x1, slabs, wgt, norm_final_g)
    return out.reshape(bsz, seq, D_MODEL)
```

```python
import functools

import jax
import jax.numpy as jnp
from jax import lax
from jax.experimental import pallas as pl
from jax.experimental.pallas import tpu as pltpu
from jax.experimental.pallas import tpu_sc as plsc

F32 = jnp.float32
BF16 = jnp.bfloat16
I32 = jnp.int32

D_MODEL = 1024
CHUNK = 64
N_HEADS = 8
HEAD_DIM = 64
D_ATTN = N_HEADS * HEAD_DIM
LEFT_CHUNKS = 8
REL_MAX = 128
REL_MIN = -(CHUNK - 1)
D_CONV = D_MODEL - D_ATTN
CONV_WIDTH = 31
N_EXPERTS = 32
TOP_K = 4
SWIGLU_ALPHA = 1.702
SWIGLU_LIMIT = 7.0
RMS_EPS = 1e-5
LN_EPS = 1e-5

LEFT_PAD = LEFT_CHUNKS * CHUNK
IN_ROWS = 512
PAD_BLOCKS = LEFT_PAD // IN_ROWS
Q_ROWS = 2 * CHUNK
ATT_BLOCKS = 4
BAND_ROWS = Q_ROWS + LEFT_PAD
HEADS_PER_STEP = 4
GROUP_LANES = HEADS_PER_STEP * HEAD_DIM
MIX_ROWS = 1024
HALO_ROWS = 32
EXP_ROWS = 512
EXP_GRANULE = 64
EXP_TILES_PER_STEP = 2
DISPATCH_ROWS = 64
DISPATCH_BUFS = 2
GATHER_ROWS = 32
GATHER_BUFS = 4
NEG_BIG = -1e30
V7X_VMEM_BYTES = 64 * 1024 * 1024
VMEM_LIMIT = V7X_VMEM_BYTES - 8 * 1024 * 1024
V7X_SC_WORKERS = 32


HALF = D_MODEL // 2
HI_MASK = -65536


def _pack_rows(x):
    bits = lax.bitcast_convert_type(x.astype(BF16).astype(F32), I32)
    return lax.shift_right_logical(bits[:, :HALF], 16) | (bits[:, HALF:] & HI_MASK)


def _unpack_rows(w):
    lo = lax.bitcast_convert_type(lax.shift_left(w, 16), F32)
    hi = lax.bitcast_convert_type(w & HI_MASK, F32)
    return lo, hi


def _cast_rows(src_ref, dst_ref, rows, step=128):
    def body(c, carry):
        r = pl.multiple_of(c * step, step)
        dst_ref[pl.ds(r, step), :] = src_ref[pl.ds(r, step), :].astype(dst_ref.dtype)
        return carry
    lax.fori_loop(0, rows // step, body, 0)


def _conv_branch(hw_ref, sh_ref, rows, cw_ref, cb_ref, lg_ref, lb_ref, out_ref):
    off = HALO_ROWS - (CONV_WIDTH - 1)
    shift_rows = HALO_ROWS + rows - 8
    for b in range(1, 8):
        sh_ref[b - 1] = hw_ref[pl.ds(b, shift_rows), :]
    acc = None
    for j in range(CONV_WIDTH):
        a, b = divmod(off + j, 8)
        src = hw_ref if b == 0 else sh_ref.at[b - 1]
        term = src[pl.ds(8 * a, rows), :] * cw_ref[j:j + 1, :]
        acc = term if acc is None else acc + term
    acc = acc + cb_ref[...]
    mu = jnp.mean(acc, axis=-1, keepdims=True)
    d = acc - mu
    var = jnp.mean(d * d, axis=-1, keepdims=True)
    y = d * lax.rsqrt(var + LN_EPS) * lg_ref[...] + lb_ref[...]
    out_ref[...] = (y * jax.nn.sigmoid(y)).astype(out_ref.dtype)


def _inproj_kernel(x_ref, g_ref, w_ref, cw_ref, cb_ref, lg_ref, lb_ref,
                   q_ref, k_ref, v_ref, c_ref, wbf_ref, hw_ref, sh_ref):
    b = pl.program_id(0)
    j = pl.program_id(1)

    @pl.when((b == 0) & (j == 0))
    def _():
        _cast_rows(w_ref, wbf_ref, D_MODEL)

    @pl.when(j < PAD_BLOCKS)
    def _():
        k_ref[...] = jnp.zeros_like(k_ref)
        v_ref[...] = jnp.zeros_like(v_ref)
        hw_ref[IN_ROWS:IN_ROWS + HALO_ROWS, :] = jnp.zeros((HALO_ROWS, D_CONV), F32)

    @pl.when(j >= PAD_BLOCKS)
    def _():
        x = x_ref[...]
        ms = jnp.mean(x * x, axis=-1, keepdims=True)
        hb = ((x * lax.rsqrt(ms + RMS_EPS)) * g_ref[...]).astype(BF16)

        def proj(c0, width):
            return jnp.dot(hb, wbf_ref[:, c0:c0 + width], preferred_element_type=F32)

        a = proj(3 * D_ATTN, D_CONV)
        gate = proj(3 * D_ATTN + D_CONV, D_CONV)
        hw_ref[0:HALO_ROWS, :] = hw_ref[IN_ROWS:IN_ROWS + HALO_ROWS, :]
        hw_ref[HALO_ROWS:HALO_ROWS + IN_ROWS, :] = a * jax.nn.sigmoid(gate)
        _conv_branch(hw_ref, sh_ref, IN_ROWS, cw_ref, cb_ref, lg_ref, lb_ref, c_ref)

        q_ref[...] = (proj(0, D_ATTN) * (HEAD_DIM ** -0.5)).astype(BF16)
        k_ref[...] = proj(D_ATTN, D_ATTN).astype(BF16)
        v_ref[...] = proj(2 * D_ATTN, D_ATTN).astype(BF16)


def _inproj(x, norm_g, w_in, conv_w, conv_b, ln_g, ln_b):
    bsz, seq, _ = x.shape
    nblk = seq // IN_ROWS
    d_cols = w_in.shape[1]
    row_blk = lambda b, j: (b, jnp.maximum(j - PAD_BLOCKS, 0), 0)
    const = lambda b, j: (0, 0)
    vec = lambda v: v.reshape(1, -1)
    return pl.pallas_call(
        _inproj_kernel,
        grid=(bsz, nblk + PAD_BLOCKS),
        in_specs=[
            pl.BlockSpec((None, IN_ROWS, D_MODEL), row_blk),
            pl.BlockSpec((1, D_MODEL), const),
            pl.BlockSpec((D_MODEL, d_cols), const),
            pl.BlockSpec((CONV_WIDTH, D_CONV), const),
            pl.BlockSpec((1, D_CONV), const),
            pl.BlockSpec((1, D_CONV), const),
            pl.BlockSpec((1, D_CONV), const),
        ],
        out_specs=[
            pl.BlockSpec((None, IN_ROWS, D_ATTN), row_blk),
            pl.BlockSpec((None, IN_ROWS, D_ATTN), lambda b, j: (b, j, 0)),
            pl.BlockSpec((None, IN_ROWS, D_ATTN), lambda b, j: (b, j, 0)),
            pl.BlockSpec((None, IN_ROWS, D_CONV), row_blk),
        ],
        out_shape=[
            jax.ShapeDtypeStruct((bsz, seq, D_ATTN), BF16),
            jax.ShapeDtypeStruct((bsz, seq + LEFT_PAD, D_ATTN), BF16),
            jax.ShapeDtypeStruct((bsz, seq + LEFT_PAD, D_ATTN), BF16),
            jax.ShapeDtypeStruct((bsz, seq, D_CONV), BF16),
        ],
        scratch_shapes=[
            pltpu.VMEM((D_MODEL, d_cols), BF16),
            pltpu.VMEM((HALO_ROWS + IN_ROWS, D_CONV), F32),
            pltpu.VMEM((7, HALO_ROWS + IN_ROWS - 8, D_CONV), F32),
        ],
        compiler_params=pltpu.CompilerParams(
            dimension_semantics=("arbitrary", "arbitrary"), vmem_limit_bytes=VMEM_LIMIT),
        name="inproj",
    )(x, vec(norm_g), w_in, conv_w, vec(conv_b), vec(ln_g), vec(ln_b))


def _attn_kernel(q_ref, k_ref, v_ref, bias_ref, o_ref):
    i = pl.program_id(1)
    first_steps = -(-LEFT_PAD // (ATT_BLOCKS * Q_ROWS))

    @pl.when(i < first_steps)
    def _():
        _attn_step(i, True, q_ref, k_ref, v_ref, bias_ref, o_ref)

    @pl.when(i >= first_steps)
    def _():
        _attn_step(i, False, q_ref, k_ref, v_ref, bias_ref, o_ref)


def _attn_step(i, mask_start, q_ref, k_ref, v_ref, bias_ref, o_ref):
    lane = lax.broadcasted_iota(I32, (Q_ROWS, GROUP_LANES), 1) // HEAD_DIM
    col = lax.broadcasted_iota(I32, (HEADS_PER_STEP * Q_ROWS, BAND_ROWS), 1)
    for qb in range(ATT_BLOCKS):
        blk = i * ATT_BLOCKS + qb
        start = pl.multiple_of(blk * Q_ROWS, Q_ROWS)
        qrows = slice(qb * Q_ROWS, (qb + 1) * Q_ROWS)
        key_ok = col >= LEFT_PAD - blk * Q_ROWS
        for g in range(N_HEADS // HEADS_PER_STEP):
            lanes = slice(g * GROUP_LANES, (g + 1) * GROUP_LANES)
            q = q_ref[qrows, lanes]
            qs = jnp.concatenate(
                [jnp.where(lane == h, q, jnp.zeros_like(q)) for h in range(HEADS_PER_STEP)], axis=0)
            kb = k_ref[pl.ds(start, BAND_ROWS), lanes]
            vb = v_ref[pl.ds(start, BAND_ROWS), lanes]
            s = lax.dot_general(qs, kb, (((1,), (1,)), ((), ())), preferred_element_type=F32)
            rows = slice(g * HEADS_PER_STEP * Q_ROWS, (g + 1) * HEADS_PER_STEP * Q_ROWS)
            s = s + bias_ref[rows, :]
            if mask_start:
                s = jnp.where(key_ok, s, NEG_BIG)
            m = jnp.max(s, axis=-1, keepdims=True)
            p = jnp.exp(s - m)
            l = jnp.sum(p, axis=-1, keepdims=True)
            o = jnp.dot(p.astype(BF16), vb, preferred_element_type=F32) / l
            out = o[0:Q_ROWS]
            for h in range(1, HEADS_PER_STEP):
                out = jnp.where(lane == h, o[h * Q_ROWS:(h + 1) * Q_ROWS], out)
            o_ref[qrows, lanes] = out.astype(o_ref.dtype)


def _attention(q, kpad, vpad, bias):
    bsz, seq, _ = q.shape
    step_rows = ATT_BLOCKS * Q_ROWS
    return pl.pallas_call(
        _attn_kernel,
        grid=(bsz, seq // step_rows),
        in_specs=[
            pl.BlockSpec((None, step_rows, D_ATTN), lambda b, i: (b, i, 0)),
            pl.BlockSpec((None, seq + LEFT_PAD, D_ATTN), lambda b, i: (b, 0, 0)),
            pl.BlockSpec((None, seq + LEFT_PAD, D_ATTN), lambda b, i: (b, 0, 0)),
            pl.BlockSpec((N_HEADS * Q_ROWS, BAND_ROWS), lambda b, i: (0, 0)),
        ],
        out_specs=pl.BlockSpec((None, step_rows, D_ATTN), lambda b, i: (b, i, 0)),
        out_shape=jax.ShapeDtypeStruct((bsz, seq, D_ATTN), BF16),
        compiler_params=pltpu.CompilerParams(
            dimension_semantics=("arbitrary", "arbitrary"), vmem_limit_bytes=VMEM_LIMIT),
        name="chunk_attn",
    )(q, kpad, vpad, bias)


def _band_bias(rel_bias):
    n_rel = REL_MAX - REL_MIN + 1
    far = jnp.broadcast_to(rel_bias[:, n_rel - 1:n_rel], (N_HEADS, BAND_ROWS - 1 - REL_MAX))
    near = rel_bias[:, ::-1]
    ahead = jnp.broadcast_to(rel_bias[:, 0:1], (N_HEADS, Q_ROWS - 1 + REL_MIN))
    diag = jnp.concatenate([far, near, ahead], axis=1).astype(F32)
    bias = jnp.stack(
        [diag[:, Q_ROWS - 1 - r:Q_ROWS - 1 - r + BAND_ROWS] for r in range(Q_ROWS)], axis=1)
    r = jnp.arange(Q_ROWS)[:, None]
    m = jnp.arange(BAND_ROWS)[None, :]
    cq = r // CHUNK
    ck = m // CHUNK
    in_band = (ck >= cq) & (ck <= cq + LEFT_CHUNKS)
    bias = jnp.where(in_band[None], bias, NEG_BIG)
    return bias.reshape(N_HEADS * Q_ROWS, BAND_ROWS)


def _split_bf16(v):
    hi = v.astype(BF16)
    lo = (v - hi.astype(F32)).astype(BF16)
    return hi, lo


def _mix_route_kernel(x_ref, a_ref, c_ref, wo_ref, ng_ref, rwt_ref, rb_ref,
                      x1_ref, t_ref, idx_ref, wgt_ref, rank_ref, cnt_ref,
                      wobf_ref, cntacc_ref):
    i = pl.program_id(0)

    @pl.when(i == 0)
    def _():
        _cast_rows(wo_ref, wobf_ref, D_MODEL)
        cntacc_ref[...] = jnp.zeros_like(cntacc_ref)

    mix_in = jnp.concatenate([a_ref[...], c_ref[...]], axis=1)
    x1 = x_ref[...] + jnp.dot(mix_in, wobf_ref[...], preferred_element_type=F32)
    x1_ref[...] = x1
    ms = jnp.mean(x1 * x1, axis=-1, keepdims=True)
    t = (x1 * lax.rsqrt(ms + RMS_EPS)) * ng_ref[...]
    t_ref[...] = _pack_rows(t)

    nt = (((1,), (1,)), ((), ()))
    w_hi, w_lo = _split_bf16(rwt_ref[...])
    t_hi, t_lo = _split_bf16(t)
    both = lax.dot_general(jnp.concatenate([w_hi, w_lo], axis=0), t_hi, nt,
                           preferred_element_type=F32)
    logits = (both[:N_EXPERTS] + both[N_EXPERTS:]
              + lax.dot_general(w_hi, t_lo, nt, preferred_element_type=F32)) + rb_ref[...]
    e_iota = lax.broadcasted_iota(I32, (N_EXPERTS, MIX_ROWS), 0)
    vals, idxs, hots = [], [], []
    for _ in range(TOP_K):
        m = jnp.max(logits, axis=0, keepdims=True)
        am = jnp.min(jnp.where(logits == m, e_iota, N_EXPERTS), axis=0, keepdims=True)
        hot = e_iota == am
        vals.append(m)
        idxs.append(am)
        hots.append(hot)
        logits = jnp.where(hot, -jnp.inf, logits)
    exps = [jnp.exp(v - vals[0]) for v in vals]
    den = exps[0] + exps[1] + exps[2] + exps[3]
    wts = [e / den for e in exps]

    hot_f = (hots[0] | hots[1] | hots[2] | hots[3]).astype(F32)
    ra = lax.broadcasted_iota(I32, (MIX_ROWS, MIX_ROWS), 0)
    rc = lax.broadcasted_iota(I32, (MIX_ROWS, MIX_ROWS), 1)
    upper = (ra < rc).astype(BF16)
    prefix = jnp.dot(hot_f.astype(BF16), upper, preferred_element_type=F32)
    base = prefix + cntacc_ref[...]
    ranks = [jnp.sum(jnp.where(h, base, 0.0), axis=0, keepdims=True) for h in hots]
    counts = cntacc_ref[...] + jnp.sum(hot_f, axis=1, keepdims=True)
    cntacc_ref[...] = counts

    idx_ref[...] = jnp.concatenate(idxs, axis=0)
    rank_ref[...] = jnp.concatenate(ranks, axis=0).astype(I32)
    cnt_ref[...] = jnp.broadcast_to(counts, cnt_ref.shape)
    w8 = jnp.concatenate(wts + [jnp.zeros((8 - TOP_K, MIX_ROWS), F32)], axis=0)
    wgt_ref[...] = w8.T


def _mix_route(x2, attn2, conv2, w_out, norm_g, router_w, router_b):
    n_tok = x2.shape[0]
    row = lambda i: (i, 0)
    const = lambda i: (0, 0)
    vec = lambda v: v.reshape(1, -1)
    return pl.pallas_call(
        _mix_route_kernel,
        grid=(n_tok // MIX_ROWS,),
        in_specs=[
            pl.BlockSpec((MIX_ROWS, D_MODEL), row),
            pl.BlockSpec((MIX_ROWS, D_ATTN), row),
            pl.BlockSpec((MIX_ROWS, D_CONV), row),
            pl.BlockSpec((D_MODEL, D_MODEL), const),
            pl.BlockSpec((1, D_MODEL), const),
            pl.BlockSpec((N_EXPERTS, D_MODEL), const),
            pl.BlockSpec((N_EXPERTS, 1), const),
        ],
        out_specs=[
            pl.BlockSpec((MIX_ROWS, D_MODEL), row),
            pl.BlockSpec((MIX_ROWS, HALF), row),
            pl.BlockSpec((TOP_K, MIX_ROWS), lambda i: (0, i)),
            pl.BlockSpec((MIX_ROWS, 8), row),
            pl.BlockSpec((TOP_K, MIX_ROWS), lambda i: (0, i)),
            pl.BlockSpec((N_EXPERTS, 128), const),
        ],
        out_shape=[
            jax.ShapeDtypeStruct((n_tok, D_MODEL), F32),
            jax.ShapeDtypeStruct((n_tok, HALF), I32),
            jax.ShapeDtypeStruct((TOP_K, n_tok), I32),
            jax.ShapeDtypeStruct((n_tok, 8), F32),
            jax.ShapeDtypeStruct((TOP_K, n_tok), I32),
            jax.ShapeDtypeStruct((N_EXPERTS, 128), F32),
        ],
        scratch_shapes=[
            pltpu.VMEM((D_MODEL, D_MODEL), BF16),
            pltpu.VMEM((N_EXPERTS, 1), F32),
        ],
        compiler_params=pltpu.CompilerParams(
            dimension_semantics=("arbitrary",), vmem_limit_bytes=VMEM_LIMIT),
        name="mix_route",
    )(x2, attn2, conv2, w_out, vec(norm_g), router_w.T, router_b.reshape(N_EXPERTS, 1))


def _tile_plan(counts, n_tiles):
    tiles_per = (counts + EXP_ROWS - 1) // EXP_ROWS
    tile_end = jnp.cumsum(tiles_per)
    tile_begin = tile_end - tiles_per
    n_valid = tile_end[-1]
    tiles = jnp.arange(n_tiles, dtype=I32)
    tile_valid = tiles < n_valid
    capped = jnp.minimum(tiles, n_valid - 1)
    tile_expert = jnp.sum((capped[:, None] >= tile_end[None, :]).astype(I32), axis=1)
    tile_expert = jnp.minimum(tile_expert, N_EXPERTS - 1)
    is_first = (tiles[:, None] == tile_begin[None, :]) & (tiles_per[None, :] > 0)
    tile_first = tile_valid & jnp.any(is_first, axis=1)
    group_start = tile_begin * EXP_ROWS
    experts = jnp.arange(N_EXPERTS, dtype=I32)
    nonempty = tiles_per > 0
    parity = (jnp.cumsum(nonempty.astype(I32)) - 1) % 2
    later = nonempty[None, :] & (experts[None, :] > experts[:, None])
    nxt = jnp.min(jnp.where(later, experts[None, :], N_EXPERTS), axis=1)
    nxt = jnp.where(nxt == N_EXPERTS, -1, nxt)
    hot = (tile_expert[:, None] == experts[None, :]).astype(I32)
    tile_slot = jnp.sum(hot * parity[None, :], axis=1)
    tile_next = jnp.sum(hot * nxt[None, :], axis=1)
    filled = jnp.sum(hot * counts[None, :], axis=1) - (tiles - jnp.sum(hot * tile_begin[None, :], axis=1)) * EXP_ROWS
    filled = jnp.where(tile_valid, jnp.clip(filled, 0, EXP_ROWS), 0)
    tile_rows = (filled + EXP_GRANULE - 1) // EXP_GRANULE * EXP_GRANULE
    flags = (tile_expert, tile_first.astype(I32), tile_rows.astype(I32),
             tile_slot.astype(I32), tile_next.astype(I32))
    return flags, group_start.astype(I32)


def _slot_kernel(gstart_ref, idx_ref, rank_ref, pos_ref):
    idx = idx_ref[...]
    pos = rank_ref[...]
    for e in range(N_EXPERTS):
        pos = pos + jnp.where(idx == e, gstart_ref[e], 0)
    pos_ref[...] = pos


def _slots(group_start, idx, rank):
    full = pl.BlockSpec(idx.shape, lambda i, gs: (0, 0))
    return pl.pallas_call(
        _slot_kernel,
        grid_spec=pltpu.PrefetchScalarGridSpec(
            num_scalar_prefetch=1, grid=(1,), in_specs=[full, full], out_specs=full),
        out_shape=jax.ShapeDtypeStruct(idx.shape, I32),
        name="slots",
    )(group_start, idx, rank)


def _sc_mesh():
    return plsc.VectorSubcoreMesh(core_axis_name="core", subcore_axis_name="subcore")


def _sc_worker():
    info = plsc.get_sparse_core_info()
    wid = lax.axis_index("subcore") * info.num_cores + lax.axis_index("core")
    return wid, info.num_cores * info.num_subcores


def _dispatch(t2, pos_flat, n_slots):
    n_tok, d = t2.shape
    n_workers = V7X_SC_WORKERS
    per_w = n_tok // n_workers
    group = DISPATCH_BUFS * DISPATCH_ROWS
    assert per_w % group == 0

    @functools.partial(
        pl.kernel, mesh=_sc_mesh(),
        out_type=jax.ShapeDtypeStruct((n_slots, d), t2.dtype),
        scratch_types=(
            [pltpu.VMEM((DISPATCH_ROWS,), I32) for _ in range(DISPATCH_BUFS * TOP_K)]
            + [pltpu.VMEM((DISPATCH_ROWS, d), t2.dtype) for _ in range(DISPATCH_BUFS)]
            + [pltpu.SemaphoreType.DMA for _ in range(DISPATCH_BUFS * (TOP_K + 1))]),
        name="dispatch",
    )
    def k(t_hbm, pos_hbm, o_hbm, *scratch):
        n_idx = DISPATCH_BUFS * TOP_K
        idx_v = scratch[:n_idx]
        rows_v = scratch[n_idx:n_idx + DISPATCH_BUFS]
        rsem = scratch[n_idx + DISPATCH_BUFS:n_idx + 2 * DISPATCH_BUFS]
        ssem = scratch[n_idx + 2 * DISPATCH_BUFS:]
        wid, nw = _sc_worker()
        assert nw == n_workers

        @pl.loop(0, per_w // group)
        def _(c):
            base = pl.multiple_of(wid * per_w + c * group, group)
            reads, scatters = [], []
            for b in range(DISPATCH_BUFS):
                rows = pl.ds(base + b * DISPATCH_ROWS, DISPATCH_ROWS)
                reads.append(pltpu.async_copy(t_hbm.at[rows], rows_v[b], rsem[b]))
            for b in range(DISPATCH_BUFS):
                for kk in range(TOP_K):
                    pltpu.sync_copy(
                        pos_hbm.at[pl.ds(kk * n_tok + base + b * DISPATCH_ROWS, DISPATCH_ROWS)],
                        idx_v[b * TOP_K + kk])
            for b in range(DISPATCH_BUFS):
                reads[b].wait()
                for kk in range(TOP_K):
                    i = b * TOP_K + kk
                    scatters.append(pltpu.async_copy(rows_v[b], o_hbm.at[idx_v[i]], ssem[i]))
            for cp in scatters:
                cp.wait()

    return k(t2, pos_flat)


def _gather_back(sorted_out, pos_flat):
    n_rows = pos_flat.shape[0]
    d = sorted_out.shape[1]
    n_workers = V7X_SC_WORKERS
    per_w = n_rows // n_workers
    group = GATHER_BUFS * GATHER_ROWS
    assert per_w % group == 0

    @functools.partial(
        pl.kernel, mesh=_sc_mesh(),
        out_type=jax.ShapeDtypeStruct((n_rows, d), sorted_out.dtype),
        scratch_types=(
            [pltpu.VMEM((GATHER_ROWS,), I32) for _ in range(GATHER_BUFS)]
            + [pltpu.VMEM((GATHER_ROWS, d), sorted_out.dtype) for _ in range(GATHER_BUFS)]
            + [pltpu.SemaphoreType.DMA for _ in range(2 * GATHER_BUFS)]),
        name="gather_back",
    )
    def k(s_hbm, pos_hbm, o_hbm, *scratch):
        idx_v = scratch[:GATHER_BUFS]
        rows_v = scratch[GATHER_BUFS:2 * GATHER_BUFS]
        gsem = scratch[2 * GATHER_BUFS:3 * GATHER_BUFS]
        wsem = scratch[3 * GATHER_BUFS:]
        wid, nw = _sc_worker()
        assert nw == n_workers

        @pl.loop(0, per_w // group)
        def _(c):
            base = pl.multiple_of(wid * per_w + c * group, group)
            gathers, writes = [], []
            for b in range(GATHER_BUFS):
                rows = pl.ds(base + b * GATHER_ROWS, GATHER_ROWS)
                pltpu.sync_copy(pos_hbm.at[rows], idx_v[b])
                gathers.append(pltpu.async_copy(s_hbm.at[idx_v[b]], rows_v[b], gsem[b]))
            for b in range(GATHER_BUFS):
                rows = pl.ds(base + b * GATHER_ROWS, GATHER_ROWS)
                gathers[b].wait()
                writes.append(pltpu.async_copy(rows_v[b], o_hbm.at[rows], wsem[b]))
            for b in range(GATHER_BUFS):
                writes[b].wait()

    return k(sorted_out, pos_flat)


def _weight_copies(w_hbm, wf32, sem, expert, slot):
    return [pltpu.make_async_copy(w.at[expert], wf32.at[slot, mtx], sem.at[slot])
            for mtx, w in enumerate(w_hbm)]


def _expert_rows(rows, x_ref, bg_ref, bu_ref, bd_ref, o_ref, wbf):
    xs = jnp.concatenate(_unpack_rows(x_ref[0:rows, :]), axis=1).astype(BF16)
    cn = 256
    hids = []
    for c in range(D_MODEL // cn):
        sl = slice(c * cn, (c + 1) * cn)
        g = jnp.dot(xs, wbf[0, :, sl], preferred_element_type=F32) + bg_ref[:, sl]
        u = jnp.dot(xs, wbf[1, :, sl], preferred_element_type=F32) + bu_ref[:, sl]
        g = jnp.minimum(g, SWIGLU_LIMIT)
        u = jnp.clip(u, -SWIGLU_LIMIT, SWIGLU_LIMIT)
        hids.append(((u + 1.0) * (g * jax.nn.sigmoid(SWIGLU_ALPHA * g))).astype(BF16))
    acc = jnp.dot(jnp.concatenate(hids, axis=1), wbf[2], preferred_element_type=F32)
    o_ref[0:rows, :] = _pack_rows(acc + bd_ref[...])
    if rows < EXP_ROWS:
        o_ref[rows:EXP_ROWS, :] = jnp.zeros((EXP_ROWS - rows, HALF), I32)


def _expert_kernel(texp_ref, tfirst_ref, trows_ref, tslot_ref, tnext_ref,
                   x_ref, wg_hbm, bg_ref, wu_hbm, bu_ref, wd_hbm, bd_ref,
                   o_ref, wf32, wbf, wsem):
    w_hbm = (wg_hbm, wu_hbm, wd_hbm)
    for sub in range(EXP_TILES_PER_STEP):
        i = pl.program_id(0) * EXP_TILES_PER_STEP + sub
        x_tile = x_ref.at[sub * EXP_ROWS:(sub + 1) * EXP_ROWS]
        o_tile = o_ref.at[sub * EXP_ROWS:(sub + 1) * EXP_ROWS]
        _expert_tile(i, texp_ref, tfirst_ref, trows_ref, tslot_ref, tnext_ref,
                     x_tile, w_hbm, bg_ref, bu_ref, bd_ref, o_tile, wf32, wbf, wsem)


def _expert_tile(i, texp_ref, tfirst_ref, trows_ref, tslot_ref, tnext_ref,
                 x_ref, w_hbm, bg_ref, bu_ref, bd_ref, o_ref, wf32, wbf, wsem):
    @pl.when(trows_ref[i] == 0)
    def _():
        o_ref[...] = jnp.zeros(o_ref.shape, o_ref.dtype)

    @pl.when(tfirst_ref[i] == 1)
    def _():
        slot = tslot_ref[i]
        expert = texp_ref[i]

        @pl.when(i == 0)
        def _():
            for cp in _weight_copies(w_hbm, wf32, wsem, expert, slot):
                cp.start()

        for cp in _weight_copies(w_hbm, wf32, wsem, expert, slot):
            cp.wait()

        @pl.when(tnext_ref[i] >= 0)
        def _():
            for cp in _weight_copies(w_hbm, wf32, wsem, tnext_ref[i], 1 - slot):
                cp.start()

        for mtx in range(3):
            _cast_rows(wf32.at[slot, mtx], wbf.at[mtx], D_MODEL)

    for rows in range(EXP_GRANULE, EXP_ROWS + 1, EXP_GRANULE):
        @pl.when(trows_ref[i] == rows)
        def _(rows=rows):
            expert = texp_ref[i]
            _expert_rows(rows, x_ref, bg_ref.at[expert], bu_ref.at[expert], bd_ref.at[expert],
                         o_ref, wbf)


def _experts(sorted_t, tile_flags, w_gate, b_gate, w_up, b_up, w_down, b_down, n_tiles):
    row_blk = pl.BlockSpec((EXP_TILES_PER_STEP * EXP_ROWS, HALF), lambda i, *_: (i, 0))
    w_any = pl.BlockSpec(memory_space=pl.ANY)
    b_blk = pl.BlockSpec((N_EXPERTS, 1, D_MODEL), lambda i, *_: (0, 0, 0))
    grid_spec = pltpu.PrefetchScalarGridSpec(
        num_scalar_prefetch=len(tile_flags),
        grid=(n_tiles // EXP_TILES_PER_STEP,),
        in_specs=[row_blk, w_any, b_blk, w_any, b_blk, w_any, b_blk],
        out_specs=row_blk,
        scratch_shapes=[
            pltpu.VMEM((2, 3, D_MODEL, D_MODEL), F32),
            pltpu.VMEM((3, D_MODEL, D_MODEL), BF16),
            pltpu.SemaphoreType.DMA((2,)),
        ],
    )
    b3 = lambda b: b.reshape(N_EXPERTS, 1, D_MODEL)
    return pl.pallas_call(
        _expert_kernel,
        grid_spec=grid_spec,
        out_shape=jax.ShapeDtypeStruct((n_tiles * EXP_ROWS, HALF), I32),
        compiler_params=pltpu.CompilerParams(
            dimension_semantics=("arbitrary",), vmem_limit_bytes=VMEM_LIMIT),
        name="experts",
    )(*tile_flags, sorted_t,
      w_gate, b3(b_gate), w_up, b3(b_up), w_down, b3(b_down))


def _combine_kernel(x1_ref, s0_ref, s1_ref, s2_ref, s3_ref, w_ref, g_ref, o_ref):
    w = w_ref[...]
    y_lo = x1_ref[:, :HALF]
    y_hi = x1_ref[:, HALF:]
    for k, s_ref in enumerate((s0_ref, s1_ref, s2_ref, s3_ref)):
        lo, hi = _unpack_rows(s_ref[...])
        y_lo = y_lo + w[:, k:k + 1] * lo
        y_hi = y_hi + w[:, k:k + 1] * hi
    sq = jnp.sum(y_lo * y_lo, axis=-1, keepdims=True) + jnp.sum(y_hi * y_hi, axis=-1, keepdims=True)
    scale = lax.rsqrt(sq * (1.0 / D_MODEL) + RMS_EPS)
    o_ref[:, :HALF] = (y_lo * scale) * g_ref[:, :HALF]
    o_ref[:, HALF:] = (y_hi * scale) * g_ref[:, HALF:]


def _combine(x1, slabs, wgt, norm_g):
    n_tok = x1.shape[0]
    rows = MIX_ROWS
    ntiles = n_tok // rows
    slab_spec = lambda k: pl.BlockSpec((rows, HALF), lambda i, k=k: (k * ntiles + i, 0))
    return pl.pallas_call(
        _combine_kernel,
        grid=(ntiles,),
        in_specs=[pl.BlockSpec((rows, D_MODEL), lambda i: (i, 0))]
        + [slab_spec(k) for k in range(TOP_K)]
        + [pl.BlockSpec((rows, 8), lambda i: (i, 0)),
           pl.BlockSpec((1, D_MODEL), lambda i: (0, 0))],
        out_specs=pl.BlockSpec((rows, D_MODEL), lambda i: (i, 0)),
        out_shape=jax.ShapeDtypeStruct((n_tok, D_MODEL), F32),
        compiler_params=pltpu.CompilerParams(
            dimension_semantics=("arbitrary",), vmem_limit_bytes=VMEM_LIMIT),
        name="combine",
    )(x1, slabs, slabs, slabs, slabs, wgt, norm_g.reshape(1, D_MODEL))


def kernel(x, norm_mix_g, w_in, conv_dw_w, conv_dw_b, conv_ln_g, conv_ln_b, rel_bias, w_out,
           norm_ffn_g, router_w, router_b, exp_w_gate, exp_b_gate, exp_w_up, exp_b_up,
           exp_w_down, exp_b_down, norm_final_g):
    bsz, seq, _ = x.shape
    n_tok = bsz * seq
    assert norm_mix_g.shape[0] == 1, "single-layer block"
    assert seq % IN_ROWS == 0 and seq % MIX_ROWS == 0 and LEFT_PAD % IN_ROWS == 0
    n_tiles = (TOP_K * n_tok) // EXP_ROWS + N_EXPERTS - 1
    n_tiles = -(-n_tiles // EXP_TILES_PER_STEP) * EXP_TILES_PER_STEP

    q, kpad, vpad, conv = _inproj(x, norm_mix_g[0], w_in[0], conv_dw_w[0], conv_dw_b[0],
                                  conv_ln_g[0], conv_ln_b[0])
    attn = _attention(q, kpad, vpad, _band_bias(rel_bias[0]))
    x1, t, idx, wgt, rank, cnt = _mix_route(
        x.reshape(n_tok, D_MODEL), attn.reshape(n_tok, D_ATTN), conv.reshape(n_tok, D_CONV),
        w_out[0], norm_ffn_g[0], router_w[0], router_b[0])
    tile_flags, group_start = _tile_plan(cnt[:, 0].astype(I32), n_tiles)
    pos_flat = _slots(group_start, idx, rank).reshape(TOP_K * n_tok)
    sorted_t = _dispatch(t, pos_flat, n_tiles * EXP_ROWS)
    sorted_out = _experts(sorted_t, tile_flags,
                          exp_w_gate[0], exp_b_gate[0], exp_w_up[0], exp_b_up[0],
                          exp_w_down[0], exp_b_down[0], n_tiles)
    slabs = _gather_back(sorted_out, pos_flat)
    out = _combine(x1, slabs, wgt, norm_final_g)
    return out.reshape(bsz, seq, D_MODEL)
```

```python
import functools

import jax
import jax.numpy as jnp
from jax import lax
from jax.experimental import pallas as pl
from jax.experimental.pallas import tpu as pltpu
from jax.experimental.pallas import tpu_sc as plsc

F32 = jnp.float32
BF16 = jnp.bfloat16
I32 = jnp.int32

D_MODEL = 1024
CHUNK = 64
N_HEADS = 8
HEAD_DIM = 64
D_ATTN = N_HEADS * HEAD_DIM
LEFT_CHUNKS = 8
REL_MAX = 128
REL_MIN = -(CHUNK - 1)
D_CONV = D_MODEL - D_ATTN
CONV_WIDTH = 31
N_EXPERTS = 32
TOP_K = 4
SWIGLU_ALPHA = 1.702
SWIGLU_LIMIT = 7.0
RMS_EPS = 1e-5
LN_EPS = 1e-5

LEFT_PAD = LEFT_CHUNKS * CHUNK
IN_ROWS = 512
PAD_BLOCKS = LEFT_PAD // IN_ROWS
Q_ROWS = 2 * CHUNK
ATT_BLOCKS = 4
BAND_ROWS = Q_ROWS + LEFT_PAD
HEADS_PER_STEP = 4
GROUP_LANES = HEADS_PER_STEP * HEAD_DIM
MIX_ROWS = 1024
HALO_ROWS = 32
EXP_ROWS = 512
EXP_GRANULE = 128
EXP_TILES_PER_STEP = 2
DISPATCH_ROWS = 64
DISPATCH_BUFS = 2
GATHER_ROWS = 16
GATHER_BUFS = 4
NEG_BIG = -1e30
V7X_VMEM_BYTES = 64 * 1024 * 1024
VMEM_LIMIT = V7X_VMEM_BYTES - 8 * 1024 * 1024
V7X_SC_WORKERS = 32


HALF = D_MODEL // 2
HI_MASK = -65536


def _pack_rows(x):
    bits = lax.bitcast_convert_type(x.astype(BF16).astype(F32), I32)
    return lax.shift_right_logical(bits[:, :HALF], 16) | (bits[:, HALF:] & HI_MASK)


def _unpack_rows(w):
    lo = lax.bitcast_convert_type(lax.shift_left(w, 16), F32)
    hi = lax.bitcast_convert_type(w & HI_MASK, F32)
    return lo, hi


def _cast_rows(src_ref, dst_ref, rows, step=128):
    def body(c, carry):
        r = pl.multiple_of(c * step, step)
        dst_ref[pl.ds(r, step), :] = src_ref[pl.ds(r, step), :].astype(dst_ref.dtype)
        return carry
    lax.fori_loop(0, rows // step, body, 0)


def _conv_branch(hw_ref, sh_ref, rows, cw_ref, cb_ref, lg_ref, lb_ref, out_ref):
    off = HALO_ROWS - (CONV_WIDTH - 1)
    shift_rows = HALO_ROWS + rows - 8
    for b in range(1, 8):
        sh_ref[b - 1] = hw_ref[pl.ds(b, shift_rows), :]
    acc = None
    for j in range(CONV_WIDTH):
        a, b = divmod(off + j, 8)
        src = hw_ref if b == 0 else sh_ref.at[b - 1]
        term = src[pl.ds(8 * a, rows), :] * cw_ref[j:j + 1, :]
        acc = term if acc is None else acc + term
    acc = acc + cb_ref[...]
    mu = jnp.mean(acc, axis=-1, keepdims=True)
    d = acc - mu
    var = jnp.mean(d * d, axis=-1, keepdims=True)
    y = d * lax.rsqrt(var + LN_EPS) * lg_ref[...] + lb_ref[...]
    out_ref[...] = (y * jax.nn.sigmoid(y)).astype(out_ref.dtype)


def _inproj_kernel(x_ref, g_ref, w_ref, cw_ref, cb_ref, lg_ref, lb_ref,
                   q_ref, k_ref, v_ref, c_ref, wbf_ref, hw_ref, sh_ref):
    b = pl.program_id(0)
    j = pl.program_id(1)

    @pl.when((b == 0) & (j == 0))
    def _():
        _cast_rows(w_ref, wbf_ref, D_MODEL)

    @pl.when(j < PAD_BLOCKS)
    def _():
        k_ref[...] = jnp.zeros_like(k_ref)
        v_ref[...] = jnp.zeros_like(v_ref)
        hw_ref[IN_ROWS:IN_ROWS + HALO_ROWS, :] = jnp.zeros((HALO_ROWS, D_CONV), F32)

    @pl.when(j >= PAD_BLOCKS)
    def _():
        x = x_ref[...]
        ms = jnp.mean(x * x, axis=-1, keepdims=True)
        hb = ((x * lax.rsqrt(ms + RMS_EPS)) * g_ref[...]).astype(BF16)

        def proj(c0, width):
            return jnp.dot(hb, wbf_ref[:, c0:c0 + width], preferred_element_type=F32)

        a = proj(3 * D_ATTN, D_CONV)
        gate = proj(3 * D_ATTN + D_CONV, D_CONV)
        hw_ref[0:HALO_ROWS, :] = hw_ref[IN_ROWS:IN_ROWS + HALO_ROWS, :]
        hw_ref[HALO_ROWS:HALO_ROWS + IN_ROWS, :] = a * jax.nn.sigmoid(gate)
        _conv_branch(hw_ref, sh_ref, IN_ROWS, cw_ref, cb_ref, lg_ref, lb_ref, c_ref)

        q_ref[...] = (proj(0, D_ATTN) * (HEAD_DIM ** -0.5)).astype(BF16)
        k_ref[...] = proj(D_ATTN, D_ATTN).astype(BF16)
        v_ref[...] = proj(2 * D_ATTN, D_ATTN).astype(BF16)


def _inproj(x, norm_g, w_in, conv_w, conv_b, ln_g, ln_b):
    bsz, seq, _ = x.shape
    nblk = seq // IN_ROWS
    d_cols = w_in.shape[1]
    row_blk = lambda b, j: (b, jnp.maximum(j - PAD_BLOCKS, 0), 0)
    const = lambda b, j: (0, 0)
    vec = lambda v: v.reshape(1, -1)
    return pl.pallas_call(
        _inproj_kernel,
        grid=(bsz, nblk + PAD_BLOCKS),
        in_specs=[
            pl.BlockSpec((None, IN_ROWS, D_MODEL), row_blk),
            pl.BlockSpec((1, D_MODEL), const),
            pl.BlockSpec((D_MODEL, d_cols), const),
            pl.BlockSpec((CONV_WIDTH, D_CONV), const),
            pl.BlockSpec((1, D_CONV), const),
            pl.BlockSpec((1, D_CONV), const),
            pl.BlockSpec((1, D_CONV), const),
        ],
        out_specs=[
            pl.BlockSpec((None, IN_ROWS, D_ATTN), row_blk),
            pl.BlockSpec((None, IN_ROWS, D_ATTN), lambda b, j: (b, j, 0)),
            pl.BlockSpec((None, IN_ROWS, D_ATTN), lambda b, j: (b, j, 0)),
            pl.BlockSpec((None, IN_ROWS, D_CONV), row_blk),
        ],
        out_shape=[
            jax.ShapeDtypeStruct((bsz, seq, D_ATTN), BF16),
            jax.ShapeDtypeStruct((bsz, seq + LEFT_PAD, D_ATTN), BF16),
            jax.ShapeDtypeStruct((bsz, seq + LEFT_PAD, D_ATTN), BF16),
            jax.ShapeDtypeStruct((bsz, seq, D_CONV), BF16),
        ],
        scratch_shapes=[
            pltpu.VMEM((D_MODEL, d_cols), BF16),
            pltpu.VMEM((HALO_ROWS + IN_ROWS, D_CONV), F32),
            pltpu.VMEM((7, HALO_ROWS + IN_ROWS - 8, D_CONV), F32),
        ],
        compiler_params=pltpu.CompilerParams(
            dimension_semantics=("arbitrary", "arbitrary"), vmem_limit_bytes=VMEM_LIMIT),
        name="inproj",
    )(x, vec(norm_g), w_in, conv_w, vec(conv_b), vec(ln_g), vec(ln_b))


def _attn_kernel(q_ref, k_ref, v_ref, bias_ref, o_ref):
    i = pl.program_id(1)
    first_steps = -(-LEFT_PAD // (ATT_BLOCKS * Q_ROWS))

    @pl.when(i < first_steps)
    def _():
        _attn_step(i, True, q_ref, k_ref, v_ref, bias_ref, o_ref)

    @pl.when(i >= first_steps)
    def _():
        _attn_step(i, False, q_ref, k_ref, v_ref, bias_ref, o_ref)


def _attn_step(i, mask_start, q_ref, k_ref, v_ref, bias_ref, o_ref):
    lane = lax.broadcasted_iota(I32, (Q_ROWS, GROUP_LANES), 1) // HEAD_DIM
    col = lax.broadcasted_iota(I32, (HEADS_PER_STEP * Q_ROWS, BAND_ROWS), 1)
    for qb in range(ATT_BLOCKS):
        blk = i * ATT_BLOCKS + qb
        start = pl.multiple_of(blk * Q_ROWS, Q_ROWS)
        qrows = slice(qb * Q_ROWS, (qb + 1) * Q_ROWS)
        key_ok = col >= LEFT_PAD - blk * Q_ROWS
        for g in range(N_HEADS // HEADS_PER_STEP):
            lanes = slice(g * GROUP_LANES, (g + 1) * GROUP_LANES)
            q = q_ref[qrows, lanes]
            qs = jnp.concatenate(
                [jnp.where(lane == h, q, jnp.zeros_like(q)) for h in range(HEADS_PER_STEP)], axis=0)
            kb = k_ref[pl.ds(start, BAND_ROWS), lanes]
            vb = v_ref[pl.ds(start, BAND_ROWS), lanes]
            s = lax.dot_general(qs, kb, (((1,), (1,)), ((), ())), preferred_element_type=F32)
            rows = slice(g * HEADS_PER_STEP * Q_ROWS, (g + 1) * HEADS_PER_STEP * Q_ROWS)
            s = s + bias_ref[rows, :]
            if mask_start:
                s = jnp.where(key_ok, s, NEG_BIG)
            m = jnp.max(s, axis=-1, keepdims=True)
            p = jnp.exp(s - m)
            l = jnp.sum(p, axis=-1, keepdims=True)
            o = jnp.dot(p.astype(BF16), vb, preferred_element_type=F32) / l
            out = o[0:Q_ROWS]
            for h in range(1, HEADS_PER_STEP):
                out = jnp.where(lane == h, o[h * Q_ROWS:(h + 1) * Q_ROWS], out)
            o_ref[qrows, lanes] = out.astype(o_ref.dtype)


def _attention(q, kpad, vpad, bias):
    bsz, seq, _ = q.shape
    step_rows = ATT_BLOCKS * Q_ROWS
    return pl.pallas_call(
        _attn_kernel,
        grid=(bsz, seq // step_rows),
        in_specs=[
            pl.BlockSpec((None, step_rows, D_ATTN), lambda b, i: (b, i, 0)),
            pl.BlockSpec((None, seq + LEFT_PAD, D_ATTN), lambda b, i: (b, 0, 0)),
            pl.BlockSpec((None, seq + LEFT_PAD, D_ATTN), lambda b, i: (b, 0, 0)),
            pl.BlockSpec((N_HEADS * Q_ROWS, BAND_ROWS), lambda b, i: (0, 0)),
        ],
        out_specs=pl.BlockSpec((None, step_rows, D_ATTN), lambda b, i: (b, i, 0)),
        out_shape=jax.ShapeDtypeStruct((bsz, seq, D_ATTN), BF16),
        compiler_params=pltpu.CompilerParams(
            dimension_semantics=("arbitrary", "arbitrary"), vmem_limit_bytes=VMEM_LIMIT),
        name="chunk_attn",
    )(q, kpad, vpad, bias)


def _band_bias(rel_bias):
    n_rel = REL_MAX - REL_MIN + 1
    far = jnp.broadcast_to(rel_bias[:, n_rel - 1:n_rel], (N_HEADS, BAND_ROWS - 1 - REL_MAX))
    near = rel_bias[:, ::-1]
    ahead = jnp.broadcast_to(rel_bias[:, 0:1], (N_HEADS, Q_ROWS - 1 + REL_MIN))
    diag = jnp.concatenate([far, near, ahead], axis=1).astype(F32)
    bias = jnp.stack(
        [diag[:, Q_ROWS - 1 - r:Q_ROWS - 1 - r + BAND_ROWS] for r in range(Q_ROWS)], axis=1)
    r = jnp.arange(Q_ROWS)[:, None]
    m = jnp.arange(BAND_ROWS)[None, :]
    cq = r // CHUNK
    ck = m // CHUNK
    in_band = (ck >= cq) & (ck <= cq + LEFT_CHUNKS)
    bias = jnp.where(in_band[None], bias, NEG_BIG)
    return bias.reshape(N_HEADS * Q_ROWS, BAND_ROWS)


def _split_bf16(v):
    hi = v.astype(BF16)
    lo = (v - hi.astype(F32)).astype(BF16)
    return hi, lo


def _mix_route_kernel(x_ref, a_ref, c_ref, wo_ref, ng_ref, rwt_ref, rb_ref,
                      x1_ref, t_ref, idx_ref, wgt_ref, rank_ref, cnt_ref,
                      wobf_ref, cntacc_ref):
    i = pl.program_id(0)

    @pl.when(i == 0)
    def _():
        _cast_rows(wo_ref, wobf_ref, D_MODEL)
        cntacc_ref[...] = jnp.zeros_like(cntacc_ref)

    mix_in = jnp.concatenate([a_ref[...], c_ref[...]], axis=1)
    x1 = x_ref[...] + jnp.dot(mix_in, wobf_ref[...], preferred_element_type=F32)
    x1_ref[...] = x1
    ms = jnp.mean(x1 * x1, axis=-1, keepdims=True)
    t = (x1 * lax.rsqrt(ms + RMS_EPS)) * ng_ref[...]
    t_ref[...] = _pack_rows(t)

    nt = (((1,), (1,)), ((), ()))
    w_hi, w_lo = _split_bf16(rwt_ref[...])
    t_hi, t_lo = _split_bf16(t)
    both = lax.dot_general(jnp.concatenate([w_hi, w_lo], axis=0), t_hi, nt,
                           preferred_element_type=F32)
    logits = (both[:N_EXPERTS] + both[N_EXPERTS:]
              + lax.dot_general(w_hi, t_lo, nt, preferred_element_type=F32)) + rb_ref[...]
    e_iota = lax.broadcasted_iota(I32, (N_EXPERTS, MIX_ROWS), 0)
    vals, idxs, hots = [], [], []
    for _ in range(TOP_K):
        m = jnp.max(logits, axis=0, keepdims=True)
        am = jnp.min(jnp.where(logits == m, e_iota, N_EXPERTS), axis=0, keepdims=True)
        hot = e_iota == am
        vals.append(m)
        idxs.append(am)
        hots.append(hot)
        logits = jnp.where(hot, -jnp.inf, logits)
    exps = [jnp.exp(v - vals[0]) for v in vals]
    den = exps[0] + exps[1] + exps[2] + exps[3]
    wts = [e / den for e in exps]

    hot_f = (hots[0] | hots[1] | hots[2] | hots[3]).astype(F32)
    ra = lax.broadcasted_iota(I32, (MIX_ROWS, MIX_ROWS), 0)
    rc = lax.broadcasted_iota(I32, (MIX_ROWS, MIX_ROWS), 1)
    upper = (ra < rc).astype(BF16)
    prefix = jnp.dot(hot_f.astype(BF16), upper, preferred_element_type=F32)
    base = prefix + cntacc_ref[...]
    ranks = [jnp.sum(jnp.where(h, base, 0.0), axis=0, keepdims=True) for h in hots]
    counts = cntacc_ref[...] + jnp.sum(hot_f, axis=1, keepdims=True)
    cntacc_ref[...] = counts

    idx_ref[...] = jnp.concatenate(idxs, axis=0)
    rank_ref[...] = jnp.concatenate(ranks, axis=0).astype(I32)
    cnt_ref[...] = jnp.broadcast_to(counts, cnt_ref.shape)
    w8 = jnp.concatenate(wts + [jnp.zeros((8 - TOP_K, MIX_ROWS), F32)], axis=0)
    wgt_ref[...] = w8.T


def _mix_route(x2, attn2, conv2, w_out, norm_g, router_w, router_b):
    n_tok = x2.shape[0]
    row = lambda i: (i, 0)
    const = lambda i: (0, 0)
    vec = lambda v: v.reshape(1, -1)
    return pl.pallas_call(
        _mix_route_kernel,
        grid=(n_tok // MIX_ROWS,),
        in_specs=[
            pl.BlockSpec((MIX_ROWS, D_MODEL), row),
            pl.BlockSpec((MIX_ROWS, D_ATTN), row),
            pl.BlockSpec((MIX_ROWS, D_CONV), row),
            pl.BlockSpec((D_MODEL, D_MODEL), const),
            pl.BlockSpec((1, D_MODEL), const),
            pl.BlockSpec((N_EXPERTS, D_MODEL), const),
            pl.BlockSpec((N_EXPERTS, 1), const),
        ],
        out_specs=[
            pl.BlockSpec((MIX_ROWS, D_MODEL), row),
            pl.BlockSpec((MIX_ROWS, HALF), row),
            pl.BlockSpec((TOP_K, MIX_ROWS), lambda i: (0, i)),
            pl.BlockSpec((MIX_ROWS, 8), row),
            pl.BlockSpec((TOP_K, MIX_ROWS), lambda i: (0, i)),
            pl.BlockSpec((N_EXPERTS, 128), const),
        ],
        out_shape=[
            jax.ShapeDtypeStruct((n_tok, D_MODEL), F32),
            jax.ShapeDtypeStruct((n_tok, HALF), I32),
            jax.ShapeDtypeStruct((TOP_K, n_tok), I32),
            jax.ShapeDtypeStruct((n_tok, 8), F32),
            jax.ShapeDtypeStruct((TOP_K, n_tok), I32),
            jax.ShapeDtypeStruct((N_EXPERTS, 128), F32),
        ],
        scratch_shapes=[
            pltpu.VMEM((D_MODEL, D_MODEL), BF16),
            pltpu.VMEM((N_EXPERTS, 1), F32),
        ],
        compiler_params=pltpu.CompilerParams(
            dimension_semantics=("arbitrary",), vmem_limit_bytes=VMEM_LIMIT),
        name="mix_route",
    )(x2, attn2, conv2, w_out, vec(norm_g), router_w.T, router_b.reshape(N_EXPERTS, 1))


def _tile_plan(counts, n_tiles):
    tiles_per = (counts + EXP_ROWS - 1) // EXP_ROWS
    tile_end = jnp.cumsum(tiles_per)
    tile_begin = tile_end - tiles_per
    n_valid = tile_end[-1]
    tiles = jnp.arange(n_tiles, dtype=I32)
    tile_valid = tiles < n_valid
    capped = jnp.minimum(tiles, n_valid - 1)
    tile_expert = jnp.sum((capped[:, None] >= tile_end[None, :]).astype(I32), axis=1)
    tile_expert = jnp.minimum(tile_expert, N_EXPERTS - 1)
    is_first = (tiles[:, None] == tile_begin[None, :]) & (tiles_per[None, :] > 0)
    tile_first = tile_valid & jnp.any(is_first, axis=1)
    group_start = tile_begin * EXP_ROWS
    experts = jnp.arange(N_EXPERTS, dtype=I32)
    nonempty = tiles_per > 0
    parity = (jnp.cumsum(nonempty.astype(I32)) - 1) % 2
    later = nonempty[None, :] & (experts[None, :] > experts[:, None])
    nxt = jnp.min(jnp.where(later, experts[None, :], N_EXPERTS), axis=1)
    nxt = jnp.where(nxt == N_EXPERTS, -1, nxt)
    hot = (tile_expert[:, None] == experts[None, :]).astype(I32)
    tile_slot = jnp.sum(hot * parity[None, :], axis=1)
    tile_next = jnp.sum(hot * nxt[None, :], axis=1)
    filled = jnp.sum(hot * counts[None, :], axis=1) - (tiles - jnp.sum(hot * tile_begin[None, :], axis=1)) * EXP_ROWS
    filled = jnp.where(tile_valid, jnp.clip(filled, 0, EXP_ROWS), 0)
    tile_rows = (filled + EXP_GRANULE - 1) // EXP_GRANULE * EXP_GRANULE
    flags = (tile_expert, tile_first.astype(I32), tile_rows.astype(I32),
             tile_slot.astype(I32), tile_next.astype(I32))
    return flags, group_start.astype(I32)


def _slot_kernel(gstart_ref, idx_ref, rank_ref, pos_ref):
    idx = idx_ref[...]
    pos = rank_ref[...]
    for e in range(N_EXPERTS):
        pos = pos + jnp.where(idx == e, gstart_ref[e], 0)
    pos_ref[...] = pos


def _slots(group_start, idx, rank):
    full = pl.BlockSpec(idx.shape, lambda i, gs: (0, 0))
    return pl.pallas_call(
        _slot_kernel,
        grid_spec=pltpu.PrefetchScalarGridSpec(
            num_scalar_prefetch=1, grid=(1,), in_specs=[full, full], out_specs=full),
        out_shape=jax.ShapeDtypeStruct(idx.shape, I32),
        name="slots",
    )(group_start, idx, rank)


def _sc_mesh():
    return plsc.VectorSubcoreMesh(core_axis_name="core", subcore_axis_name="subcore")


def _sc_worker():
    info = plsc.get_sparse_core_info()
    wid = lax.axis_index("subcore") * info.num_cores + lax.axis_index("core")
    return wid, info.num_cores * info.num_subcores


def _dispatch(t2, pos_flat, n_slots):
    n_tok, d = t2.shape
    n_workers = V7X_SC_WORKERS
    per_w = n_tok // n_workers
    group = DISPATCH_BUFS * DISPATCH_ROWS
    assert per_w % group == 0

    @functools.partial(
        pl.kernel, mesh=_sc_mesh(),
        out_type=jax.ShapeDtypeStruct((n_slots, d), t2.dtype),
        scratch_types=(
            [pltpu.VMEM((DISPATCH_ROWS,), I32) for _ in range(DISPATCH_BUFS * TOP_K)]
            + [pltpu.VMEM((DISPATCH_ROWS, d), t2.dtype) for _ in range(DISPATCH_BUFS)]
            + [pltpu.SemaphoreType.DMA for _ in range(DISPATCH_BUFS * (TOP_K + 1))]),
        name="dispatch",
    )
    def k(t_hbm, pos_hbm, o_hbm, *scratch):
        n_idx = DISPATCH_BUFS * TOP_K
        idx_v = scratch[:n_idx]
        rows_v = scratch[n_idx:n_idx + DISPATCH_BUFS]
        rsem = scratch[n_idx + DISPATCH_BUFS:n_idx + 2 * DISPATCH_BUFS]
        ssem = scratch[n_idx + 2 * DISPATCH_BUFS:]
        wid, nw = _sc_worker()
        assert nw == n_workers

        @pl.loop(0, per_w // group)
        def _(c):
            base = pl.multiple_of(wid * per_w + c * group, group)
            reads, scatters = [], []
            for b in range(DISPATCH_BUFS):
                rows = pl.ds(base + b * DISPATCH_ROWS, DISPATCH_ROWS)
                reads.append(pltpu.async_copy(t_hbm.at[rows], rows_v[b], rsem[b]))
            for b in range(DISPATCH_BUFS):
                for kk in range(TOP_K):
                    pltpu.sync_copy(
                        pos_hbm.at[pl.ds(kk * n_tok + base + b * DISPATCH_ROWS, DISPATCH_ROWS)],
                        idx_v[b * TOP_K + kk])
            for b in range(DISPATCH_BUFS):
                reads[b].wait()
                for kk in range(TOP_K):
                    i = b * TOP_K + kk
                    scatters.append(pltpu.async_copy(rows_v[b], o_hbm.at[idx_v[i]], ssem[i]))
            for cp in scatters:
                cp.wait()

    return k(t2, pos_flat)


def _gather_back(sorted_out, pos_flat):
    n_rows = pos_flat.shape[0]
    d = sorted_out.shape[1]
    n_workers = V7X_SC_WORKERS
    per_w = n_rows // n_workers
    group = GATHER_BUFS * GATHER_ROWS
    n_pairs = per_w // (2 * group)
    assert per_w % (2 * group) == 0

    @functools.partial(
        pl.kernel, mesh=_sc_mesh(),
        out_type=jax.ShapeDtypeStruct((n_rows, d), sorted_out.dtype),
        scratch_types=(
            [pltpu.VMEM((GATHER_ROWS,), I32) for _ in range(2 * GATHER_BUFS)]
            + [pltpu.VMEM((GATHER_ROWS, d), sorted_out.dtype) for _ in range(2 * GATHER_BUFS)]
            + [pltpu.SemaphoreType.DMA for _ in range(4 * GATHER_BUFS)]),
        name="gather_back",
    )
    def k(s_hbm, pos_hbm, o_hbm, *scratch):
        nb = 2 * GATHER_BUFS
        idx_v, rows_v = scratch[:nb], scratch[nb:2 * nb]
        gsem, wsem = scratch[2 * nb:3 * nb], scratch[3 * nb:]
        wid, nw = _sc_worker()
        assert nw == n_workers
        first = wid * per_w

        def bufs(s):
            return range(s * GATHER_BUFS, (s + 1) * GATHER_BUFS)

        def rows_of(g, b):
            return pl.ds(first + g * group + (b % GATHER_BUFS) * GATHER_ROWS, GATHER_ROWS)

        def gather(b):
            return pltpu.make_async_copy(s_hbm.at[idx_v[b]], rows_v[b], gsem[b])

        def write(g, b):
            return pltpu.make_async_copy(rows_v[b], o_hbm.at[rows_of(g, b)], wsem[b])

        def start_gathers(g, s):
            for b in bufs(s):
                pltpu.sync_copy(pos_hbm.at[rows_of(g, b)], idx_v[b])
                gather(b).start()

        def drain(g, s):
            for b in bufs(s):
                gather(b).wait()
                write(g, b).start()
            for b in bufs(s):
                write(g, b).wait()

        start_gathers(0, 0)

        @pl.loop(0, n_pairs)
        def _(t):
            start_gathers(2 * t + 1, 1)
            drain(2 * t, 0)

            @pl.when(t + 1 < n_pairs)
            def _():
                start_gathers(2 * t + 2, 0)

            drain(2 * t + 1, 1)

    return k(sorted_out, pos_flat)


def _weight_copies(w_hbm, wf32, sem, expert, slot):
    return [pltpu.make_async_copy(w.at[expert], wf32.at[slot, mtx], sem.at[slot])
            for mtx, w in enumerate(w_hbm)]


def _expert_rows(rows, x_ref, bg_ref, bu_ref, bd_ref, o_ref, wbf):
    xs = jnp.concatenate(_unpack_rows(x_ref[0:rows, :]), axis=1).astype(BF16)
    cn = 256
    hids = []
    for c in range(D_MODEL // cn):
        sl = slice(c * cn, (c + 1) * cn)
        g = jnp.dot(xs, wbf[0, :, sl], preferred_element_type=F32) + bg_ref[:, sl]
        u = jnp.dot(xs, wbf[1, :, sl], preferred_element_type=F32) + bu_ref[:, sl]
        g = jnp.minimum(g, SWIGLU_LIMIT)
        u = jnp.clip(u, -SWIGLU_LIMIT, SWIGLU_LIMIT)
        hids.append(((u + 1.0) * (g * jax.nn.sigmoid(SWIGLU_ALPHA * g))).astype(BF16))
    acc = jnp.dot(jnp.concatenate(hids, axis=1), wbf[2], preferred_element_type=F32)
    o_ref[0:rows, :] = _pack_rows(acc + bd_ref[...])
    if rows < EXP_ROWS:
        o_ref[rows:EXP_ROWS, :] = jnp.zeros((EXP_ROWS - rows, HALF), I32)


def _expert_kernel(texp_ref, tfirst_ref, trows_ref, tslot_ref, tnext_ref,
                   x_ref, wg_hbm, bg_ref, wu_hbm, bu_ref, wd_hbm, bd_ref,
                   o_ref, wf32, wbf, wsem):
    w_hbm = (wg_hbm, wu_hbm, wd_hbm)
    for sub in range(EXP_TILES_PER_STEP):
        i = pl.program_id(0) * EXP_TILES_PER_STEP + sub
        x_tile = x_ref.at[sub * EXP_ROWS:(sub + 1) * EXP_ROWS]
        o_tile = o_ref.at[sub * EXP_ROWS:(sub + 1) * EXP_ROWS]
        _expert_tile(i, texp_ref, tfirst_ref, trows_ref, tslot_ref, tnext_ref,
                     x_tile, w_hbm, bg_ref, bu_ref, bd_ref, o_tile, wf32, wbf, wsem)


def _expert_tile(i, texp_ref, tfirst_ref, trows_ref, tslot_ref, tnext_ref,
                 x_ref, w_hbm, bg_ref, bu_ref, bd_ref, o_ref, wf32, wbf, wsem):
    @pl.when(trows_ref[i] == 0)
    def _():
        o_ref[...] = jnp.zeros(o_ref.shape, o_ref.dtype)

    @pl.when(tfirst_ref[i] == 1)
    def _():
        slot = tslot_ref[i]
        expert = texp_ref[i]

        @pl.when(i == 0)
        def _():
            for cp in _weight_copies(w_hbm, wf32, wsem, expert, slot):
                cp.start()

        for cp in _weight_copies(w_hbm, wf32, wsem, expert, slot):
            cp.wait()

        @pl.when(tnext_ref[i] >= 0)
        def _():
            for cp in _weight_copies(w_hbm, wf32, wsem, tnext_ref[i], 1 - slot):
                cp.start()

        for mtx in range(3):
            _cast_rows(wf32.at[slot, mtx], wbf.at[mtx], D_MODEL)

    for rows in range(EXP_GRANULE, EXP_ROWS + 1, EXP_GRANULE):
        @pl.when(trows_ref[i] == rows)
        def _(rows=rows):
            expert = texp_ref[i]
            _expert_rows(rows, x_ref, bg_ref.at[expert], bu_ref.at[expert], bd_ref.at[expert],
                         o_ref, wbf)


def _experts(sorted_t, tile_flags, w_gate, b_gate, w_up, b_up, w_down, b_down, n_tiles):
    row_blk = pl.BlockSpec((EXP_TILES_PER_STEP * EXP_ROWS, HALF), lambda i, *_: (i, 0))
    w_any = pl.BlockSpec(memory_space=pl.ANY)
    b_blk = pl.BlockSpec((N_EXPERTS, 1, D_MODEL), lambda i, *_: (0, 0, 0))
    grid_spec = pltpu.PrefetchScalarGridSpec(
        num_scalar_prefetch=len(tile_flags),
        grid=(n_tiles // EXP_TILES_PER_STEP,),
        in_specs=[row_blk, w_any, b_blk, w_any, b_blk, w_any, b_blk],
        out_specs=row_blk,
        scratch_shapes=[
            pltpu.VMEM((2, 3, D_MODEL, D_MODEL), F32),
            pltpu.VMEM((3, D_MODEL, D_MODEL), BF16),
            pltpu.SemaphoreType.DMA((2,)),
        ],
    )
    b3 = lambda b: b.reshape(N_EXPERTS, 1, D_MODEL)
    return pl.pallas_call(
        _expert_kernel,
        grid_spec=grid_spec,
        out_shape=jax.ShapeDtypeStruct((n_tiles * EXP_ROWS, HALF), I32),
        compiler_params=pltpu.CompilerParams(
            dimension_semantics=("arbitrary",), vmem_limit_bytes=VMEM_LIMIT),
        name="experts",
    )(*tile_flags, sorted_t,
      w_gate, b3(b_gate), w_up, b3(b_up), w_down, b3(b_down))


def _combine_kernel(x1_ref, s0_ref, s1_ref, s2_ref, s3_ref, w_ref, g_ref, o_ref):
    w = w_ref[...]
    y_lo = x1_ref[:, :HALF]
    y_hi = x1_ref[:, HALF:]
    for k, s_ref in enumerate((s0_ref, s1_ref, s2_ref, s3_ref)):
        lo, hi = _unpack_rows(s_ref[...])
        y_lo = y_lo + w[:, k:k + 1] * lo
        y_hi = y_hi + w[:, k:k + 1] * hi
    sq = jnp.sum(y_lo * y_lo, axis=-1, keepdims=True) + jnp.sum(y_hi * y_hi, axis=-1, keepdims=True)
    scale = lax.rsqrt(sq * (1.0 / D_MODEL) + RMS_EPS)
    o_ref[:, :HALF] = (y_lo * scale) * g_ref[:, :HALF]
    o_ref[:, HALF:] = (y_hi * scale) * g_ref[:, HALF:]


def _combine(x1, slabs, wgt, norm_g):
    n_tok = x1.shape[0]
    rows = MIX_ROWS
    ntiles = n_tok // rows
    slab_spec = lambda k: pl.BlockSpec((rows, HALF), lambda i, k=k: (k * ntiles + i, 0))
    return pl.pallas_call(
        _combine_kernel,
        grid=(ntiles,),
        in_specs=[pl.BlockSpec((rows, D_MODEL), lambda i: (i, 0))]
        + [slab_spec(k) for k in range(TOP_K)]
        + [pl.BlockSpec((rows, 8), lambda i: (i, 0)),
           pl.BlockSpec((1, D_MODEL), lambda i: (0, 0))],
        out_specs=pl.BlockSpec((rows, D_MODEL), lambda i: (i, 0)),
        out_shape=jax.ShapeDtypeStruct((n_tok, D_MODEL), F32),
        compiler_params=pltpu.CompilerParams(
            dimension_semantics=("arbitrary",), vmem_limit_bytes=VMEM_LIMIT),
        name="combine",
    )(x1, slabs, slabs, slabs, slabs, wgt, norm_g.reshape(1, D_MODEL))


def kernel(x, norm_mix_g, w_in, conv_dw_w, conv_dw_b, conv_ln_g, conv_ln_b, rel_bias, w_out,
           norm_ffn_g, router_w, router_b, exp_w_gate, exp_b_gate, exp_w_up, exp_b_up,
           exp_w_down, exp_b_down, norm_final_g):
    bsz, seq, _ = x.shape
    n_tok = bsz * seq
    assert norm_mix_g.shape[0] == 1, "single-layer block"
    assert seq % IN_ROWS == 0 and seq % MIX_ROWS == 0 and LEFT_PAD % IN_ROWS == 0
    n_tiles = (TOP_K * n_tok) // EXP_ROWS + N_EXPERTS - 1
    n_tiles = -(-n_tiles // EXP_TILES_PER_STEP) * EXP_TILES_PER_STEP

    q, kpad, vpad, conv = _inproj(x, norm_mix_g[0], w_in[0], conv_dw_w[0], conv_dw_b[0],
                                  conv_ln_g[0], conv_ln_b[0])
    attn = _attention(q, kpad, vpad, _band_bias(rel_bias[0]))
    x1, t, idx, wgt, rank, cnt = _mix_route(
        x.reshape(n_tok, D_MODEL), attn.reshape(n_tok, D_ATTN), conv.reshape(n_tok, D_CONV),
        w_out[0], norm_ffn_g[0], router_w[0], router_b[0])
    tile_flags, group_start = _tile_plan(cnt[:, 0].astype(I32), n_tiles)
    pos_flat = _slots(group_start, idx, rank).reshape(TOP_K * n_tok)
    sorted_t = _dispatch(t, pos_flat, n_tiles * EXP_ROWS)
    sorted_out = _experts(sorted_t, tile_flags,
                          exp_w_gate[0], exp_b_gate[0], exp_w_up[0], exp_b_up[0],
                          exp_w_down[0], exp_b_down[0], n_tiles)
    slabs = _gather_back(sorted_out, pos_flat)
    out = _combine(x1, slabs, wgt, norm_final_g)
    return out.reshape(bsz, seq, D_MODEL)
```

```python
import functools

import jax
import jax.numpy as jnp
from jax import lax
from jax.experimental import pallas as pl
from jax.experimental.pallas import tpu as pltpu
from jax.experimental.pallas import tpu_sc as plsc

F32 = jnp.float32
BF16 = jnp.bfloat16
I32 = jnp.int32

D_MODEL = 1024
CHUNK = 64
N_HEADS = 8
HEAD_DIM = 64
D_ATTN = N_HEADS * HEAD_DIM
LEFT_CHUNKS = 8
REL_MAX = 128
REL_MIN = -(CHUNK - 1)
D_CONV = D_MODEL - D_ATTN
CONV_WIDTH = 31
N_EXPERTS = 32
TOP_K = 4
SWIGLU_ALPHA = 1.702
SWIGLU_LIMIT = 7.0
RMS_EPS = 1e-5
LN_EPS = 1e-5

LEFT_PAD = LEFT_CHUNKS * CHUNK
IN_ROWS = 512
PAD_BLOCKS = LEFT_PAD // IN_ROWS
Q_ROWS = 2 * CHUNK
ATT_BLOCKS = 4
BAND_ROWS = Q_ROWS + LEFT_PAD
HEADS_PER_STEP = 4
GROUP_LANES = HEADS_PER_STEP * HEAD_DIM
MIX_ROWS = 1024
HALO_ROWS = 32
EXP_ROWS = 512
EXP_GRANULE = 128
EXP_TILES_PER_STEP = 2
DISPATCH_ROWS = 64
DISPATCH_BUFS = 2
GATHER_ROWS = 32
GATHER_BUFS = 4
NEG_BIG = -1e30
V7X_VMEM_BYTES = 64 * 1024 * 1024
VMEM_LIMIT = V7X_VMEM_BYTES - 8 * 1024 * 1024
V7X_SC_WORKERS = 32


HALF = D_MODEL // 2
HI_MASK = -65536


def _pack_rows(x):
    bits = lax.bitcast_convert_type(x.astype(BF16).astype(F32), I32)
    return lax.shift_right_logical(bits[:, :HALF], 16) | (bits[:, HALF:] & HI_MASK)


def _unpack_rows(w):
    lo = lax.bitcast_convert_type(lax.shift_left(w, 16), F32)
    hi = lax.bitcast_convert_type(w & HI_MASK, F32)
    return lo, hi


def _cast_rows(src_ref, dst_ref, rows, step=128):
    def body(c, carry):
        r = pl.multiple_of(c * step, step)
        dst_ref[pl.ds(r, step), :] = src_ref[pl.ds(r, step), :].astype(dst_ref.dtype)
        return carry
    lax.fori_loop(0, rows // step, body, 0)


def _conv_branch(hw_ref, sh_ref, rows, cw_ref, cb_ref, lg_ref, lb_ref, out_ref):
    off = HALO_ROWS - (CONV_WIDTH - 1)
    shift_rows = HALO_ROWS + rows - 8
    for b in range(1, 8):
        sh_ref[b - 1] = hw_ref[pl.ds(b, shift_rows), :]
    acc = None
    for j in range(CONV_WIDTH):
        a, b = divmod(off + j, 8)
        src = hw_ref if b == 0 else sh_ref.at[b - 1]
        term = src[pl.ds(8 * a, rows), :] * cw_ref[j:j + 1, :]
        acc = term if acc is None else acc + term
    acc = acc + cb_ref[...]
    mu = jnp.mean(acc, axis=-1, keepdims=True)
    d = acc - mu
    var = jnp.mean(d * d, axis=-1, keepdims=True)
    y = d * lax.rsqrt(var + LN_EPS) * lg_ref[...] + lb_ref[...]
    out_ref[...] = (y * jax.nn.sigmoid(y)).astype(out_ref.dtype)


def _inproj_kernel(x_ref, g_ref, w_ref, cw_ref, cb_ref, lg_ref, lb_ref,
                   q_ref, k_ref, v_ref, c_ref, wbf_ref, hw_ref, sh_ref):
    b = pl.program_id(0)
    j = pl.program_id(1)

    @pl.when((b == 0) & (j == 0))
    def _():
        _cast_rows(w_ref, wbf_ref, D_MODEL)

    @pl.when(j < PAD_BLOCKS)
    def _():
        k_ref[...] = jnp.zeros_like(k_ref)
        v_ref[...] = jnp.zeros_like(v_ref)
        hw_ref[IN_ROWS:IN_ROWS + HALO_ROWS, :] = jnp.zeros((HALO_ROWS, D_CONV), F32)

    @pl.when(j >= PAD_BLOCKS)
    def _():
        x = x_ref[...]
        ms = jnp.mean(x * x, axis=-1, keepdims=True)
        hb = ((x * lax.rsqrt(ms + RMS_EPS)) * g_ref[...]).astype(BF16)

        def proj(c0, width):
            return jnp.dot(hb, wbf_ref[:, c0:c0 + width], preferred_element_type=F32)

        a = proj(3 * D_ATTN, D_CONV)
        gate = proj(3 * D_ATTN + D_CONV, D_CONV)
        hw_ref[0:HALO_ROWS, :] = hw_ref[IN_ROWS:IN_ROWS + HALO_ROWS, :]
        hw_ref[HALO_ROWS:HALO_ROWS + IN_ROWS, :] = a * jax.nn.sigmoid(gate)
        _conv_branch(hw_ref, sh_ref, IN_ROWS, cw_ref, cb_ref, lg_ref, lb_ref, c_ref)

        q_ref[...] = (proj(0, D_ATTN) * (HEAD_DIM ** -0.5)).astype(BF16)
        k_ref[...] = proj(D_ATTN, D_ATTN).astype(BF16)
        v_ref[...] = proj(2 * D_ATTN, D_ATTN).astype(BF16)


def _inproj(x, norm_g, w_in, conv_w, conv_b, ln_g, ln_b):
    bsz, seq, _ = x.shape
    nblk = seq // IN_ROWS
    d_cols = w_in.shape[1]
    row_blk = lambda b, j: (b, jnp.maximum(j - PAD_BLOCKS, 0), 0)
    const = lambda b, j: (0, 0)
    vec = lambda v: v.reshape(1, -1)
    return pl.pallas_call(
        _inproj_kernel,
        grid=(bsz, nblk + PAD_BLOCKS),
        in_specs=[
            pl.BlockSpec((None, IN_ROWS, D_MODEL), row_blk),
            pl.BlockSpec((1, D_MODEL), const),
            pl.BlockSpec((D_MODEL, d_cols), const),
            pl.BlockSpec((CONV_WIDTH, D_CONV), const),
            pl.BlockSpec((1, D_CONV), const),
            pl.BlockSpec((1, D_CONV), const),
            pl.BlockSpec((1, D_CONV), const),
        ],
        out_specs=[
            pl.BlockSpec((None, IN_ROWS, D_ATTN), row_blk),
            pl.BlockSpec((None, IN_ROWS, D_ATTN), lambda b, j: (b, j, 0)),
            pl.BlockSpec((None, IN_ROWS, D_ATTN), lambda b, j: (b, j, 0)),
            pl.BlockSpec((None, IN_ROWS, D_CONV), row_blk),
        ],
        out_shape=[
            jax.ShapeDtypeStruct((bsz, seq, D_ATTN), BF16),
            jax.ShapeDtypeStruct((bsz, seq + LEFT_PAD, D_ATTN), BF16),
            jax.ShapeDtypeStruct((bsz, seq + LEFT_PAD, D_ATTN), BF16),
            jax.ShapeDtypeStruct((bsz, seq, D_CONV), BF16),
        ],
        scratch_shapes=[
            pltpu.VMEM((D_MODEL, d_cols), BF16),
            pltpu.VMEM((HALO_ROWS + IN_ROWS, D_CONV), F32),
            pltpu.VMEM((7, HALO_ROWS + IN_ROWS - 8, D_CONV), F32),
        ],
        compiler_params=pltpu.CompilerParams(
            dimension_semantics=("arbitrary", "arbitrary"), vmem_limit_bytes=VMEM_LIMIT),
        name="inproj",
    )(x, vec(norm_g), w_in, conv_w, vec(conv_b), vec(ln_g), vec(ln_b))


def _attn_kernel(q_ref, k_ref, v_ref, bias_ref, o_ref):
    i = pl.program_id(1)
    first_steps = -(-LEFT_PAD // (ATT_BLOCKS * Q_ROWS))

    @pl.when(i < first_steps)
    def _():
        _attn_step(i, True, q_ref, k_ref, v_ref, bias_ref, o_ref)

    @pl.when(i >= first_steps)
    def _():
        _attn_step(i, False, q_ref, k_ref, v_ref, bias_ref, o_ref)


def _attn_step(i, mask_start, q_ref, k_ref, v_ref, bias_ref, o_ref):
    lane = lax.broadcasted_iota(I32, (Q_ROWS, GROUP_LANES), 1) // HEAD_DIM
    col = lax.broadcasted_iota(I32, (HEADS_PER_STEP * Q_ROWS, BAND_ROWS), 1)
    for qb in range(ATT_BLOCKS):
        blk = i * ATT_BLOCKS + qb
        start = pl.multiple_of(blk * Q_ROWS, Q_ROWS)
        qrows = slice(qb * Q_ROWS, (qb + 1) * Q_ROWS)
        key_ok = col >= LEFT_PAD - blk * Q_ROWS
        for g in range(N_HEADS // HEADS_PER_STEP):
            lanes = slice(g * GROUP_LANES, (g + 1) * GROUP_LANES)
            q = q_ref[qrows, lanes]
            qs = jnp.concatenate(
                [jnp.where(lane == h, q, jnp.zeros_like(q)) for h in range(HEADS_PER_STEP)], axis=0)
            kb = k_ref[pl.ds(start, BAND_ROWS), lanes]
            vb = v_ref[pl.ds(start, BAND_ROWS), lanes]
            s = lax.dot_general(qs, kb, (((1,), (1,)), ((), ())), preferred_element_type=F32)
            rows = slice(g * HEADS_PER_STEP * Q_ROWS, (g + 1) * HEADS_PER_STEP * Q_ROWS)
            s = s + bias_ref[rows, :]
            if mask_start:
                s = jnp.where(key_ok, s, NEG_BIG)
            m = jnp.max(s, axis=-1, keepdims=True)
            p = jnp.exp(s - m)
            l = jnp.sum(p, axis=-1, keepdims=True)
            o = jnp.dot(p.astype(BF16), vb, preferred_element_type=F32) / l
            out = o[0:Q_ROWS]
            for h in range(1, HEADS_PER_STEP):
                out = jnp.where(lane == h, o[h * Q_ROWS:(h + 1) * Q_ROWS], out)
            o_ref[qrows, lanes] = out.astype(o_ref.dtype)


def _attention(q, kpad, vpad, bias):
    bsz, seq, _ = q.shape
    step_rows = ATT_BLOCKS * Q_ROWS
    return pl.pallas_call(
        _attn_kernel,
        grid=(bsz, seq // step_rows),
        in_specs=[
            pl.BlockSpec((None, step_rows, D_ATTN), lambda b, i: (b, i, 0)),
            pl.BlockSpec((None, seq + LEFT_PAD, D_ATTN), lambda b, i: (b, 0, 0)),
            pl.BlockSpec((None, seq + LEFT_PAD, D_ATTN), lambda b, i: (b, 0, 0)),
            pl.BlockSpec((N_HEADS * Q_ROWS, BAND_ROWS), lambda b, i: (0, 0)),
        ],
        out_specs=pl.BlockSpec((None, step_rows, D_ATTN), lambda b, i: (b, i, 0)),
        out_shape=jax.ShapeDtypeStruct((bsz, seq, D_ATTN), BF16),
        compiler_params=pltpu.CompilerParams(
            dimension_semantics=("arbitrary", "arbitrary"), vmem_limit_bytes=VMEM_LIMIT),
        name="chunk_attn",
    )(q, kpad, vpad, bias)


def _band_bias(rel_bias):
    n_rel = REL_MAX - REL_MIN + 1
    far = jnp.broadcast_to(rel_bias[:, n_rel - 1:n_rel], (N_HEADS, BAND_ROWS - 1 - REL_MAX))
    near = rel_bias[:, ::-1]
    ahead = jnp.broadcast_to(rel_bias[:, 0:1], (N_HEADS, Q_ROWS - 1 + REL_MIN))
    diag = jnp.concatenate([far, near, ahead], axis=1).astype(F32)
    bias = jnp.stack(
        [diag[:, Q_ROWS - 1 - r:Q_ROWS - 1 - r + BAND_ROWS] for r in range(Q_ROWS)], axis=1)
    r = jnp.arange(Q_ROWS)[:, None]
    m = jnp.arange(BAND_ROWS)[None, :]
    cq = r // CHUNK
    ck = m // CHUNK
    in_band = (ck >= cq) & (ck <= cq + LEFT_CHUNKS)
    bias = jnp.where(in_band[None], bias, NEG_BIG)
    return bias.reshape(N_HEADS * Q_ROWS, BAND_ROWS)


def _split_bf16(v):
    hi = v.astype(BF16)
    lo = (v - hi.astype(F32)).astype(BF16)
    return hi, lo


def _mix_route_kernel(x_ref, a_ref, c_ref, wo_ref, ng_ref, rwt_ref, rb_ref,
                      x1_ref, t_ref, idx_ref, wgt_ref, rank_ref, cnt_ref,
                      wobf_ref, cntacc_ref):
    i = pl.program_id(0)

    @pl.when(i == 0)
    def _():
        _cast_rows(wo_ref, wobf_ref, D_MODEL)
        cntacc_ref[...] = jnp.zeros_like(cntacc_ref)

    mix_in = jnp.concatenate([a_ref[...], c_ref[...]], axis=1)
    x1 = x_ref[...] + jnp.dot(mix_in, wobf_ref[...], preferred_element_type=F32)
    x1_ref[...] = x1
    ms = jnp.mean(x1 * x1, axis=-1, keepdims=True)
    t = (x1 * lax.rsqrt(ms + RMS_EPS)) * ng_ref[...]
    t_ref[...] = _pack_rows(t)

    nt = (((1,), (1,)), ((), ()))
    w_hi, w_lo = _split_bf16(rwt_ref[...])
    t_hi, t_lo = _split_bf16(t)
    both = lax.dot_general(jnp.concatenate([w_hi, w_lo], axis=0), t_hi, nt,
                           preferred_element_type=F32)
    logits = (both[:N_EXPERTS] + both[N_EXPERTS:]
              + lax.dot_general(w_hi, t_lo, nt, preferred_element_type=F32)) + rb_ref[...]
    e_iota = lax.broadcasted_iota(I32, (N_EXPERTS, MIX_ROWS), 0)
    vals, idxs, hots = [], [], []
    for _ in range(TOP_K):
        m = jnp.max(logits, axis=0, keepdims=True)
        am = jnp.min(jnp.where(logits == m, e_iota, N_EXPERTS), axis=0, keepdims=True)
        hot = e_iota == am
        vals.append(m)
        idxs.append(am)
        hots.append(hot)
        logits = jnp.where(hot, -jnp.inf, logits)
    exps = [jnp.exp(v - vals[0]) for v in vals]
    den = exps[0] + exps[1] + exps[2] + exps[3]
    wts = [e / den for e in exps]

    hot_f = (hots[0] | hots[1] | hots[2] | hots[3]).astype(F32)
    ra = lax.broadcasted_iota(I32, (MIX_ROWS, MIX_ROWS), 0)
    rc = lax.broadcasted_iota(I32, (MIX_ROWS, MIX_ROWS), 1)
    upper = (ra < rc).astype(BF16)
    prefix = jnp.dot(hot_f.astype(BF16), upper, preferred_element_type=F32)
    base = prefix + cntacc_ref[...]
    ranks = [jnp.sum(jnp.where(h, base, 0.0), axis=0, keepdims=True) for h in hots]
    counts = cntacc_ref[...] + jnp.sum(hot_f, axis=1, keepdims=True)
    cntacc_ref[...] = counts

    idx_ref[...] = jnp.concatenate(idxs, axis=0)
    rank_ref[...] = jnp.concatenate(ranks, axis=0).astype(I32)
    cnt_ref[...] = jnp.broadcast_to(counts, cnt_ref.shape)
    w8 = jnp.concatenate(wts + [jnp.zeros((8 - TOP_K, MIX_ROWS), F32)], axis=0)
    wgt_ref[...] = w8.T


def _mix_route(x2, attn2, conv2, w_out, norm_g, router_w, router_b):
    n_tok = x2.shape[0]
    row = lambda i: (i, 0)
    const = lambda i: (0, 0)
    vec = lambda v: v.reshape(1, -1)
    return pl.pallas_call(
        _mix_route_kernel,
        grid=(n_tok // MIX_ROWS,),
        in_specs=[
            pl.BlockSpec((MIX_ROWS, D_MODEL), row),
            pl.BlockSpec((MIX_ROWS, D_ATTN), row),
            pl.BlockSpec((MIX_ROWS, D_CONV), row),
            pl.BlockSpec((D_MODEL, D_MODEL), const),
            pl.BlockSpec((1, D_MODEL), const),
            pl.BlockSpec((N_EXPERTS, D_MODEL), const),
            pl.BlockSpec((N_EXPERTS, 1), const),
        ],
        out_specs=[
            pl.BlockSpec((MIX_ROWS, D_MODEL), row),
            pl.BlockSpec((MIX_ROWS, HALF), row),
            pl.BlockSpec((TOP_K, MIX_ROWS), lambda i: (0, i)),
            pl.BlockSpec((MIX_ROWS, 8), row),
            pl.BlockSpec((TOP_K, MIX_ROWS), lambda i: (0, i)),
            pl.BlockSpec((N_EXPERTS, 128), const),
        ],
        out_shape=[
            jax.ShapeDtypeStruct((n_tok, D_MODEL), F32),
            jax.ShapeDtypeStruct((n_tok, HALF), I32),
            jax.ShapeDtypeStruct((TOP_K, n_tok), I32),
            jax.ShapeDtypeStruct((n_tok, 8), F32),
            jax.ShapeDtypeStruct((TOP_K, n_tok), I32),
            jax.ShapeDtypeStruct((N_EXPERTS, 128), F32),
        ],
        scratch_shapes=[
            pltpu.VMEM((D_MODEL, D_MODEL), BF16),
            pltpu.VMEM((N_EXPERTS, 1), F32),
        ],
        compiler_params=pltpu.CompilerParams(
            dimension_semantics=("arbitrary",), vmem_limit_bytes=VMEM_LIMIT),
        name="mix_route",
    )(x2, attn2, conv2, w_out, vec(norm_g), router_w.T, router_b.reshape(N_EXPERTS, 1))


def _tile_plan(counts, n_tiles):
    tiles_per = (counts + EXP_ROWS - 1) // EXP_ROWS
    tile_end = jnp.cumsum(tiles_per)
    tile_begin = tile_end - tiles_per
    n_valid = tile_end[-1]
    tiles = jnp.arange(n_tiles, dtype=I32)
    tile_valid = tiles < n_valid
    capped = jnp.minimum(tiles, n_valid - 1)
    tile_expert = jnp.sum((capped[:, None] >= tile_end[None, :]).astype(I32), axis=1)
    tile_expert = jnp.minimum(tile_expert, N_EXPERTS - 1)
    is_first = (tiles[:, None] == tile_begin[None, :]) & (tiles_per[None, :] > 0)
    tile_first = tile_valid & jnp.any(is_first, axis=1)
    group_start = tile_begin * EXP_ROWS
    experts = jnp.arange(N_EXPERTS, dtype=I32)
    nonempty = tiles_per > 0
    parity = (jnp.cumsum(nonempty.astype(I32)) - 1) % 2
    later = nonempty[None, :] & (experts[None, :] > experts[:, None])
    nxt = jnp.min(jnp.where(later, experts[None, :], N_EXPERTS), axis=1)
    nxt = jnp.where(nxt == N_EXPERTS, -1, nxt)
    hot = (tile_expert[:, None] == experts[None, :]).astype(I32)
    tile_slot = jnp.sum(hot * parity[None, :], axis=1)
    tile_next = jnp.sum(hot * nxt[None, :], axis=1)
    group_rows = jnp.sum(hot * counts[None, :], axis=1)
    tile_in_group = tiles - jnp.sum(hot * tile_begin[None, :], axis=1)
    filled = jnp.clip(group_rows - tile_in_group * EXP_ROWS, 0, EXP_ROWS)
    filled = jnp.where(tile_valid, filled, 0)
    tile_rows = (filled + EXP_GRANULE - 1) // EXP_GRANULE * EXP_GRANULE
    flags = (tile_expert, tile_first.astype(I32), tile_rows.astype(I32),
             tile_slot.astype(I32), tile_next.astype(I32))
    return flags, group_start.astype(I32)


def _slot_kernel(gstart_ref, idx_ref, rank_ref, pos_ref):
    idx = idx_ref[...]
    pos = rank_ref[...]
    for e in range(N_EXPERTS):
        pos = pos + jnp.where(idx == e, gstart_ref[e], 0)
    pos_ref[...] = pos


def _slots(group_start, idx, rank):
    full = pl.BlockSpec(idx.shape, lambda i, gs: (0, 0))
    return pl.pallas_call(
        _slot_kernel,
        grid_spec=pltpu.PrefetchScalarGridSpec(
            num_scalar_prefetch=1, grid=(1,), in_specs=[full, full], out_specs=full),
        out_shape=jax.ShapeDtypeStruct(idx.shape, I32),
        name="slots",
    )(group_start, idx, rank)


def _sc_mesh():
    return plsc.VectorSubcoreMesh(core_axis_name="core", subcore_axis_name="subcore")


def _sc_worker():
    info = plsc.get_sparse_core_info()
    wid = lax.axis_index("subcore") * info.num_cores + lax.axis_index("core")
    return wid, info.num_cores * info.num_subcores


def _dispatch(t2, pos_flat, n_slots):
    n_tok, d = t2.shape
    n_workers = V7X_SC_WORKERS
    per_w = n_tok // n_workers
    group = DISPATCH_BUFS * DISPATCH_ROWS
    assert per_w % group == 0

    @functools.partial(
        pl.kernel, mesh=_sc_mesh(),
        out_type=jax.ShapeDtypeStruct((n_slots, d), t2.dtype),
        scratch_types=(
            [pltpu.VMEM((DISPATCH_ROWS,), I32) for _ in range(DISPATCH_BUFS * TOP_K)]
            + [pltpu.VMEM((DISPATCH_ROWS, d), t2.dtype) for _ in range(DISPATCH_BUFS)]
            + [pltpu.SemaphoreType.DMA for _ in range(DISPATCH_BUFS * (TOP_K + 1))]),
        name="dispatch",
    )
    def k(t_hbm, pos_hbm, o_hbm, *scratch):
        n_idx = DISPATCH_BUFS * TOP_K
        idx_v = scratch[:n_idx]
        rows_v = scratch[n_idx:n_idx + DISPATCH_BUFS]
        rsem = scratch[n_idx + DISPATCH_BUFS:n_idx + 2 * DISPATCH_BUFS]
        ssem = scratch[n_idx + 2 * DISPATCH_BUFS:]
        wid, nw = _sc_worker()
        assert nw == n_workers

        @pl.loop(0, per_w // group)
        def _(c):
            base = pl.multiple_of(wid * per_w + c * group, group)
            reads, scatters = [], []
            for b in range(DISPATCH_BUFS):
                rows = pl.ds(base + b * DISPATCH_ROWS, DISPATCH_ROWS)
                reads.append(pltpu.async_copy(t_hbm.at[rows], rows_v[b], rsem[b]))
            for b in range(DISPATCH_BUFS):
                for kk in range(TOP_K):
                    pltpu.sync_copy(
                        pos_hbm.at[pl.ds(kk * n_tok + base + b * DISPATCH_ROWS, DISPATCH_ROWS)],
                        idx_v[b * TOP_K + kk])
            for b in range(DISPATCH_BUFS):
                reads[b].wait()
                for kk in range(TOP_K):
                    i = b * TOP_K + kk
                    scatters.append(pltpu.async_copy(rows_v[b], o_hbm.at[idx_v[i]], ssem[i]))
            for cp in scatters:
                cp.wait()

    return k(t2, pos_flat)


def _gather_back(sorted_out, pos_flat):
    n_rows = pos_flat.shape[0]
    d = sorted_out.shape[1]
    n_workers = V7X_SC_WORKERS
    per_w = n_rows // n_workers
    group = GATHER_BUFS * GATHER_ROWS
    assert per_w % group == 0

    @functools.partial(
        pl.kernel, mesh=_sc_mesh(),
        out_type=jax.ShapeDtypeStruct((n_rows, d), sorted_out.dtype),
        scratch_types=(
            [pltpu.VMEM((GATHER_ROWS,), I32) for _ in range(GATHER_BUFS)]
            + [pltpu.VMEM((GATHER_ROWS, d), sorted_out.dtype) for _ in range(GATHER_BUFS)]
            + [pltpu.SemaphoreType.DMA for _ in range(2 * GATHER_BUFS)]),
        name="gather_back",
    )
    def k(s_hbm, pos_hbm, o_hbm, *scratch):
        idx_v = scratch[:GATHER_BUFS]
        rows_v = scratch[GATHER_BUFS:2 * GATHER_BUFS]
        gsem = scratch[2 * GATHER_BUFS:3 * GATHER_BUFS]
        wsem = scratch[3 * GATHER_BUFS:]
        wid, nw = _sc_worker()
        assert nw == n_workers

        @pl.loop(0, per_w // group)
        def _(c):
            base = pl.multiple_of(wid * per_w + c * group, group)
            gathers, writes = [], []
            for b in range(GATHER_BUFS):
                rows = pl.ds(base + b * GATHER_ROWS, GATHER_ROWS)
                pltpu.sync_copy(pos_hbm.at[rows], idx_v[b])
                gathers.append(pltpu.async_copy(s_hbm.at[idx_v[b]], rows_v[b], gsem[b]))
            for b in range(GATHER_BUFS):
                rows = pl.ds(base + b * GATHER_ROWS, GATHER_ROWS)
                gathers[b].wait()
                writes.append(pltpu.async_copy(rows_v[b], o_hbm.at[rows], wsem[b]))
            for b in range(GATHER_BUFS):
                writes[b].wait()

    return k(sorted_out, pos_flat)


def _weight_copies(w_hbm, wf32, sem, expert, slot):
    return [pltpu.make_async_copy(w.at[expert], wf32.at[slot, mtx], sem.at[slot])
            for mtx, w in enumerate(w_hbm)]


def _expert_rows(rows, x_ref, bg_ref, bu_ref, bd_ref, o_ref, wbf):
    xs = jnp.concatenate(_unpack_rows(x_ref[0:rows, :]), axis=1).astype(BF16)
    cn = 256
    hids = []
    for c in range(D_MODEL // cn):
        sl = slice(c * cn, (c + 1) * cn)
        g = jnp.dot(xs, wbf[0, :, sl], preferred_element_type=F32) + bg_ref[:, sl]
        u = jnp.dot(xs, wbf[1, :, sl], preferred_element_type=F32) + bu_ref[:, sl]
        g = jnp.minimum(g, SWIGLU_LIMIT)
        u = jnp.clip(u, -SWIGLU_LIMIT, SWIGLU_LIMIT)
        hids.append(((u + 1.0) * (g * jax.nn.sigmoid(SWIGLU_ALPHA * g))).astype(BF16))
    acc = jnp.dot(jnp.concatenate(hids, axis=1), wbf[2], preferred_element_type=F32)
    o_ref[0:rows, :] = _pack_rows(acc + bd_ref[...])
    if rows < EXP_ROWS:
        o_ref[rows:EXP_ROWS, :] = jnp.zeros((EXP_ROWS - rows, HALF), I32)


def _expert_kernel(texp_ref, tfirst_ref, trows_ref, tslot_ref, tnext_ref,
                   x_ref, wg_hbm, bg_ref, wu_hbm, bu_ref, wd_hbm, bd_ref,
                   o_ref, wf32, wbf, wsem):
    w_hbm = (wg_hbm, wu_hbm, wd_hbm)
    for sub in range(EXP_TILES_PER_STEP):
        i = pl.program_id(0) * EXP_TILES_PER_STEP + sub
        x_tile = x_ref.at[sub * EXP_ROWS:(sub + 1) * EXP_ROWS]
        o_tile = o_ref.at[sub * EXP_ROWS:(sub + 1) * EXP_ROWS]
        _expert_tile(i, texp_ref, tfirst_ref, trows_ref, tslot_ref, tnext_ref,
                     x_tile, w_hbm, bg_ref, bu_ref, bd_ref, o_tile, wf32, wbf, wsem)


def _expert_tile(i, texp_ref, tfirst_ref, trows_ref, tslot_ref, tnext_ref,
                 x_ref, w_hbm, bg_ref, bu_ref, bd_ref, o_ref, wf32, wbf, wsem):
    @pl.when(trows_ref[i] == 0)
    def _():
        o_ref[...] = jnp.zeros(o_ref.shape, o_ref.dtype)

    @pl.when(tfirst_ref[i] == 1)
    def _():
        slot = tslot_ref[i]
        expert = texp_ref[i]

        @pl.when(i == 0)
        def _():
            for cp in _weight_copies(w_hbm, wf32, wsem, expert, slot):
                cp.start()

        for cp in _weight_copies(w_hbm, wf32, wsem, expert, slot):
            cp.wait()

        @pl.when(tnext_ref[i] >= 0)
        def _():
            for cp in _weight_copies(w_hbm, wf32, wsem, tnext_ref[i], 1 - slot):
                cp.start()

        for mtx in range(3):
            _cast_rows(wf32.at[slot, mtx], wbf.at[mtx], D_MODEL)

    for rows in range(EXP_GRANULE, EXP_ROWS + 1, EXP_GRANULE):
        @pl.when(trows_ref[i] == rows)
        def _(rows=rows):
            expert = texp_ref[i]
            _expert_rows(rows, x_ref, bg_ref.at[expert], bu_ref.at[expert], bd_ref.at[expert],
                         o_ref, wbf)


def _experts(sorted_t, tile_flags, w_gate, b_gate, w_up, b_up, w_down, b_down, n_tiles):
    row_blk = pl.BlockSpec((EXP_TILES_PER_STEP * EXP_ROWS, HALF), lambda i, *_: (i, 0))
    w_any = pl.BlockSpec(memory_space=pl.ANY)
    b_blk = pl.BlockSpec((N_EXPERTS, 1, D_MODEL), lambda i, *_: (0, 0, 0))
    grid_spec = pltpu.PrefetchScalarGridSpec(
        num_scalar_prefetch=len(tile_flags),
        grid=(n_tiles // EXP_TILES_PER_STEP,),
        in_specs=[row_blk, w_any, b_blk, w_any, b_blk, w_any, b_blk],
        out_specs=row_blk,
        scratch_shapes=[
            pltpu.VMEM((2, 3, D_MODEL, D_MODEL), F32),
            pltpu.VMEM((3, D_MODEL, D_MODEL), BF16),
            pltpu.SemaphoreType.DMA((2,)),
        ],
    )
    b3 = lambda b: b.reshape(N_EXPERTS, 1, D_MODEL)
    return pl.pallas_call(
        _expert_kernel,
        grid_spec=grid_spec,
        out_shape=jax.ShapeDtypeStruct((n_tiles * EXP_ROWS, HALF), I32),
        compiler_params=pltpu.CompilerParams(
            dimension_semantics=("arbitrary",), vmem_limit_bytes=VMEM_LIMIT),
        name="experts",
    )(*tile_flags, sorted_t,
      w_gate, b3(b_gate), w_up, b3(b_up), w_down, b3(b_down))


def _combine_kernel(x1_ref, s0_ref, s1_ref, s2_ref, s3_ref, w_ref, g_ref, o_ref):
    w = w_ref[...]
    y_lo = x1_ref[:, :HALF]
    y_hi = x1_ref[:, HALF:]
    for k, s_ref in enumerate((s0_ref, s1_ref, s2_ref, s3_ref)):
        lo, hi = _unpack_rows(s_ref[...])
        y_lo = y_lo + w[:, k:k + 1] * lo
        y_hi = y_hi + w[:, k:k + 1] * hi
    sq = jnp.sum(y_lo * y_lo, axis=-1, keepdims=True) + jnp.sum(y_hi * y_hi, axis=-1, keepdims=True)
    scale = lax.rsqrt(sq * (1.0 / D_MODEL) + RMS_EPS)
    o_ref[:, :HALF] = (y_lo * scale) * g_ref[:, :HALF]
    o_ref[:, HALF:] = (y_hi * scale) * g_ref[:, HALF:]


def _combine(x1, slabs, wgt, norm_g):
    n_tok = x1.shape[0]
    rows = MIX_ROWS
    ntiles = n_tok // rows
    slab_spec = lambda k: pl.BlockSpec((rows, HALF), lambda i, k=k: (k * ntiles + i, 0))
    return pl.pallas_call(
        _combine_kernel,
        grid=(ntiles,),
        in_specs=[pl.BlockSpec((rows, D_MODEL), lambda i: (i, 0))]
        + [slab_spec(k) for k in range(TOP_K)]
        + [pl.BlockSpec((rows, 8), lambda i: (i, 0)),
           pl.BlockSpec((1, D_MODEL), lambda i: (0, 0))],
        out_specs=pl.BlockSpec((rows, D_MODEL), lambda i: (i, 0)),
        out_shape=jax.ShapeDtypeStruct((n_tok, D_MODEL), F32),
        compiler_params=pltpu.CompilerParams(
            dimension_semantics=("arbitrary",), vmem_limit_bytes=VMEM_LIMIT),
        name="combine",
    )(x1, slabs, slabs, slabs, slabs, wgt, norm_g.reshape(1, D_MODEL))


def kernel(x, norm_mix_g, w_in, conv_dw_w, conv_dw_b, conv_ln_g, conv_ln_b, rel_bias, w_out,
           norm_ffn_g, router_w, router_b, exp_w_gate, exp_b_gate, exp_w_up, exp_b_up,
           exp_w_down, exp_b_down, norm_final_g):
    bsz, seq, _ = x.shape
    n_tok = bsz * seq
    assert norm_mix_g.shape[0] == 1, "single-layer block"
    assert seq % IN_ROWS == 0 and seq % MIX_ROWS == 0 and LEFT_PAD % IN_ROWS == 0
    n_tiles = (TOP_K * n_tok) // EXP_ROWS + N_EXPERTS - 1
    n_tiles = -(-n_tiles // EXP_TILES_PER_STEP) * EXP_TILES_PER_STEP

    q, kpad, vpad, conv = _inproj(x, norm_mix_g[0], w_in[0], conv_dw_w[0], conv_dw_b[0],
                                  conv_ln_g[0], conv_ln_b[0])
    attn = _attention(q, kpad, vpad, _band_bias(rel_bias[0]))
    x1, t, idx, wgt, rank, cnt = _mix_route(
        x.reshape(n_tok, D_MODEL), attn.reshape(n_tok, D_ATTN), conv.reshape(n_tok, D_CONV),
        w_out[0], norm_ffn_g[0], router_w[0], router_b[0])
    tile_flags, group_start = _tile_plan(cnt[:, 0].astype(I32), n_tiles)
    pos_flat = _slots(group_start, idx, rank).reshape(TOP_K * n_tok)
    sorted_t = _dispatch(t, pos_flat, n_tiles * EXP_ROWS)
    sorted_out = _experts(sorted_t, tile_flags,
                          exp_w_gate[0], exp_b_gate[0], exp_w_up[0], exp_b_up[0],
                          exp_w_down[0], exp_b_down[0], n_tiles)
    slabs = _gather_back(sorted_out, pos_flat)
    out = _combine(x1, slabs, wgt, norm_final_g)
    return out.reshape(bsz, seq, D_MODEL)
```

```python
import functools

import jax
import jax.numpy as jnp
from jax import lax
from jax.experimental import pallas as pl
from jax.experimental.pallas import tpu as pltpu
from jax.experimental.pallas import tpu_sc as plsc

F32 = jnp.float32
BF16 = jnp.bfloat16
I32 = jnp.int32

D_MODEL = 1024
CHUNK = 64
N_HEADS = 8
HEAD_DIM = 64
D_ATTN = N_HEADS * HEAD_DIM
LEFT_CHUNKS = 8
REL_MAX = 128
REL_MIN = -(CHUNK - 1)
D_CONV = D_MODEL - D_ATTN
CONV_WIDTH = 31
N_EXPERTS = 32
TOP_K = 4
SWIGLU_ALPHA = 1.702
SWIGLU_LIMIT = 7.0
RMS_EPS = 1e-5
LN_EPS = 1e-5

LEFT_PAD = LEFT_CHUNKS * CHUNK
IN_ROWS = 512
PAD_BLOCKS = LEFT_PAD // IN_ROWS
Q_ROWS = 2 * CHUNK
ATT_BLOCKS = 4
BAND_ROWS = Q_ROWS + LEFT_PAD
HEADS_PER_STEP = 4
GROUP_LANES = HEADS_PER_STEP * HEAD_DIM
MIX_ROWS = 1024
HALO_ROWS = 32
EXP_ROWS = 512
EXP_GRANULE = 128
EXP_TILES_PER_STEP = 2
DISPATCH_ROWS = 64
DISPATCH_BUFS = 2
GATHER_ROWS = 32
GATHER_BUFS = 2
NEG_BIG = -1e30
V7X_VMEM_BYTES = 64 * 1024 * 1024
VMEM_LIMIT = V7X_VMEM_BYTES - 8 * 1024 * 1024
V7X_SC_WORKERS = 32


HALF = D_MODEL // 2
HI_MASK = -65536


def _pack_rows(x):
    bits = lax.bitcast_convert_type(x.astype(BF16).astype(F32), I32)
    return lax.shift_right_logical(bits[:, :HALF], 16) | (bits[:, HALF:] & HI_MASK)


def _unpack_rows(w):
    lo = lax.bitcast_convert_type(lax.shift_left(w, 16), F32)
    hi = lax.bitcast_convert_type(w & HI_MASK, F32)
    return lo, hi


def _cast_rows(src_ref, dst_ref, rows, step=128):
    def body(c, carry):
        r = pl.multiple_of(c * step, step)
        dst_ref[pl.ds(r, step), :] = src_ref[pl.ds(r, step), :].astype(dst_ref.dtype)
        return carry
    lax.fori_loop(0, rows // step, body, 0)


def _conv_branch(hw_ref, sh_ref, rows, cw_ref, cb_ref, lg_ref, lb_ref, out_ref):
    off = HALO_ROWS - (CONV_WIDTH - 1)
    shift_rows = HALO_ROWS + rows - 8
    for b in range(1, 8):
        sh_ref[b - 1] = hw_ref[pl.ds(b, shift_rows), :]
    acc = None
    for j in range(CONV_WIDTH):
        a, b = divmod(off + j, 8)
        src = hw_ref if b == 0 else sh_ref.at[b - 1]
        term = src[pl.ds(8 * a, rows), :] * cw_ref[j:j + 1, :]
        acc = term if acc is None else acc + term
    acc = acc + cb_ref[...]
    mu = jnp.mean(acc, axis=-1, keepdims=True)
    d = acc - mu
    var = jnp.mean(d * d, axis=-1, keepdims=True)
    y = d * lax.rsqrt(var + LN_EPS) * lg_ref[...] + lb_ref[...]
    out_ref[...] = (y * jax.nn.sigmoid(y)).astype(out_ref.dtype)


def _inproj_kernel(x_ref, g_ref, w_ref, cw_ref, cb_ref, lg_ref, lb_ref,
                   q_ref, k_ref, v_ref, c_ref, wbf_ref, hw_ref, sh_ref):
    b = pl.program_id(0)
    j = pl.program_id(1)

    @pl.when((b == 0) & (j == 0))
    def _():
        _cast_rows(w_ref, wbf_ref, D_MODEL)

    @pl.when(j < PAD_BLOCKS)
    def _():
        k_ref[...] = jnp.zeros_like(k_ref)
        v_ref[...] = jnp.zeros_like(v_ref)
        hw_ref[IN_ROWS:IN_ROWS + HALO_ROWS, :] = jnp.zeros((HALO_ROWS, D_CONV), F32)

    @pl.when(j >= PAD_BLOCKS)
    def _():
        x = x_ref[...]
        ms = jnp.mean(x * x, axis=-1, keepdims=True)
        hb = ((x * lax.rsqrt(ms + RMS_EPS)) * g_ref[...]).astype(BF16)

        def proj(c0, width):
            return jnp.dot(hb, wbf_ref[:, c0:c0 + width], preferred_element_type=F32)

        a = proj(3 * D_ATTN, D_CONV)
        gate = proj(3 * D_ATTN + D_CONV, D_CONV)
        hw_ref[0:HALO_ROWS, :] = hw_ref[IN_ROWS:IN_ROWS + HALO_ROWS, :]
        hw_ref[HALO_ROWS:HALO_ROWS + IN_ROWS, :] = a * jax.nn.sigmoid(gate)
        _conv_branch(hw_ref, sh_ref, IN_ROWS, cw_ref, cb_ref, lg_ref, lb_ref, c_ref)

        q_ref[...] = (proj(0, D_ATTN) * (HEAD_DIM ** -0.5)).astype(BF16)
        k_ref[...] = proj(D_ATTN, D_ATTN).astype(BF16)
        v_ref[...] = proj(2 * D_ATTN, D_ATTN).astype(BF16)


def _inproj(x, norm_g, w_in, conv_w, conv_b, ln_g, ln_b):
    bsz, seq, _ = x.shape
    nblk = seq // IN_ROWS
    d_cols = w_in.shape[1]
    row_blk = lambda b, j: (b, jnp.maximum(j - PAD_BLOCKS, 0), 0)
    const = lambda b, j: (0, 0)
    vec = lambda v: v.reshape(1, -1)
    return pl.pallas_call(
        _inproj_kernel,
        grid=(bsz, nblk + PAD_BLOCKS),
        in_specs=[
            pl.BlockSpec((None, IN_ROWS, D_MODEL), row_blk),
            pl.BlockSpec((1, D_MODEL), const),
            pl.BlockSpec((D_MODEL, d_cols), const),
            pl.BlockSpec((CONV_WIDTH, D_CONV), const),
            pl.BlockSpec((1, D_CONV), const),
            pl.BlockSpec((1, D_CONV), const),
            pl.BlockSpec((1, D_CONV), const),
        ],
        out_specs=[
            pl.BlockSpec((None, IN_ROWS, D_ATTN), row_blk),
            pl.BlockSpec((None, IN_ROWS, D_ATTN), lambda b, j: (b, j, 0)),
            pl.BlockSpec((None, IN_ROWS, D_ATTN), lambda b, j: (b, j, 0)),
            pl.BlockSpec((None, IN_ROWS, D_CONV), row_blk),
        ],
        out_shape=[
            jax.ShapeDtypeStruct((bsz, seq, D_ATTN), BF16),
            jax.ShapeDtypeStruct((bsz, seq + LEFT_PAD, D_ATTN), BF16),
            jax.ShapeDtypeStruct((bsz, seq + LEFT_PAD, D_ATTN), BF16),
            jax.ShapeDtypeStruct((bsz, seq, D_CONV), BF16),
        ],
        scratch_shapes=[
            pltpu.VMEM((D_MODEL, d_cols), BF16),
            pltpu.VMEM((HALO_ROWS + IN_ROWS, D_CONV), F32),
            pltpu.VMEM((7, HALO_ROWS + IN_ROWS - 8, D_CONV), F32),
        ],
        compiler_params=pltpu.CompilerParams(
            dimension_semantics=("arbitrary", "arbitrary"), vmem_limit_bytes=VMEM_LIMIT),
        name="inproj",
    )(x, vec(norm_g), w_in, conv_w, vec(conv_b), vec(ln_g), vec(ln_b))


def _attn_kernel(q_ref, k_ref, v_ref, bias_ref, o_ref):
    i = pl.program_id(1)
    first_steps = -(-LEFT_PAD // (ATT_BLOCKS * Q_ROWS))

    @pl.when(i < first_steps)
    def _():
        _attn_step(i, True, q_ref, k_ref, v_ref, bias_ref, o_ref)

    @pl.when(i >= first_steps)
    def _():
        _attn_step(i, False, q_ref, k_ref, v_ref, bias_ref, o_ref)


def _attn_step(i, mask_start, q_ref, k_ref, v_ref, bias_ref, o_ref):
    lane = lax.broadcasted_iota(I32, (Q_ROWS, GROUP_LANES), 1) // HEAD_DIM
    col = lax.broadcasted_iota(I32, (HEADS_PER_STEP * Q_ROWS, BAND_ROWS), 1)
    for qb in range(ATT_BLOCKS):
        blk = i * ATT_BLOCKS + qb
        start = pl.multiple_of(blk * Q_ROWS, Q_ROWS)
        qrows = slice(qb * Q_ROWS, (qb + 1) * Q_ROWS)
        key_ok = col >= LEFT_PAD - blk * Q_ROWS
        for g in range(N_HEADS // HEADS_PER_STEP):
            lanes = slice(g * GROUP_LANES, (g + 1) * GROUP_LANES)
            q = q_ref[qrows, lanes]
            qs = jnp.concatenate(
                [jnp.where(lane == h, q, jnp.zeros_like(q)) for h in range(HEADS_PER_STEP)], axis=0)
            kb = k_ref[pl.ds(start, BAND_ROWS), lanes]
            vb = v_ref[pl.ds(start, BAND_ROWS), lanes]
            s = lax.dot_general(qs, kb, (((1,), (1,)), ((), ())), preferred_element_type=F32)
            rows = slice(g * HEADS_PER_STEP * Q_ROWS, (g + 1) * HEADS_PER_STEP * Q_ROWS)
            s = s + bias_ref[rows, :]
            if mask_start:
                s = jnp.where(key_ok, s, NEG_BIG)
            m = jnp.max(s, axis=-1, keepdims=True)
            p = jnp.exp(s - m)
            l = jnp.sum(p, axis=-1, keepdims=True)
            o = jnp.dot(p.astype(BF16), vb, preferred_element_type=F32) / l
            out = o[0:Q_ROWS]
            for h in range(1, HEADS_PER_STEP):
                out = jnp.where(lane == h, o[h * Q_ROWS:(h + 1) * Q_ROWS], out)
            o_ref[qrows, lanes] = out.astype(o_ref.dtype)


def _attention(q, kpad, vpad, bias):
    bsz, seq, _ = q.shape
    step_rows = ATT_BLOCKS * Q_ROWS
    return pl.pallas_call(
        _attn_kernel,
        grid=(bsz, seq // step_rows),
        in_specs=[
            pl.BlockSpec((None, step_rows, D_ATTN), lambda b, i: (b, i, 0)),
            pl.BlockSpec((None, seq + LEFT_PAD, D_ATTN), lambda b, i: (b, 0, 0)),
            pl.BlockSpec((None, seq + LEFT_PAD, D_ATTN), lambda b, i: (b, 0, 0)),
            pl.BlockSpec((N_HEADS * Q_ROWS, BAND_ROWS), lambda b, i: (0, 0)),
        ],
        out_specs=pl.BlockSpec((None, step_rows, D_ATTN), lambda b, i: (b, i, 0)),
        out_shape=jax.ShapeDtypeStruct((bsz, seq, D_ATTN), BF16),
        compiler_params=pltpu.CompilerParams(
            dimension_semantics=("arbitrary", "arbitrary"), vmem_limit_bytes=VMEM_LIMIT),
        name="chunk_attn",
    )(q, kpad, vpad, bias)


def _band_bias(rel_bias):
    n_rel = REL_MAX - REL_MIN + 1
    far = jnp.broadcast_to(rel_bias[:, n_rel - 1:n_rel], (N_HEADS, BAND_ROWS - 1 - REL_MAX))
    near = rel_bias[:, ::-1]
    ahead = jnp.broadcast_to(rel_bias[:, 0:1], (N_HEADS, Q_ROWS - 1 + REL_MIN))
    diag = jnp.concatenate([far, near, ahead], axis=1).astype(F32)
    bias = jnp.stack(
        [diag[:, Q_ROWS - 1 - r:Q_ROWS - 1 - r + BAND_ROWS] for r in range(Q_ROWS)], axis=1)
    r = jnp.arange(Q_ROWS)[:, None]
    m = jnp.arange(BAND_ROWS)[None, :]
    cq = r // CHUNK
    ck = m // CHUNK
    in_band = (ck >= cq) & (ck <= cq + LEFT_CHUNKS)
    bias = jnp.where(in_band[None], bias, NEG_BIG)
    return bias.reshape(N_HEADS * Q_ROWS, BAND_ROWS)


def _split_bf16(v):
    hi = v.astype(BF16)
    lo = (v - hi.astype(F32)).astype(BF16)
    return hi, lo


def _mix_route_kernel(x_ref, a_ref, c_ref, wo_ref, ng_ref, rwt_ref, rb_ref,
                      x1_ref, t_ref, idx_ref, wgt_ref, rank_ref, cnt_ref,
                      wobf_ref, cntacc_ref):
    i = pl.program_id(0)

    @pl.when(i == 0)
    def _():
        _cast_rows(wo_ref, wobf_ref, D_MODEL)
        cntacc_ref[...] = jnp.zeros_like(cntacc_ref)

    mix_in = jnp.concatenate([a_ref[...], c_ref[...]], axis=1)
    x1 = x_ref[...] + jnp.dot(mix_in, wobf_ref[...], preferred_element_type=F32)
    x1_ref[...] = x1
    ms = jnp.mean(x1 * x1, axis=-1, keepdims=True)
    t = (x1 * lax.rsqrt(ms + RMS_EPS)) * ng_ref[...]
    t_ref[...] = _pack_rows(t)

    nt = (((1,), (1,)), ((), ()))
    w_hi, w_lo = _split_bf16(rwt_ref[...])
    t_hi, t_lo = _split_bf16(t)
    both = lax.dot_general(jnp.concatenate([w_hi, w_lo], axis=0), t_hi, nt,
                           preferred_element_type=F32)
    logits = (both[:N_EXPERTS] + both[N_EXPERTS:]
              + lax.dot_general(w_hi, t_lo, nt, preferred_element_type=F32)) + rb_ref[...]
    e_iota = lax.broadcasted_iota(I32, (N_EXPERTS, MIX_ROWS), 0)
    vals, idxs, hots = [], [], []
    for _ in range(TOP_K):
        m = jnp.max(logits, axis=0, keepdims=True)
        am = jnp.min(jnp.where(logits == m, e_iota, N_EXPERTS), axis=0, keepdims=True)
        hot = e_iota == am
        vals.append(m)
        idxs.append(am)
        hots.append(hot)
        logits = jnp.where(hot, -jnp.inf, logits)
    exps = [jnp.exp(v - vals[0]) for v in vals]
    den = exps[0] + exps[1] + exps[2] + exps[3]
    wts = [e / den for e in exps]

    hot_f = (hots[0] | hots[1] | hots[2] | hots[3]).astype(F32)
    ra = lax.broadcasted_iota(I32, (MIX_ROWS, MIX_ROWS), 0)
    rc = lax.broadcasted_iota(I32, (MIX_ROWS, MIX_ROWS), 1)
    upper = (ra < rc).astype(BF16)
    prefix = jnp.dot(hot_f.astype(BF16), upper, preferred_element_type=F32)
    base = prefix + cntacc_ref[...]
    ranks = [jnp.sum(jnp.where(h, base, 0.0), axis=0, keepdims=True) for h in hots]
    counts = cntacc_ref[...] + jnp.sum(hot_f, axis=1, keepdims=True)
    cntacc_ref[...] = counts

    idx_ref[...] = jnp.concatenate(idxs, axis=0)
    rank_ref[...] = jnp.concatenate(ranks, axis=0).astype(I32)
    cnt_ref[...] = jnp.broadcast_to(counts, cnt_ref.shape)
    w8 = jnp.concatenate(wts + [jnp.zeros((8 - TOP_K, MIX_ROWS), F32)], axis=0)
    wgt_ref[...] = w8.T


def _mix_route(x2, attn2, conv2, w_out, norm_g, router_w, router_b):
    n_tok = x2.shape[0]
    row = lambda i: (i, 0)
    const = lambda i: (0, 0)
    vec = lambda v: v.reshape(1, -1)
    return pl.pallas_call(
        _mix_route_kernel,
        grid=(n_tok // MIX_ROWS,),
        in_specs=[
            pl.BlockSpec((MIX_ROWS, D_MODEL), row),
            pl.BlockSpec((MIX_ROWS, D_ATTN), row),
            pl.BlockSpec((MIX_ROWS, D_CONV), row),
            pl.BlockSpec((D_MODEL, D_MODEL), const),
            pl.BlockSpec((1, D_MODEL), const),
            pl.BlockSpec((N_EXPERTS, D_MODEL), const),
            pl.BlockSpec((N_EXPERTS, 1), const),
        ],
        out_specs=[
            pl.BlockSpec((MIX_ROWS, D_MODEL), row),
            pl.BlockSpec((MIX_ROWS, HALF), row),
            pl.BlockSpec((TOP_K, MIX_ROWS), lambda i: (0, i)),
            pl.BlockSpec((MIX_ROWS, 8), row),
            pl.BlockSpec((TOP_K, MIX_ROWS), lambda i: (0, i)),
            pl.BlockSpec((N_EXPERTS, 128), const),
        ],
        out_shape=[
            jax.ShapeDtypeStruct((n_tok, D_MODEL), F32),
            jax.ShapeDtypeStruct((n_tok, HALF), I32),
            jax.ShapeDtypeStruct((TOP_K, n_tok), I32),
            jax.ShapeDtypeStruct((n_tok, 8), F32),
            jax.ShapeDtypeStruct((TOP_K, n_tok), I32),
            jax.ShapeDtypeStruct((N_EXPERTS, 128), F32),
        ],
        scratch_shapes=[
            pltpu.VMEM((D_MODEL, D_MODEL), BF16),
            pltpu.VMEM((N_EXPERTS, 1), F32),
        ],
        compiler_params=pltpu.CompilerParams(
            dimension_semantics=("arbitrary",), vmem_limit_bytes=VMEM_LIMIT),
        name="mix_route",
    )(x2, attn2, conv2, w_out, vec(norm_g), router_w.T, router_b.reshape(N_EXPERTS, 1))


def _tile_plan(counts, n_tiles):
    tiles_per = (counts + EXP_ROWS - 1) // EXP_ROWS
    tile_end = jnp.cumsum(tiles_per)
    tile_begin = tile_end - tiles_per
    n_valid = tile_end[-1]
    tiles = jnp.arange(n_tiles, dtype=I32)
    tile_valid = tiles < n_valid
    capped = jnp.minimum(tiles, n_valid - 1)
    tile_expert = jnp.sum((capped[:, None] >= tile_end[None, :]).astype(I32), axis=1)
    tile_expert = jnp.minimum(tile_expert, N_EXPERTS - 1)
    is_first = (tiles[:, None] == tile_begin[None, :]) & (tiles_per[None, :] > 0)
    tile_first = tile_valid & jnp.any(is_first, axis=1)
    group_start = tile_begin * EXP_ROWS
    experts = jnp.arange(N_EXPERTS, dtype=I32)
    nonempty = tiles_per > 0
    parity = (jnp.cumsum(nonempty.astype(I32)) - 1) % 2
    later = nonempty[None, :] & (experts[None, :] > experts[:, None])
    nxt = jnp.min(jnp.where(later, experts[None, :], N_EXPERTS), axis=1)
    nxt = jnp.where(nxt == N_EXPERTS, -1, nxt)
    hot = (tile_expert[:, None] == experts[None, :]).astype(I32)
    tile_slot = jnp.sum(hot * parity[None, :], axis=1)
    tile_next = jnp.sum(hot * nxt[None, :], axis=1)
    group_rows = jnp.sum(hot * counts[None, :], axis=1)
    tile_in_group = tiles - jnp.sum(hot * tile_begin[None, :], axis=1)
    filled = jnp.clip(group_rows - tile_in_group * EXP_ROWS, 0, EXP_ROWS)
    filled = jnp.where(tile_valid, filled, 0)
    tile_rows = (filled + EXP_GRANULE - 1) // EXP_GRANULE * EXP_GRANULE
    flags = (tile_expert, tile_first.astype(I32), tile_rows.astype(I32),
             tile_slot.astype(I32), tile_next.astype(I32))
    return flags, group_start.astype(I32)


def _slot_kernel(gstart_ref, idx_ref, rank_ref, pos_ref):
    idx = idx_ref[...]
    pos = rank_ref[...]
    for e in range(N_EXPERTS):
        pos = pos + jnp.where(idx == e, gstart_ref[e], 0)
    pos_ref[...] = pos


def _slots(group_start, idx, rank):
    full = pl.BlockSpec(idx.shape, lambda i, gs: (0, 0))
    return pl.pallas_call(
        _slot_kernel,
        grid_spec=pltpu.PrefetchScalarGridSpec(
            num_scalar_prefetch=1, grid=(1,), in_specs=[full, full], out_specs=full),
        out_shape=jax.ShapeDtypeStruct(idx.shape, I32),
        name="slots",
    )(group_start, idx, rank)


def _sc_mesh():
    return plsc.VectorSubcoreMesh(core_axis_name="core", subcore_axis_name="subcore")


def _sc_worker():
    info = plsc.get_sparse_core_info()
    wid = lax.axis_index("subcore") * info.num_cores + lax.axis_index("core")
    return wid, info.num_cores * info.num_subcores


def _dispatch(t2, pos_flat, n_slots):
    n_tok, d = t2.shape
    n_workers = V7X_SC_WORKERS
    per_w = n_tok // n_workers
    group = DISPATCH_BUFS * DISPATCH_ROWS
    assert per_w % group == 0

    @functools.partial(
        pl.kernel, mesh=_sc_mesh(),
        out_type=jax.ShapeDtypeStruct((n_slots, d), t2.dtype),
        scratch_types=(
            [pltpu.VMEM((DISPATCH_ROWS,), I32) for _ in range(DISPATCH_BUFS * TOP_K)]
            + [pltpu.VMEM((DISPATCH_ROWS, d), t2.dtype) for _ in range(DISPATCH_BUFS)]
            + [pltpu.SemaphoreType.DMA for _ in range(DISPATCH_BUFS * (TOP_K + 1))]),
        name="dispatch",
    )
    def k(t_hbm, pos_hbm, o_hbm, *scratch):
        n_idx = DISPATCH_BUFS * TOP_K
        idx_v = scratch[:n_idx]
        rows_v = scratch[n_idx:n_idx + DISPATCH_BUFS]
        rsem = scratch[n_idx + DISPATCH_BUFS:n_idx + 2 * DISPATCH_BUFS]
        ssem = scratch[n_idx + 2 * DISPATCH_BUFS:]
        wid, nw = _sc_worker()
        assert nw == n_workers

        @pl.loop(0, per_w // group)
        def _(c):
            base = pl.multiple_of(wid * per_w + c * group, group)
            reads, scatters = [], []
            for b in range(DISPATCH_BUFS):
                rows = pl.ds(base + b * DISPATCH_ROWS, DISPATCH_ROWS)
                reads.append(pltpu.async_copy(t_hbm.at[rows], rows_v[b], rsem[b]))
            for b in range(DISPATCH_BUFS):
                for kk in range(TOP_K):
                    pltpu.sync_copy(
                        pos_hbm.at[pl.ds(kk * n_tok + base + b * DISPATCH_ROWS, DISPATCH_ROWS)],
                        idx_v[b * TOP_K + kk])
            for b in range(DISPATCH_BUFS):
                reads[b].wait()
                for kk in range(TOP_K):
                    i = b * TOP_K + kk
                    scatters.append(pltpu.async_copy(rows_v[b], o_hbm.at[idx_v[i]], ssem[i]))
            for cp in scatters:
                cp.wait()

    return k(t2, pos_flat)


def _gather_back(sorted_out, pos_flat):
    n_rows = pos_flat.shape[0]
    d = sorted_out.shape[1]
    n_workers = V7X_SC_WORKERS
    per_w = n_rows // n_workers
    group = GATHER_BUFS * GATHER_ROWS
    n_pairs = per_w // (2 * group)
    assert per_w % (2 * group) == 0

    @functools.partial(
        pl.kernel, mesh=_sc_mesh(),
        out_type=jax.ShapeDtypeStruct((n_rows, d), sorted_out.dtype),
        scratch_types=(
            [pltpu.VMEM((GATHER_ROWS,), I32) for _ in range(2 * GATHER_BUFS)]
            + [pltpu.VMEM((GATHER_ROWS, d), sorted_out.dtype) for _ in range(2 * GATHER_BUFS)]
            + [pltpu.SemaphoreType.DMA for _ in range(4 * GATHER_BUFS)]),
        name="gather_back",
    )
    def k(s_hbm, pos_hbm, o_hbm, *scratch):
        nb = 2 * GATHER_BUFS
        idx_v, rows_v = scratch[:nb], scratch[nb:2 * nb]
        gsem, wsem = scratch[2 * nb:3 * nb], scratch[3 * nb:]
        wid, nw = _sc_worker()
        assert nw == n_workers
        first = wid * per_w

        def bufs(s):
            return range(s * GATHER_BUFS, (s + 1) * GATHER_BUFS)

        def rows_of(g, b):
            return pl.ds(first + g * group + (b % GATHER_BUFS) * GATHER_ROWS, GATHER_ROWS)

        def gather(b):
            return pltpu.make_async_copy(s_hbm.at[idx_v[b]], rows_v[b], gsem[b])

        def write(g, b):
            return pltpu.make_async_copy(rows_v[b], o_hbm.at[rows_of(g, b)], wsem[b])

        def start_gathers(g, s):
            for b in bufs(s):
                pltpu.sync_copy(pos_hbm.at[rows_of(g, b)], idx_v[b])
                gather(b).start()

        def drain(g, s):
            for b in bufs(s):
                gather(b).wait()
                write(g, b).start()
            for b in bufs(s):
                write(g, b).wait()

        start_gathers(0, 0)

        @pl.loop(0, n_pairs)
        def _(t):
            start_gathers(2 * t + 1, 1)
            drain(2 * t, 0)

            @pl.when(t + 1 < n_pairs)
            def _():
                start_gathers(2 * t + 2, 0)

            drain(2 * t + 1, 1)

    return k(sorted_out, pos_flat)


def _weight_copies(w_hbm, wf32, sem, expert, slot):
    return [pltpu.make_async_copy(w.at[expert], wf32.at[slot, mtx], sem.at[slot])
            for mtx, w in enumerate(w_hbm)]


def _expert_rows(rows, x_ref, bg_ref, bu_ref, bd_ref, o_ref, wbf):
    xs = jnp.concatenate(_unpack_rows(x_ref[0:rows, :]), axis=1).astype(BF16)
    cn = 256
    hids = []
    for c in range(D_MODEL // cn):
        sl = slice(c * cn, (c + 1) * cn)
        g = jnp.dot(xs, wbf[0, :, sl], preferred_element_type=F32) + bg_ref[:, sl]
        u = jnp.dot(xs, wbf[1, :, sl], preferred_element_type=F32) + bu_ref[:, sl]
        g = jnp.minimum(g, SWIGLU_LIMIT)
        u = jnp.clip(u, -SWIGLU_LIMIT, SWIGLU_LIMIT)
        hids.append(((u + 1.0) * (g * jax.nn.sigmoid(SWIGLU_ALPHA * g))).astype(BF16))
    acc = jnp.dot(jnp.concatenate(hids, axis=1), wbf[2], preferred_element_type=F32)
    o_ref[0:rows, :] = _pack_rows(acc + bd_ref[...])
    if rows < EXP_ROWS:
        o_ref[rows:EXP_ROWS, :] = jnp.zeros((EXP_ROWS - rows, HALF), I32)


def _expert_kernel(texp_ref, tfirst_ref, trows_ref, tslot_ref, tnext_ref,
                   x_ref, wg_hbm, bg_ref, wu_hbm, bu_ref, wd_hbm, bd_ref,
                   o_ref, wf32, wbf, wsem):
    w_hbm = (wg_hbm, wu_hbm, wd_hbm)
    for sub in range(EXP_TILES_PER_STEP):
        i = pl.program_id(0) * EXP_TILES_PER_STEP + sub
        x_tile = x_ref.at[sub * EXP_ROWS:(sub + 1) * EXP_ROWS]
        o_tile = o_ref.at[sub * EXP_ROWS:(sub + 1) * EXP_ROWS]
        _expert_tile(i, texp_ref, tfirst_ref, trows_ref, tslot_ref, tnext_ref,
                     x_tile, w_hbm, bg_ref, bu_ref, bd_ref, o_tile, wf32, wbf, wsem)


def _expert_tile(i, texp_ref, tfirst_ref, trows_ref, tslot_ref, tnext_ref,
                 x_ref, w_hbm, bg_ref, bu_ref, bd_ref, o_ref, wf32, wbf, wsem):
    @pl.when(trows_ref[i] == 0)
    def _():
        o_ref[...] = jnp.zeros(o_ref.shape, o_ref.dtype)

    @pl.when(tfirst_ref[i] == 1)
    def _():
        slot = tslot_ref[i]
        expert = texp_ref[i]

        @pl.when(i == 0)
        def _():
            for cp in _weight_copies(w_hbm, wf32, wsem, expert, slot):
                cp.start()

        for cp in _weight_copies(w_hbm, wf32, wsem, expert, slot):
            cp.wait()

        @pl.when(tnext_ref[i] >= 0)
        def _():
            for cp in _weight_copies(w_hbm, wf32, wsem, tnext_ref[i], 1 - slot):
                cp.start()

        for mtx in range(3):
            _cast_rows(wf32.at[slot, mtx], wbf.at[mtx], D_MODEL)

    for rows in range(EXP_GRANULE, EXP_ROWS + 1, EXP_GRANULE):
        @pl.when(trows_ref[i] == rows)
        def _(rows=rows):
            expert = texp_ref[i]
            _expert_rows(rows, x_ref, bg_ref.at[expert], bu_ref.at[expert], bd_ref.at[expert],
                         o_ref, wbf)


def _experts(sorted_t, tile_flags, w_gate, b_gate, w_up, b_up, w_down, b_down, n_tiles):
    row_blk = pl.BlockSpec((EXP_TILES_PER_STEP * EXP_ROWS, HALF), lambda i, *_: (i, 0))
    w_any = pl.BlockSpec(memory_space=pl.ANY)
    b_blk = pl.BlockSpec((N_EXPERTS, 1, D_MODEL), lambda i, *_: (0, 0, 0))
    grid_spec = pltpu.PrefetchScalarGridSpec(
        num_scalar_prefetch=len(tile_flags),
        grid=(n_tiles // EXP_TILES_PER_STEP,),
        in_specs=[row_blk, w_any, b_blk, w_any, b_blk, w_any, b_blk],
        out_specs=row_blk,
        scratch_shapes=[
            pltpu.VMEM((2, 3, D_MODEL, D_MODEL), F32),
            pltpu.VMEM((3, D_MODEL, D_MODEL), BF16),
            pltpu.SemaphoreType.DMA((2,)),
        ],
    )
    b3 = lambda b: b.reshape(N_EXPERTS, 1, D_MODEL)
    return pl.pallas_call(
        _expert_kernel,
        grid_spec=grid_spec,
        out_shape=jax.ShapeDtypeStruct((n_tiles * EXP_ROWS, HALF), I32),
        compiler_params=pltpu.CompilerParams(
            dimension_semantics=("arbitrary",), vmem_limit_bytes=VMEM_LIMIT),
        name="experts",
    )(*tile_flags, sorted_t,
      w_gate, b3(b_gate), w_up, b3(b_up), w_down, b3(b_down))


def _combine_kernel(x1_ref, s0_ref, s1_ref, s2_ref, s3_ref, w_ref, g_ref, o_ref):
    w = w_ref[...]
    y_lo = x1_ref[:, :HALF]
    y_hi = x1_ref[:, HALF:]
    for k, s_ref in enumerate((s0_ref, s1_ref, s2_ref, s3_ref)):
        lo, hi = _unpack_rows(s_ref[...])
        y_lo = y_lo + w[:, k:k + 1] * lo
        y_hi = y_hi + w[:, k:k + 1] * hi
    sq = jnp.sum(y_lo * y_lo, axis=-1, keepdims=True) + jnp.sum(y_hi * y_hi, axis=-1, keepdims=True)
    scale = lax.rsqrt(sq * (1.0 / D_MODEL) + RMS_EPS)
    o_ref[:, :HALF] = (y_lo * scale) * g_ref[:, :HALF]
    o_ref[:, HALF:] = (y_hi * scale) * g_ref[:, HALF:]


def _combine(x1, slabs, wgt, norm_g):
    n_tok = x1.shape[0]
    rows = MIX_ROWS
    ntiles = n_tok // rows
    slab_spec = lambda k: pl.BlockSpec((rows, HALF), lambda i, k=k: (k * ntiles + i, 0))
    return pl.pallas_call(
        _combine_kernel,
        grid=(ntiles,),
        in_specs=[pl.BlockSpec((rows, D_MODEL), lambda i: (i, 0))]
        + [slab_spec(k) for k in range(TOP_K)]
        + [pl.BlockSpec((rows, 8), lambda i: (i, 0)),
           pl.BlockSpec((1, D_MODEL), lambda i: (0, 0))],
        out_specs=pl.BlockSpec((rows, D_MODEL), lambda i: (i, 0)),
        out_shape=jax.ShapeDtypeStruct((n_tok, D_MODEL), F32),
        compiler_params=pltpu.CompilerParams(
            dimension_semantics=("arbitrary",), vmem_limit_bytes=VMEM_LIMIT),
        name="combine",
    )(x1, slabs, slabs, slabs, slabs, wgt, norm_g.reshape(1, D_MODEL))


def kernel(x, norm_mix_g, w_in, conv_dw_w, conv_dw_b, conv_ln_g, conv_ln_b, rel_bias, w_out,
           norm_ffn_g, router_w, router_b, exp_w_gate, exp_b_gate, exp_w_up, exp_b_up,
           exp_w_down, exp_b_down, norm_final_g):
    bsz, seq, _ = x.shape
    n_tok = bsz * seq
    assert norm_mix_g.shape[0] == 1, "single-layer block"
    assert seq % IN_ROWS == 0 and seq % MIX_ROWS == 0 and LEFT_PAD % IN_ROWS == 0
    n_tiles = (TOP_K * n_tok) // EXP_ROWS + N_EXPERTS - 1
    n_tiles = -(-n_tiles // EXP_TILES_PER_STEP) * EXP_TILES_PER_STEP

    q, kpad, vpad, conv = _inproj(x, norm_mix_g[0], w_in[0], conv_dw_w[0], conv_dw_b[0],
                                  conv_ln_g[0], conv_ln_b[0])
    attn = _attention(q, kpad, vpad, _band_bias(rel_bias[0]))
    x1, t, idx, wgt, rank, cnt = _mix_route(
        x.reshape(n_tok, D_MODEL), attn.reshape(n_tok, D_ATTN), conv.reshape(n_tok, D_CONV),
        w_out[0], norm_ffn_g[0], router_w[0], router_b[0])
    tile_flags, group_start = _tile_plan(cnt[:, 0].astype(I32), n_tiles)
    pos_flat = _slots(group_start, idx, rank).reshape(TOP_K * n_tok)
    sorted_t = _dispatch(t, pos_flat, n_tiles * EXP_ROWS)
    sorted_out = _experts(sorted_t, tile_flags,
                          exp_w_gate[0], exp_b_gate[0], exp_w_up[0], exp_b_up[0],
                          exp_w_down[0], exp_b_down[0], n_tiles)
    slabs = _gather_back(sorted_out, pos_flat)
    out = _combine(x1, slabs, wgt, norm_final_g)
    return out.reshape(bsz, seq, D_MODEL)
```

```python
import functools

import jax
import jax.numpy as jnp
from jax import lax
from jax.experimental import pallas as pl
from jax.experimental.pallas import tpu as pltpu
from jax.experimental.pallas import tpu_sc as plsc

F32 = jnp.float32
BF16 = jnp.bfloat16
I32 = jnp.int32

D_MODEL = 1024
CHUNK = 64
N_HEADS = 8
HEAD_DIM = 64
D_ATTN = N_HEADS * HEAD_DIM
LEFT_CHUNKS = 8
REL_MAX = 128
REL_MIN = -(CHUNK - 1)
D_CONV = D_MODEL - D_ATTN
CONV_WIDTH = 31
N_EXPERTS = 32
TOP_K = 4
SWIGLU_ALPHA = 1.702
SWIGLU_LIMIT = 7.0
RMS_EPS = 1e-5
LN_EPS = 1e-5

LEFT_PAD = LEFT_CHUNKS * CHUNK
IN_ROWS = 512
PAD_BLOCKS = LEFT_PAD // IN_ROWS
Q_ROWS = 2 * CHUNK
ATT_BLOCKS = 4
BAND_ROWS = Q_ROWS + LEFT_PAD
HEADS_PER_STEP = 4
GROUP_LANES = HEADS_PER_STEP * HEAD_DIM
MIX_ROWS = 1024
HALO_ROWS = 32
EXP_ROWS = 512
EXP_GRANULE = 128
EXP_TILES_PER_STEP = 2
DISPATCH_ROWS = 64
GATHER_ROWS = 32
GATHER_BUFS = 2
NEG_BIG = -1e30
V7X_VMEM_BYTES = 64 * 1024 * 1024
VMEM_LIMIT = V7X_VMEM_BYTES - 8 * 1024 * 1024
V7X_SC_WORKERS = 32


HALF = D_MODEL // 2
HI_MASK = -65536


def _pack_rows(x):
    bits = lax.bitcast_convert_type(x.astype(BF16).astype(F32), I32)
    return lax.shift_right_logical(bits[:, :HALF], 16) | (bits[:, HALF:] & HI_MASK)


def _unpack_rows(w):
    lo = lax.bitcast_convert_type(lax.shift_left(w, 16), F32)
    hi = lax.bitcast_convert_type(w & HI_MASK, F32)
    return lo, hi


def _cast_rows(src_ref, dst_ref, rows, step=128):
    def body(c, carry):
        r = pl.multiple_of(c * step, step)
        dst_ref[pl.ds(r, step), :] = src_ref[pl.ds(r, step), :].astype(dst_ref.dtype)
        return carry
    lax.fori_loop(0, rows // step, body, 0)


def _conv_branch(hw_ref, sh_ref, rows, cw_ref, cb_ref, lg_ref, lb_ref, out_ref):
    off = HALO_ROWS - (CONV_WIDTH - 1)
    shift_rows = HALO_ROWS + rows - 8
    for b in range(1, 8):
        sh_ref[b - 1] = hw_ref[pl.ds(b, shift_rows), :]
    acc = None
    for j in range(CONV_WIDTH):
        a, b = divmod(off + j, 8)
        src = hw_ref if b == 0 else sh_ref.at[b - 1]
        term = src[pl.ds(8 * a, rows), :] * cw_ref[j:j + 1, :]
        acc = term if acc is None else acc + term
    acc = acc + cb_ref[...]
    mu = jnp.mean(acc, axis=-1, keepdims=True)
    d = acc - mu
    var = jnp.mean(d * d, axis=-1, keepdims=True)
    y = d * lax.rsqrt(var + LN_EPS) * lg_ref[...] + lb_ref[...]
    out_ref[...] = (y * jax.nn.sigmoid(y)).astype(out_ref.dtype)


def _inproj_kernel(x_ref, g_ref, w_ref, cw_ref, cb_ref, lg_ref, lb_ref,
                   q_ref, k_ref, v_ref, c_ref, wbf_ref, hw_ref, sh_ref):
    b = pl.program_id(0)
    j = pl.program_id(1)

    @pl.when((b == 0) & (j == 0))
    def _():
        _cast_rows(w_ref, wbf_ref, D_MODEL)

    @pl.when(j < PAD_BLOCKS)
    def _():
        k_ref[...] = jnp.zeros_like(k_ref)
        v_ref[...] = jnp.zeros_like(v_ref)
        hw_ref[IN_ROWS:IN_ROWS + HALO_ROWS, :] = jnp.zeros((HALO_ROWS, D_CONV), F32)

    @pl.when(j >= PAD_BLOCKS)
    def _():
        x = x_ref[...]
        ms = jnp.mean(x * x, axis=-1, keepdims=True)
        hb = ((x * lax.rsqrt(ms + RMS_EPS)) * g_ref[...]).astype(BF16)

        def proj(c0, width):
            return jnp.dot(hb, wbf_ref[:, c0:c0 + width], preferred_element_type=F32)

        a = proj(3 * D_ATTN, D_CONV)
        gate = proj(3 * D_ATTN + D_CONV, D_CONV)
        hw_ref[0:HALO_ROWS, :] = hw_ref[IN_ROWS:IN_ROWS + HALO_ROWS, :]
        hw_ref[HALO_ROWS:HALO_ROWS + IN_ROWS, :] = a * jax.nn.sigmoid(gate)
        _conv_branch(hw_ref, sh_ref, IN_ROWS, cw_ref, cb_ref, lg_ref, lb_ref, c_ref)

        q_ref[...] = (proj(0, D_ATTN) * (HEAD_DIM ** -0.5)).astype(BF16)
        k_ref[...] = proj(D_ATTN, D_ATTN).astype(BF16)
        v_ref[...] = proj(2 * D_ATTN, D_ATTN).astype(BF16)


def _inproj(x, norm_g, w_in, conv_w, conv_b, ln_g, ln_b):
    bsz, seq, _ = x.shape
    nblk = seq // IN_ROWS
    d_cols = w_in.shape[1]
    row_blk = lambda b, j: (b, jnp.maximum(j - PAD_BLOCKS, 0), 0)
    const = lambda b, j: (0, 0)
    vec = lambda v: v.reshape(1, -1)
    return pl.pallas_call(
        _inproj_kernel,
        grid=(bsz, nblk + PAD_BLOCKS),
        in_specs=[
            pl.BlockSpec((None, IN_ROWS, D_MODEL), row_blk),
            pl.BlockSpec((1, D_MODEL), const),
            pl.BlockSpec((D_MODEL, d_cols), const),
            pl.BlockSpec((CONV_WIDTH, D_CONV), const),
            pl.BlockSpec((1, D_CONV), const),
            pl.BlockSpec((1, D_CONV), const),
            pl.BlockSpec((1, D_CONV), const),
        ],
        out_specs=[
            pl.BlockSpec((None, IN_ROWS, D_ATTN), row_blk),
            pl.BlockSpec((None, IN_ROWS, D_ATTN), lambda b, j: (b, j, 0)),
            pl.BlockSpec((None, IN_ROWS, D_ATTN), lambda b, j: (b, j, 0)),
            pl.BlockSpec((None, IN_ROWS, D_CONV), row_blk),
        ],
        out_shape=[
            jax.ShapeDtypeStruct((bsz, seq, D_ATTN), BF16),
            jax.ShapeDtypeStruct((bsz, seq + LEFT_PAD, D_ATTN), BF16),
            jax.ShapeDtypeStruct((bsz, seq + LEFT_PAD, D_ATTN), BF16),
            jax.ShapeDtypeStruct((bsz, seq, D_CONV), BF16),
        ],
        scratch_shapes=[
            pltpu.VMEM((D_MODEL, d_cols), BF16),
            pltpu.VMEM((HALO_ROWS + IN_ROWS, D_CONV), F32),
            pltpu.VMEM((7, HALO_ROWS + IN_ROWS - 8, D_CONV), F32),
        ],
        compiler_params=pltpu.CompilerParams(
            dimension_semantics=("arbitrary", "arbitrary"), vmem_limit_bytes=VMEM_LIMIT),
        name="inproj",
    )(x, vec(norm_g), w_in, conv_w, vec(conv_b), vec(ln_g), vec(ln_b))


def _attn_kernel(q_ref, k_ref, v_ref, bias_ref, o_ref):
    i = pl.program_id(1)
    first_steps = -(-LEFT_PAD // (ATT_BLOCKS * Q_ROWS))

    @pl.when(i < first_steps)
    def _():
        _attn_step(i, True, q_ref, k_ref, v_ref, bias_ref, o_ref)

    @pl.when(i >= first_steps)
    def _():
        _attn_step(i, False, q_ref, k_ref, v_ref, bias_ref, o_ref)


def _attn_step(i, mask_start, q_ref, k_ref, v_ref, bias_ref, o_ref):
    lane = lax.broadcasted_iota(I32, (Q_ROWS, GROUP_LANES), 1) // HEAD_DIM
    col = lax.broadcasted_iota(I32, (HEADS_PER_STEP * Q_ROWS, BAND_ROWS), 1)
    for qb in range(ATT_BLOCKS):
        blk = i * ATT_BLOCKS + qb
        start = pl.multiple_of(blk * Q_ROWS, Q_ROWS)
        qrows = slice(qb * Q_ROWS, (qb + 1) * Q_ROWS)
        key_ok = col >= LEFT_PAD - blk * Q_ROWS
        for g in range(N_HEADS // HEADS_PER_STEP):
            lanes = slice(g * GROUP_LANES, (g + 1) * GROUP_LANES)
            q = q_ref[qrows, lanes]
            qs = jnp.concatenate(
                [jnp.where(lane == h, q, jnp.zeros_like(q)) for h in range(HEADS_PER_STEP)], axis=0)
            kb = k_ref[pl.ds(start, BAND_ROWS), lanes]
            vb = v_ref[pl.ds(start, BAND_ROWS), lanes]
            s = lax.dot_general(qs, kb, (((1,), (1,)), ((), ())), preferred_element_type=F32)
            rows = slice(g * HEADS_PER_STEP * Q_ROWS, (g + 1) * HEADS_PER_STEP * Q_ROWS)
            s = s + bias_ref[rows, :]
            if mask_start:
                s = jnp.where(key_ok, s, NEG_BIG)
            m = jnp.max(s, axis=-1, keepdims=True)
            p = jnp.exp(s - m)
            l = jnp.sum(p, axis=-1, keepdims=True)
            o = jnp.dot(p.astype(BF16), vb, preferred_element_type=F32) / l
            out = o[0:Q_ROWS]
            for h in range(1, HEADS_PER_STEP):
                out = jnp.where(lane == h, o[h * Q_ROWS:(h + 1) * Q_ROWS], out)
            o_ref[qrows, lanes] = out.astype(o_ref.dtype)


def _attention(q, kpad, vpad, bias):
    bsz, seq, _ = q.shape
    step_rows = ATT_BLOCKS * Q_ROWS
    return pl.pallas_call(
        _attn_kernel,
        grid=(bsz, seq // step_rows),
        in_specs=[
            pl.BlockSpec((None, step_rows, D_ATTN), lambda b, i: (b, i, 0)),
            pl.BlockSpec((None, seq + LEFT_PAD, D_ATTN), lambda b, i: (b, 0, 0)),
            pl.BlockSpec((None, seq + LEFT_PAD, D_ATTN), lambda b, i: (b, 0, 0)),
            pl.BlockSpec((N_HEADS * Q_ROWS, BAND_ROWS), lambda b, i: (0, 0)),
        ],
        out_specs=pl.BlockSpec((None, step_rows, D_ATTN), lambda b, i: (b, i, 0)),
        out_shape=jax.ShapeDtypeStruct((bsz, seq, D_ATTN), BF16),
        compiler_params=pltpu.CompilerParams(
            dimension_semantics=("arbitrary", "arbitrary"), vmem_limit_bytes=VMEM_LIMIT),
        name="chunk_attn",
    )(q, kpad, vpad, bias)


def _band_bias(rel_bias):
    n_rel = REL_MAX - REL_MIN + 1
    far = jnp.broadcast_to(rel_bias[:, n_rel - 1:n_rel], (N_HEADS, BAND_ROWS - 1 - REL_MAX))
    near = rel_bias[:, ::-1]
    ahead = jnp.broadcast_to(rel_bias[:, 0:1], (N_HEADS, Q_ROWS - 1 + REL_MIN))
    diag = jnp.concatenate([far, near, ahead], axis=1).astype(F32)
    bias = jnp.stack(
        [diag[:, Q_ROWS - 1 - r:Q_ROWS - 1 - r + BAND_ROWS] for r in range(Q_ROWS)], axis=1)
    r = jnp.arange(Q_ROWS)[:, None]
    m = jnp.arange(BAND_ROWS)[None, :]
    cq = r // CHUNK
    ck = m // CHUNK
    in_band = (ck >= cq) & (ck <= cq + LEFT_CHUNKS)
    bias = jnp.where(in_band[None], bias, NEG_BIG)
    return bias.reshape(N_HEADS * Q_ROWS, BAND_ROWS)


def _split_bf16(v):
    hi = v.astype(BF16)
    lo = (v - hi.astype(F32)).astype(BF16)
    return hi, lo


def _mix_route_kernel(x_ref, a_ref, c_ref, wo_ref, ng_ref, rwt_ref, rb_ref,
                      x1_ref, t_ref, idx_ref, wgt_ref, rank_ref, cnt_ref,
                      wobf_ref, cntacc_ref):
    i = pl.program_id(0)

    @pl.when(i == 0)
    def _():
        _cast_rows(wo_ref, wobf_ref, D_MODEL)
        cntacc_ref[...] = jnp.zeros_like(cntacc_ref)

    mix_in = jnp.concatenate([a_ref[...], c_ref[...]], axis=1)
    x1 = x_ref[...] + jnp.dot(mix_in, wobf_ref[...], preferred_element_type=F32)
    x1_ref[...] = x1
    ms = jnp.mean(x1 * x1, axis=-1, keepdims=True)
    t = (x1 * lax.rsqrt(ms + RMS_EPS)) * ng_ref[...]
    t_ref[...] = _pack_rows(t)

    nt = (((1,), (1,)), ((), ()))
    w_hi, w_lo = _split_bf16(rwt_ref[...])
    t_hi, t_lo = _split_bf16(t)
    both = lax.dot_general(jnp.concatenate([w_hi, w_lo], axis=0), t_hi, nt,
                           preferred_element_type=F32)
    logits = (both[:N_EXPERTS] + both[N_EXPERTS:]
              + lax.dot_general(w_hi, t_lo, nt, preferred_element_type=F32)) + rb_ref[...]
    e_iota = lax.broadcasted_iota(I32, (N_EXPERTS, MIX_ROWS), 0)
    vals, idxs, hots = [], [], []
    for _ in range(TOP_K):
        m = jnp.max(logits, axis=0, keepdims=True)
        am = jnp.min(jnp.where(logits == m, e_iota, N_EXPERTS), axis=0, keepdims=True)
        hot = e_iota == am
        vals.append(m)
        idxs.append(am)
        hots.append(hot)
        logits = jnp.where(hot, -jnp.inf, logits)
    exps = [jnp.exp(v - vals[0]) for v in vals]
    den = exps[0] + exps[1] + exps[2] + exps[3]
    wts = [e / den for e in exps]

    hot_f = (hots[0] | hots[1] | hots[2] | hots[3]).astype(F32)
    ra = lax.broadcasted_iota(I32, (MIX_ROWS, MIX_ROWS), 0)
    rc = lax.broadcasted_iota(I32, (MIX_ROWS, MIX_ROWS), 1)
    upper = (ra < rc).astype(BF16)
    prefix = jnp.dot(hot_f.astype(BF16), upper, preferred_element_type=F32)
    base = prefix + cntacc_ref[...]
    ranks = [jnp.sum(jnp.where(h, base, 0.0), axis=0, keepdims=True) for h in hots]
    counts = cntacc_ref[...] + jnp.sum(hot_f, axis=1, keepdims=True)
    cntacc_ref[...] = counts

    idx_ref[...] = jnp.concatenate(idxs, axis=0)
    rank_ref[...] = jnp.concatenate(ranks, axis=0).astype(I32)
    cnt_ref[...] = jnp.broadcast_to(counts, cnt_ref.shape)
    w8 = jnp.concatenate(wts + [jnp.zeros((8 - TOP_K, MIX_ROWS), F32)], axis=0)
    wgt_ref[...] = w8.T


def _mix_route(x2, attn2, conv2, w_out, norm_g, router_w, router_b):
    n_tok = x2.shape[0]
    row = lambda i: (i, 0)
    const = lambda i: (0, 0)
    vec = lambda v: v.reshape(1, -1)
    return pl.pallas_call(
        _mix_route_kernel,
        grid=(n_tok // MIX_ROWS,),
        in_specs=[
            pl.BlockSpec((MIX_ROWS, D_MODEL), row),
            pl.BlockSpec((MIX_ROWS, D_ATTN), row),
            pl.BlockSpec((MIX_ROWS, D_CONV), row),
            pl.BlockSpec((D_MODEL, D_MODEL), const),
            pl.BlockSpec((1, D_MODEL), const),
            pl.BlockSpec((N_EXPERTS, D_MODEL), const),
            pl.BlockSpec((N_EXPERTS, 1), const),
        ],
        out_specs=[
            pl.BlockSpec((MIX_ROWS, D_MODEL), row),
            pl.BlockSpec((MIX_ROWS, HALF), row),
            pl.BlockSpec((TOP_K, MIX_ROWS), lambda i: (0, i)),
            pl.BlockSpec((MIX_ROWS, 8), row),
            pl.BlockSpec((TOP_K, MIX_ROWS), lambda i: (0, i)),
            pl.BlockSpec((N_EXPERTS, 128), const),
        ],
        out_shape=[
            jax.ShapeDtypeStruct((n_tok, D_MODEL), F32),
            jax.ShapeDtypeStruct((n_tok, HALF), I32),
            jax.ShapeDtypeStruct((TOP_K, n_tok), I32),
            jax.ShapeDtypeStruct((n_tok, 8), F32),
            jax.ShapeDtypeStruct((TOP_K, n_tok), I32),
            jax.ShapeDtypeStruct((N_EXPERTS, 128), F32),
        ],
        scratch_shapes=[
            pltpu.VMEM((D_MODEL, D_MODEL), BF16),
            pltpu.VMEM((N_EXPERTS, 1), F32),
        ],
        compiler_params=pltpu.CompilerParams(
            dimension_semantics=("arbitrary",), vmem_limit_bytes=VMEM_LIMIT),
        name="mix_route",
    )(x2, attn2, conv2, w_out, vec(norm_g), router_w.T, router_b.reshape(N_EXPERTS, 1))


def _tile_plan(counts, n_tiles):
    tiles_per = (counts + EXP_ROWS - 1) // EXP_ROWS
    tile_end = jnp.cumsum(tiles_per)
    tile_begin = tile_end - tiles_per
    n_valid = tile_end[-1]
    tiles = jnp.arange(n_tiles, dtype=I32)
    tile_valid = tiles < n_valid
    capped = jnp.minimum(tiles, n_valid - 1)
    tile_expert = jnp.sum((capped[:, None] >= tile_end[None, :]).astype(I32), axis=1)
    tile_expert = jnp.minimum(tile_expert, N_EXPERTS - 1)
    is_first = (tiles[:, None] == tile_begin[None, :]) & (tiles_per[None, :] > 0)
    tile_first = tile_valid & jnp.any(is_first, axis=1)
    group_start = tile_begin * EXP_ROWS
    experts = jnp.arange(N_EXPERTS, dtype=I32)
    nonempty = tiles_per > 0
    parity = (jnp.cumsum(nonempty.astype(I32)) - 1) % 2
    later = nonempty[None, :] & (experts[None, :] > experts[:, None])
    nxt = jnp.min(jnp.where(later, experts[None, :], N_EXPERTS), axis=1)
    nxt = jnp.where(nxt == N_EXPERTS, -1, nxt)
    hot = (tile_expert[:, None] == experts[None, :]).astype(I32)
    tile_slot = jnp.sum(hot * parity[None, :], axis=1)
    tile_next = jnp.sum(hot * nxt[None, :], axis=1)
    group_rows = jnp.sum(hot * counts[None, :], axis=1)
    tile_in_group = tiles - jnp.sum(hot * tile_begin[None, :], axis=1)
    filled = jnp.clip(group_rows - tile_in_group * EXP_ROWS, 0, EXP_ROWS)
    filled = jnp.where(tile_valid, filled, 0)
    tile_rows = (filled + EXP_GRANULE - 1) // EXP_GRANULE * EXP_GRANULE
    flags = (tile_expert, tile_first.astype(I32), tile_rows.astype(I32),
             tile_slot.astype(I32), tile_next.astype(I32))
    return flags, group_start.astype(I32)


def _slot_kernel(gstart_ref, idx_ref, rank_ref, pos_ref):
    idx = idx_ref[...]
    pos = rank_ref[...]
    for e in range(N_EXPERTS):
        pos = pos + jnp.where(idx == e, gstart_ref[e], 0)
    pos_ref[...] = pos


def _slots(group_start, idx, rank):
    full = pl.BlockSpec(idx.shape, lambda i, gs: (0, 0))
    return pl.pallas_call(
        _slot_kernel,
        grid_spec=pltpu.PrefetchScalarGridSpec(
            num_scalar_prefetch=1, grid=(1,), in_specs=[full, full], out_specs=full),
        out_shape=jax.ShapeDtypeStruct(idx.shape, I32),
        name="slots",
    )(group_start, idx, rank)


def _sc_mesh():
    return plsc.VectorSubcoreMesh(core_axis_name="core", subcore_axis_name="subcore")


def _sc_worker():
    info = plsc.get_sparse_core_info()
    wid = lax.axis_index("subcore") * info.num_cores + lax.axis_index("core")
    return wid, info.num_cores * info.num_subcores


def _dispatch(t2, pos_flat, n_slots):
    n_tok, d = t2.shape
    n_workers = V7X_SC_WORKERS
    per_w = n_tok // n_workers
    n_pairs = per_w // (2 * DISPATCH_ROWS)
    assert per_w % (2 * DISPATCH_ROWS) == 0

    @functools.partial(
        pl.kernel, mesh=_sc_mesh(),
        out_type=jax.ShapeDtypeStruct((n_slots, d), t2.dtype),
        scratch_types=(
            [pltpu.VMEM((DISPATCH_ROWS,), I32) for _ in range(2 * TOP_K)]
            + [pltpu.VMEM((DISPATCH_ROWS, d), t2.dtype) for _ in range(2)]
            + [pltpu.SemaphoreType.DMA for _ in range(2 * (TOP_K + 1))]),
        name="dispatch",
    )
    def k(t_hbm, pos_hbm, o_hbm, *scratch):
        idx_v = scratch[:2 * TOP_K]
        rows_v = scratch[2 * TOP_K:2 * TOP_K + 2]
        rsem = scratch[2 * TOP_K + 2:2 * TOP_K + 4]
        ssem = scratch[2 * TOP_K + 4:]
        wid, nw = _sc_worker()
        assert nw == n_workers
        first = wid * per_w

        def read(g, s):
            rows = pl.ds(first + g * DISPATCH_ROWS, DISPATCH_ROWS)
            return pltpu.make_async_copy(t_hbm.at[rows], rows_v[s], rsem[s])

        def scatter_chunk(g, s):
            for kk in range(TOP_K):
                pltpu.sync_copy(
                    pos_hbm.at[pl.ds(kk * n_tok + first + g * DISPATCH_ROWS, DISPATCH_ROWS)],
                    idx_v[s * TOP_K + kk])
            read(g, s).wait()
            scatters = [pltpu.async_copy(rows_v[s], o_hbm.at[idx_v[s * TOP_K + kk]],
                                         ssem[s * TOP_K + kk]) for kk in range(TOP_K)]
            for cp in scatters:
                cp.wait()

        read(0, 0).start()

        @pl.loop(0, n_pairs)
        def _(t):
            read(2 * t + 1, 1).start()
            scatter_chunk(2 * t, 0)

            @pl.when(t + 1 < n_pairs)
            def _():
                read(2 * t + 2, 0).start()

            scatter_chunk(2 * t + 1, 1)

    return k(t2, pos_flat)


def _gather_back(sorted_out, pos_flat):
    n_rows = pos_flat.shape[0]
    d = sorted_out.shape[1]
    n_workers = V7X_SC_WORKERS
    per_w = n_rows // n_workers
    group = GATHER_BUFS * GATHER_ROWS
    n_pairs = per_w // (2 * group)
    assert per_w % (2 * group) == 0

    @functools.partial(
        pl.kernel, mesh=_sc_mesh(),
        out_type=jax.ShapeDtypeStruct((n_rows, d), sorted_out.dtype),
        scratch_types=(
            [pltpu.VMEM((GATHER_ROWS,), I32) for _ in range(2 * GATHER_BUFS)]
            + [pltpu.VMEM((GATHER_ROWS, d), sorted_out.dtype) for _ in range(2 * GATHER_BUFS)]
            + [pltpu.SemaphoreType.DMA for _ in range(4 * GATHER_BUFS)]),
        name="gather_back",
    )
    def k(s_hbm, pos_hbm, o_hbm, *scratch):
        nb = 2 * GATHER_BUFS
        idx_v, rows_v = scratch[:nb], scratch[nb:2 * nb]
        gsem, wsem = scratch[2 * nb:3 * nb], scratch[3 * nb:]
        wid, nw = _sc_worker()
        assert nw == n_workers
        first = wid * per_w

        def bufs(s):
            return range(s * GATHER_BUFS, (s + 1) * GATHER_BUFS)

        def rows_of(g, b):
            return pl.ds(first + g * group + (b % GATHER_BUFS) * GATHER_ROWS, GATHER_ROWS)

        def gather(b):
            return pltpu.make_async_copy(s_hbm.at[idx_v[b]], rows_v[b], gsem[b])

        def write(g, b):
            return pltpu.make_async_copy(rows_v[b], o_hbm.at[rows_of(g, b)], wsem[b])

        def start_gathers(g, s):
            for b in bufs(s):
                pltpu.sync_copy(pos_hbm.at[rows_of(g, b)], idx_v[b])
                gather(b).start()

        def drain(g, s):
            for b in bufs(s):
                gather(b).wait()
                write(g, b).start()
            for b in bufs(s):
                write(g, b).wait()

        start_gathers(0, 0)

        @pl.loop(0, n_pairs)
        def _(t):
            start_gathers(2 * t + 1, 1)
            drain(2 * t, 0)

            @pl.when(t + 1 < n_pairs)
            def _():
                start_gathers(2 * t + 2, 0)

            drain(2 * t + 1, 1)

    return k(sorted_out, pos_flat)


def _weight_copies(w_hbm, wf32, sem, expert, slot):
    return [pltpu.make_async_copy(w.at[expert], wf32.at[slot, mtx], sem.at[slot])
            for mtx, w in enumerate(w_hbm)]


def _expert_rows(rows, x_ref, bg_ref, bu_ref, bd_ref, o_ref, wbf):
    xs = jnp.concatenate(_unpack_rows(x_ref[0:rows, :]), axis=1).astype(BF16)
    cn = 256
    hids = []
    for c in range(D_MODEL // cn):
        sl = slice(c * cn, (c + 1) * cn)
        g = jnp.dot(xs, wbf[0, :, sl], preferred_element_type=F32) + bg_ref[:, sl]
        u = jnp.dot(xs, wbf[1, :, sl], preferred_element_type=F32) + bu_ref[:, sl]
        g = jnp.minimum(g, SWIGLU_LIMIT)
        u = jnp.clip(u, -SWIGLU_LIMIT, SWIGLU_LIMIT)
        hids.append(((u + 1.0) * (g * jax.nn.sigmoid(SWIGLU_ALPHA * g))).astype(BF16))
    acc = jnp.dot(jnp.concatenate(hids, axis=1), wbf[2], preferred_element_type=F32)
    o_ref[0:rows, :] = _pack_rows(acc + bd_ref[...])
    if rows < EXP_ROWS:
        o_ref[rows:EXP_ROWS, :] = jnp.zeros((EXP_ROWS - rows, HALF), I32)


def _expert_kernel(texp_ref, tfirst_ref, trows_ref, tslot_ref, tnext_ref,
                   x_ref, wg_hbm, bg_ref, wu_hbm, bu_ref, wd_hbm, bd_ref,
                   o_ref, wf32, wbf, wsem):
    w_hbm = (wg_hbm, wu_hbm, wd_hbm)
    for sub in range(EXP_TILES_PER_STEP):
        i = pl.program_id(0) * EXP_TILES_PER_STEP + sub
        x_tile = x_ref.at[sub * EXP_ROWS:(sub + 1) * EXP_ROWS]
        o_tile = o_ref.at[sub * EXP_ROWS:(sub + 1) * EXP_ROWS]
        _expert_tile(i, texp_ref, tfirst_ref, trows_ref, tslot_ref, tnext_ref,
                     x_tile, w_hbm, bg_ref, bu_ref, bd_ref, o_tile, wf32, wbf, wsem)


def _expert_tile(i, texp_ref, tfirst_ref, trows_ref, tslot_ref, tnext_ref,
                 x_ref, w_hbm, bg_ref, bu_ref, bd_ref, o_ref, wf32, wbf, wsem):
    @pl.when(trows_ref[i] == 0)
    def _():
        o_ref[...] = jnp.zeros(o_ref.shape, o_ref.dtype)

    @pl.when(tfirst_ref[i] == 1)
    def _():
        slot = tslot_ref[i]
        expert = texp_ref[i]

        @pl.when(i == 0)
        def _():
            for cp in _weight_copies(w_hbm, wf32, wsem, expert, slot):
                cp.start()

        for cp in _weight_copies(w_hbm, wf32, wsem, expert, slot):
            cp.wait()

        @pl.when(tnext_ref[i] >= 0)
        def _():
            for cp in _weight_copies(w_hbm, wf32, wsem, tnext_ref[i], 1 - slot):
                cp.start()

        for mtx in range(3):
            _cast_rows(wf32.at[slot, mtx], wbf.at[mtx], D_MODEL)

    for rows in range(EXP_GRANULE, EXP_ROWS + 1, EXP_GRANULE):
        @pl.when(trows_ref[i] == rows)
        def _(rows=rows):
            expert = texp_ref[i]
            _expert_rows(rows, x_ref, bg_ref.at[expert], bu_ref.at[expert], bd_ref.at[expert],
                         o_ref, wbf)


def _experts(sorted_t, tile_flags, w_gate, b_gate, w_up, b_up, w_down, b_down, n_tiles):
    row_blk = pl.BlockSpec((EXP_TILES_PER_STEP * EXP_ROWS, HALF), lambda i, *_: (i, 0))
    w_any = pl.BlockSpec(memory_space=pl.ANY)
    b_blk = pl.BlockSpec((N_EXPERTS, 1, D_MODEL), lambda i, *_: (0, 0, 0))
    grid_spec = pltpu.PrefetchScalarGridSpec(
        num_scalar_prefetch=len(tile_flags),
        grid=(n_tiles // EXP_TILES_PER_STEP,),
        in_specs=[row_blk, w_any, b_blk, w_any, b_blk, w_any, b_blk],
        out_specs=row_blk,
        scratch_shapes=[
            pltpu.VMEM((2, 3, D_MODEL, D_MODEL), F32),
            pltpu.VMEM((3, D_MODEL, D_MODEL), BF16),
            pltpu.SemaphoreType.DMA((2,)),
        ],
    )
    b3 = lambda b: b.reshape(N_EXPERTS, 1, D_MODEL)
    return pl.pallas_call(
        _expert_kernel,
        grid_spec=grid_spec,
        out_shape=jax.ShapeDtypeStruct((n_tiles * EXP_ROWS, HALF), I32),
        compiler_params=pltpu.CompilerParams(
            dimension_semantics=("arbitrary",), vmem_limit_bytes=VMEM_LIMIT),
        name="experts",
    )(*tile_flags, sorted_t,
      w_gate, b3(b_gate), w_up, b3(b_up), w_down, b3(b_down))


def _combine_kernel(x1_ref, s0_ref, s1_ref, s2_ref, s3_ref, w_ref, g_ref, o_ref):
    w = w_ref[...]
    y_lo = x1_ref[:, :HALF]
    y_hi = x1_ref[:, HALF:]
    for k, s_ref in enumerate((s0_ref, s1_ref, s2_ref, s3_ref)):
        lo, hi = _unpack_rows(s_ref[...])
        y_lo = y_lo + w[:, k:k + 1] * lo
        y_hi = y_hi + w[:, k:k + 1] * hi
    sq = jnp.sum(y_lo * y_lo, axis=-1, keepdims=True) + jnp.sum(y_hi * y_hi, axis=-1, keepdims=True)
    scale = lax.rsqrt(sq * (1.0 / D_MODEL) + RMS_EPS)
    o_ref[:, :HALF] = (y_lo * scale) * g_ref[:, :HALF]
    o_ref[:, HALF:] = (y_hi * scale) * g_ref[:, HALF:]


def _combine(x1, slabs, wgt, norm_g):
    n_tok = x1.shape[0]
    rows = MIX_ROWS
    ntiles = n_tok // rows
    slab_spec = lambda k: pl.BlockSpec((rows, HALF), lambda i, k=k: (k * ntiles + i, 0))
    return pl.pallas_call(
        _combine_kernel,
        grid=(ntiles,),
        in_specs=[pl.BlockSpec((rows, D_MODEL), lambda i: (i, 0))]
        + [slab_spec(k) for k in range(TOP_K)]
        + [pl.BlockSpec((rows, 8), lambda i: (i, 0)),
           pl.BlockSpec((1, D_MODEL), lambda i: (0, 0))],
        out_specs=pl.BlockSpec((rows, D_MODEL), lambda i: (i, 0)),
        out_shape=jax.ShapeDtypeStruct((n_tok, D_MODEL), F32),
        compiler_params=pltpu.CompilerParams(
            dimension_semantics=("arbitrary",), vmem_limit_bytes=VMEM_LIMIT),
        name="combine",
    )(x1, slabs, slabs, slabs, slabs, wgt, norm_g.reshape(1, D_MODEL))


def kernel(x, norm_mix_g, w_in, conv_dw_w, conv_dw_b, conv_ln_g, conv_ln_b, rel_bias, w_out,
           norm_ffn_g, router_w, router_b, exp_w_gate, exp_b_gate, exp_w_up, exp_b_up,
           exp_w_down, exp_b_down, norm_final_g):
    bsz, seq, _ = x.shape
    n_tok = bsz * seq
    assert norm_mix_g.shape[0] == 1, "single-layer block"
    assert seq % IN_ROWS == 0 and seq % MIX_ROWS == 0 and LEFT_PAD % IN_ROWS == 0
    n_tiles = (TOP_K * n_tok) // EXP_ROWS + N_EXPERTS - 1
    n_tiles = -(-n_tiles // EXP_TILES_PER_STEP) * EXP_TILES_PER_STEP

    q, kpad, vpad, conv = _inproj(x, norm_mix_g[0], w_in[0], conv_dw_w[0], conv_dw_b[0],
                                  conv_ln_g[0], conv_ln_b[0])
    attn = _attention(q, kpad, vpad, _band_bias(rel_bias[0]))
    x1, t, idx, wgt, rank, cnt = _mix_route(
        x.reshape(n_tok, D_MODEL), attn.reshape(n_tok, D_ATTN), conv.reshape(n_tok, D_CONV),
        w_out[0], norm_ffn_g[0], router_w[0], router_b[0])
    tile_flags, group_start = _tile_plan(cnt[:, 0].astype(I32), n_tiles)
    pos_flat = _slots(group_start, idx, rank).reshape(TOP_K * n_tok)
    sorted_t = _dispatch(t, pos_flat, n_tiles * EXP_ROWS)
    sorted_out = _experts(sorted_t, tile_flags,
                          exp_w_gate[0], exp_b_gate[0], exp_w_up[0], exp_b_up[0],
                          exp_w_down[0], exp_b_down[0], n_tiles)
    slabs = _gather_back(sorted_out, pos_flat)
    out = _combine(x1, slabs, wgt, norm_final_g)
    return out.reshape(bsz, seq, D_MODEL)
```

```python
import functools

import jax
import jax.numpy as jnp
from jax import lax
from jax.experimental import pallas as pl
from jax.experimental.pallas import tpu as pltpu
from jax.experimental.pallas import tpu_sc as plsc

F32 = jnp.float32
BF16 = jnp.bfloat16
I32 = jnp.int32

D_MODEL = 1024
CHUNK = 64
N_HEADS = 8
HEAD_DIM = 64
D_ATTN = N_HEADS * HEAD_DIM
LEFT_CHUNKS = 8
REL_MAX = 128
REL_MIN = -(CHUNK - 1)
D_CONV = D_MODEL - D_ATTN
CONV_WIDTH = 31
N_EXPERTS = 32
TOP_K = 4
SWIGLU_ALPHA = 1.702
SWIGLU_LIMIT = 7.0
RMS_EPS = 1e-5
LN_EPS = 1e-5

LEFT_PAD = LEFT_CHUNKS * CHUNK
IN_ROWS = 512
PAD_BLOCKS = LEFT_PAD // IN_ROWS
Q_ROWS = 2 * CHUNK
ATT_BLOCKS = 4
BAND_ROWS = Q_ROWS + LEFT_PAD
HEADS_PER_STEP = 4
GROUP_LANES = HEADS_PER_STEP * HEAD_DIM
MIX_ROWS = 1024
HALO_ROWS = 32
EXP_ROWS = 512
EXP_GRANULE = 128
EXP_TILES_PER_STEP = 2
DISPATCH_ROWS = 64
GATHER_ROWS = 32
GATHER_BUFS = 2
NEG_BIG = -1e30
V7X_VMEM_BYTES = 64 * 1024 * 1024
VMEM_LIMIT = V7X_VMEM_BYTES - 8 * 1024 * 1024
V7X_SC_WORKERS = 32


HALF = D_MODEL // 2
HI_MASK = -65536


def _pack_rows(x):
    bits = lax.bitcast_convert_type(x.astype(BF16).astype(F32), I32)
    return lax.shift_right_logical(bits[:, :HALF], 16) | (bits[:, HALF:] & HI_MASK)


def _unpack_rows(w):
    lo = lax.bitcast_convert_type(lax.shift_left(w, 16), F32)
    hi = lax.bitcast_convert_type(w & HI_MASK, F32)
    return lo, hi


def _cast_rows(src_ref, dst_ref, rows, step=128):
    def body(c, carry):
        r = pl.multiple_of(c * step, step)
        dst_ref[pl.ds(r, step), :] = src_ref[pl.ds(r, step), :].astype(dst_ref.dtype)
        return carry
    lax.fori_loop(0, rows // step, body, 0)


def _conv_branch(hw_ref, sh_ref, rows, cw_ref, cb_ref, lg_ref, lb_ref, out_ref):
    off = HALO_ROWS - (CONV_WIDTH - 1)
    shift_rows = HALO_ROWS + rows - 8
    for b in range(1, 8):
        sh_ref[b - 1] = hw_ref[pl.ds(b, shift_rows), :]
    acc = None
    for j in range(CONV_WIDTH):
        a, b = divmod(off + j, 8)
        src = hw_ref if b == 0 else sh_ref.at[b - 1]
        term = src[pl.ds(8 * a, rows), :] * cw_ref[j:j + 1, :]
        acc = term if acc is None else acc + term
    acc = acc + cb_ref[...]
    mu = jnp.mean(acc, axis=-1, keepdims=True)
    d = acc - mu
    var = jnp.mean(d * d, axis=-1, keepdims=True)
    y = d * lax.rsqrt(var + LN_EPS) * lg_ref[...] + lb_ref[...]
    out_ref[...] = (y * jax.nn.sigmoid(y)).astype(out_ref.dtype)


def _inproj_kernel(x_ref, g_ref, w_ref, cw_ref, cb_ref, lg_ref, lb_ref,
                   q_ref, k_ref, v_ref, c_ref, wbf_ref, hw_ref, sh_ref):
    b = pl.program_id(0)
    j = pl.program_id(1)

    @pl.when((b == 0) & (j == 0))
    def _():
        _cast_rows(w_ref, wbf_ref, D_MODEL)

    @pl.when(j < PAD_BLOCKS)
    def _():
        k_ref[...] = jnp.zeros_like(k_ref)
        v_ref[...] = jnp.zeros_like(v_ref)
        hw_ref[IN_ROWS:IN_ROWS + HALO_ROWS, :] = jnp.zeros((HALO_ROWS, D_CONV), F32)

    @pl.when(j >= PAD_BLOCKS)
    def _():
        x = x_ref[...]
        ms = jnp.mean(x * x, axis=-1, keepdims=True)
        hb = ((x * lax.rsqrt(ms + RMS_EPS)) * g_ref[...]).astype(BF16)

        def proj(c0, width):
            return jnp.dot(hb, wbf_ref[:, c0:c0 + width], preferred_element_type=F32)

        a = proj(3 * D_ATTN, D_CONV)
        gate = proj(3 * D_ATTN + D_CONV, D_CONV)
        hw_ref[0:HALO_ROWS, :] = hw_ref[IN_ROWS:IN_ROWS + HALO_ROWS, :]
        hw_ref[HALO_ROWS:HALO_ROWS + IN_ROWS, :] = a * jax.nn.sigmoid(gate)
        _conv_branch(hw_ref, sh_ref, IN_ROWS, cw_ref, cb_ref, lg_ref, lb_ref, c_ref)

        q_ref[...] = (proj(0, D_ATTN) * (HEAD_DIM ** -0.5)).astype(BF16)
        k_ref[...] = proj(D_ATTN, D_ATTN).astype(BF16)
        v_ref[...] = proj(2 * D_ATTN, D_ATTN).astype(BF16)


def _inproj(x, norm_g, w_in, conv_w, conv_b, ln_g, ln_b):
    bsz, seq, _ = x.shape
    nblk = seq // IN_ROWS
    d_cols = w_in.shape[1]
    row_blk = lambda b, j: (b, jnp.maximum(j - PAD_BLOCKS, 0), 0)
    const = lambda b, j: (0, 0)
    vec = lambda v: v.reshape(1, -1)
    return pl.pallas_call(
        _inproj_kernel,
        grid=(bsz, nblk + PAD_BLOCKS),
        in_specs=[
            pl.BlockSpec((None, IN_ROWS, D_MODEL), row_blk),
            pl.BlockSpec((1, D_MODEL), const),
            pl.BlockSpec((D_MODEL, d_cols), const),
            pl.BlockSpec((CONV_WIDTH, D_CONV), const),
            pl.BlockSpec((1, D_CONV), const),
            pl.BlockSpec((1, D_CONV), const),
            pl.BlockSpec((1, D_CONV), const),
        ],
        out_specs=[
            pl.BlockSpec((None, IN_ROWS, D_ATTN), row_blk),
            pl.BlockSpec((None, IN_ROWS, D_ATTN), lambda b, j: (b, j, 0)),
            pl.BlockSpec((None, IN_ROWS, D_ATTN), lambda b, j: (b, j, 0)),
            pl.BlockSpec((None, IN_ROWS, D_CONV), row_blk),
        ],
        out_shape=[
            jax.ShapeDtypeStruct((bsz, seq, D_ATTN), BF16),
            jax.ShapeDtypeStruct((bsz, seq + LEFT_PAD, D_ATTN), BF16),
            jax.ShapeDtypeStruct((bsz, seq + LEFT_PAD, D_ATTN), BF16),
            jax.ShapeDtypeStruct((bsz, seq, D_CONV), BF16),
        ],
        scratch_shapes=[
            pltpu.VMEM((D_MODEL, d_cols), BF16),
            pltpu.VMEM((HALO_ROWS + IN_ROWS, D_CONV), F32),
            pltpu.VMEM((7, HALO_ROWS + IN_ROWS - 8, D_CONV), F32),
        ],
        compiler_params=pltpu.CompilerParams(
            dimension_semantics=("arbitrary", "arbitrary"), vmem_limit_bytes=VMEM_LIMIT),
        name="inproj",
    )(x, vec(norm_g), w_in, conv_w, vec(conv_b), vec(ln_g), vec(ln_b))


def _attn_kernel(q_ref, k_ref, v_ref, bias_ref, o_ref):
    i = pl.program_id(1)
    first_steps = -(-LEFT_PAD // (ATT_BLOCKS * Q_ROWS))

    @pl.when(i < first_steps)
    def _():
        _attn_step(i, True, q_ref, k_ref, v_ref, bias_ref, o_ref)

    @pl.when(i >= first_steps)
    def _():
        _attn_step(i, False, q_ref, k_ref, v_ref, bias_ref, o_ref)


def _attn_step(i, mask_start, q_ref, k_ref, v_ref, bias_ref, o_ref):
    lane = lax.broadcasted_iota(I32, (Q_ROWS, GROUP_LANES), 1) // HEAD_DIM
    col = lax.broadcasted_iota(I32, (HEADS_PER_STEP * Q_ROWS, BAND_ROWS), 1)
    for qb in range(ATT_BLOCKS):
        blk = i * ATT_BLOCKS + qb
        start = pl.multiple_of(blk * Q_ROWS, Q_ROWS)
        qrows = slice(qb * Q_ROWS, (qb + 1) * Q_ROWS)
        key_ok = col >= LEFT_PAD - blk * Q_ROWS
        for g in range(N_HEADS // HEADS_PER_STEP):
            lanes = slice(g * GROUP_LANES, (g + 1) * GROUP_LANES)
            q = q_ref[qrows, lanes]
            qs = jnp.concatenate(
                [jnp.where(lane == h, q, jnp.zeros_like(q)) for h in range(HEADS_PER_STEP)], axis=0)
            kb = k_ref[pl.ds(start, BAND_ROWS), lanes]
            vb = v_ref[pl.ds(start, BAND_ROWS), lanes]
            s = lax.dot_general(qs, kb, (((1,), (1,)), ((), ())), preferred_element_type=F32)
            rows = slice(g * HEADS_PER_STEP * Q_ROWS, (g + 1) * HEADS_PER_STEP * Q_ROWS)
            s = s + bias_ref[rows, :]
            if mask_start:
                s = jnp.where(key_ok, s, NEG_BIG)
            m = jnp.max(s, axis=-1, keepdims=True)
            p = jnp.exp(s - m)
            l = jnp.sum(p, axis=-1, keepdims=True)
            o = jnp.dot(p.astype(BF16), vb, preferred_element_type=F32) / l
            out = o[0:Q_ROWS]
            for h in range(1, HEADS_PER_STEP):
                out = jnp.where(lane == h, o[h * Q_ROWS:(h + 1) * Q_ROWS], out)
            o_ref[qrows, lanes] = out.astype(o_ref.dtype)


def _attention(q, kpad, vpad, bias):
    bsz, seq, _ = q.shape
    step_rows = ATT_BLOCKS * Q_ROWS
    return pl.pallas_call(
        _attn_kernel,
        grid=(bsz, seq // step_rows),
        in_specs=[
            pl.BlockSpec((None, step_rows, D_ATTN), lambda b, i: (b, i, 0)),
            pl.BlockSpec((None, seq + LEFT_PAD, D_ATTN), lambda b, i: (b, 0, 0)),
            pl.BlockSpec((None, seq + LEFT_PAD, D_ATTN), lambda b, i: (b, 0, 0)),
            pl.BlockSpec((N_HEADS * Q_ROWS, BAND_ROWS), lambda b, i: (0, 0)),
        ],
        out_specs=pl.BlockSpec((None, step_rows, D_ATTN), lambda b, i: (b, i, 0)),
        out_shape=jax.ShapeDtypeStruct((bsz, seq, D_ATTN), BF16),
        compiler_params=pltpu.CompilerParams(
            dimension_semantics=("arbitrary", "arbitrary"), vmem_limit_bytes=VMEM_LIMIT),
        name="chunk_attn",
    )(q, kpad, vpad, bias)


def _band_bias(rel_bias):
    n_rel = REL_MAX - REL_MIN + 1
    far = jnp.broadcast_to(rel_bias[:, n_rel - 1:n_rel], (N_HEADS, BAND_ROWS - 1 - REL_MAX))
    near = rel_bias[:, ::-1]
    ahead = jnp.broadcast_to(rel_bias[:, 0:1], (N_HEADS, Q_ROWS - 1 + REL_MIN))
    diag = jnp.concatenate([far, near, ahead], axis=1).astype(F32)
    bias = jnp.stack(
        [diag[:, Q_ROWS - 1 - r:Q_ROWS - 1 - r + BAND_ROWS] for r in range(Q_ROWS)], axis=1)
    r = jnp.arange(Q_ROWS)[:, None]
    m = jnp.arange(BAND_ROWS)[None, :]
    cq = r // CHUNK
    ck = m // CHUNK
    in_band = (ck >= cq) & (ck <= cq + LEFT_CHUNKS)
    bias = jnp.where(in_band[None], bias, NEG_BIG)
    return bias.reshape(N_HEADS * Q_ROWS, BAND_ROWS)


def _split_bf16(v):
    hi = v.astype(BF16)
    lo = (v - hi.astype(F32)).astype(BF16)
    return hi, lo


def _mix_route_kernel(x_ref, a_ref, c_ref, wo_ref, ng_ref, rwt_ref, rb_ref,
                      x1_ref, t_ref, idx_ref, wgt_ref, rank_ref, cnt_ref,
                      wobf_ref, cntacc_ref):
    i = pl.program_id(0)

    @pl.when(i == 0)
    def _():
        _cast_rows(wo_ref, wobf_ref, D_MODEL)
        cntacc_ref[...] = jnp.zeros_like(cntacc_ref)

    mix_in = jnp.concatenate([a_ref[...], c_ref[...]], axis=1)
    x1 = x_ref[...] + jnp.dot(mix_in, wobf_ref[...], preferred_element_type=F32)
    x1_ref[...] = x1
    ms = jnp.mean(x1 * x1, axis=-1, keepdims=True)
    t = (x1 * lax.rsqrt(ms + RMS_EPS)) * ng_ref[...]
    t_ref[...] = _pack_rows(t)

    nt = (((1,), (1,)), ((), ()))
    w_hi, w_lo = _split_bf16(rwt_ref[...])
    t_hi, t_lo = _split_bf16(t)
    both = lax.dot_general(jnp.concatenate([w_hi, w_lo], axis=0), t_hi, nt,
                           preferred_element_type=F32)
    logits = (both[:N_EXPERTS] + both[N_EXPERTS:]
              + lax.dot_general(w_hi, t_lo, nt, preferred_element_type=F32)) + rb_ref[...]
    e_iota = lax.broadcasted_iota(I32, (N_EXPERTS, MIX_ROWS), 0)
    vals, idxs, hots = [], [], []
    for _ in range(TOP_K):
        m = jnp.max(logits, axis=0, keepdims=True)
        am = jnp.min(jnp.where(logits == m, e_iota, N_EXPERTS), axis=0, keepdims=True)
        hot = e_iota == am
        vals.append(m)
        idxs.append(am)
        hots.append(hot)
        logits = jnp.where(hot, -jnp.inf, logits)
    exps = [jnp.exp(v - vals[0]) for v in vals]
    den = exps[0] + exps[1] + exps[2] + exps[3]
    wts = [e / den for e in exps]

    hot_f = (hots[0] | hots[1] | hots[2] | hots[3]).astype(F32)
    ra = lax.broadcasted_iota(I32, (MIX_ROWS, MIX_ROWS), 0)
    rc = lax.broadcasted_iota(I32, (MIX_ROWS, MIX_ROWS), 1)
    upper = (ra < rc).astype(BF16)
    prefix = jnp.dot(hot_f.astype(BF16), upper, preferred_element_type=F32)
    base = prefix + cntacc_ref[...]
    ranks = [jnp.sum(jnp.where(h, base, 0.0), axis=0, keepdims=True) for h in hots]
    counts = cntacc_ref[...] + jnp.sum(hot_f, axis=1, keepdims=True)
    cntacc_ref[...] = counts

    idx_ref[...] = jnp.concatenate(idxs, axis=0)
    rank_ref[...] = jnp.concatenate(ranks, axis=0).astype(I32)
    cnt_ref[...] = jnp.broadcast_to(counts, cnt_ref.shape)
    w8 = jnp.concatenate(wts + [jnp.zeros((8 - TOP_K, MIX_ROWS), F32)], axis=0)
    wgt_ref[...] = w8.T


def _mix_route(x2, attn2, conv2, w_out, norm_g, router_w, router_b):
    n_tok = x2.shape[0]
    row = lambda i: (i, 0)
    const = lambda i: (0, 0)
    vec = lambda v: v.reshape(1, -1)
    return pl.pallas_call(
        _mix_route_kernel,
        grid=(n_tok // MIX_ROWS,),
        in_specs=[
            pl.BlockSpec((MIX_ROWS, D_MODEL), row),
            pl.BlockSpec((MIX_ROWS, D_ATTN), row),
            pl.BlockSpec((MIX_ROWS, D_CONV), row),
            pl.BlockSpec((D_MODEL, D_MODEL), const),
            pl.BlockSpec((1, D_MODEL), const),
            pl.BlockSpec((N_EXPERTS, D_MODEL), const),
            pl.BlockSpec((N_EXPERTS, 1), const),
        ],
        out_specs=[
            pl.BlockSpec((MIX_ROWS, D_MODEL), row),
            pl.BlockSpec((MIX_ROWS, HALF), row),
            pl.BlockSpec((TOP_K, MIX_ROWS), lambda i: (0, i)),
            pl.BlockSpec((MIX_ROWS, 8), row),
            pl.BlockSpec((TOP_K, MIX_ROWS), lambda i: (0, i)),
            pl.BlockSpec((N_EXPERTS, 128), const),
        ],
        out_shape=[
            jax.ShapeDtypeStruct((n_tok, D_MODEL), F32),
            jax.ShapeDtypeStruct((n_tok, HALF), I32),
            jax.ShapeDtypeStruct((TOP_K, n_tok), I32),
            jax.ShapeDtypeStruct((n_tok, 8), F32),
            jax.ShapeDtypeStruct((TOP_K, n_tok), I32),
            jax.ShapeDtypeStruct((N_EXPERTS, 128), F32),
        ],
        scratch_shapes=[
            pltpu.VMEM((D_MODEL, D_MODEL), BF16),
            pltpu.VMEM((N_EXPERTS, 1), F32),
        ],
        compiler_params=pltpu.CompilerParams(
            dimension_semantics=("arbitrary",), vmem_limit_bytes=VMEM_LIMIT),
        name="mix_route",
    )(x2, attn2, conv2, w_out, vec(norm_g), router_w.T, router_b.reshape(N_EXPERTS, 1))


def _tile_plan(counts, n_tiles):
    tiles_per = (counts + EXP_ROWS - 1) // EXP_ROWS
    tile_end = jnp.cumsum(tiles_per)
    tile_begin = tile_end - tiles_per
    n_valid = tile_end[-1]
    tiles = jnp.arange(n_tiles, dtype=I32)
    tile_valid = tiles < n_valid
    capped = jnp.minimum(tiles, n_valid - 1)
    tile_expert = jnp.sum((capped[:, None] >= tile_end[None, :]).astype(I32), axis=1)
    tile_expert = jnp.minimum(tile_expert, N_EXPERTS - 1)
    is_first = (tiles[:, None] == tile_begin[None, :]) & (tiles_per[None, :] > 0)
    tile_first = tile_valid & jnp.any(is_first, axis=1)
    group_start = tile_begin * EXP_ROWS
    experts = jnp.arange(N_EXPERTS, dtype=I32)
    nonempty = tiles_per > 0
    parity = (jnp.cumsum(nonempty.astype(I32)) - 1) % 2
    later = nonempty[None, :] & (experts[None, :] > experts[:, None])
    nxt = jnp.min(jnp.where(later, experts[None, :], N_EXPERTS), axis=1)
    nxt = jnp.where(nxt == N_EXPERTS, -1, nxt)
    hot = (tile_expert[:, None] == experts[None, :]).astype(I32)
    tile_slot = jnp.sum(hot * parity[None, :], axis=1)
    tile_next = jnp.sum(hot * nxt[None, :], axis=1)
    group_rows = jnp.sum(hot * counts[None, :], axis=1)
    tile_in_group = tiles - jnp.sum(hot * tile_begin[None, :], axis=1)
    filled = jnp.clip(group_rows - tile_in_group * EXP_ROWS, 0, EXP_ROWS)
    filled = jnp.where(tile_valid, filled, 0)
    tile_rows = (filled + EXP_GRANULE - 1) // EXP_GRANULE * EXP_GRANULE
    flags = (tile_expert, tile_first.astype(I32), tile_rows.astype(I32),
             tile_slot.astype(I32), tile_next.astype(I32))
    return flags, group_start.astype(I32)


def _slot_kernel(gstart_ref, idx_ref, rank_ref, pos_ref):
    idx = idx_ref[...]
    pos = rank_ref[...]
    for e in range(N_EXPERTS):
        pos = pos + jnp.where(idx == e, gstart_ref[e], 0)
    pos_ref[...] = pos


def _slots(group_start, idx, rank):
    full = pl.BlockSpec(idx.shape, lambda i, gs: (0, 0))
    return pl.pallas_call(
        _slot_kernel,
        grid_spec=pltpu.PrefetchScalarGridSpec(
            num_scalar_prefetch=1, grid=(1,), in_specs=[full, full], out_specs=full),
        out_shape=jax.ShapeDtypeStruct(idx.shape, I32),
        name="slots",
    )(group_start, idx, rank)


def _sc_mesh():
    return plsc.VectorSubcoreMesh(core_axis_name="core", subcore_axis_name="subcore")


def _sc_worker():
    info = plsc.get_sparse_core_info()
    wid = lax.axis_index("subcore") * info.num_cores + lax.axis_index("core")
    return wid, info.num_cores * info.num_subcores


def _dispatch(t2, pos_flat, n_slots):
    n_tok, d = t2.shape
    n_workers = V7X_SC_WORKERS
    per_w = n_tok // n_workers
    n_pairs = per_w // (2 * DISPATCH_ROWS)
    assert per_w % (2 * DISPATCH_ROWS) == 0

    @functools.partial(
        pl.kernel, mesh=_sc_mesh(),
        out_type=jax.ShapeDtypeStruct((n_slots, d), t2.dtype),
        scratch_types=(
            [pltpu.VMEM((DISPATCH_ROWS,), I32) for _ in range(2 * TOP_K)]
            + [pltpu.VMEM((DISPATCH_ROWS, d), t2.dtype) for _ in range(2)]
            + [pltpu.SemaphoreType.DMA for _ in range(2 * (TOP_K + 1))]),
        name="dispatch",
    )
    def k(t_hbm, pos_hbm, o_hbm, *scratch):
        idx_v = scratch[:2 * TOP_K]
        rows_v = scratch[2 * TOP_K:2 * TOP_K + 2]
        rsem = scratch[2 * TOP_K + 2:2 * TOP_K + 4]
        ssem = scratch[2 * TOP_K + 4:]
        wid, nw = _sc_worker()
        assert nw == n_workers
        first = wid * per_w

        def read(g, s):
            rows = pl.ds(first + g * DISPATCH_ROWS, DISPATCH_ROWS)
            return pltpu.make_async_copy(t_hbm.at[rows], rows_v[s], rsem[s])

        def scatter_chunk(g, s):
            for kk in range(TOP_K):
                pltpu.sync_copy(
                    pos_hbm.at[pl.ds(kk * n_tok + first + g * DISPATCH_ROWS, DISPATCH_ROWS)],
                    idx_v[s * TOP_K + kk])
            read(g, s).wait()
            scatters = [pltpu.async_copy(rows_v[s], o_hbm.at[idx_v[s * TOP_K + kk]],
                                         ssem[s * TOP_K + kk]) for kk in range(TOP_K)]
            for cp in scatters:
                cp.wait()

        read(0, 0).start()

        @pl.loop(0, n_pairs)
        def _(t):
            read(2 * t + 1, 1).start()
            scatter_chunk(2 * t, 0)

            @pl.when(t + 1 < n_pairs)
            def _():
                read(2 * t + 2, 0).start()

            scatter_chunk(2 * t + 1, 1)

    return k(t2, pos_flat)


def _gather_back(sorted_out, pos_flat):
    n_rows = pos_flat.shape[0]
    d = sorted_out.shape[1]
    n_workers = V7X_SC_WORKERS
    per_w = n_rows // n_workers
    group = GATHER_BUFS * GATHER_ROWS
    n_pairs = per_w // (2 * group)
    assert per_w % (2 * group) == 0

    @functools.partial(
        pl.kernel, mesh=_sc_mesh(),
        out_type=jax.ShapeDtypeStruct((n_rows, d), sorted_out.dtype),
        scratch_types=(
            [pltpu.VMEM((per_w // GATHER_ROWS, GATHER_ROWS), I32)]
            + [pltpu.VMEM((GATHER_ROWS, d), sorted_out.dtype) for _ in range(2 * GATHER_BUFS)]
            + [pltpu.SemaphoreType.DMA for _ in range(4 * GATHER_BUFS)]),
        name="gather_back",
    )
    def k(s_hbm, pos_hbm, o_hbm, idx_all, *scratch):
        nb = 2 * GATHER_BUFS
        rows_v = scratch[:nb]
        gsem, wsem = scratch[nb:2 * nb], scratch[2 * nb:]
        wid, nw = _sc_worker()
        assert nw == n_workers
        first = wid * per_w

        def bufs(s):
            return range(s * GATHER_BUFS, (s + 1) * GATHER_BUFS)

        def rows_of(g, b):
            return pl.ds(first + g * group + (b % GATHER_BUFS) * GATHER_ROWS, GATHER_ROWS)

        def gather(g, b):
            idx = idx_all.at[g * GATHER_BUFS + b % GATHER_BUFS]
            return pltpu.make_async_copy(s_hbm.at[idx], rows_v[b], gsem[b])

        def write(g, b):
            return pltpu.make_async_copy(rows_v[b], o_hbm.at[rows_of(g, b)], wsem[b])

        def start_gathers(g, s):
            for b in bufs(s):
                gather(g, b).start()

        def drain(g, s):
            for b in bufs(s):
                gather(g, b).wait()
                write(g, b).start()
            for b in bufs(s):
                write(g, b).wait()

        rows_per_w = per_w // GATHER_ROWS
        pltpu.sync_copy(pos_hbm.at[pl.ds(wid * rows_per_w, rows_per_w)], idx_all)

        start_gathers(0, 0)

        @pl.loop(0, n_pairs)
        def _(t):
            start_gathers(2 * t + 1, 1)
            drain(2 * t, 0)

            @pl.when(t + 1 < n_pairs)
            def _():
                start_gathers(2 * t + 2, 0)

            drain(2 * t + 1, 1)

    return k(sorted_out, pos_flat.reshape(n_rows // GATHER_ROWS, GATHER_ROWS))


def _weight_copies(w_hbm, wf32, sem, expert, slot):
    return [pltpu.make_async_copy(w.at[expert], wf32.at[slot, mtx], sem.at[slot])
            for mtx, w in enumerate(w_hbm)]


def _expert_rows(rows, x_ref, bg_ref, bu_ref, bd_ref, o_ref, wbf):
    xs = jnp.concatenate(_unpack_rows(x_ref[0:rows, :]), axis=1).astype(BF16)
    cn = 256
    hids = []
    for c in range(D_MODEL // cn):
        sl = slice(c * cn, (c + 1) * cn)
        g = jnp.dot(xs, wbf[0, :, sl], preferred_element_type=F32) + bg_ref[:, sl]
        u = jnp.dot(xs, wbf[1, :, sl], preferred_element_type=F32) + bu_ref[:, sl]
        g = jnp.minimum(g, SWIGLU_LIMIT)
        u = jnp.clip(u, -SWIGLU_LIMIT, SWIGLU_LIMIT)
        hids.append(((u + 1.0) * (g * jax.nn.sigmoid(SWIGLU_ALPHA * g))).astype(BF16))
    acc = jnp.dot(jnp.concatenate(hids, axis=1), wbf[2], preferred_element_type=F32)
    o_ref[0:rows, :] = _pack_rows(acc + bd_ref[...])
    if rows < EXP_ROWS:
        o_ref[rows:EXP_ROWS, :] = jnp.zeros((EXP_ROWS - rows, HALF), I32)


def _expert_kernel(texp_ref, tfirst_ref, trows_ref, tslot_ref, tnext_ref,
                   x_ref, wg_hbm, bg_ref, wu_hbm, bu_ref, wd_hbm, bd_ref,
                   o_ref, wf32, wbf, wsem):
    w_hbm = (wg_hbm, wu_hbm, wd_hbm)
    for sub in range(EXP_TILES_PER_STEP):
        i = pl.program_id(0) * EXP_TILES_PER_STEP + sub
        x_tile = x_ref.at[sub * EXP_ROWS:(sub + 1) * EXP_ROWS]
        o_tile = o_ref.at[sub * EXP_ROWS:(sub + 1) * EXP_ROWS]
        _expert_tile(i, texp_ref, tfirst_ref, trows_ref, tslot_ref, tnext_ref,
                     x_tile, w_hbm, bg_ref, bu_ref, bd_ref, o_tile, wf32, wbf, wsem)


def _expert_tile(i, texp_ref, tfirst_ref, trows_ref, tslot_ref, tnext_ref,
                 x_ref, w_hbm, bg_ref, bu_ref, bd_ref, o_ref, wf32, wbf, wsem):
    @pl.when(trows_ref[i] == 0)
    def _():
        o_ref[...] = jnp.zeros(o_ref.shape, o_ref.dtype)

    @pl.when(tfirst_ref[i] == 1)
    def _():
        slot = tslot_ref[i]
        expert = texp_ref[i]

        @pl.when(i == 0)
        def _():
            for cp in _weight_copies(w_hbm, wf32, wsem, expert, slot):
                cp.start()

        for cp in _weight_copies(w_hbm, wf32, wsem, expert, slot):
            cp.wait()

        @pl.when(tnext_ref[i] >= 0)
        def _():
            for cp in _weight_copies(w_hbm, wf32, wsem, tnext_ref[i], 1 - slot):
                cp.start()

        for mtx in range(3):
            _cast_rows(wf32.at[slot, mtx], wbf.at[mtx], D_MODEL)

    for rows in range(EXP_GRANULE, EXP_ROWS + 1, EXP_GRANULE):
        @pl.when(trows_ref[i] == rows)
        def _(rows=rows):
            expert = texp_ref[i]
            _expert_rows(rows, x_ref, bg_ref.at[expert], bu_ref.at[expert], bd_ref.at[expert],
                         o_ref, wbf)


def _experts(sorted_t, tile_flags, w_gate, b_gate, w_up, b_up, w_down, b_down, n_tiles):
    row_blk = pl.BlockSpec((EXP_TILES_PER_STEP * EXP_ROWS, HALF), lambda i, *_: (i, 0))
    w_any = pl.BlockSpec(memory_space=pl.ANY)
    b_blk = pl.BlockSpec((N_EXPERTS, 1, D_MODEL), lambda i, *_: (0, 0, 0))
    grid_spec = pltpu.PrefetchScalarGridSpec(
        num_scalar_prefetch=len(tile_flags),
        grid=(n_tiles // EXP_TILES_PER_STEP,),
        in_specs=[row_blk, w_any, b_blk, w_any, b_blk, w_any, b_blk],
        out_specs=row_blk,
        scratch_shapes=[
            pltpu.VMEM((2, 3, D_MODEL, D_MODEL), F32),
            pltpu.VMEM((3, D_MODEL, D_MODEL), BF16),
            pltpu.SemaphoreType.DMA((2,)),
        ],
    )
    b3 = lambda b: b.reshape(N_EXPERTS, 1, D_MODEL)
    return pl.pallas_call(
        _expert_kernel,
        grid_spec=grid_spec,
        out_shape=jax.ShapeDtypeStruct((n_tiles * EXP_ROWS, HALF), I32),
        compiler_params=pltpu.CompilerParams(
            dimension_semantics=("arbitrary",), vmem_limit_bytes=VMEM_LIMIT),
        name="experts",
    )(*tile_flags, sorted_t,
      w_gate, b3(b_gate), w_up, b3(b_up), w_down, b3(b_down))


def _combine_kernel(x1_ref, s0_ref, s1_ref, s2_ref, s3_ref, w_ref, g_ref, o_ref):
    w = w_ref[...]
    y_lo = x1_ref[:, :HALF]
    y_hi = x1_ref[:, HALF:]
    for k, s_ref in enumerate((s0_ref, s1_ref, s2_ref, s3_ref)):
        lo, hi = _unpack_rows(s_ref[...])
        y_lo = y_lo + w[:, k:k + 1] * lo
        y_hi = y_hi + w[:, k:k + 1] * hi
    sq = jnp.sum(y_lo * y_lo, axis=-1, keepdims=True) + jnp.sum(y_hi * y_hi, axis=-1, keepdims=True)
    scale = lax.rsqrt(sq * (1.0 / D_MODEL) + RMS_EPS)
    o_ref[:, :HALF] = (y_lo * scale) * g_ref[:, :HALF]
    o_ref[:, HALF:] = (y_hi * scale) * g_ref[:, HALF:]


def _combine(x1, slabs, wgt, norm_g):
    n_tok = x1.shape[0]
    rows = MIX_ROWS
    ntiles = n_tok // rows
    slab_spec = lambda k: pl.BlockSpec((rows, HALF), lambda i, k=k: (k * ntiles + i, 0))
    return pl.pallas_call(
        _combine_kernel,
        grid=(ntiles,),
        in_specs=[pl.BlockSpec((rows, D_MODEL), lambda i: (i, 0))]
        + [slab_spec(k) for k in range(TOP_K)]
        + [pl.BlockSpec((rows, 8), lambda i: (i, 0)),
           pl.BlockSpec((1, D_MODEL), lambda i: (0, 0))],
        out_specs=pl.BlockSpec((rows, D_MODEL), lambda i: (i, 0)),
        out_shape=jax.ShapeDtypeStruct((n_tok, D_MODEL), F32),
        compiler_params=pltpu.CompilerParams(
            dimension_semantics=("arbitrary",), vmem_limit_bytes=VMEM_LIMIT),
        name="combine",
    )(x1, slabs, slabs, slabs, slabs, wgt, norm_g.reshape(1, D_MODEL))


def kernel(x, norm_mix_g, w_in, conv_dw_w, conv_dw_b, conv_ln_g, conv_ln_b, rel_bias, w_out,
           norm_ffn_g, router_w, router_b, exp_w_gate, exp_b_gate, exp_w_up, exp_b_up,
           exp_w_down, exp_b_down, norm_final_g):
    bsz, seq, _ = x.shape
    n_tok = bsz * seq
    assert norm_mix_g.shape[0] == 1, "single-layer block"
    assert seq % IN_ROWS == 0 and seq % MIX_ROWS == 0 and LEFT_PAD % IN_ROWS == 0
    n_tiles = (TOP_K * n_tok) // EXP_ROWS + N_EXPERTS - 1
    n_tiles = -(-n_tiles // EXP_TILES_PER_STEP) * EXP_TILES_PER_STEP

    q, kpad, vpad, conv = _inproj(x, norm_mix_g[0], w_in[0], conv_dw_w[0], conv_dw_b[0],
                                  conv_ln_g[0], conv_ln_b[0])
    attn = _attention(q, kpad, vpad, _band_bias(rel_bias[0]))
    x1, t, idx, wgt, rank, cnt = _mix_route(
        x.reshape(n_tok, D_MODEL), attn.reshape(n_tok, D_ATTN), conv.reshape(n_tok, D_CONV),
        w_out[0], norm_ffn_g[0], router_w[0], router_b[0])
    tile_flags, group_start = _tile_plan(cnt[:, 0].astype(I32), n_tiles)
    pos_flat = _slots(group_start, idx, rank).reshape(TOP_K * n_tok)
    sorted_t = _dispatch(t, pos_flat, n_tiles * EXP_ROWS)
    sorted_out = _experts(sorted_t, tile_flags,
                          exp_w_gate[0], exp_b_gate[0], exp_w_up[0], exp_b_up[0],
                          exp_w_down[0], exp_b_down[0], n_tiles)
    slabs = _gather_back(sorted_out, pos_flat)
    out = _combine(x1, slabs, wgt, norm_final_g)
    return out.reshape(bsz, seq, D_MODEL)
```

```python
import functools

import jax
import jax.numpy as jnp
from jax import lax
from jax.experimental import pallas as pl
from jax.experimental.pallas import tpu as pltpu
from jax.experimental.pallas import tpu_sc as plsc

F32 = jnp.float32
BF16 = jnp.bfloat16
I32 = jnp.int32

D_MODEL = 1024
CHUNK = 64
N_HEADS = 8
HEAD_DIM = 64
D_ATTN = N_HEADS * HEAD_DIM
LEFT_CHUNKS = 8
REL_MAX = 128
REL_MIN = -(CHUNK - 1)
D_CONV = D_MODEL - D_ATTN
CONV_WIDTH = 31
N_EXPERTS = 32
TOP_K = 4
SWIGLU_ALPHA = 1.702
SWIGLU_LIMIT = 7.0
RMS_EPS = 1e-5
LN_EPS = 1e-5

LEFT_PAD = LEFT_CHUNKS * CHUNK
IN_ROWS = 512
PAD_BLOCKS = LEFT_PAD // IN_ROWS
Q_ROWS = 2 * CHUNK
ATT_BLOCKS = 4
BAND_ROWS = Q_ROWS + LEFT_PAD
HEADS_PER_STEP = 4
GROUP_LANES = HEADS_PER_STEP * HEAD_DIM
MIX_ROWS = 1024
HALO_ROWS = 32
EXP_ROWS = 512
EXP_GRANULE = 128
EXP_TILES_PER_STEP = 2
DISPATCH_ROWS = 64
GATHER_ROWS = 32
GATHER_BUFS = 2
NEG_BIG = -1e30
V7X_VMEM_BYTES = 64 * 1024 * 1024
VMEM_LIMIT = V7X_VMEM_BYTES - 8 * 1024 * 1024
V7X_SC_WORKERS = 32


HALF = D_MODEL // 2
HI_MASK = -65536


def _pack_rows(x):
    bits = lax.bitcast_convert_type(x.astype(BF16).astype(F32), I32)
    return lax.shift_right_logical(bits[:, :HALF], 16) | (bits[:, HALF:] & HI_MASK)


def _unpack_rows(w):
    lo = lax.bitcast_convert_type(lax.shift_left(w, 16), F32)
    hi = lax.bitcast_convert_type(w & HI_MASK, F32)
    return lo, hi


def _cast_rows(src_ref, dst_ref, rows, step=128):
    def body(c, carry):
        r = pl.multiple_of(c * step, step)
        dst_ref[pl.ds(r, step), :] = src_ref[pl.ds(r, step), :].astype(dst_ref.dtype)
        return carry
    lax.fori_loop(0, rows // step, body, 0)


def _conv_branch(hw_ref, sh_ref, rows, cw_ref, cb_ref, lg_ref, lb_ref, out_ref):
    off = HALO_ROWS - (CONV_WIDTH - 1)
    shift_rows = HALO_ROWS + rows - 8
    for b in range(1, 8):
        sh_ref[b - 1] = hw_ref[pl.ds(b, shift_rows), :]
    acc = None
    for j in range(CONV_WIDTH):
        a, b = divmod(off + j, 8)
        src = hw_ref if b == 0 else sh_ref.at[b - 1]
        term = src[pl.ds(8 * a, rows), :] * cw_ref[j:j + 1, :]
        acc = term if acc is None else acc + term
    acc = acc + cb_ref[...]
    mu = jnp.mean(acc, axis=-1, keepdims=True)
    d = acc - mu
    var = jnp.mean(d * d, axis=-1, keepdims=True)
    y = d * lax.rsqrt(var + LN_EPS) * lg_ref[...] + lb_ref[...]
    out_ref[...] = (y * jax.nn.sigmoid(y)).astype(out_ref.dtype)


def _inproj_kernel(x_ref, g_ref, w_ref, cw_ref, cb_ref, lg_ref, lb_ref,
                   q_ref, k_ref, v_ref, c_ref, wbf_ref, hw_ref, sh_ref):
    b = pl.program_id(0)
    j = pl.program_id(1)

    @pl.when((b == 0) & (j == 0))
    def _():
        _cast_rows(w_ref, wbf_ref, D_MODEL)

    @pl.when(j < PAD_BLOCKS)
    def _():
        k_ref[...] = jnp.zeros_like(k_ref)
        v_ref[...] = jnp.zeros_like(v_ref)
        hw_ref[IN_ROWS:IN_ROWS + HALO_ROWS, :] = jnp.zeros((HALO_ROWS, D_CONV), F32)

    @pl.when(j >= PAD_BLOCKS)
    def _():
        x = x_ref[...]
        ms = jnp.mean(x * x, axis=-1, keepdims=True)
        hb = ((x * lax.rsqrt(ms + RMS_EPS)) * g_ref[...]).astype(BF16)

        def proj(c0, width):
            return jnp.dot(hb, wbf_ref[:, c0:c0 + width], preferred_element_type=F32)

        a = proj(3 * D_ATTN, D_CONV)
        gate = proj(3 * D_ATTN + D_CONV, D_CONV)
        hw_ref[0:HALO_ROWS, :] = hw_ref[IN_ROWS:IN_ROWS + HALO_ROWS, :]
        hw_ref[HALO_ROWS:HALO_ROWS + IN_ROWS, :] = a * jax.nn.sigmoid(gate)
        _conv_branch(hw_ref, sh_ref, IN_ROWS, cw_ref, cb_ref, lg_ref, lb_ref, c_ref)

        q_ref[...] = (proj(0, D_ATTN) * (HEAD_DIM ** -0.5)).astype(BF16)
        k_ref[...] = proj(D_ATTN, D_ATTN).astype(BF16)
        v_ref[...] = proj(2 * D_ATTN, D_ATTN).astype(BF16)


def _inproj(x, norm_g, w_in, conv_w, conv_b, ln_g, ln_b):
    bsz, seq, _ = x.shape
    nblk = seq // IN_ROWS
    d_cols = w_in.shape[1]
    row_blk = lambda b, j: (b, jnp.maximum(j - PAD_BLOCKS, 0), 0)
    const = lambda b, j: (0, 0)
    vec = lambda v: v.reshape(1, -1)
    return pl.pallas_call(
        _inproj_kernel,
        grid=(bsz, nblk + PAD_BLOCKS),
        in_specs=[
            pl.BlockSpec((None, IN_ROWS, D_MODEL), row_blk),
            pl.BlockSpec((1, D_MODEL), const),
            pl.BlockSpec((D_MODEL, d_cols), const),
            pl.BlockSpec((CONV_WIDTH, D_CONV), const),
            pl.BlockSpec((1, D_CONV), const),
            pl.BlockSpec((1, D_CONV), const),
            pl.BlockSpec((1, D_CONV), const),
        ],
        out_specs=[
            pl.BlockSpec((None, IN_ROWS, D_ATTN), row_blk),
            pl.BlockSpec((None, IN_ROWS, D_ATTN), lambda b, j: (b, j, 0)),
            pl.BlockSpec((None, IN_ROWS, D_ATTN), lambda b, j: (b, j, 0)),
            pl.BlockSpec((None, IN_ROWS, D_CONV), row_blk),
        ],
        out_shape=[
            jax.ShapeDtypeStruct((bsz, seq, D_ATTN), BF16),
            jax.ShapeDtypeStruct((bsz, seq + LEFT_PAD, D_ATTN), BF16),
            jax.ShapeDtypeStruct((bsz, seq + LEFT_PAD, D_ATTN), BF16),
            jax.ShapeDtypeStruct((bsz, seq, D_CONV), BF16),
        ],
        scratch_shapes=[
            pltpu.VMEM((D_MODEL, d_cols), BF16),
            pltpu.VMEM((HALO_ROWS + IN_ROWS, D_CONV), F32),
            pltpu.VMEM((7, HALO_ROWS + IN_ROWS - 8, D_CONV), F32),
        ],
        compiler_params=pltpu.CompilerParams(
            dimension_semantics=("arbitrary", "arbitrary"), vmem_limit_bytes=VMEM_LIMIT),
        name="inproj",
    )(x, vec(norm_g), w_in, conv_w, vec(conv_b), vec(ln_g), vec(ln_b))


def _attn_kernel(q_ref, k_ref, v_ref, bias_ref, o_ref):
    i = pl.program_id(1)
    first_steps = -(-LEFT_PAD // (ATT_BLOCKS * Q_ROWS))

    @pl.when(i < first_steps)
    def _():
        _attn_step(i, True, q_ref, k_ref, v_ref, bias_ref, o_ref)

    @pl.when(i >= first_steps)
    def _():
        _attn_step(i, False, q_ref, k_ref, v_ref, bias_ref, o_ref)


def _attn_step(i, mask_start, q_ref, k_ref, v_ref, bias_ref, o_ref):
    lane = lax.broadcasted_iota(I32, (Q_ROWS, GROUP_LANES), 1) // HEAD_DIM
    col = lax.broadcasted_iota(I32, (HEADS_PER_STEP * Q_ROWS, BAND_ROWS), 1)
    for qb in range(ATT_BLOCKS):
        blk = i * ATT_BLOCKS + qb
        start = pl.multiple_of(blk * Q_ROWS, Q_ROWS)
        qrows = slice(qb * Q_ROWS, (qb + 1) * Q_ROWS)
        key_ok = col >= LEFT_PAD - blk * Q_ROWS
        for g in range(N_HEADS // HEADS_PER_STEP):
            lanes = slice(g * GROUP_LANES, (g + 1) * GROUP_LANES)
            q = q_ref[qrows, lanes]
            qs = jnp.concatenate(
                [jnp.where(lane == h, q, jnp.zeros_like(q)) for h in range(HEADS_PER_STEP)], axis=0)
            kb = k_ref[pl.ds(start, BAND_ROWS), lanes]
            vb = v_ref[pl.ds(start, BAND_ROWS), lanes]
            s = lax.dot_general(qs, kb, (((1,), (1,)), ((), ())), preferred_element_type=F32)
            rows = slice(g * HEADS_PER_STEP * Q_ROWS, (g + 1) * HEADS_PER_STEP * Q_ROWS)
            s = s + bias_ref[rows, :]
            if mask_start:
                s = jnp.where(key_ok, s, NEG_BIG)
            m = jnp.max(s, axis=-1, keepdims=True)
            p = jnp.exp(s - m)
            l = jnp.sum(p, axis=-1, keepdims=True)
            o = jnp.dot(p.astype(BF16), vb, preferred_element_type=F32) / l
            out = o[0:Q_ROWS]
            for h in range(1, HEADS_PER_STEP):
                out = jnp.where(lane == h, o[h * Q_ROWS:(h + 1) * Q_ROWS], out)
            o_ref[qrows, lanes] = out.astype(o_ref.dtype)


def _attention(q, kpad, vpad, bias):
    bsz, seq, _ = q.shape
    step_rows = ATT_BLOCKS * Q_ROWS
    return pl.pallas_call(
        _attn_kernel,
        grid=(bsz, seq // step_rows),
        in_specs=[
            pl.BlockSpec((None, step_rows, D_ATTN), lambda b, i: (b, i, 0)),
            pl.BlockSpec((None, seq + LEFT_PAD, D_ATTN), lambda b, i: (b, 0, 0)),
            pl.BlockSpec((None, seq + LEFT_PAD, D_ATTN), lambda b, i: (b, 0, 0)),
            pl.BlockSpec((N_HEADS * Q_ROWS, BAND_ROWS), lambda b, i: (0, 0)),
        ],
        out_specs=pl.BlockSpec((None, step_rows, D_ATTN), lambda b, i: (b, i, 0)),
        out_shape=jax.ShapeDtypeStruct((bsz, seq, D_ATTN), BF16),
        compiler_params=pltpu.CompilerParams(
            dimension_semantics=("arbitrary", "arbitrary"), vmem_limit_bytes=VMEM_LIMIT),
        name="chunk_attn",
    )(q, kpad, vpad, bias)


def _band_bias(rel_bias):
    n_rel = REL_MAX - REL_MIN + 1
    far = jnp.broadcast_to(rel_bias[:, n_rel - 1:n_rel], (N_HEADS, BAND_ROWS - 1 - REL_MAX))
    near = rel_bias[:, ::-1]
    ahead = jnp.broadcast_to(rel_bias[:, 0:1], (N_HEADS, Q_ROWS - 1 + REL_MIN))
    diag = jnp.concatenate([far, near, ahead], axis=1).astype(F32)
    bias = jnp.stack(
        [diag[:, Q_ROWS - 1 - r:Q_ROWS - 1 - r + BAND_ROWS] for r in range(Q_ROWS)], axis=1)
    r = jnp.arange(Q_ROWS)[:, None]
    m = jnp.arange(BAND_ROWS)[None, :]
    cq = r // CHUNK
    ck = m // CHUNK
    in_band = (ck >= cq) & (ck <= cq + LEFT_CHUNKS)
    bias = jnp.where(in_band[None], bias, NEG_BIG)
    return bias.reshape(N_HEADS * Q_ROWS, BAND_ROWS)


def _split_bf16(v):
    hi = v.astype(BF16)
    lo = (v - hi.astype(F32)).astype(BF16)
    return hi, lo


def _mix_route_kernel(x_ref, a_ref, c_ref, wo_ref, ng_ref, rwt_ref, rb_ref,
                      x1_ref, t_ref, idx_ref, wgt_ref, rank_ref, cnt_ref,
                      wobf_ref, cntacc_ref):
    i = pl.program_id(0)

    @pl.when(i == 0)
    def _():
        _cast_rows(wo_ref, wobf_ref, D_MODEL)
        cntacc_ref[...] = jnp.zeros_like(cntacc_ref)

    mix_in = jnp.concatenate([a_ref[...], c_ref[...]], axis=1)
    x1 = x_ref[...] + jnp.dot(mix_in, wobf_ref[...], preferred_element_type=F32)
    x1_ref[...] = x1
    ms = jnp.mean(x1 * x1, axis=-1, keepdims=True)
    t = (x1 * lax.rsqrt(ms + RMS_EPS)) * ng_ref[...]
    t_ref[...] = _pack_rows(t)

    nt = (((1,), (1,)), ((), ()))
    w_hi, w_lo = _split_bf16(rwt_ref[...])
    t_hi, t_lo = _split_bf16(t)
    both = lax.dot_general(jnp.concatenate([w_hi, w_lo], axis=0), t_hi, nt,
                           preferred_element_type=F32)
    logits = (both[:N_EXPERTS] + both[N_EXPERTS:]
              + lax.dot_general(w_hi, t_lo, nt, preferred_element_type=F32)) + rb_ref[...]
    e_iota = lax.broadcasted_iota(I32, (N_EXPERTS, MIX_ROWS), 0)
    vals, idxs, hots = [], [], []
    for _ in range(TOP_K):
        m = jnp.max(logits, axis=0, keepdims=True)
        am = jnp.min(jnp.where(logits == m, e_iota, N_EXPERTS), axis=0, keepdims=True)
        hot = e_iota == am
        vals.append(m)
        idxs.append(am)
        hots.append(hot)
        logits = jnp.where(hot, -jnp.inf, logits)
    exps = [jnp.exp(v - vals[0]) for v in vals]
    den = exps[0] + exps[1] + exps[2] + exps[3]
    wts = [e / den for e in exps]

    hot_f = (hots[0] | hots[1] | hots[2] | hots[3]).astype(F32)
    ra = lax.broadcasted_iota(I32, (MIX_ROWS, MIX_ROWS), 0)
    rc = lax.broadcasted_iota(I32, (MIX_ROWS, MIX_ROWS), 1)
    upper = (ra < rc).astype(BF16)
    prefix = jnp.dot(hot_f.astype(BF16), upper, preferred_element_type=F32)
    base = prefix + cntacc_ref[...]
    ranks = [jnp.sum(jnp.where(h, base, 0.0), axis=0, keepdims=True) for h in hots]
    counts = cntacc_ref[...] + jnp.sum(hot_f, axis=1, keepdims=True)
    cntacc_ref[...] = counts

    idx_ref[...] = jnp.concatenate(idxs, axis=0)
    rank_ref[...] = jnp.concatenate(ranks, axis=0).astype(I32)
    cnt_ref[...] = jnp.broadcast_to(counts, cnt_ref.shape)
    w8 = jnp.concatenate(wts + [jnp.zeros((8 - TOP_K, MIX_ROWS), F32)], axis=0)
    wgt_ref[...] = w8.T


def _mix_route(x2, attn2, conv2, w_out, norm_g, router_w, router_b):
    n_tok = x2.shape[0]
    row = lambda i: (i, 0)
    const = lambda i: (0, 0)
    vec = lambda v: v.reshape(1, -1)
    return pl.pallas_call(
        _mix_route_kernel,
        grid=(n_tok // MIX_ROWS,),
        in_specs=[
            pl.BlockSpec((MIX_ROWS, D_MODEL), row),
            pl.BlockSpec((MIX_ROWS, D_ATTN), row),
            pl.BlockSpec((MIX_ROWS, D_CONV), row),
            pl.BlockSpec((D_MODEL, D_MODEL), const),
            pl.BlockSpec((1, D_MODEL), const),
            pl.BlockSpec((N_EXPERTS, D_MODEL), const),
            pl.BlockSpec((N_EXPERTS, 1), const),
        ],
        out_specs=[
            pl.BlockSpec((MIX_ROWS, D_MODEL), row),
            pl.BlockSpec((MIX_ROWS, HALF), row),
            pl.BlockSpec((TOP_K, MIX_ROWS), lambda i: (0, i)),
            pl.BlockSpec((MIX_ROWS, 8), row),
            pl.BlockSpec((TOP_K, MIX_ROWS), lambda i: (0, i)),
            pl.BlockSpec((N_EXPERTS, 128), const),
        ],
        out_shape=[
            jax.ShapeDtypeStruct((n_tok, D_MODEL), F32),
            jax.ShapeDtypeStruct((n_tok, HALF), I32),
            jax.ShapeDtypeStruct((TOP_K, n_tok), I32),
            jax.ShapeDtypeStruct((n_tok, 8), F32),
            jax.ShapeDtypeStruct((TOP_K, n_tok), I32),
            jax.ShapeDtypeStruct((N_EXPERTS, 128), F32),
        ],
        scratch_shapes=[
            pltpu.VMEM((D_MODEL, D_MODEL), BF16),
            pltpu.VMEM((N_EXPERTS, 1), F32),
        ],
        compiler_params=pltpu.CompilerParams(
            dimension_semantics=("arbitrary",), vmem_limit_bytes=VMEM_LIMIT),
        name="mix_route",
    )(x2, attn2, conv2, w_out, vec(norm_g), router_w.T, router_b.reshape(N_EXPERTS, 1))


def _tile_plan(counts, n_tiles):
    tiles_per = (counts + EXP_ROWS - 1) // EXP_ROWS
    tile_end = jnp.cumsum(tiles_per)
    tile_begin = tile_end - tiles_per
    n_valid = tile_end[-1]
    tiles = jnp.arange(n_tiles, dtype=I32)
    tile_valid = tiles < n_valid
    capped = jnp.minimum(tiles, n_valid - 1)
    tile_expert = jnp.sum((capped[:, None] >= tile_end[None, :]).astype(I32), axis=1)
    tile_expert = jnp.minimum(tile_expert, N_EXPERTS - 1)
    is_first = (tiles[:, None] == tile_begin[None, :]) & (tiles_per[None, :] > 0)
    tile_first = tile_valid & jnp.any(is_first, axis=1)
    group_start = tile_begin * EXP_ROWS
    experts = jnp.arange(N_EXPERTS, dtype=I32)
    nonempty = tiles_per > 0
    parity = (jnp.cumsum(nonempty.astype(I32)) - 1) % 2
    later = nonempty[None, :] & (experts[None, :] > experts[:, None])
    nxt = jnp.min(jnp.where(later, experts[None, :], N_EXPERTS), axis=1)
    nxt = jnp.where(nxt == N_EXPERTS, -1, nxt)
    hot = (tile_expert[:, None] == experts[None, :]).astype(I32)
    tile_slot = jnp.sum(hot * parity[None, :], axis=1)
    tile_next = jnp.sum(hot * nxt[None, :], axis=1)
    group_rows = jnp.sum(hot * counts[None, :], axis=1)
    tile_in_group = tiles - jnp.sum(hot * tile_begin[None, :], axis=1)
    filled = jnp.clip(group_rows - tile_in_group * EXP_ROWS, 0, EXP_ROWS)
    filled = jnp.where(tile_valid, filled, 0)
    tile_rows = (filled + EXP_GRANULE - 1) // EXP_GRANULE * EXP_GRANULE
    steps = jnp.arange(n_tiles // EXP_TILES_PER_STEP, dtype=I32)
    step_block = jnp.minimum(steps, (n_valid - 1) // EXP_TILES_PER_STEP)
    flags = (tile_expert, tile_first.astype(I32), tile_rows.astype(I32),
             tile_slot.astype(I32), tile_next.astype(I32), step_block.astype(I32))
    return flags, group_start.astype(I32)


def _slot_kernel(gstart_ref, idx_ref, rank_ref, pos_ref):
    idx = idx_ref[...]
    pos = rank_ref[...]
    for e in range(N_EXPERTS):
        pos = pos + jnp.where(idx == e, gstart_ref[e], 0)
    pos_ref[...] = pos


def _slots(group_start, idx, rank):
    full = pl.BlockSpec(idx.shape, lambda i, gs: (0, 0))
    return pl.pallas_call(
        _slot_kernel,
        grid_spec=pltpu.PrefetchScalarGridSpec(
            num_scalar_prefetch=1, grid=(1,), in_specs=[full, full], out_specs=full),
        out_shape=jax.ShapeDtypeStruct(idx.shape, I32),
        name="slots",
    )(group_start, idx, rank)


def _sc_mesh():
    return plsc.VectorSubcoreMesh(core_axis_name="core", subcore_axis_name="subcore")


def _sc_worker():
    info = plsc.get_sparse_core_info()
    wid = lax.axis_index("subcore") * info.num_cores + lax.axis_index("core")
    return wid, info.num_cores * info.num_subcores


def _dispatch(t2, pos_flat, n_slots):
    n_tok, d = t2.shape
    n_workers = V7X_SC_WORKERS
    per_w = n_tok // n_workers
    n_pairs = per_w // (2 * DISPATCH_ROWS)
    assert per_w % (2 * DISPATCH_ROWS) == 0

    @functools.partial(
        pl.kernel, mesh=_sc_mesh(),
        out_type=jax.ShapeDtypeStruct((n_slots, d), t2.dtype),
        scratch_types=(
            [pltpu.VMEM((DISPATCH_ROWS,), I32) for _ in range(2 * TOP_K)]
            + [pltpu.VMEM((DISPATCH_ROWS, d), t2.dtype) for _ in range(2)]
            + [pltpu.SemaphoreType.DMA for _ in range(2 * (TOP_K + 1))]),
        name="dispatch",
    )
    def k(t_hbm, pos_hbm, o_hbm, *scratch):
        idx_v = scratch[:2 * TOP_K]
        rows_v = scratch[2 * TOP_K:2 * TOP_K + 2]
        rsem = scratch[2 * TOP_K + 2:2 * TOP_K + 4]
        ssem = scratch[2 * TOP_K + 4:]
        wid, nw = _sc_worker()
        assert nw == n_workers
        first = wid * per_w

        def read(g, s):
            rows = pl.ds(first + g * DISPATCH_ROWS, DISPATCH_ROWS)
            return pltpu.make_async_copy(t_hbm.at[rows], rows_v[s], rsem[s])

        def scatter_chunk(g, s):
            for kk in range(TOP_K):
                pltpu.sync_copy(
                    pos_hbm.at[pl.ds(kk * n_tok + first + g * DISPATCH_ROWS, DISPATCH_ROWS)],
                    idx_v[s * TOP_K + kk])
            read(g, s).wait()
            scatters = [pltpu.async_copy(rows_v[s], o_hbm.at[idx_v[s * TOP_K + kk]],
                                         ssem[s * TOP_K + kk]) for kk in range(TOP_K)]
            for cp in scatters:
                cp.wait()

        read(0, 0).start()

        @pl.loop(0, n_pairs)
        def _(t):
            read(2 * t + 1, 1).start()
            scatter_chunk(2 * t, 0)

            @pl.when(t + 1 < n_pairs)
            def _():
                read(2 * t + 2, 0).start()

            scatter_chunk(2 * t + 1, 1)

    return k(t2, pos_flat)


def _gather_back(sorted_out, pos_flat):
    n_rows = pos_flat.shape[0]
    d = sorted_out.shape[1]
    n_workers = V7X_SC_WORKERS
    per_w = n_rows // n_workers
    group = GATHER_BUFS * GATHER_ROWS
    n_pairs = per_w // (2 * group)
    assert per_w % (2 * group) == 0

    @functools.partial(
        pl.kernel, mesh=_sc_mesh(),
        out_type=jax.ShapeDtypeStruct((n_rows, d), sorted_out.dtype),
        scratch_types=(
            [pltpu.VMEM((per_w // GATHER_ROWS, GATHER_ROWS), I32)]
            + [pltpu.VMEM((GATHER_ROWS, d), sorted_out.dtype) for _ in range(2 * GATHER_BUFS)]
            + [pltpu.SemaphoreType.DMA for _ in range(4 * GATHER_BUFS)]),
        name="gather_back",
    )
    def k(s_hbm, pos_hbm, o_hbm, idx_all, *scratch):
        nb = 2 * GATHER_BUFS
        rows_v = scratch[:nb]
        gsem, wsem = scratch[nb:2 * nb], scratch[2 * nb:]
        wid, nw = _sc_worker()
        assert nw == n_workers
        first = wid * per_w

        def bufs(s):
            return range(s * GATHER_BUFS, (s + 1) * GATHER_BUFS)

        def rows_of(g, b):
            return pl.ds(first + g * group + (b % GATHER_BUFS) * GATHER_ROWS, GATHER_ROWS)

        def gather(g, b):
            idx = idx_all.at[g * GATHER_BUFS + b % GATHER_BUFS]
            return pltpu.make_async_copy(s_hbm.at[idx], rows_v[b], gsem[b])

        def write(g, b):
            return pltpu.make_async_copy(rows_v[b], o_hbm.at[rows_of(g, b)], wsem[b])

        def start_gathers(g, s):
            for b in bufs(s):
                gather(g, b).start()

        def drain(g, s):
            for b in bufs(s):
                gather(g, b).wait()
                write(g, b).start()
            for b in bufs(s):
                write(g, b).wait()

        rows_per_w = per_w // GATHER_ROWS
        pltpu.sync_copy(pos_hbm.at[pl.ds(wid * rows_per_w, rows_per_w)], idx_all)

        start_gathers(0, 0)

        @pl.loop(0, n_pairs)
        def _(t):
            start_gathers(2 * t + 1, 1)
            drain(2 * t, 0)

            @pl.when(t + 1 < n_pairs)
            def _():
                start_gathers(2 * t + 2, 0)

            drain(2 * t + 1, 1)

    return k(sorted_out, pos_flat.reshape(n_rows // GATHER_ROWS, GATHER_ROWS))


def _weight_copies(w_hbm, wf32, sem, expert, slot):
    return [pltpu.make_async_copy(w.at[expert], wf32.at[slot, mtx], sem.at[slot])
            for mtx, w in enumerate(w_hbm)]


def _expert_rows(rows, x_ref, bg_ref, bu_ref, bd_ref, o_ref, wbf):
    xs = jnp.concatenate(_unpack_rows(x_ref[0:rows, :]), axis=1).astype(BF16)
    cn = 256
    hids = []
    for c in range(D_MODEL // cn):
        sl = slice(c * cn, (c + 1) * cn)
        g = jnp.dot(xs, wbf[0, :, sl], preferred_element_type=F32) + bg_ref[:, sl]
        u = jnp.dot(xs, wbf[1, :, sl], preferred_element_type=F32) + bu_ref[:, sl]
        g = jnp.minimum(g, SWIGLU_LIMIT)
        u = jnp.clip(u, -SWIGLU_LIMIT, SWIGLU_LIMIT)
        hids.append(((u + 1.0) * (g * jax.nn.sigmoid(SWIGLU_ALPHA * g))).astype(BF16))
    acc = jnp.dot(jnp.concatenate(hids, axis=1), wbf[2], preferred_element_type=F32)
    o_ref[0:rows, :] = _pack_rows(acc + bd_ref[...])
    if rows < EXP_ROWS:
        o_ref[rows:EXP_ROWS, :] = jnp.zeros((EXP_ROWS - rows, HALF), I32)


def _expert_kernel(texp_ref, tfirst_ref, trows_ref, tslot_ref, tnext_ref, sblk_ref,
                   x_ref, wg_hbm, bg_ref, wu_hbm, bu_ref, wd_hbm, bd_ref,
                   o_ref, wf32, wbf, wsem):
    w_hbm = (wg_hbm, wu_hbm, wd_hbm)
    own_block = sblk_ref[pl.program_id(0)] == pl.program_id(0)
    for sub in range(EXP_TILES_PER_STEP):
        i = pl.program_id(0) * EXP_TILES_PER_STEP + sub
        x_tile = x_ref.at[sub * EXP_ROWS:(sub + 1) * EXP_ROWS]
        o_tile = o_ref.at[sub * EXP_ROWS:(sub + 1) * EXP_ROWS]
        _expert_tile(i, own_block, texp_ref, tfirst_ref, trows_ref, tslot_ref, tnext_ref,
                     x_tile, w_hbm, bg_ref, bu_ref, bd_ref, o_tile, wf32, wbf, wsem)


def _expert_tile(i, own_block, texp_ref, tfirst_ref, trows_ref, tslot_ref, tnext_ref,
                 x_ref, w_hbm, bg_ref, bu_ref, bd_ref, o_ref, wf32, wbf, wsem):
    @pl.when((trows_ref[i] == 0) & own_block)
    def _():
        o_ref[...] = jnp.zeros(o_ref.shape, o_ref.dtype)

    @pl.when(tfirst_ref[i] == 1)
    def _():
        slot = tslot_ref[i]
        expert = texp_ref[i]

        @pl.when(i == 0)
        def _():
            for cp in _weight_copies(w_hbm, wf32, wsem, expert, slot):
                cp.start()

        for cp in _weight_copies(w_hbm, wf32, wsem, expert, slot):
            cp.wait()

        @pl.when(tnext_ref[i] >= 0)
        def _():
            for cp in _weight_copies(w_hbm, wf32, wsem, tnext_ref[i], 1 - slot):
                cp.start()

        for mtx in range(3):
            _cast_rows(wf32.at[slot, mtx], wbf.at[mtx], D_MODEL)

    for rows in range(EXP_GRANULE, EXP_ROWS + 1, EXP_GRANULE):
        @pl.when(trows_ref[i] == rows)
        def _(rows=rows):
            expert = texp_ref[i]
            _expert_rows(rows, x_ref, bg_ref.at[expert], bu_ref.at[expert], bd_ref.at[expert],
                         o_ref, wbf)


def _experts(sorted_t, tile_flags, w_gate, b_gate, w_up, b_up, w_down, b_down, n_tiles):
    row_blk = pl.BlockSpec((EXP_TILES_PER_STEP * EXP_ROWS, HALF), lambda i, *flags: (flags[5][i], 0))
    w_any = pl.BlockSpec(memory_space=pl.ANY)
    b_blk = pl.BlockSpec((N_EXPERTS, 1, D_MODEL), lambda i, *_: (0, 0, 0))
    grid_spec = pltpu.PrefetchScalarGridSpec(
        num_scalar_prefetch=len(tile_flags),
        grid=(n_tiles // EXP_TILES_PER_STEP,),
        in_specs=[row_blk, w_any, b_blk, w_any, b_blk, w_any, b_blk],
        out_specs=row_blk,
        scratch_shapes=[
            pltpu.VMEM((2, 3, D_MODEL, D_MODEL), F32),
            pltpu.VMEM((3, D_MODEL, D_MODEL), BF16),
            pltpu.SemaphoreType.DMA((2,)),
        ],
    )
    b3 = lambda b: b.reshape(N_EXPERTS, 1, D_MODEL)
    return pl.pallas_call(
        _expert_kernel,
        grid_spec=grid_spec,
        out_shape=jax.ShapeDtypeStruct((n_tiles * EXP_ROWS, HALF), I32),
        compiler_params=pltpu.CompilerParams(
            dimension_semantics=("arbitrary",), vmem_limit_bytes=VMEM_LIMIT),
        name="experts",
    )(*tile_flags, sorted_t,
      w_gate, b3(b_gate), w_up, b3(b_up), w_down, b3(b_down))


def _combine_kernel(x1_ref, s0_ref, s1_ref, s2_ref, s3_ref, w_ref, g_ref, o_ref):
    w = w_ref[...]
    y_lo = x1_ref[:, :HALF]
    y_hi = x1_ref[:, HALF:]
    for k, s_ref in enumerate((s0_ref, s1_ref, s2_ref, s3_ref)):
        lo, hi = _unpack_rows(s_ref[...])
        y_lo = y_lo + w[:, k:k + 1] * lo
        y_hi = y_hi + w[:, k:k + 1] * hi
    sq = jnp.sum(y_lo * y_lo, axis=-1, keepdims=True) + jnp.sum(y_hi * y_hi, axis=-1, keepdims=True)
    scale = lax.rsqrt(sq * (1.0 / D_MODEL) + RMS_EPS)
    o_ref[:, :HALF] = (y_lo * scale) * g_ref[:, :HALF]
    o_ref[:, HALF:] = (y_hi * scale) * g_ref[:, HALF:]


def _combine(x1, slabs, wgt, norm_g):
    n_tok = x1.shape[0]
    rows = MIX_ROWS
    ntiles = n_tok // rows
    slab_spec = lambda k: pl.BlockSpec((rows, HALF), lambda i, k=k: (k * ntiles + i, 0))
    return pl.pallas_call(
        _combine_kernel,
        grid=(ntiles,),
        in_specs=[pl.BlockSpec((rows, D_MODEL), lambda i: (i, 0))]
        + [slab_spec(k) for k in range(TOP_K)]
        + [pl.BlockSpec((rows, 8), lambda i: (i, 0)),
           pl.BlockSpec((1, D_MODEL), lambda i: (0, 0))],
        out_specs=pl.BlockSpec((rows, D_MODEL), lambda i: (i, 0)),
        out_shape=jax.ShapeDtypeStruct((n_tok, D_MODEL), F32),
        compiler_params=pltpu.CompilerParams(
            dimension_semantics=("arbitrary",), vmem_limit_bytes=VMEM_LIMIT),
        name="combine",
    )(x1, slabs, slabs, slabs, slabs, wgt, norm_g.reshape(1, D_MODEL))


def kernel(x, norm_mix_g, w_in, conv_dw_w, conv_dw_b, conv_ln_g, conv_ln_b, rel_bias, w_out,
           norm_ffn_g, router_w, router_b, exp_w_gate, exp_b_gate, exp_w_up, exp_b_up,
           exp_w_down, exp_b_down, norm_final_g):
    bsz, seq, _ = x.shape
    n_tok = bsz * seq
    assert norm_mix_g.shape[0] == 1, "single-layer block"
    assert seq % IN_ROWS == 0 and seq % MIX_ROWS == 0 and LEFT_PAD % IN_ROWS == 0
    n_tiles = (TOP_K * n_tok) // EXP_ROWS + N_EXPERTS - 1
    n_tiles = -(-n_tiles // EXP_TILES_PER_STEP) * EXP_TILES_PER_STEP

    q, kpad, vpad, conv = _inproj(x, norm_mix_g[0], w_in[0], conv_dw_w[0], conv_dw_b[0],
                                  conv_ln_g[0], conv_ln_b[0])
    attn = _attention(q, kpad, vpad, _band_bias(rel_bias[0]))
    x1, t, idx, wgt, rank, cnt = _mix_route(
        x.reshape(n_tok, D_MODEL), attn.reshape(n_tok, D_ATTN), conv.reshape(n_tok, D_CONV),
        w_out[0], norm_ffn_g[0], router_w[0], router_b[0])
    tile_flags, group_start = _tile_plan(cnt[:, 0].astype(I32), n_tiles)
    pos_flat = _slots(group_start, idx, rank).reshape(TOP_K * n_tok)
    sorted_t = _dispatch(t, pos_flat, n_tiles * EXP_ROWS)
    sorted_out = _experts(sorted_t, tile_flags,
                          exp_w_gate[0], exp_b_gate[0], exp_w_up[0], exp_b_up[0],
                          exp_w_down[0], exp_b_down[0], n_tiles)
    slabs = _gather_back(sorted_out, pos_flat)
    out = _combine(x1, slabs, wgt, norm_final_g)
    return out.reshape(bsz, seq, D_MODEL)
```

```python
import functools

import jax
import jax.numpy as jnp
from jax import lax
from jax.experimental import pallas as pl
from jax.experimental.pallas import tpu as pltpu
from jax.experimental.pallas import tpu_sc as plsc

F32 = jnp.float32
BF16 = jnp.bfloat16
I32 = jnp.int32

D_MODEL = 1024
CHUNK = 64
N_HEADS = 8
HEAD_DIM = 64
D_ATTN = N_HEADS * HEAD_DIM
LEFT_CHUNKS = 8
REL_MAX = 128
REL_MIN = -(CHUNK - 1)
D_CONV = D_MODEL - D_ATTN
CONV_WIDTH = 31
N_EXPERTS = 32
TOP_K = 4
SWIGLU_ALPHA = 1.702
SWIGLU_LIMIT = 7.0
RMS_EPS = 1e-5
LN_EPS = 1e-5

LEFT_PAD = LEFT_CHUNKS * CHUNK
IN_ROWS = 512
PAD_BLOCKS = LEFT_PAD // IN_ROWS
Q_ROWS = 2 * CHUNK
ATT_BLOCKS = 4
BAND_ROWS = Q_ROWS + LEFT_PAD
HEADS_PER_STEP = 4
GROUP_LANES = HEADS_PER_STEP * HEAD_DIM
MIX_ROWS = 1024
HALO_ROWS = 32
EXP_ROWS = 512
EXP_GRANULE = 128
EXP_SHIFT = EXP_ROWS.bit_length() - 1
assert EXP_ROWS == 1 << EXP_SHIFT and EXP_GRANULE & (EXP_GRANULE - 1) == 0
EXP_TILES_PER_STEP = 2
DISPATCH_ROWS = 64
GATHER_ROWS = 32
GATHER_BUFS = 2
NEG_BIG = -1e30
V7X_VMEM_BYTES = 64 * 1024 * 1024
VMEM_LIMIT = V7X_VMEM_BYTES - 8 * 1024 * 1024
V7X_SC_WORKERS = 32


HALF = D_MODEL // 2
HI_MASK = -65536


def _pack_rows(x):
    bits = lax.bitcast_convert_type(x.astype(BF16).astype(F32), I32)
    return lax.shift_right_logical(bits[:, :HALF], 16) | (bits[:, HALF:] & HI_MASK)


def _unpack_rows(w):
    lo = lax.bitcast_convert_type(lax.shift_left(w, 16), F32)
    hi = lax.bitcast_convert_type(w & HI_MASK, F32)
    return lo, hi


def _cast_rows(src_ref, dst_ref, rows, step=128):
    def body(c, carry):
        r = pl.multiple_of(c * step, step)
        dst_ref[pl.ds(r, step), :] = src_ref[pl.ds(r, step), :].astype(dst_ref.dtype)
        return carry
    lax.fori_loop(0, rows // step, body, 0)


def _conv_branch(hw_ref, sh_ref, rows, cw_ref, cb_ref, lg_ref, lb_ref, out_ref):
    off = HALO_ROWS - (CONV_WIDTH - 1)
    shift_rows = HALO_ROWS + rows - 8
    for b in range(1, 8):
        sh_ref[b - 1] = hw_ref[pl.ds(b, shift_rows), :]
    acc = None
    for j in range(CONV_WIDTH):
        a, b = divmod(off + j, 8)
        src = hw_ref if b == 0 else sh_ref.at[b - 1]
        term = src[pl.ds(8 * a, rows), :] * cw_ref[j:j + 1, :]
        acc = term if acc is None else acc + term
    acc = acc + cb_ref[...]
    mu = jnp.mean(acc, axis=-1, keepdims=True)
    d = acc - mu
    var = jnp.mean(d * d, axis=-1, keepdims=True)
    y = d * lax.rsqrt(var + LN_EPS) * lg_ref[...] + lb_ref[...]
    out_ref[...] = (y * jax.nn.sigmoid(y)).astype(out_ref.dtype)


def _inproj_kernel(x_ref, g_ref, w_ref, cw_ref, cb_ref, lg_ref, lb_ref,
                   q_ref, k_ref, v_ref, c_ref, wbf_ref, hw_ref, sh_ref):
    b = pl.program_id(0)
    j = pl.program_id(1)

    @pl.when((b == 0) & (j == 0))
    def _():
        _cast_rows(w_ref, wbf_ref, D_MODEL)

    @pl.when(j < PAD_BLOCKS)
    def _():
        k_ref[...] = jnp.zeros_like(k_ref)
        v_ref[...] = jnp.zeros_like(v_ref)
        hw_ref[IN_ROWS:IN_ROWS + HALO_ROWS, :] = jnp.zeros((HALO_ROWS, D_CONV), F32)

    @pl.when(j >= PAD_BLOCKS)
    def _():
        x = x_ref[...]
        ms = jnp.mean(x * x, axis=-1, keepdims=True)
        hb = ((x * lax.rsqrt(ms + RMS_EPS)) * g_ref[...]).astype(BF16)

        def proj(c0, width):
            return jnp.dot(hb, wbf_ref[:, c0:c0 + width], preferred_element_type=F32)

        a = proj(3 * D_ATTN, D_CONV)
        gate = proj(3 * D_ATTN + D_CONV, D_CONV)
        hw_ref[0:HALO_ROWS, :] = hw_ref[IN_ROWS:IN_ROWS + HALO_ROWS, :]
        hw_ref[HALO_ROWS:HALO_ROWS + IN_ROWS, :] = a * jax.nn.sigmoid(gate)
        _conv_branch(hw_ref, sh_ref, IN_ROWS, cw_ref, cb_ref, lg_ref, lb_ref, c_ref)

        q_ref[...] = (proj(0, D_ATTN) * (HEAD_DIM ** -0.5)).astype(BF16)
        k_ref[...] = proj(D_ATTN, D_ATTN).astype(BF16)
        v_ref[...] = proj(2 * D_ATTN, D_ATTN).astype(BF16)


def _inproj(x, norm_g, w_in, conv_w, conv_b, ln_g, ln_b):
    bsz, seq, _ = x.shape
    nblk = seq // IN_ROWS
    d_cols = w_in.shape[1]
    row_blk = lambda b, j: (b, jnp.maximum(j - PAD_BLOCKS, 0), 0)
    const = lambda b, j: (0, 0)
    vec = lambda v: v.reshape(1, -1)
    return pl.pallas_call(
        _inproj_kernel,
        grid=(bsz, nblk + PAD_BLOCKS),
        in_specs=[
            pl.BlockSpec((None, IN_ROWS, D_MODEL), row_blk),
            pl.BlockSpec((1, D_MODEL), const),
            pl.BlockSpec((D_MODEL, d_cols), const),
            pl.BlockSpec((CONV_WIDTH, D_CONV), const),
            pl.BlockSpec((1, D_CONV), const),
            pl.BlockSpec((1, D_CONV), const),
            pl.BlockSpec((1, D_CONV), const),
        ],
        out_specs=[
            pl.BlockSpec((None, IN_ROWS, D_ATTN), row_blk),
            pl.BlockSpec((None, IN_ROWS, D_ATTN), lambda b, j: (b, j, 0)),
            pl.BlockSpec((None, IN_ROWS, D_ATTN), lambda b, j: (b, j, 0)),
            pl.BlockSpec((None, IN_ROWS, D_CONV), row_blk),
        ],
        out_shape=[
            jax.ShapeDtypeStruct((bsz, seq, D_ATTN), BF16),
            jax.ShapeDtypeStruct((bsz, seq + LEFT_PAD, D_ATTN), BF16),
            jax.ShapeDtypeStruct((bsz, seq + LEFT_PAD, D_ATTN), BF16),
            jax.ShapeDtypeStruct((bsz, seq, D_CONV), BF16),
        ],
        scratch_shapes=[
            pltpu.VMEM((D_MODEL, d_cols), BF16),
            pltpu.VMEM((HALO_ROWS + IN_ROWS, D_CONV), F32),
            pltpu.VMEM((7, HALO_ROWS + IN_ROWS - 8, D_CONV), F32),
        ],
        compiler_params=pltpu.CompilerParams(
            dimension_semantics=("arbitrary", "arbitrary"), vmem_limit_bytes=VMEM_LIMIT),
        name="inproj",
    )(x, vec(norm_g), w_in, conv_w, vec(conv_b), vec(ln_g), vec(ln_b))


def _attn_kernel(q_ref, k_ref, v_ref, bias_ref, o_ref):
    i = pl.program_id(1)
    first_steps = -(-LEFT_PAD // (ATT_BLOCKS * Q_ROWS))

    @pl.when(i < first_steps)
    def _():
        _attn_step(i, True, q_ref, k_ref, v_ref, bias_ref, o_ref)

    @pl.when(i >= first_steps)
    def _():
        _attn_step(i, False, q_ref, k_ref, v_ref, bias_ref, o_ref)


def _attn_step(i, mask_start, q_ref, k_ref, v_ref, bias_ref, o_ref):
    lane = lax.broadcasted_iota(I32, (Q_ROWS, GROUP_LANES), 1) // HEAD_DIM
    col = lax.broadcasted_iota(I32, (HEADS_PER_STEP * Q_ROWS, BAND_ROWS), 1)
    for qb in range(ATT_BLOCKS):
        blk = i * ATT_BLOCKS + qb
        start = pl.multiple_of(blk * Q_ROWS, Q_ROWS)
        qrows = slice(qb * Q_ROWS, (qb + 1) * Q_ROWS)
        key_ok = col >= LEFT_PAD - blk * Q_ROWS
        for g in range(N_HEADS // HEADS_PER_STEP):
            lanes = slice(g * GROUP_LANES, (g + 1) * GROUP_LANES)
            q = q_ref[qrows, lanes]
            qs = jnp.concatenate(
                [jnp.where(lane == h, q, jnp.zeros_like(q)) for h in range(HEADS_PER_STEP)], axis=0)
            kb = k_ref[pl.ds(start, BAND_ROWS), lanes]
            vb = v_ref[pl.ds(start, BAND_ROWS), lanes]
            s = lax.dot_general(qs, kb, (((1,), (1,)), ((), ())), preferred_element_type=F32)
            rows = slice(g * HEADS_PER_STEP * Q_ROWS, (g + 1) * HEADS_PER_STEP * Q_ROWS)
            s = s + bias_ref[rows, :]
            if mask_start:
                s = jnp.where(key_ok, s, NEG_BIG)
            m = jnp.max(s, axis=-1, keepdims=True)
            p = jnp.exp(s - m)
            l = jnp.sum(p, axis=-1, keepdims=True)
            o = jnp.dot(p.astype(BF16), vb, preferred_element_type=F32) / l
            out = o[0:Q_ROWS]
            for h in range(1, HEADS_PER_STEP):
                out = jnp.where(lane == h, o[h * Q_ROWS:(h + 1) * Q_ROWS], out)
            o_ref[qrows, lanes] = out.astype(o_ref.dtype)


def _attention(q, kpad, vpad, bias):
    bsz, seq, _ = q.shape
    step_rows = ATT_BLOCKS * Q_ROWS
    return pl.pallas_call(
        _attn_kernel,
        grid=(bsz, seq // step_rows),
        in_specs=[
            pl.BlockSpec((None, step_rows, D_ATTN), lambda b, i: (b, i, 0)),
            pl.BlockSpec((None, seq + LEFT_PAD, D_ATTN), lambda b, i: (b, 0, 0)),
            pl.BlockSpec((None, seq + LEFT_PAD, D_ATTN), lambda b, i: (b, 0, 0)),
            pl.BlockSpec((N_HEADS * Q_ROWS, BAND_ROWS), lambda b, i: (0, 0)),
        ],
        out_specs=pl.BlockSpec((None, step_rows, D_ATTN), lambda b, i: (b, i, 0)),
        out_shape=jax.ShapeDtypeStruct((bsz, seq, D_ATTN), BF16),
        compiler_params=pltpu.CompilerParams(
            dimension_semantics=("arbitrary", "arbitrary"), vmem_limit_bytes=VMEM_LIMIT),
        name="chunk_attn",
    )(q, kpad, vpad, bias)


def _band_bias(rel_bias):
    n_rel = REL_MAX - REL_MIN + 1
    far = jnp.broadcast_to(rel_bias[:, n_rel - 1:n_rel], (N_HEADS, BAND_ROWS - 1 - REL_MAX))
    near = rel_bias[:, ::-1]
    ahead = jnp.broadcast_to(rel_bias[:, 0:1], (N_HEADS, Q_ROWS - 1 + REL_MIN))
    diag = jnp.concatenate([far, near, ahead], axis=1).astype(F32)
    bias = jnp.stack(
        [diag[:, Q_ROWS - 1 - r:Q_ROWS - 1 - r + BAND_ROWS] for r in range(Q_ROWS)], axis=1)
    r = jnp.arange(Q_ROWS)[:, None]
    m = jnp.arange(BAND_ROWS)[None, :]
    cq = r // CHUNK
    ck = m // CHUNK
    in_band = (ck >= cq) & (ck <= cq + LEFT_CHUNKS)
    bias = jnp.where(in_band[None], bias, NEG_BIG)
    return bias.reshape(N_HEADS * Q_ROWS, BAND_ROWS)


def _split_bf16(v):
    hi = v.astype(BF16)
    lo = (v - hi.astype(F32)).astype(BF16)
    return hi, lo


def _mix_route_kernel(x_ref, a_ref, c_ref, wo_ref, ng_ref, rwt_ref, rb_ref,
                      x1_ref, t_ref, idx_ref, wgt_ref, rank_ref, cnt_ref,
                      wobf_ref, cntacc_ref):
    i = pl.program_id(0)

    @pl.when(i == 0)
    def _():
        _cast_rows(wo_ref, wobf_ref, D_MODEL)
        cntacc_ref[...] = jnp.zeros_like(cntacc_ref)

    mix_in = jnp.concatenate([a_ref[...], c_ref[...]], axis=1)
    x1 = x_ref[...] + jnp.dot(mix_in, wobf_ref[...], preferred_element_type=F32)
    x1_ref[...] = x1
    ms = jnp.mean(x1 * x1, axis=-1, keepdims=True)
    t = (x1 * lax.rsqrt(ms + RMS_EPS)) * ng_ref[...]
    t_ref[...] = _pack_rows(t)

    nt = (((1,), (1,)), ((), ()))
    w_hi, w_lo = _split_bf16(rwt_ref[...])
    t_hi, t_lo = _split_bf16(t)
    both = lax.dot_general(jnp.concatenate([w_hi, w_lo], axis=0), t_hi, nt,
                           preferred_element_type=F32)
    logits = (both[:N_EXPERTS] + both[N_EXPERTS:]
              + lax.dot_general(w_hi, t_lo, nt, preferred_element_type=F32)) + rb_ref[...]
    e_iota = lax.broadcasted_iota(I32, (N_EXPERTS, MIX_ROWS), 0)
    vals, idxs, hots = [], [], []
    for _ in range(TOP_K):
        m = jnp.max(logits, axis=0, keepdims=True)
        am = jnp.min(jnp.where(logits == m, e_iota, N_EXPERTS), axis=0, keepdims=True)
        hot = e_iota == am
        vals.append(m)
        idxs.append(am)
        hots.append(hot)
        logits = jnp.where(hot, -jnp.inf, logits)
    exps = [jnp.exp(v - vals[0]) for v in vals]
    den = exps[0] + exps[1] + exps[2] + exps[3]
    wts = [e / den for e in exps]

    hot_f = (hots[0] | hots[1] | hots[2] | hots[3]).astype(F32)
    ra = lax.broadcasted_iota(I32, (MIX_ROWS, MIX_ROWS), 0)
    rc = lax.broadcasted_iota(I32, (MIX_ROWS, MIX_ROWS), 1)
    upper = (ra < rc).astype(BF16)
    prefix = jnp.dot(hot_f.astype(BF16), upper, preferred_element_type=F32)
    base = prefix + cntacc_ref[...]
    ranks = [jnp.sum(jnp.where(h, base, 0.0), axis=0, keepdims=True) for h in hots]
    counts = cntacc_ref[...] + jnp.sum(hot_f, axis=1, keepdims=True)
    cntacc_ref[...] = counts

    idx_ref[...] = jnp.concatenate(idxs, axis=0)
    rank_ref[...] = jnp.concatenate(ranks, axis=0).astype(I32)
    cnt_ref[...] = jnp.broadcast_to(counts, cnt_ref.shape)
    w8 = jnp.concatenate(wts + [jnp.zeros((8 - TOP_K, MIX_ROWS), F32)], axis=0)
    wgt_ref[...] = w8.T


def _mix_route(x2, attn2, conv2, w_out, norm_g, router_w, router_b):
    n_tok = x2.shape[0]
    row = lambda i: (i, 0)
    const = lambda i: (0, 0)
    vec = lambda v: v.reshape(1, -1)
    return pl.pallas_call(
        _mix_route_kernel,
        grid=(n_tok // MIX_ROWS,),
        in_specs=[
            pl.BlockSpec((MIX_ROWS, D_MODEL), row),
            pl.BlockSpec((MIX_ROWS, D_ATTN), row),
            pl.BlockSpec((MIX_ROWS, D_CONV), row),
            pl.BlockSpec((D_MODEL, D_MODEL), const),
            pl.BlockSpec((1, D_MODEL), const),
            pl.BlockSpec((N_EXPERTS, D_MODEL), const),
            pl.BlockSpec((N_EXPERTS, 1), const),
        ],
        out_specs=[
            pl.BlockSpec((MIX_ROWS, D_MODEL), row),
            pl.BlockSpec((MIX_ROWS, HALF), row),
            pl.BlockSpec((TOP_K, MIX_ROWS), lambda i: (0, i)),
            pl.BlockSpec((MIX_ROWS, 8), row),
            pl.BlockSpec((TOP_K, MIX_ROWS), lambda i: (0, i)),
            pl.BlockSpec((N_EXPERTS, 128), const),
        ],
        out_shape=[
            jax.ShapeDtypeStruct((n_tok, D_MODEL), F32),
            jax.ShapeDtypeStruct((n_tok, HALF), I32),
            jax.ShapeDtypeStruct((TOP_K, n_tok), I32),
            jax.ShapeDtypeStruct((n_tok, 8), F32),
            jax.ShapeDtypeStruct((TOP_K, n_tok), I32),
            jax.ShapeDtypeStruct((N_EXPERTS, 128), F32),
        ],
        scratch_shapes=[
            pltpu.VMEM((D_MODEL, D_MODEL), BF16),
            pltpu.VMEM((N_EXPERTS, 1), F32),
        ],
        compiler_params=pltpu.CompilerParams(
            dimension_semantics=("arbitrary",), vmem_limit_bytes=VMEM_LIMIT),
        name="mix_route",
    )(x2, attn2, conv2, w_out, vec(norm_g), router_w.T, router_b.reshape(N_EXPERTS, 1))


def _plan_kernel(cnt_ref, idx_ref, rank_ref, pos_ref, flags_ref):
    lanes = flags_ref.shape[1]
    cnt = cnt_ref[...].astype(I32)
    sub = lax.broadcasted_iota(I32, cnt.shape, 0)
    lane = lax.broadcasted_iota(I32, cnt.shape, 1)
    col_sum = lambda a: jnp.sum(a, axis=1, keepdims=True)
    row_sum = lambda a: jnp.sum(a, axis=0, keepdims=True)
    counts = cnt[:, 0:1]
    counts_row = row_sum(jnp.where(sub == lane, cnt, 0))
    tiles_per = (counts + (EXP_ROWS - 1)) >> EXP_SHIFT
    tiles_per_row = (counts_row + (EXP_ROWS - 1)) >> EXP_SHIFT
    tile_end = col_sum(jnp.where(lane <= sub, tiles_per_row, 0))
    tile_begin = tile_end - tiles_per
    n_valid = col_sum(tiles_per_row)
    tiles = lax.broadcasted_iota(I32, (1, lanes), 1)
    tile_valid = tiles < n_valid
    capped = jnp.minimum(tiles, n_valid - 1)
    tile_expert = row_sum((capped >= tile_end).astype(I32))
    tile_expert = jnp.minimum(tile_expert, N_EXPERTS - 1)
    is_first = (tiles == tile_begin) & (tiles_per > 0)
    tile_first = tile_valid & (row_sum(is_first.astype(I32)) > 0)
    nonempty_row = tiles_per_row > 0
    parity = (col_sum(jnp.where((lane <= sub) & nonempty_row, 1, 0)) - 1) & 1
    nxt = jnp.min(jnp.where((lane > sub) & nonempty_row, lane, N_EXPERTS), axis=1, keepdims=True)
    nxt = jnp.where(nxt == N_EXPERTS, -1, nxt)
    hot = tile_expert == lax.broadcasted_iota(I32, (N_EXPERTS, lanes), 0)
    pick = lambda col: row_sum(jnp.where(hot, col, 0))
    tile_slot = pick(parity)
    tile_next = pick(nxt)
    tile_in_group = tiles - pick(tile_begin)
    filled = jnp.clip(pick(counts) - tile_in_group * EXP_ROWS, 0, EXP_ROWS)
    filled = jnp.where(tile_valid, filled, 0)
    tile_rows = (filled + (EXP_GRANULE - 1)) & -EXP_GRANULE
    steps = lax.broadcasted_iota(I32, (1, lanes), 1)
    own_steps = jnp.sum((steps * EXP_TILES_PER_STEP < n_valid).astype(I32), axis=1, keepdims=True)
    step_block = jnp.minimum(steps, own_steps - 1)
    flags_ref[...] = jnp.concatenate(
        [tile_expert, tile_first.astype(I32), tile_rows, tile_slot, tile_next, step_block,
         jnp.zeros((2, lanes), I32)], axis=0)

    group_start = jnp.broadcast_to(tile_begin * EXP_ROWS, cnt.shape)
    idx = idx_ref[...]
    pos = rank_ref[...]
    for e in range(N_EXPERTS):
        pos = pos + jnp.where(idx == e, group_start[e:e + 1, 0:1], 0)
    pos_ref[...] = pos


def _plan_slots(cnt, idx, rank, n_tiles):
    lanes = -(-n_tiles // 128) * 128
    whole = lambda a: pl.BlockSpec(a.shape, lambda i: (0, 0))
    pos, table = pl.pallas_call(
        _plan_kernel,
        grid=(1,),
        in_specs=[whole(cnt), whole(idx), whole(rank)],
        out_specs=[pl.BlockSpec(idx.shape, lambda i: (0, 0)), pl.BlockSpec((8, lanes), lambda i: (0, 0))],
        out_shape=[jax.ShapeDtypeStruct(idx.shape, I32), jax.ShapeDtypeStruct((8, lanes), I32)],
        name="plan_slots",
    )(cnt, idx, rank)
    flags = tuple(table[r, :n_tiles] for r in range(5)) + (table[5, :n_tiles // EXP_TILES_PER_STEP],)
    return pos, flags


def _sc_mesh():
    return plsc.VectorSubcoreMesh(core_axis_name="core", subcore_axis_name="subcore")


def _sc_worker():
    info = plsc.get_sparse_core_info()
    wid = lax.axis_index("subcore") * info.num_cores + lax.axis_index("core")
    return wid, info.num_cores * info.num_subcores


def _dispatch(t2, pos_flat, n_slots):
    n_tok, d = t2.shape
    n_workers = V7X_SC_WORKERS
    per_w = n_tok // n_workers
    n_pairs = per_w // (2 * DISPATCH_ROWS)
    assert per_w % (2 * DISPATCH_ROWS) == 0

    @functools.partial(
        pl.kernel, mesh=_sc_mesh(),
        out_type=jax.ShapeDtypeStruct((n_slots, d), t2.dtype),
        scratch_types=(
            [pltpu.VMEM((DISPATCH_ROWS,), I32) for _ in range(2 * TOP_K)]
            + [pltpu.VMEM((DISPATCH_ROWS, d), t2.dtype) for _ in range(2)]
            + [pltpu.SemaphoreType.DMA for _ in range(2 * (TOP_K + 1))]),
        name="dispatch",
    )
    def k(t_hbm, pos_hbm, o_hbm, *scratch):
        idx_v = scratch[:2 * TOP_K]
        rows_v = scratch[2 * TOP_K:2 * TOP_K + 2]
        rsem = scratch[2 * TOP_K + 2:2 * TOP_K + 4]
        ssem = scratch[2 * TOP_K + 4:]
        wid, nw = _sc_worker()
        assert nw == n_workers
        first = wid * per_w

        def read(g, s):
            rows = pl.ds(first + g * DISPATCH_ROWS, DISPATCH_ROWS)
            return pltpu.make_async_copy(t_hbm.at[rows], rows_v[s], rsem[s])

        def scatter_chunk(g, s):
            for kk in range(TOP_K):
                pltpu.sync_copy(
                    pos_hbm.at[pl.ds(kk * n_tok + first + g * DISPATCH_ROWS, DISPATCH_ROWS)],
                    idx_v[s * TOP_K + kk])
            read(g, s).wait()
            scatters = [pltpu.async_copy(rows_v[s], o_hbm.at[idx_v[s * TOP_K + kk]],
                                         ssem[s * TOP_K + kk]) for kk in range(TOP_K)]
            for cp in scatters:
                cp.wait()

        read(0, 0).start()

        @pl.loop(0, n_pairs)
        def _(t):
            read(2 * t + 1, 1).start()
            scatter_chunk(2 * t, 0)

            @pl.when(t + 1 < n_pairs)
            def _():
                read(2 * t + 2, 0).start()

            scatter_chunk(2 * t + 1, 1)

    return k(t2, pos_flat)


def _gather_back(sorted_out, pos_flat):
    n_rows = pos_flat.shape[0]
    d = sorted_out.shape[1]
    n_workers = V7X_SC_WORKERS
    per_w = n_rows // n_workers
    group = GATHER_BUFS * GATHER_ROWS
    n_pairs = per_w // (2 * group)
    assert per_w % (2 * group) == 0

    @functools.partial(
        pl.kernel, mesh=_sc_mesh(),
        out_type=jax.ShapeDtypeStruct((n_rows, d), sorted_out.dtype),
        scratch_types=(
            [pltpu.VMEM((per_w // GATHER_ROWS, GATHER_ROWS), I32)]
            + [pltpu.VMEM((GATHER_ROWS, d), sorted_out.dtype) for _ in range(2 * GATHER_BUFS)]
            + [pltpu.SemaphoreType.DMA for _ in range(4 * GATHER_BUFS)]),
        name="gather_back",
    )
    def k(s_hbm, pos_hbm, o_hbm, idx_all, *scratch):
        nb = 2 * GATHER_BUFS
        rows_v = scratch[:nb]
        gsem, wsem = scratch[nb:2 * nb], scratch[2 * nb:]
        wid, nw = _sc_worker()
        assert nw == n_workers
        first = wid * per_w

        def bufs(s):
            return range(s * GATHER_BUFS, (s + 1) * GATHER_BUFS)

        def rows_of(g, b):
            return pl.ds(first + g * group + (b % GATHER_BUFS) * GATHER_ROWS, GATHER_ROWS)

        def gather(g, b):
            idx = idx_all.at[g * GATHER_BUFS + b % GATHER_BUFS]
            return pltpu.make_async_copy(s_hbm.at[idx], rows_v[b], gsem[b])

        def write(g, b):
            return pltpu.make_async_copy(rows_v[b], o_hbm.at[rows_of(g, b)], wsem[b])

        def start_gathers(g, s):
            for b in bufs(s):
                gather(g, b).start()

        def drain(g, s):
            for b in bufs(s):
                gather(g, b).wait()
                write(g, b).start()
            for b in bufs(s):
                write(g, b).wait()

        rows_per_w = per_w // GATHER_ROWS
        pltpu.sync_copy(pos_hbm.at[pl.ds(wid * rows_per_w, rows_per_w)], idx_all)

        start_gathers(0, 0)

        @pl.loop(0, n_pairs)
        def _(t):
            start_gathers(2 * t + 1, 1)
            drain(2 * t, 0)

            @pl.when(t + 1 < n_pairs)
            def _():
                start_gathers(2 * t + 2, 0)

            drain(2 * t + 1, 1)

    return k(sorted_out, pos_flat.reshape(n_rows // GATHER_ROWS, GATHER_ROWS))


def _weight_copies(w_hbm, wf32, sem, expert, slot):
    return [pltpu.make_async_copy(w.at[expert], wf32.at[slot, mtx], sem.at[slot])
            for mtx, w in enumerate(w_hbm)]


def _expert_rows(rows, x_ref, bg_ref, bu_ref, bd_ref, o_ref, wbf):
    xs = jnp.concatenate(_unpack_rows(x_ref[0:rows, :]), axis=1).astype(BF16)
    cn = 256
    hids = []
    for c in range(D_MODEL // cn):
        sl = slice(c * cn, (c + 1) * cn)
        g = jnp.dot(xs, wbf[0, :, sl], preferred_element_type=F32) + bg_ref[:, sl]
        u = jnp.dot(xs, wbf[1, :, sl], preferred_element_type=F32) + bu_ref[:, sl]
        g = jnp.minimum(g, SWIGLU_LIMIT)
        u = jnp.clip(u, -SWIGLU_LIMIT, SWIGLU_LIMIT)
        hids.append(((u + 1.0) * (g * jax.nn.sigmoid(SWIGLU_ALPHA * g))).astype(BF16))
    acc = jnp.dot(jnp.concatenate(hids, axis=1), wbf[2], preferred_element_type=F32)
    o_ref[0:rows, :] = _pack_rows(acc + bd_ref[...])
    if rows < EXP_ROWS:
        o_ref[rows:EXP_ROWS, :] = jnp.zeros((EXP_ROWS - rows, HALF), I32)


def _expert_kernel(texp_ref, tfirst_ref, trows_ref, tslot_ref, tnext_ref, sblk_ref,
                   x_ref, wg_hbm, bg_ref, wu_hbm, bu_ref, wd_hbm, bd_ref,
                   o_ref, wf32, wbf, wsem):
    w_hbm = (wg_hbm, wu_hbm, wd_hbm)
    own_block = sblk_ref[pl.program_id(0)] == pl.program_id(0)
    for sub in range(EXP_TILES_PER_STEP):
        i = pl.program_id(0) * EXP_TILES_PER_STEP + sub
        x_tile = x_ref.at[sub * EXP_ROWS:(sub + 1) * EXP_ROWS]
        o_tile = o_ref.at[sub * EXP_ROWS:(sub + 1) * EXP_ROWS]
        _expert_tile(i, own_block, texp_ref, tfirst_ref, trows_ref, tslot_ref, tnext_ref,
                     x_tile, w_hbm, bg_ref, bu_ref, bd_ref, o_tile, wf32, wbf, wsem)


def _expert_tile(i, own_block, texp_ref, tfirst_ref, trows_ref, tslot_ref, tnext_ref,
                 x_ref, w_hbm, bg_ref, bu_ref, bd_ref, o_ref, wf32, wbf, wsem):
    @pl.when((trows_ref[i] == 0) & own_block)
    def _():
        o_ref[...] = jnp.zeros(o_ref.shape, o_ref.dtype)

    @pl.when(tfirst_ref[i] == 1)
    def _():
        slot = tslot_ref[i]
        expert = texp_ref[i]

        @pl.when(i == 0)
        def _():
            for cp in _weight_copies(w_hbm, wf32, wsem, expert, slot):
                cp.start()

        for cp in _weight_copies(w_hbm, wf32, wsem, expert, slot):
            cp.wait()

        @pl.when(tnext_ref[i] >= 0)
        def _():
            for cp in _weight_copies(w_hbm, wf32, wsem, tnext_ref[i], 1 - slot):
                cp.start()

        for mtx in range(3):
            _cast_rows(wf32.at[slot, mtx], wbf.at[mtx], D_MODEL)

    for rows in range(EXP_GRANULE, EXP_ROWS + 1, EXP_GRANULE):
        @pl.when(trows_ref[i] == rows)
        def _(rows=rows):
            expert = texp_ref[i]
            _expert_rows(rows, x_ref, bg_ref.at[expert], bu_ref.at[expert], bd_ref.at[expert],
                         o_ref, wbf)


def _experts(sorted_t, tile_flags, w_gate, b_gate, w_up, b_up, w_down, b_down, n_tiles):
    row_blk = pl.BlockSpec((EXP_TILES_PER_STEP * EXP_ROWS, HALF), lambda i, *flags: (flags[5][i], 0))
    w_any = pl.BlockSpec(memory_space=pl.ANY)
    b_blk = pl.BlockSpec((N_EXPERTS, 1, D_MODEL), lambda i, *_: (0, 0, 0))
    grid_spec = pltpu.PrefetchScalarGridSpec(
        num_scalar_prefetch=len(tile_flags),
        grid=(n_tiles // EXP_TILES_PER_STEP,),
        in_specs=[row_blk, w_any, b_blk, w_any, b_blk, w_any, b_blk],
        out_specs=row_blk,
        scratch_shapes=[
            pltpu.VMEM((2, 3, D_MODEL, D_MODEL), F32),
            pltpu.VMEM((3, D_MODEL, D_MODEL), BF16),
            pltpu.SemaphoreType.DMA((2,)),
        ],
    )
    b3 = lambda b: b.reshape(N_EXPERTS, 1, D_MODEL)
    return pl.pallas_call(
        _expert_kernel,
        grid_spec=grid_spec,
        out_shape=jax.ShapeDtypeStruct((n_tiles * EXP_ROWS, HALF), I32),
        compiler_params=pltpu.CompilerParams(
            dimension_semantics=("arbitrary",), vmem_limit_bytes=VMEM_LIMIT),
        name="experts",
    )(*tile_flags, sorted_t,
      w_gate, b3(b_gate), w_up, b3(b_up), w_down, b3(b_down))


def _combine_kernel(x1_ref, s0_ref, s1_ref, s2_ref, s3_ref, w_ref, g_ref, o_ref):
    w = w_ref[...]
    y_lo = x1_ref[:, :HALF]
    y_hi = x1_ref[:, HALF:]
    for k, s_ref in enumerate((s0_ref, s1_ref, s2_ref, s3_ref)):
        lo, hi = _unpack_rows(s_ref[...])
        y_lo = y_lo + w[:, k:k + 1] * lo
        y_hi = y_hi + w[:, k:k + 1] * hi
    sq = jnp.sum(y_lo * y_lo, axis=-1, keepdims=True) + jnp.sum(y_hi * y_hi, axis=-1, keepdims=True)
    scale = lax.rsqrt(sq * (1.0 / D_MODEL) + RMS_EPS)
    o_ref[:, :HALF] = (y_lo * scale) * g_ref[:, :HALF]
    o_ref[:, HALF:] = (y_hi * scale) * g_ref[:, HALF:]


def _combine(x1, slabs, wgt, norm_g):
    n_tok = x1.shape[0]
    rows = MIX_ROWS
    ntiles = n_tok // rows
    slab_spec = lambda k: pl.BlockSpec((rows, HALF), lambda i, k=k: (k * ntiles + i, 0))
    return pl.pallas_call(
        _combine_kernel,
        grid=(ntiles,),
        in_specs=[pl.BlockSpec((rows, D_MODEL), lambda i: (i, 0))]
        + [slab_spec(k) for k in range(TOP_K)]
        + [pl.BlockSpec((rows, 8), lambda i: (i, 0)),
           pl.BlockSpec((1, D_MODEL), lambda i: (0, 0))],
        out_specs=pl.BlockSpec((rows, D_MODEL), lambda i: (i, 0)),
        out_shape=jax.ShapeDtypeStruct((n_tok, D_MODEL), F32),
        compiler_params=pltpu.CompilerParams(
            dimension_semantics=("arbitrary",), vmem_limit_bytes=VMEM_LIMIT),
        name="combine",
    )(x1, slabs, slabs, slabs, slabs, wgt, norm_g.reshape(1, D_MODEL))


def kernel(x, norm_mix_g, w_in, conv_dw_w, conv_dw_b, conv_ln_g, conv_ln_b, rel_bias, w_out,
           norm_ffn_g, router_w, router_b, exp_w_gate, exp_b_gate, exp_w_up, exp_b_up,
           exp_w_down, exp_b_down, norm_final_g):
    bsz, seq, _ = x.shape
    n_tok = bsz * seq
    assert norm_mix_g.shape[0] == 1, "single-layer block"
    assert seq % IN_ROWS == 0 and seq % MIX_ROWS == 0 and LEFT_PAD % IN_ROWS == 0
    n_tiles = (TOP_K * n_tok) // EXP_ROWS + N_EXPERTS - 1
    n_tiles = -(-n_tiles // EXP_TILES_PER_STEP) * EXP_TILES_PER_STEP

    q, kpad, vpad, conv = _inproj(x, norm_mix_g[0], w_in[0], conv_dw_w[0], conv_dw_b[0],
                                  conv_ln_g[0], conv_ln_b[0])
    attn = _attention(q, kpad, vpad, _band_bias(rel_bias[0]))
    x1, t, idx, wgt, rank, cnt = _mix_route(
        x.reshape(n_tok, D_MODEL), attn.reshape(n_tok, D_ATTN), conv.reshape(n_tok, D_CONV),
        w_out[0], norm_ffn_g[0], router_w[0], router_b[0])
    pos, tile_flags = _plan_slots(cnt, idx, rank, n_tiles)
    pos_flat = pos.reshape(TOP_K * n_tok)
    sorted_t = _dispatch(t, pos_flat, n_tiles * EXP_ROWS)
    sorted_out = _experts(sorted_t, tile_flags,
                          exp_w_gate[0], exp_b_gate[0], exp_w_up[0], exp_b_up[0],
                          exp_w_down[0], exp_b_down[0], n_tiles)
    slabs = _gather_back(sorted_out, pos_flat)
    out = _combine(x1, slabs, wgt, norm_final_g)
    return out.reshape(bsz, seq, D_MODEL)
```
